```python
import math
import jax, jax.numpy as jnp
from jax import lax
import numpy as np

D_MODEL = 1024
BATCH = 2
SEQ = 8192
DEPTH = 1

CHUNK = 64
CONV_CH = D_MODEL // 2
CONV_WIDTH = 31
RET_HEADS = 8
RET_DV = (D_MODEL // 2) // RET_HEADS
RET_DK = RET_DV // 2
D_MIX = CONV_CH + RET_HEADS * RET_DV
N_MEM = 256
XATTN_HEADS = 4
XATTN_HEAD_DIM = D_MODEL // XATTN_HEADS
D_FF = 4 * D_MODEL
ROPE_BASE = 10000.0
EPS = 1e-6

IN_SPLITS = (CONV_CH, CONV_CH, RET_HEADS * RET_DK, RET_HEADS * RET_DK, RET_HEADS * RET_DV, RET_HEADS * RET_DV)
D_IN = sum(IN_SPLITS)

kernel_name = "hybrid_conv_retention_memxattn_block"


def rmsnorm(x, w):
    xf = x.astype(jnp.float32)
    y = xf * lax.rsqrt(jnp.mean(xf * xf, axis=-1, keepdims=True) + EPS)
    return (y * w.astype(jnp.float32)).astype(x.dtype)


def apply_rotary(t, positions):
    half = t.shape[-1] // 2
    inv_freq = ROPE_BASE ** (-jnp.arange(half, dtype=jnp.float32) / half)
    ang = positions.astype(jnp.float32)[..., None] * inv_freq
    cos = jnp.cos(ang)[:, :, None, :]
    sin = jnp.sin(ang)[:, :, None, :]
    t1, t2 = t[..., :half].astype(jnp.float32), t[..., half:].astype(jnp.float32)
    return jnp.concatenate([t1 * cos - t2 * sin, t1 * sin + t2 * cos], axis=-1)


def conv_module(a, b, conv_w, conv_b, ln_w, ln_b):
    u = a * jax.nn.sigmoid(b)
    y = lax.conv_general_dilated(
        u, conv_w[:, None, :], window_strides=(1,), padding=[(CONV_WIDTH - 1, 0)],
        dimension_numbers=("NWC", "WIO", "NWC"), feature_group_count=CONV_CH) + conv_b
    yf = y.astype(jnp.float32)
    mu = jnp.mean(yf, axis=-1, keepdims=True)
    var = jnp.mean(jnp.square(yf - mu), axis=-1, keepdims=True)
    yn = (yf - mu) * lax.rsqrt(var + EPS) * ln_w + ln_b
    return jax.nn.silu(yn)


def retention(q, k, v, positions):
    B, S = q.shape[:2]
    N = S // CHUNK
    q = apply_rotary(q.reshape(B, S, RET_HEADS, RET_DK), positions) * (RET_DK ** -0.5)
    k = apply_rotary(k.reshape(B, S, RET_HEADS, RET_DK), positions)
    v = v.reshape(B, S, RET_HEADS, RET_DV).astype(jnp.float32)
    qc = q.reshape(B, N, CHUNK, RET_HEADS, RET_DK)
    kc = k.reshape(B, N, CHUNK, RET_HEADS, RET_DK)
    vc = v.reshape(B, N, CHUNK, RET_HEADS, RET_DV)

    log_g = jnp.log1p(-jnp.exp2(-5.0 - jnp.arange(RET_HEADS, dtype=jnp.float32)))
    idx = jnp.arange(CHUNK, dtype=jnp.float32)
    dist = jnp.abs(idx[:, None] - idx[None, :])
    decay_in = jnp.exp(log_g[:, None, None] * dist)
    zeta = jnp.exp(log_g[:, None] * (CHUNK - 1 - idx)[None, :])
    xi = jnp.exp(log_g[:, None] * (idx + 1.0)[None, :])
    g_chunk = jnp.exp(log_g * CHUNK)

    scores = jnp.einsum('bnihd,bnjhd->bnhij', qc, kc) * decay_in
    y_in = jnp.einsum('bnhij,bnjhe->bnihe', scores, vc)

    kv = jnp.einsum('bnjhd,bnjhe,hj->nbhde', kc, vc, zeta)

    def step(state, kv_n):
        return g_chunk[None, :, None, None] * state + kv_n, state

    init = jnp.zeros((B, RET_HEADS, RET_DK, RET_DV), jnp.float32)
    _, state_prev = lax.scan(step, init, kv)
    y_x = jnp.einsum('bnihd,nbhde,hi->bnihe', qc, state_prev, xi)
    return (y_in + y_x).reshape(B, S, RET_HEADS, RET_DV)


def head_group_mixer(h, positions, w_in, conv_w, conv_b, conv_ln_w, conv_ln_b, ret_gn_w, w_out):
    B, S, _ = h.shape
    proj = h @ w_in
    cuts = list(np.cumsum(IN_SPLITS)[:-1])
    ca, cb, rq, rk, rv, rg = jnp.split(proj, cuts, axis=-1)
    y_conv = conv_module(ca, cb, conv_w, conv_b, conv_ln_w, conv_ln_b)
    y = retention(rq, rk, rv, positions)
    mu = jnp.mean(y, axis=-1, keepdims=True)
    var = jnp.mean(jnp.square(y - mu), axis=-1, keepdims=True)
    y = (y - mu) * lax.rsqrt(var + EPS) * ret_gn_w.reshape(RET_HEADS, RET_DV)
    y_ret = jax.nn.silu(rg.astype(jnp.float32)) * y.reshape(B, S, RET_HEADS * RET_DV)
    merged = jnp.concatenate([y_conv, y_ret], axis=-1).astype(h.dtype)
    return merged @ w_out


def memory_cross_attention(h, mem_n, xq_w, xkv_w, xo_w):
    B, S, _ = h.shape
    q = (h @ xq_w).reshape(B, S, XATTN_HEADS, XATTN_HEAD_DIM)
    kv = (mem_n @ xkv_w).reshape(B, N_MEM, 2, XATTN_HEADS, XATTN_HEAD_DIM)
    k, v = kv[:, :, 0], kv[:, :, 1]
    s = jnp.einsum('bshd,bmhd->bhsm', q.astype(jnp.float32), k.astype(jnp.float32)) * (XATTN_HEAD_DIM ** -0.5)
    p = jax.nn.softmax(s, axis=-1)
    o = jnp.einsum('bhsm,bmhd->bshd', p, v.astype(jnp.float32)).reshape(B, S, D_MODEL)
    return o.astype(h.dtype) @ xo_w


def sq_relu_mlp(h, up_w, down_w):
    return jnp.square(jax.nn.relu(h @ up_w)) @ down_w


def setup_inputs(seed: int = 0) -> dict:
    key = jax.random.key(seed)
    ks = jax.random.split(key, 24)
    f32 = jnp.float32

    def w(k, shape, fan_in):
        return jax.random.normal(k, shape, f32) * (fan_in ** -0.5)

    def gain(k, n):
        return 1.0 + 0.02 * jax.random.normal(k, (n,), f32)

    x = jax.random.normal(ks[0], (BATCH, SEQ, D_MODEL), f32)
    mem = jax.random.normal(ks[1], (BATCH, N_MEM, D_MODEL), f32)
    start = jax.random.randint(ks[2], (BATCH, 1), 0, 1000, dtype=jnp.int32) * CHUNK
    positions = (start + jnp.arange(SEQ, dtype=jnp.int32)[None, :]).astype(jnp.int32)
    return {
        "x": x,
        "mem": mem,
        "positions": positions,
        "norm_mix_w": gain(ks[3], D_MODEL),
        "w_in": w(ks[4], (D_MODEL, D_IN), D_MODEL),
        "conv_w": w(ks[5], (CONV_WIDTH, CONV_CH), CONV_WIDTH),
        "conv_b": 0.02 * jax.random.normal(ks[6], (CONV_CH,), f32),
        "conv_ln_w": gain(ks[7], CONV_CH),
        "conv_ln_b": 0.02 * jax.random.normal(ks[8], (CONV_CH,), f32),
        "ret_gn_w": gain(ks[9], RET_HEADS * RET_DV),
        "w_out": w(ks[10], (D_MIX, D_MODEL), D_MIX),
        "norm_xattn_w": gain(ks[11], D_MODEL),
        "norm_mem_w": gain(ks[12], D_MODEL),
        "xq_w": w(ks[13], (D_MODEL, D_MODEL), D_MODEL),
        "xkv_w": w(ks[14], (D_MODEL, 2 * D_MODEL), D_MODEL),
        "xo_w": w(ks[15], (D_MODEL, D_MODEL), D_MODEL),
        "norm_mlp_w": gain(ks[16], D_MODEL),
        "mlp_up_w": w(ks[17], (D_MODEL, D_FF), D_MODEL),
        "mlp_down_w": w(ks[18], (D_FF, D_MODEL), D_FF),
        "norm_f_w": gain(ks[19], D_MODEL),
    }


def reference(x, mem, positions, norm_mix_w, w_in, conv_w, conv_b, conv_ln_w, conv_ln_b, ret_gn_w, w_out,
              norm_xattn_w, norm_mem_w, xq_w, xkv_w, xo_w, norm_mlp_w, mlp_up_w, mlp_down_w, norm_f_w):
    h = x
    mem_n = rmsnorm(mem, norm_mem_w)
    for _ in range(DEPTH):
        h = h + head_group_mixer(rmsnorm(h, norm_mix_w), positions, w_in, conv_w, conv_b,
                                 conv_ln_w, conv_ln_b, ret_gn_w, w_out)
        h = h + memory_cross_attention(rmsnorm(h, norm_xattn_w), mem_n, xq_w, xkv_w, xo_w)
        h = h + sq_relu_mlp(rmsnorm(h, norm_mlp_w), mlp_up_w, mlp_down_w)
    return rmsnorm(h, norm_f_w).astype(x.dtype)
```

```python
import functools
import math

import numpy as np
import jax
import jax.numpy as jnp
from jax import lax
from jax.experimental import pallas as pl
from jax.experimental.pallas import tpu as pltpu

D_MODEL = 1024
CHUNK = 64
CONV_CH = 512
CONV_WIDTH = 31
RET_HEADS = 8
RET_DV = 64
RET_DK = 32
RET_QK = RET_HEADS * RET_DK
RET_V = RET_HEADS * RET_DV
N_MEM = 256
XATTN_HEADS = 4
XATTN_HEAD_DIM = 256
D_FF = 4096
ROPE_BASE = 10000.0
EPS = 1e-6

OFF_A, OFF_B, OFF_Q, OFF_K, OFF_V, OFF_G = 0, 512, 1024, 1280, 1536, 2048
D_IN = 2560

LANES = 128
MIX_BLOCK = 256
CONV_HALO = 32
CONV_ROWS = 32
TOK_BLOCK = 512
FF_CHUNK = 1024
VMEM_LIMIT = 56 * 1024 * 1024

_f32 = jnp.float32
_bf16 = jnp.bfloat16


def _dot(a, b):
    return jnp.dot(a, b, preferred_element_type=_f32)


def _rms(x, w):
    return x * lax.rsqrt(jnp.mean(x * x, axis=-1, keepdims=True) + EPS) * w


def _sigmoid(x):
    return 1.0 / (1.0 + jnp.exp(-x))


def _mem_kv_kernel(mem_ref, nw_ref, wkv_ref, kt_ref, v_ref):
    m = _rms(mem_ref[0], nw_ref[...]).astype(_bf16)
    k = _dot(m, wkv_ref[:, :D_MODEL])
    kt_ref[0] = k.T.astype(_bf16)
    v_ref[0] = _dot(m, wkv_ref[:, D_MODEL:]).astype(_bf16)


def _mem_kv(mem, norm_mem_w, xkv_w):
    B = mem.shape[0]
    return pl.pallas_call(
        _mem_kv_kernel,
        grid=(B,),
        in_specs=[
            pl.BlockSpec((1, N_MEM, D_MODEL), lambda b: (b, 0, 0)),
            pl.BlockSpec((1, D_MODEL), lambda b: (0, 0)),
            pl.BlockSpec((D_MODEL, 2 * D_MODEL), lambda b: (0, 0)),
        ],
        out_specs=[
            pl.BlockSpec((1, D_MODEL, N_MEM), lambda b: (b, 0, 0)),
            pl.BlockSpec((1, N_MEM, D_MODEL), lambda b: (b, 0, 0)),
        ],
        out_shape=[
            jax.ShapeDtypeStruct((B, D_MODEL, N_MEM), _bf16),
            jax.ShapeDtypeStruct((B, N_MEM, D_MODEL), _bf16),
        ],
        compiler_params=pltpu.CompilerParams(
            dimension_semantics=("arbitrary",), vmem_limit_bytes=VMEM_LIMIT),
        name="mem_kv",
    )(mem, norm_mem_w.reshape(1, D_MODEL), xkv_w)


def _group_norm_halves(y, lo):
    inv = 1.0 / RET_DV
    s_lo = jnp.sum(jnp.where(lo, y, 0.0), axis=-1, keepdims=True)
    s_hi = jnp.sum(jnp.where(lo, 0.0, y), axis=-1, keepdims=True)
    d = y - jnp.where(lo, s_lo, s_hi) * inv
    d2 = d * d
    v_lo = jnp.sum(jnp.where(lo, d2, 0.0), axis=-1, keepdims=True)
    v_hi = jnp.sum(jnp.where(lo, 0.0, d2), axis=-1, keepdims=True)
    return d * lax.rsqrt(jnp.where(lo, v_lo, v_hi) * inv + EPS)


def _mixer_kernel(x_ref, pos_ref, nw_ref, win_ref, cw_ref, cb_ref, lnw_ref, lnb_ref, gnw_ref, wout_ref,
                  dmat_ref, xi_ref, zeta_ref, gl_ref, invf_ref, o_ref,
                  ubuf, state, mbuf):
    L = MIX_BLOCK

    @pl.when(pl.program_id(1) == 0)
    def _():
        ubuf[0:CONV_HALO, :] = jnp.zeros((CONV_HALO, CONV_CH), _f32)
        state[...] = jnp.zeros_like(state)

    x = x_ref[0]
    hn = _rms(x, nw_ref[...]).astype(_bf16)

    a = _dot(hn, win_ref[:, OFF_A:OFF_A + CONV_CH])
    b = _dot(hn, win_ref[:, OFF_B:OFF_B + CONV_CH])
    ubuf[CONV_HALO:CONV_HALO + L, :] = a * _sigmoid(b)
    first = CONV_HALO - (CONV_WIDTH - 1)
    for r in range(L // CONV_ROWS):
        acc = jnp.broadcast_to(cb_ref[...], (CONV_ROWS, CONV_CH))
        for k in range(CONV_WIDTH):
            acc = acc + cw_ref[k:k + 1, :] * ubuf[pl.ds(r * CONV_ROWS + first + k, CONV_ROWS), :]
        mu = jnp.mean(acc, axis=-1, keepdims=True)
        d = acc - mu
        var = jnp.mean(d * d, axis=-1, keepdims=True)
        yn = d * lax.rsqrt(var + EPS) * lnw_ref[...] + lnb_ref[...]
        mbuf[r * CONV_ROWS:(r + 1) * CONV_ROWS, 0:CONV_CH] = (yn * _sigmoid(yn)).astype(_bf16)
    ubuf[0:CONV_HALO, :] = ubuf[L:L + CONV_HALO, :]

    half = RET_QK // 2
    ang = pos_ref[0].astype(_f32) * invf_ref[...]
    cos, sin = jnp.cos(ang), jnp.sin(ang)

    def rotary(t):
        t1, t2 = t[:, :half], t[:, half:]
        return jnp.concatenate([t1 * cos - t2 * sin, t1 * sin + t2 * cos], axis=-1)

    qr = rotary(_dot(hn, win_ref[:, OFF_Q:OFF_Q + RET_QK])) * (RET_DK ** -0.5)
    kr = rotary(_dot(hn, win_ref[:, OFF_K:OFF_K + RET_QK]))
    v = _dot(hn, win_ref[:, OFF_V:OFF_V + RET_V])
    g = _dot(hn, win_ref[:, OFF_G:OFF_G + RET_V])
    kt = kr.T.astype(_bf16)
    vb = v.astype(_bf16)

    yx = _dot(qr.astype(_bf16), state[...].astype(_bf16)) * xi_ref[...]

    q_head = (lax.broadcasted_iota(jnp.int32, (1, RET_QK), 1) % half) // (RET_DK // 2)
    lo = lax.broadcasted_iota(jnp.int32, (1, LANES), 1) < RET_DV
    for p in range(RET_HEADS // 2):
        cols = slice(p * LANES, (p + 1) * LANES)
        vp = vb[:, cols]
        ys = []
        for h in (2 * p, 2 * p + 1):
            qm = jnp.where(q_head == h, qr, 0.0).astype(_bf16)
            s = _dot(qm, kt)
            ys.append(_dot((s * dmat_ref[h]).astype(_bf16), vp))
        y = jnp.where(lo, ys[0], ys[1]) + yx[:, cols]
        yn = _group_norm_halves(y, lo) * gnw_ref[:, cols]
        gp = g[:, cols]
        mbuf[:, CONV_CH + p * LANES:CONV_CH + (p + 1) * LANES] = (gp * _sigmoid(gp) * yn).astype(_bf16)

    kv = _dot(kt, (v * zeta_ref[...]).astype(_bf16))
    row_head = (lax.broadcasted_iota(jnp.int32, (RET_QK, RET_V), 0) % half) // (RET_DK // 2)
    col_head = lax.broadcasted_iota(jnp.int32, (RET_QK, RET_V), 1) // RET_DV
    state[...] = gl_ref[...] * state[...] + jnp.where(row_head == col_head, kv, 0.0)

    o_ref[0] = x + _dot(mbuf[...], wout_ref[...])


def _retention_tables(L):
    h = np.arange(RET_HEADS, dtype=np.float64)
    log_g = np.log1p(-np.exp2(-5.0 - h))
    idx = np.arange(L, dtype=np.float64)
    dist = np.abs(idx[:, None] - idx[None, :])
    visible = (idx[None, :] // CHUNK) <= (idx[:, None] // CHUNK)
    dmat = np.where(visible[None], np.exp(log_g[:, None, None] * dist[None]), 0.0)
    xi = np.exp(log_g[None, :] * (idx[:, None] + 1.0))
    zeta = np.exp(log_g[None, :] * (L - 1.0 - idx[:, None]))
    gl = np.exp(log_g * L)
    rep = lambda t: np.repeat(t, RET_DV, axis=-1)
    inv_freq = ROPE_BASE ** (-np.arange(RET_DK // 2, dtype=np.float32) / np.float32(RET_DK // 2))
    invf = np.tile(inv_freq.astype(np.float32), RET_HEADS)[None, :]
    f = lambda t: jnp.asarray(t, dtype=_f32)
    return f(dmat), f(rep(xi)), f(rep(zeta)), f(rep(gl[None, :])), f(invf)


def _in_proj_columns():
    c = np.arange(RET_QK)
    qk = ((c % (RET_QK // 2)) // (RET_DK // 2)) * RET_DK + (c // (RET_QK // 2)) * (RET_DK // 2) + c % (RET_DK // 2)
    base = np.arange(D_IN)
    base[OFF_Q:OFF_Q + RET_QK] = OFF_Q + qk
    base[OFF_K:OFF_K + RET_QK] = OFF_K + qk
    return base


def _mixer(x, positions, norm_w, w_in, conv_w, conv_b, ln_w, ln_b, gn_w, w_out):
    B, S, D = x.shape
    L = MIX_BLOCK
    dmat, xi, zeta, gl, invf = _retention_tables(L)
    const = lambda shape: pl.BlockSpec(shape, lambda b, j: (0,) * len(shape))
    row = lambda n: pl.BlockSpec((1, n), lambda b, j: (0, 0))
    return pl.pallas_call(
        _mixer_kernel,
        grid=(B, S // L),
        in_specs=[
            pl.BlockSpec((1, L, D), lambda b, j: (b, j, 0)),
            pl.BlockSpec((1, L, 1), lambda b, j: (b, j, 0)),
            row(D),
            const((D, D_IN)),
            const((CONV_WIDTH, CONV_CH)),
            row(CONV_CH), row(CONV_CH), row(CONV_CH), row(RET_V),
            const((D, D)),
            const((RET_HEADS, L, L)),
            const((L, RET_V)), const((L, RET_V)), row(RET_V), row(LANES),
        ],
        out_specs=pl.BlockSpec((1, L, D), lambda b, j: (b, j, 0)),
        out_shape=jax.ShapeDtypeStruct((B, S, D), _f32),
        scratch_shapes=[
            pltpu.VMEM((CONV_HALO + L, CONV_CH), _f32),
            pltpu.VMEM((RET_QK, RET_V), _f32),
            pltpu.VMEM((L, D), _bf16),
        ],
        compiler_params=pltpu.CompilerParams(
            dimension_semantics=("arbitrary", "arbitrary"), vmem_limit_bytes=VMEM_LIMIT),
        name="mixer",
    )(x, positions.reshape(B, S, 1), norm_w.reshape(1, D), w_in, conv_w, conv_b.reshape(1, -1),
      ln_w.reshape(1, -1), ln_b.reshape(1, -1), gn_w.reshape(1, -1), w_out, dmat, xi, zeta, gl, invf)


def _xattn_kernel(h_ref, nw_ref, wq_ref, kt_ref, v_ref, wo_ref, o_ref, obuf):
    h = h_ref[0]
    hn = _rms(h, nw_ref[...]).astype(_bf16)
    q = (_dot(hn, wq_ref[...]) * (XATTN_HEAD_DIM ** -0.5)).astype(_bf16)
    for i in range(XATTN_HEADS):
        cols = slice(i * XATTN_HEAD_DIM, (i + 1) * XATTN_HEAD_DIM)
        s = _dot(q[:, cols], kt_ref[0, cols, :])
        e = jnp.exp(s - jnp.max(s, axis=-1, keepdims=True))
        o = _dot(e.astype(_bf16), v_ref[0, :, cols])
        obuf[:, cols] = (o * (1.0 / jnp.sum(e, axis=-1, keepdims=True))).astype(_bf16)
    o_ref[0] = h + _dot(obuf[...], wo_ref[...])


def _xattn(h, norm_w, xq_w, kt, v, xo_w):
    B, S, D = h.shape
    T = TOK_BLOCK
    return pl.pallas_call(
        _xattn_kernel,
        grid=(B, S // T),
        in_specs=[
            pl.BlockSpec((1, T, D), lambda b, j: (b, j, 0)),
            pl.BlockSpec((1, D), lambda b, j: (0, 0)),
            pl.BlockSpec((D, D), lambda b, j: (0, 0)),
            pl.BlockSpec((1, D, N_MEM), lambda b, j: (b, 0, 0)),
            pl.BlockSpec((1, N_MEM, D), lambda b, j: (b, 0, 0)),
            pl.BlockSpec((D, D), lambda b, j: (0, 0)),
        ],
        out_specs=pl.BlockSpec((1, T, D), lambda b, j: (b, j, 0)),
        out_shape=jax.ShapeDtypeStruct((B, S, D), _f32),
        scratch_shapes=[pltpu.VMEM((T, D), _bf16)],
        compiler_params=pltpu.CompilerParams(
            dimension_semantics=("arbitrary", "arbitrary"), vmem_limit_bytes=VMEM_LIMIT),
        name="xattn",
    )(h, norm_w.reshape(1, D), xq_w, kt, v, xo_w)


def _mlp_kernel(h_ref, nw_ref, wu_ref, wd_ref, fw_ref, o_ref):
    h = h_ref[...]
    hn = _rms(h, nw_ref[...]).astype(_bf16)
    acc = h
    for c in range(D_FF // FF_CHUNK):
        cols = slice(c * FF_CHUNK, (c + 1) * FF_CHUNK)
        u = jnp.maximum(_dot(hn, wu_ref[:, cols]), 0.0)
        acc = acc + _dot((u * u).astype(_bf16), wd_ref[cols, :])
    o_ref[...] = _rms(acc, fw_ref[...])


def _mlp(h, norm_w, up_w, down_w, norm_f_w):
    B, S, D = h.shape
    T = TOK_BLOCK
    h2 = h.reshape(B * S, D)
    out = pl.pallas_call(
        _mlp_kernel,
        grid=(B * S // T,),
        in_specs=[
            pl.BlockSpec((T, D), lambda i: (i, 0)),
            pl.BlockSpec((1, D), lambda i: (0, 0)),
            pl.BlockSpec((D, D_FF), lambda i: (0, 0)),
            pl.BlockSpec((D_FF, D), lambda i: (0, 0)),
            pl.BlockSpec((1, D), lambda i: (0, 0)),
        ],
        out_specs=pl.BlockSpec((T, D), lambda i: (i, 0)),
        out_shape=jax.ShapeDtypeStruct((B * S, D), _f32),
        compiler_params=pltpu.CompilerParams(
            dimension_semantics=("arbitrary",), vmem_limit_bytes=VMEM_LIMIT),
        name="mlp",
    )(h2, norm_w.reshape(1, D), up_w, down_w, norm_f_w.reshape(1, D))
    return out.reshape(B, S, D)


def kernel(x, mem, positions, norm_mix_w, w_in, conv_w, conv_b, conv_ln_w, conv_ln_b, ret_gn_w, w_out,
           norm_xattn_w, norm_mem_w, xq_w, xkv_w, xo_w, norm_mlp_w, mlp_up_w, mlp_down_w, norm_f_w):
    bf = lambda w: w.astype(_bf16)
    w_in_b = bf(w_in[:, _in_proj_columns()])
    kt, v = _mem_kv(mem, norm_mem_w, bf(xkv_w))
    h = _mixer(x, positions, norm_mix_w, w_in_b, conv_w, conv_b, conv_ln_w, conv_ln_b, ret_gn_w, bf(w_out))
    h = _xattn(h, norm_xattn_w, bf(xq_w), kt, v, bf(xo_w))
    return _mlp(h, norm_mlp_w, bf(mlp_up_w), bf(mlp_down_w), norm_f_w)
```

```python
import functools
import math

import numpy as np
import jax
import jax.numpy as jnp
from jax import lax
from jax.experimental import pallas as pl
from jax.experimental.pallas import tpu as pltpu

D_MODEL = 1024
CHUNK = 64
CONV_CH = 512
CONV_WIDTH = 31
RET_HEADS = 8
RET_DV = 64
RET_DK = 32
RET_QK = RET_HEADS * RET_DK
RET_V = RET_HEADS * RET_DV
N_MEM = 256
XATTN_HEADS = 4
XATTN_HEAD_DIM = 256
D_FF = 4096
ROPE_BASE = 10000.0
EPS = 1e-6

OFF_A, OFF_B, OFF_Q, OFF_K, OFF_V, OFF_G = 0, 512, 1024, 1280, 1536, 2048
D_IN = 2560

LANES = 128
SUBLANES = 8
MIX_BLOCK = 256
CONV_HALO = 32
CONV_TILE = 128
CONV_ROWS = 32
TOK_BLOCK = 512
FF_CHUNK = 1024
VMEM_LIMIT = 56 * 1024 * 1024

_f32 = jnp.float32
_bf16 = jnp.bfloat16


def _dot(a, b):
    return jnp.dot(a, b, preferred_element_type=_f32)


def _rms(x, w):
    return x * lax.rsqrt(jnp.mean(x * x, axis=-1, keepdims=True) + EPS) * w


def _sigmoid(x):
    return 1.0 / (1.0 + jnp.exp(-x))


def _mem_kv_kernel(mem_ref, nw_ref, wkv_ref, kt_ref, v_ref):
    m = _rms(mem_ref[0], nw_ref[...]).astype(_bf16)
    k = _dot(m, wkv_ref[:, :D_MODEL])
    kt_ref[0] = k.T.astype(_bf16)
    v_ref[0] = _dot(m, wkv_ref[:, D_MODEL:]).astype(_bf16)


def _mem_kv(mem, norm_mem_w, xkv_w):
    B = mem.shape[0]
    return pl.pallas_call(
        _mem_kv_kernel,
        grid=(B,),
        in_specs=[
            pl.BlockSpec((1, N_MEM, D_MODEL), lambda b: (b, 0, 0)),
            pl.BlockSpec((1, D_MODEL), lambda b: (0, 0)),
            pl.BlockSpec((D_MODEL, 2 * D_MODEL), lambda b: (0, 0)),
        ],
        out_specs=[
            pl.BlockSpec((1, D_MODEL, N_MEM), lambda b: (b, 0, 0)),
            pl.BlockSpec((1, N_MEM, D_MODEL), lambda b: (b, 0, 0)),
        ],
        out_shape=[
            jax.ShapeDtypeStruct((B, D_MODEL, N_MEM), _bf16),
            jax.ShapeDtypeStruct((B, N_MEM, D_MODEL), _bf16),
        ],
        compiler_params=pltpu.CompilerParams(
            dimension_semantics=("arbitrary",), vmem_limit_bytes=VMEM_LIMIT),
        name="mem_kv",
    )(mem, norm_mem_w.reshape(1, D_MODEL), xkv_w)


def _group_norm_halves(y, lo):
    inv = 1.0 / RET_DV
    s_lo = jnp.sum(jnp.where(lo, y, 0.0), axis=-1, keepdims=True)
    s_hi = jnp.sum(jnp.where(lo, 0.0, y), axis=-1, keepdims=True)
    d = y - jnp.where(lo, s_lo, s_hi) * inv
    d2 = d * d
    v_lo = jnp.sum(jnp.where(lo, d2, 0.0), axis=-1, keepdims=True)
    v_hi = jnp.sum(jnp.where(lo, 0.0, d2), axis=-1, keepdims=True)
    return d * lax.rsqrt(jnp.where(lo, v_lo, v_hi) * inv + EPS)


def _mixer_kernel(x_ref, pos_ref, nw_ref, win_ref, cw_ref, cb_ref, lnw_ref, lnb_ref, gnw_ref, wout_ref,
                  dmat_ref, xi_ref, zeta_ref, gl_ref, invf_ref, cost_ref, sint_ref, o_ref,
                  ubuf, ybuf, state, mbuf):
    L = MIX_BLOCK

    @pl.when(pl.program_id(1) == 0)
    def _():
        ubuf[0:CONV_HALO, :] = jnp.zeros((CONV_HALO, CONV_CH), _f32)
        state[...] = jnp.zeros_like(state)

    x = x_ref[0]
    hn = _rms(x, nw_ref[...]).astype(_bf16)

    a = _dot(hn, win_ref[:, OFF_A:OFF_A + CONV_CH])
    b = _dot(hn, win_ref[:, OFF_B:OFF_B + CONV_CH])
    ubuf[CONV_HALO:CONV_HALO + L, :] = a * _sigmoid(b)
    first = CONV_HALO - (CONV_WIDTH - 1)
    for row0 in range(0, L, CONV_TILE):
        for c0 in range(0, CONV_CH, LANES):
            acc = None
            for r in range(SUBLANES):
                rows = CONV_TILE + (SUBLANES if r else 0)
                q = None
                for a8 in range((first + CONV_WIDTH - 1) // SUBLANES + 1):
                    k = SUBLANES * a8 + r - first
                    if 0 <= k < CONV_WIDTH:
                        term = cw_ref[k:k + 1, c0:c0 + LANES] * ubuf[pl.ds(row0 + SUBLANES * a8, rows), c0:c0 + LANES]
                        q = term if q is None else q + term
                part = q[r:r + CONV_TILE]
                acc = part if acc is None else acc + part
            ybuf[row0:row0 + CONV_TILE, c0:c0 + LANES] = acc
    for r in range(L // CONV_ROWS):
        acc = ybuf[r * CONV_ROWS:(r + 1) * CONV_ROWS, :] + cb_ref[...]
        mu = jnp.mean(acc, axis=-1, keepdims=True)
        d = acc - mu
        var = jnp.mean(d * d, axis=-1, keepdims=True)
        yn = d * lax.rsqrt(var + EPS) * lnw_ref[...] + lnb_ref[...]
        mbuf[r * CONV_ROWS:(r + 1) * CONV_ROWS, 0:CONV_CH] = (yn * _sigmoid(yn)).astype(_bf16)
    ubuf[0:CONV_HALO, :] = ubuf[L:L + CONV_HALO, :]

    half = RET_QK // 2
    ang0 = pos_ref[0, :, 0:1].astype(_f32) * invf_ref[...]
    cos0, sin0 = jnp.cos(ang0), jnp.sin(ang0)
    cos = cos0 * cost_ref[...] - sin0 * sint_ref[...]
    sin = sin0 * cost_ref[...] + cos0 * sint_ref[...]

    def rotary(t):
        t1, t2 = t[:, :half], t[:, half:]
        return jnp.concatenate([t1 * cos - t2 * sin, t1 * sin + t2 * cos], axis=-1)

    qr = rotary(_dot(hn, win_ref[:, OFF_Q:OFF_Q + RET_QK])) * (RET_DK ** -0.5)
    kr = rotary(_dot(hn, win_ref[:, OFF_K:OFF_K + RET_QK]))
    v = _dot(hn, win_ref[:, OFF_V:OFF_V + RET_V])
    g = _dot(hn, win_ref[:, OFF_G:OFF_G + RET_V])
    kt = kr.T.astype(_bf16)
    vb = v.astype(_bf16)

    yx = _dot(qr.astype(_bf16), state[...].astype(_bf16)) * xi_ref[...]

    q_head = (lax.broadcasted_iota(jnp.int32, (1, RET_QK), 1) % half) // (RET_DK // 2)
    lo = lax.broadcasted_iota(jnp.int32, (1, LANES), 1) < RET_DV
    for p in range(RET_HEADS // 2):
        cols = slice(p * LANES, (p + 1) * LANES)
        vp = vb[:, cols]
        ys = []
        for h in (2 * p, 2 * p + 1):
            qm = jnp.where(q_head == h, qr, 0.0).astype(_bf16)
            s = _dot(qm, kt)
            ys.append(_dot((s * dmat_ref[h]).astype(_bf16), vp))
        y = jnp.where(lo, ys[0], ys[1]) + yx[:, cols]
        yn = _group_norm_halves(y, lo) * gnw_ref[:, cols]
        gp = g[:, cols]
        mbuf[:, CONV_CH + p * LANES:CONV_CH + (p + 1) * LANES] = (gp * _sigmoid(gp) * yn).astype(_bf16)

    kv = _dot(kt, (v * zeta_ref[...]).astype(_bf16))
    row_head = (lax.broadcasted_iota(jnp.int32, (RET_QK, RET_V), 0) % half) // (RET_DK // 2)
    col_head = lax.broadcasted_iota(jnp.int32, (RET_QK, RET_V), 1) // RET_DV
    state[...] = gl_ref[...] * state[...] + jnp.where(row_head == col_head, kv, 0.0)

    o_ref[0] = x + _dot(mbuf[...], wout_ref[...])


def _retention_tables(L):
    h = np.arange(RET_HEADS, dtype=np.float64)
    log_g = np.log1p(-np.exp2(-5.0 - h))
    idx = np.arange(L, dtype=np.float64)
    dist = np.abs(idx[:, None] - idx[None, :])
    visible = (idx[None, :] // CHUNK) <= (idx[:, None] // CHUNK)
    dmat = np.where(visible[None], np.exp(log_g[:, None, None] * dist[None]), 0.0)
    xi = np.exp(log_g[None, :] * (idx[:, None] + 1.0))
    zeta = np.exp(log_g[None, :] * (L - 1.0 - idx[:, None]))
    gl = np.exp(log_g * L)
    rep = lambda t: np.repeat(t, RET_DV, axis=-1)
    inv_freq = ROPE_BASE ** (-np.arange(RET_DK // 2, dtype=np.float32) / np.float32(RET_DK // 2))
    invf = np.tile(inv_freq.astype(np.float32), RET_HEADS)[None, :]
    rel = idx[:, None] * invf.astype(np.float64)
    f = lambda t: jnp.asarray(t, dtype=_f32)
    return f(dmat), f(rep(xi)), f(rep(zeta)), f(rep(gl[None, :])), f(invf), f(np.cos(rel)), f(np.sin(rel))


def _in_proj_columns():
    c = np.arange(RET_QK)
    qk = ((c % (RET_QK // 2)) // (RET_DK // 2)) * RET_DK + (c // (RET_QK // 2)) * (RET_DK // 2) + c % (RET_DK // 2)
    base = np.arange(D_IN)
    base[OFF_Q:OFF_Q + RET_QK] = OFF_Q + qk
    base[OFF_K:OFF_K + RET_QK] = OFF_K + qk
    return base


def _mixer(x, positions, norm_w, w_in, conv_w, conv_b, ln_w, ln_b, gn_w, w_out):
    B, S, D = x.shape
    L = MIX_BLOCK
    dmat, xi, zeta, gl, invf, cost, sint = _retention_tables(L)
    const = lambda shape: pl.BlockSpec(shape, lambda b, j: (0,) * len(shape))
    row = lambda n: pl.BlockSpec((1, n), lambda b, j: (0, 0))
    return pl.pallas_call(
        _mixer_kernel,
        grid=(B, S // L),
        in_specs=[
            pl.BlockSpec((1, L, D), lambda b, j: (b, j, 0)),
            pl.BlockSpec((1, 1, L), lambda b, j: (b * (S // L) + j, 0, 0)),
            row(D),
            const((D, D_IN)),
            const((CONV_WIDTH, CONV_CH)),
            row(CONV_CH), row(CONV_CH), row(CONV_CH), row(RET_V),
            const((D, D)),
            const((RET_HEADS, L, L)),
            const((L, RET_V)), const((L, RET_V)), row(RET_V), row(LANES),
            const((L, LANES)), const((L, LANES)),
        ],
        out_specs=pl.BlockSpec((1, L, D), lambda b, j: (b, j, 0)),
        out_shape=jax.ShapeDtypeStruct((B, S, D), _f32),
        scratch_shapes=[
            pltpu.VMEM((CONV_HALO + L, CONV_CH), _f32),
            pltpu.VMEM((L, CONV_CH), _f32),
            pltpu.VMEM((RET_QK, RET_V), _f32),
            pltpu.VMEM((L, D), _bf16),
        ],
        compiler_params=pltpu.CompilerParams(
            dimension_semantics=("arbitrary", "arbitrary"), vmem_limit_bytes=VMEM_LIMIT),
        name="mixer",
    )(x, positions.reshape(B * S // L, 1, L), norm_w.reshape(1, D), w_in, conv_w, conv_b.reshape(1, -1),
      ln_w.reshape(1, -1), ln_b.reshape(1, -1), gn_w.reshape(1, -1), w_out, dmat, xi, zeta, gl, invf, cost, sint)


def _xattn_kernel(h_ref, nw_ref, wq_ref, kt_ref, v_ref, wo_ref, o_ref, obuf):
    h = h_ref[0]
    hn = _rms(h, nw_ref[...]).astype(_bf16)
    q = (_dot(hn, wq_ref[...]) * (XATTN_HEAD_DIM ** -0.5)).astype(_bf16)
    for i in range(XATTN_HEADS):
        cols = slice(i * XATTN_HEAD_DIM, (i + 1) * XATTN_HEAD_DIM)
        s = _dot(q[:, cols], kt_ref[0, cols, :])
        e = jnp.exp(s - jnp.max(s, axis=-1, keepdims=True))
        o = _dot(e.astype(_bf16), v_ref[0, :, cols])
        obuf[:, cols] = (o * (1.0 / jnp.sum(e, axis=-1, keepdims=True))).astype(_bf16)
    o_ref[0] = h + _dot(obuf[...], wo_ref[...])


def _xattn(h, norm_w, xq_w, kt, v, xo_w):
    B, S, D = h.shape
    T = TOK_BLOCK
    return pl.pallas_call(
        _xattn_kernel,
        grid=(B, S // T),
        in_specs=[
            pl.BlockSpec((1, T, D), lambda b, j: (b, j, 0)),
            pl.BlockSpec((1, D), lambda b, j: (0, 0)),
            pl.BlockSpec((D, D), lambda b, j: (0, 0)),
            pl.BlockSpec((1, D, N_MEM), lambda b, j: (b, 0, 0)),
            pl.BlockSpec((1, N_MEM, D), lambda b, j: (b, 0, 0)),
            pl.BlockSpec((D, D), lambda b, j: (0, 0)),
        ],
        out_specs=pl.BlockSpec((1, T, D), lambda b, j: (b, j, 0)),
        out_shape=jax.ShapeDtypeStruct((B, S, D), _f32),
        scratch_shapes=[pltpu.VMEM((T, D), _bf16)],
        compiler_params=pltpu.CompilerParams(
            dimension_semantics=("arbitrary", "arbitrary"), vmem_limit_bytes=VMEM_LIMIT),
        name="xattn",
    )(h, norm_w.reshape(1, D), xq_w, kt, v, xo_w)


def _mlp_kernel(h_ref, nw_ref, wu_ref, wd_ref, fw_ref, o_ref):
    h = h_ref[...]
    hn = _rms(h, nw_ref[...]).astype(_bf16)
    acc = h
    for c in range(D_FF // FF_CHUNK):
        cols = slice(c * FF_CHUNK, (c + 1) * FF_CHUNK)
        u = jnp.maximum(_dot(hn, wu_ref[:, cols]), 0.0)
        acc = acc + _dot((u * u).astype(_bf16), wd_ref[cols, :])
    o_ref[...] = _rms(acc, fw_ref[...])


def _mlp(h, norm_w, up_w, down_w, norm_f_w):
    B, S, D = h.shape
    T = TOK_BLOCK
    h2 = h.reshape(B * S, D)
    out = pl.pallas_call(
        _mlp_kernel,
        grid=(B * S // T,),
        in_specs=[
            pl.BlockSpec((T, D), lambda i: (i, 0)),
            pl.BlockSpec((1, D), lambda i: (0, 0)),
            pl.BlockSpec((D, D_FF), lambda i: (0, 0)),
            pl.BlockSpec((D_FF, D), lambda i: (0, 0)),
            pl.BlockSpec((1, D), lambda i: (0, 0)),
        ],
        out_specs=pl.BlockSpec((T, D), lambda i: (i, 0)),
        out_shape=jax.ShapeDtypeStruct((B * S, D), _f32),
        compiler_params=pltpu.CompilerParams(
            dimension_semantics=("arbitrary",), vmem_limit_bytes=VMEM_LIMIT),
        name="mlp",
    )(h2, norm_w.reshape(1, D), up_w, down_w, norm_f_w.reshape(1, D))
    return out.reshape(B, S, D)


def kernel(x, mem, positions, norm_mix_w, w_in, conv_w, conv_b, conv_ln_w, conv_ln_b, ret_gn_w, w_out,
           norm_xattn_w, norm_mem_w, xq_w, xkv_w, xo_w, norm_mlp_w, mlp_up_w, mlp_down_w, norm_f_w):
    bf = lambda w: w.astype(_bf16)
    w_in_b = bf(w_in[:, _in_proj_columns()])
    kt, v = _mem_kv(mem, norm_mem_w, bf(xkv_w))
    h = _mixer(x, positions, norm_mix_w, w_in_b, conv_w, conv_b, conv_ln_w, conv_ln_b, ret_gn_w, bf(w_out))
    h = _xattn(h, norm_xattn_w, bf(xq_w), kt, v, bf(xo_w))
    return _mlp(h, norm_mlp_w, bf(mlp_up_w), bf(mlp_down_w), norm_f_w)
```

```python
import numpy as np
import jax
import jax.numpy as jnp
from jax import lax
from jax.experimental import pallas as pl
from jax.experimental.pallas import tpu as pltpu

D_MODEL = 1024
CHUNK = 64
CONV_CH = 512
CONV_WIDTH = 31
RET_HEADS = 8
RET_DV = 64
RET_DK = 32
RET_QK = RET_HEADS * RET_DK
RET_V = RET_HEADS * RET_DV
N_MEM = 256
XATTN_HEADS = 4
XATTN_HEAD_DIM = 256
D_FF = 4096
ROPE_BASE = 10000.0
EPS = 1e-6

OFF_A, OFF_B, OFF_Q, OFF_K, OFF_V, OFF_G = 0, 512, 1024, 1280, 1536, 2048
D_IN = 2560

LANES = 128
SUBLANES = 8
MXU_TILE = 256
MIX_BLOCK = 256
CONV_HALO = 32
CONV_TILE = 64
CONV_RGROUP = 4
CONV_ROWS = 32
TOK_BLOCK = 512
FF_CHUNK = 1024
VMEM_LIMIT = 56 * 1024 * 1024

_f32 = jnp.float32
_bf16 = jnp.bfloat16


def _dot(a, b):
    return jnp.dot(a, b, preferred_element_type=_f32)


def _rms(x, w):
    return x * lax.rsqrt(jnp.mean(x * x, axis=-1, keepdims=True) + EPS) * w


def _sigmoid(x):
    return 1.0 / (1.0 + jnp.exp(-x))


def _cast_columns(src_ref, dst_ref):
    for c in range(0, src_ref.shape[-1], MXU_TILE):
        dst_ref[:, c:c + MXU_TILE] = src_ref[:, c:c + MXU_TILE].astype(_bf16)


def _mem_kv_kernel(mem_ref, nw_ref, wkv_ref, kt_ref, v_ref):
    m = _rms(mem_ref[0], nw_ref[...]).astype(_bf16)
    k = _dot(m, wkv_ref[:, :D_MODEL].astype(_bf16))
    kt_ref[0] = k.T.astype(_bf16)
    v_ref[0] = _dot(m, wkv_ref[:, D_MODEL:].astype(_bf16)).astype(_bf16)


def _mem_kv(mem, norm_mem_w, xkv_w):
    B = mem.shape[0]
    return pl.pallas_call(
        _mem_kv_kernel,
        grid=(B,),
        in_specs=[
            pl.BlockSpec((1, N_MEM, D_MODEL), lambda b: (b, 0, 0)),
            pl.BlockSpec((1, D_MODEL), lambda b: (0, 0)),
            pl.BlockSpec((D_MODEL, 2 * D_MODEL), lambda b: (0, 0)),
        ],
        out_specs=[
            pl.BlockSpec((1, D_MODEL, N_MEM), lambda b: (b, 0, 0)),
            pl.BlockSpec((1, N_MEM, D_MODEL), lambda b: (b, 0, 0)),
        ],
        out_shape=[
            jax.ShapeDtypeStruct((B, D_MODEL, N_MEM), _bf16),
            jax.ShapeDtypeStruct((B, N_MEM, D_MODEL), _bf16),
        ],
        compiler_params=pltpu.CompilerParams(
            dimension_semantics=("arbitrary",), vmem_limit_bytes=VMEM_LIMIT),
        name="mem_kv",
    )(mem, norm_mem_w.reshape(1, D_MODEL), xkv_w)


def _group_norm_halves(y, lo):
    inv = 1.0 / RET_DV
    s_lo = jnp.sum(jnp.where(lo, y, 0.0), axis=-1, keepdims=True)
    s_hi = jnp.sum(jnp.where(lo, 0.0, y), axis=-1, keepdims=True)
    d = y - jnp.where(lo, s_lo, s_hi) * inv
    d2 = d * d
    v_lo = jnp.sum(jnp.where(lo, d2, 0.0), axis=-1, keepdims=True)
    v_hi = jnp.sum(jnp.where(lo, 0.0, d2), axis=-1, keepdims=True)
    return d * lax.rsqrt(jnp.where(lo, v_lo, v_hi) * inv + EPS)


N_CAST = 4


def _mixer_kernel(*refs):
    (x_ref, pos_ref, nw_ref, win32_ref, cw_ref, cb_ref, lnw_ref, lnb_ref, gnw_ref, wout32_ref,
     dmat_ref, xi_ref, zeta_ref, gl_ref, invf_ref, cost_ref, sint_ref) = refs[:17]
    cast_src = refs[17:17 + N_CAST]
    o_ref = refs[17 + N_CAST]
    cast_dst = refs[18 + N_CAST:18 + 2 * N_CAST]
    win_ref, wout_ref, ubuf, ybuf, state, mbuf = refs[18 + 2 * N_CAST:]
    L = MIX_BLOCK

    @pl.when((pl.program_id(0) == 0) & (pl.program_id(1) == 0))
    def _():
        _cast_columns(win32_ref, win_ref)
        _cast_columns(wout32_ref, wout_ref)

    @pl.when(pl.program_id(1) == 0)
    def _():
        ubuf[0:CONV_HALO, :] = jnp.zeros((CONV_HALO, CONV_CH), _f32)
        state[...] = jnp.zeros_like(state)

    for src, dst in zip(cast_src, cast_dst):
        dst[...] = src[...].astype(_bf16)

    x = x_ref[0]
    hn = _rms(x, nw_ref[...]).astype(_bf16)

    a = _dot(hn, win_ref[:, OFF_A:OFF_A + CONV_CH])
    b = _dot(hn, win_ref[:, OFF_B:OFF_B + CONV_CH])
    ubuf[CONV_HALO:CONV_HALO + L, :] = a * _sigmoid(b)
    first = CONV_HALO - (CONV_WIDTH - 1)
    n_win = (first + CONV_WIDTH - 1) // SUBLANES + 1
    for c0 in range(0, CONV_CH, LANES):
        for row0 in range(0, L, CONV_TILE):
            acc = None
            for r0 in range(0, SUBLANES, CONV_RGROUP):
                q = {}
                for a8 in range(n_win):
                    taps = [(r, SUBLANES * a8 + r - first) for r in range(r0, r0 + CONV_RGROUP)
                            if 0 <= SUBLANES * a8 + r - first < CONV_WIDTH]
                    if not taps:
                        continue
                    rows = CONV_TILE + (SUBLANES if max(r for r, _ in taps) else 0)
                    win = ubuf[pl.ds(row0 + SUBLANES * a8, rows), c0:c0 + LANES]
                    for r, k in taps:
                        term = cw_ref[k:k + 1, c0:c0 + LANES] * (win if r else win[:CONV_TILE])
                        q[r] = term if r not in q else q[r] + term
                for r in sorted(q):
                    part = q[r][r:r + CONV_TILE]
                    acc = part if acc is None else acc + part
            ybuf[row0:row0 + CONV_TILE, c0:c0 + LANES] = acc
    for r in range(L // CONV_ROWS):
        acc = ybuf[r * CONV_ROWS:(r + 1) * CONV_ROWS, :] + cb_ref[...]
        mu = jnp.mean(acc, axis=-1, keepdims=True)
        d = acc - mu
        var = jnp.mean(d * d, axis=-1, keepdims=True)
        yn = d * lax.rsqrt(var + EPS) * lnw_ref[...] + lnb_ref[...]
        mbuf[r * CONV_ROWS:(r + 1) * CONV_ROWS, 0:CONV_CH] = (yn * _sigmoid(yn)).astype(_bf16)
    ubuf[0:CONV_HALO, :] = ubuf[L:L + CONV_HALO, :]

    ang0 = pos_ref[0, :, 0:1].astype(_f32) * invf_ref[...]
    cos0, sin0 = jnp.cos(ang0), jnp.sin(ang0)
    cos = cos0 * cost_ref[...] - sin0 * sint_ref[...]
    sin = sin0 * cost_ref[...] + cos0 * sint_ref[...]
    lane = lax.broadcasted_iota(jnp.int32, (1, RET_QK), 1)
    first_half = lane % RET_DK < RET_DK // 2
    cos2 = jnp.concatenate([cos, cos], axis=-1)
    sin2 = jnp.where(first_half, -1.0, 1.0) * jnp.concatenate([sin, sin], axis=-1)

    def rotary(t):
        partner = jnp.where(first_half, pltpu.roll(t, RET_QK - RET_DK // 2, 1), pltpu.roll(t, RET_DK // 2, 1))
        return t * cos2 + partner * sin2

    qr = rotary(_dot(hn, win_ref[:, OFF_Q:OFF_Q + RET_QK])) * (RET_DK ** -0.5)
    kr = rotary(_dot(hn, win_ref[:, OFF_K:OFF_K + RET_QK]))
    v = _dot(hn, win_ref[:, OFF_V:OFF_V + RET_V])
    g = _dot(hn, win_ref[:, OFF_G:OFF_G + RET_V])
    kt = kr.T.astype(_bf16)
    vb = v.astype(_bf16)

    yx = _dot(qr.astype(_bf16), state[...].astype(_bf16)) * xi_ref[...]

    q_head = lane // RET_DK
    lo = lax.broadcasted_iota(jnp.int32, (1, LANES), 1) < RET_DV
    for p in range(RET_HEADS // 2):
        cols = slice(p * LANES, (p + 1) * LANES)
        vp = vb[:, cols]
        ys = []
        for h in (2 * p, 2 * p + 1):
            qm = jnp.where(q_head == h, qr, 0.0).astype(_bf16)
            s = _dot(qm, kt)
            ys.append(_dot((s * dmat_ref[h]).astype(_bf16), vp))
        y = jnp.where(lo, ys[0], ys[1]) + yx[:, cols]
        yn = _group_norm_halves(y, lo) * gnw_ref[:, cols]
        gp = g[:, cols]
        mbuf[:, CONV_CH + p * LANES:CONV_CH + (p + 1) * LANES] = (gp * _sigmoid(gp) * yn).astype(_bf16)

    kv = _dot(kt, (v * zeta_ref[...]).astype(_bf16))
    row_head = lax.broadcasted_iota(jnp.int32, (RET_QK, RET_V), 0) // RET_DK
    col_head = lax.broadcasted_iota(jnp.int32, (RET_QK, RET_V), 1) // RET_DV
    state[...] = gl_ref[...] * state[...] + jnp.where(row_head == col_head, kv, 0.0)

    o_ref[0] = x + _dot(mbuf[...], wout_ref[...])


def _retention_tables(L):
    h = np.arange(RET_HEADS, dtype=np.float64)
    log_g = np.log1p(-np.exp2(-5.0 - h))
    idx = np.arange(L, dtype=np.float64)
    dist = np.abs(idx[:, None] - idx[None, :])
    visible = (idx[None, :] // CHUNK) <= (idx[:, None] // CHUNK)
    dmat = np.where(visible[None], np.exp(log_g[:, None, None] * dist[None]), 0.0)
    xi = np.exp(log_g[None, :] * (idx[:, None] + 1.0))
    zeta = np.exp(log_g[None, :] * (L - 1.0 - idx[:, None]))
    gl = np.exp(log_g * L)
    rep = lambda t: np.repeat(t, RET_DV, axis=-1)
    inv_freq = ROPE_BASE ** (-np.arange(RET_DK // 2, dtype=np.float32) / np.float32(RET_DK // 2))
    invf = np.tile(inv_freq.astype(np.float32), LANES // (RET_DK // 2))[None, :]
    rel = idx[:, None] * invf.astype(np.float64)
    f = lambda t: jnp.asarray(t, dtype=_f32)
    return f(dmat), f(rep(xi)), f(rep(zeta)), f(rep(gl[None, :])), f(invf), f(np.cos(rel)), f(np.sin(rel))


def _mixer(x, positions, norm_w, w_in, conv_w, conv_b, ln_w, ln_b, gn_w, w_out, later_weights):
    B, S, D = x.shape
    L = MIX_BLOCK
    n = S // L
    steps = B * n
    dmat, xi, zeta, gl, invf, cost, sint = _retention_tables(L)
    const = lambda shape: pl.BlockSpec(shape, lambda b, j: (0,) * len(shape))
    once = lambda shape: pl.BlockSpec(shape, lambda b, j: (0,) * len(shape), pipeline_mode=pl.Buffered(1))
    row = lambda n: pl.BlockSpec((1, n), lambda b, j: (0, 0))
    slabs = [w.reshape(steps, w.shape[0] // steps, w.shape[1]) for w in later_weights]
    slab_spec = lambda w: pl.BlockSpec((1,) + w.shape[1:], lambda b, j: (b * n + j, 0, 0))
    outs = pl.pallas_call(
        _mixer_kernel,
        grid=(B, n),
        in_specs=[
            pl.BlockSpec((1, L, D), lambda b, j: (b, j, 0)),
            pl.BlockSpec((1, 1, L), lambda b, j: (b * n + j, 0, 0)),
            row(D),
            once((D, D_IN)),
            const((CONV_WIDTH, CONV_CH)),
            row(CONV_CH), row(CONV_CH), row(CONV_CH), row(RET_V),
            once((D, D)),
            const((RET_HEADS, L, L)),
            const((L, RET_V)), const((L, RET_V)), row(RET_V), row(LANES),
            const((L, LANES)), const((L, LANES)),
        ] + [slab_spec(w) for w in slabs],
        out_specs=[pl.BlockSpec((1, L, D), lambda b, j: (b, j, 0))] + [slab_spec(w) for w in slabs],
        out_shape=[jax.ShapeDtypeStruct((B, S, D), _f32)] + [jax.ShapeDtypeStruct(w.shape, _bf16) for w in slabs],
        scratch_shapes=[
            pltpu.VMEM((D, D_IN), _bf16),
            pltpu.VMEM((D, D), _bf16),
            pltpu.VMEM((CONV_HALO + L, CONV_CH), _f32),
            pltpu.VMEM((L, CONV_CH), _f32),
            pltpu.VMEM((RET_QK, RET_V), _f32),
            pltpu.VMEM((L, D), _bf16),
        ],
        compiler_params=pltpu.CompilerParams(
            dimension_semantics=("arbitrary", "arbitrary"), vmem_limit_bytes=VMEM_LIMIT),
        name="mixer",
    )(x, positions.reshape(steps, 1, L), norm_w.reshape(1, D), w_in, conv_w, conv_b.reshape(1, -1),
      ln_w.reshape(1, -1), ln_b.reshape(1, -1), gn_w.reshape(1, -1), w_out, dmat, xi, zeta, gl, invf, cost, sint,
      *slabs)
    return outs[0], [o.reshape(w.shape) for o, w in zip(outs[1:], later_weights)]


def _xattn_kernel(h_ref, nw_ref, wq_ref, kt_ref, v_ref, wo_ref, o_ref, obuf):
    h = h_ref[0]
    hn = _rms(h, nw_ref[...]).astype(_bf16)
    q = (_dot(hn, wq_ref[...]) * (XATTN_HEAD_DIM ** -0.5)).astype(_bf16)
    for i in range(XATTN_HEADS):
        cols = slice(i * XATTN_HEAD_DIM, (i + 1) * XATTN_HEAD_DIM)
        s = _dot(q[:, cols], kt_ref[0, cols, :])
        e = jnp.exp(s - jnp.max(s, axis=-1, keepdims=True))
        o = _dot(e.astype(_bf16), v_ref[0, :, cols])
        obuf[:, cols] = (o * (1.0 / jnp.sum(e, axis=-1, keepdims=True))).astype(_bf16)
    o_ref[0] = h + _dot(obuf[...], wo_ref[...])


def _xattn(h, norm_w, xq_w, kt, v, xo_w):
    B, S, D = h.shape
    T = TOK_BLOCK
    return pl.pallas_call(
        _xattn_kernel,
        grid=(B, S // T),
        in_specs=[
            pl.BlockSpec((1, T, D), lambda b, j: (b, j, 0)),
            pl.BlockSpec((1, D), lambda b, j: (0, 0)),
            pl.BlockSpec((D, D), lambda b, j: (0, 0)),
            pl.BlockSpec((1, D, N_MEM), lambda b, j: (b, 0, 0)),
            pl.BlockSpec((1, N_MEM, D), lambda b, j: (b, 0, 0)),
            pl.BlockSpec((D, D), lambda b, j: (0, 0)),
        ],
        out_specs=pl.BlockSpec((1, T, D), lambda b, j: (b, j, 0)),
        out_shape=jax.ShapeDtypeStruct((B, S, D), _f32),
        scratch_shapes=[pltpu.VMEM((T, D), _bf16)],
        compiler_params=pltpu.CompilerParams(
            dimension_semantics=("arbitrary", "arbitrary"), vmem_limit_bytes=VMEM_LIMIT),
        name="xattn",
    )(h, norm_w.reshape(1, D), xq_w, kt, v, xo_w)


def _mlp_kernel(h_ref, nw_ref, wu_ref, wd_ref, fw_ref, o_ref, abuf):
    h = h_ref[...]
    hn = _rms(h, nw_ref[...]).astype(_bf16)
    for c in range(0, D_FF, FF_CHUNK):
        u = jnp.maximum(_dot(hn, wu_ref[:, c:c + FF_CHUNK]), 0.0)
        abuf[:, c:c + FF_CHUNK] = (u * u).astype(_bf16)
    o_ref[...] = _rms(h + _dot(abuf[...], wd_ref[...]), fw_ref[...])


def _mlp(h, norm_w, up_w, down_w, norm_f_w):
    B, S, D = h.shape
    T = TOK_BLOCK
    h2 = h.reshape(B * S, D)
    out = pl.pallas_call(
        _mlp_kernel,
        grid=(B * S // T,),
        in_specs=[
            pl.BlockSpec((T, D), lambda i: (i, 0)),
            pl.BlockSpec((1, D), lambda i: (0, 0)),
            pl.BlockSpec((D, D_FF), lambda i: (0, 0)),
            pl.BlockSpec((D_FF, D), lambda i: (0, 0)),
            pl.BlockSpec((1, D), lambda i: (0, 0)),
        ],
        out_specs=pl.BlockSpec((T, D), lambda i: (i, 0)),
        out_shape=jax.ShapeDtypeStruct((B * S, D), _f32),
        scratch_shapes=[pltpu.VMEM((T, D_FF), _bf16)],
        compiler_params=pltpu.CompilerParams(
            dimension_semantics=("arbitrary",), vmem_limit_bytes=VMEM_LIMIT),
        name="mlp",
    )(h2, norm_w.reshape(1, D), up_w, down_w, norm_f_w.reshape(1, D))
    return out.reshape(B, S, D)


def kernel(x, mem, positions, norm_mix_w, w_in, conv_w, conv_b, conv_ln_w, conv_ln_b, ret_gn_w, w_out,
           norm_xattn_w, norm_mem_w, xq_w, xkv_w, xo_w, norm_mlp_w, mlp_up_w, mlp_down_w, norm_f_w):
    kt, v = _mem_kv(mem, norm_mem_w, xkv_w)
    h, (xq_b, xo_b, up_b, down_b) = _mixer(x, positions, norm_mix_w, w_in, conv_w, conv_b, conv_ln_w, conv_ln_b,
                                           ret_gn_w, w_out, (xq_w, xo_w, mlp_up_w, mlp_down_w))
    h = _xattn(h, norm_xattn_w, xq_b, kt, v, xo_b)
    return _mlp(h, norm_mlp_w, up_b, down_b, norm_f_w)
```

```python
import numpy as np
import jax
import jax.numpy as jnp
from jax import lax
from jax.experimental import pallas as pl
from jax.experimental.pallas import tpu as pltpu

D_MODEL = 1024
CHUNK = 64
CONV_CH = 512
CONV_WIDTH = 31
RET_HEADS = 8
RET_DV = 64
RET_DK = 32
RET_QK = RET_HEADS * RET_DK
RET_V = RET_HEADS * RET_DV
N_MEM = 256
XATTN_HEADS = 4
XATTN_HEAD_DIM = 256
D_FF = 4096
ROPE_BASE = 10000.0
EPS = 1e-6

OFF_A, OFF_B, OFF_Q, OFF_K, OFF_V, OFF_G = 0, 512, 1024, 1280, 1536, 2048
D_IN = 2560

LANES = 128
SUBLANES = 8
MXU_TILE = 256
MIX_BLOCK = 256
CONV_HALO = 32
CONV_SEG = 128
DFT_N = CONV_HALO + CONV_SEG
N_FREQ = DFT_N // 2 + 1
FREQ_PAD = -(-N_FREQ // SUBLANES) * SUBLANES
CONV_ROWS = 32
TOK_BLOCK = 512
FF_CHUNK = 1024
VMEM_LIMIT = 56 * 1024 * 1024

_f32 = jnp.float32
_bf16 = jnp.bfloat16


def _dot(a, b):
    return jnp.dot(a, b, preferred_element_type=_f32)


def _rms(x, w):
    return x * lax.rsqrt(jnp.mean(x * x, axis=-1, keepdims=True) + EPS) * w


def _sigmoid(x):
    return 1.0 / (1.0 + jnp.exp(-x))


def _cast_columns(src_ref, dst_ref):
    for c in range(0, src_ref.shape[-1], MXU_TILE):
        dst_ref[:, c:c + MXU_TILE] = src_ref[:, c:c + MXU_TILE].astype(_bf16)


def _mem_kv_kernel(mem_ref, nw_ref, wkv_ref, kt_ref, v_ref):
    m = _rms(mem_ref[0], nw_ref[...]).astype(_bf16)
    k = _dot(m, wkv_ref[:, :D_MODEL].astype(_bf16))
    kt_ref[0] = k.T.astype(_bf16)
    v_ref[0] = _dot(m, wkv_ref[:, D_MODEL:].astype(_bf16)).astype(_bf16)


def _mem_kv(mem, norm_mem_w, xkv_w):
    B = mem.shape[0]
    return pl.pallas_call(
        _mem_kv_kernel,
        grid=(B,),
        in_specs=[
            pl.BlockSpec((1, N_MEM, D_MODEL), lambda b: (b, 0, 0)),
            pl.BlockSpec((1, D_MODEL), lambda b: (0, 0)),
            pl.BlockSpec((D_MODEL, 2 * D_MODEL), lambda b: (0, 0)),
        ],
        out_specs=[
            pl.BlockSpec((1, D_MODEL, N_MEM), lambda b: (b, 0, 0)),
            pl.BlockSpec((1, N_MEM, D_MODEL), lambda b: (b, 0, 0)),
        ],
        out_shape=[
            jax.ShapeDtypeStruct((B, D_MODEL, N_MEM), _bf16),
            jax.ShapeDtypeStruct((B, N_MEM, D_MODEL), _bf16),
        ],
        compiler_params=pltpu.CompilerParams(
            dimension_semantics=("arbitrary",), vmem_limit_bytes=VMEM_LIMIT),
        name="mem_kv",
    )(mem, norm_mem_w.reshape(1, D_MODEL), xkv_w)


def _group_norm_halves(y, lo):
    inv = 1.0 / RET_DV
    s_lo = jnp.sum(jnp.where(lo, y, 0.0), axis=-1, keepdims=True)
    s_hi = jnp.sum(jnp.where(lo, 0.0, y), axis=-1, keepdims=True)
    d = y - jnp.where(lo, s_lo, s_hi) * inv
    d2 = d * d
    v_lo = jnp.sum(jnp.where(lo, d2, 0.0), axis=-1, keepdims=True)
    v_hi = jnp.sum(jnp.where(lo, 0.0, d2), axis=-1, keepdims=True)
    return d * lax.rsqrt(jnp.where(lo, v_lo, v_hi) * inv + EPS)


N_CAST = 4


def _mixer_kernel(*refs):
    n_in = 20
    (x_ref, pos_ref, nw_ref, win32_ref, cw_ref, cb_ref, lnw_ref, lnb_ref, gnw_ref, wout32_ref,
     dmat_ref, xi_ref, zeta_ref, gl_ref, invf_ref, cost_ref, sint_ref, fwd32_ref, inv32_ref, tap_ref) = refs[:n_in]
    cast_src = refs[n_in:n_in + N_CAST]
    o_ref = refs[n_in + N_CAST]
    cast_dst = refs[n_in + 1 + N_CAST:n_in + 1 + 2 * N_CAST]
    win_ref, wout_ref, fwd_ref, inv_ref, hspec, ubuf, state, mbuf = refs[n_in + 1 + 2 * N_CAST:]
    L = MIX_BLOCK

    @pl.when((pl.program_id(0) == 0) & (pl.program_id(1) == 0))
    def _():
        _cast_columns(win32_ref, win_ref)
        _cast_columns(wout32_ref, wout_ref)
        fwd_ref[...] = fwd32_ref[...].astype(_bf16)
        inv_ref[...] = inv32_ref[...].astype(_bf16)
        hspec[...] = jnp.dot(tap_ref[...], cw_ref[...], preferred_element_type=_f32, precision=lax.Precision.HIGHEST)

    @pl.when(pl.program_id(1) == 0)
    def _():
        ubuf[0:CONV_HALO, :] = jnp.zeros((CONV_HALO, CONV_CH), _bf16)
        state[...] = jnp.zeros_like(state)

    for src, dst in zip(cast_src, cast_dst):
        dst[...] = src[...].astype(_bf16)

    x = x_ref[0]
    hn = _rms(x, nw_ref[...]).astype(_bf16)

    a = _dot(hn, win_ref[:, OFF_A:OFF_A + CONV_CH])
    b = _dot(hn, win_ref[:, OFF_B:OFF_B + CONV_CH])
    ubuf[CONV_HALO:CONV_HALO + L, :] = (a * _sigmoid(b)).astype(_bf16)
    h_re, h_im = hspec[0:FREQ_PAD, :], hspec[FREQ_PAD:2 * FREQ_PAD, :]
    for row0 in range(0, L, CONV_SEG):
        spec = _dot(fwd_ref[...], ubuf[row0:row0 + DFT_N, :])
        s_re, s_im = spec[0:FREQ_PAD, :], spec[FREQ_PAD:2 * FREQ_PAD, :]
        prod = jnp.concatenate([s_re * h_re - s_im * h_im, s_re * h_im + s_im * h_re], axis=0)
        y = _dot(inv_ref[...], prod.astype(_bf16))
        for r in range(0, CONV_SEG, CONV_ROWS):
            acc = y[r:r + CONV_ROWS, :] + cb_ref[...]
            mu = jnp.mean(acc, axis=-1, keepdims=True)
            d = acc - mu
            var = jnp.mean(d * d, axis=-1, keepdims=True)
            yn = d * lax.rsqrt(var + EPS) * lnw_ref[...] + lnb_ref[...]
            mbuf[row0 + r:row0 + r + CONV_ROWS, 0:CONV_CH] = (yn * _sigmoid(yn)).astype(_bf16)
    ubuf[0:CONV_HALO, :] = ubuf[L:L + CONV_HALO, :]

    ang0 = pos_ref[0, :, 0:1].astype(_f32) * invf_ref[...]
    cos0, sin0 = jnp.cos(ang0), jnp.sin(ang0)
    cos = cos0 * cost_ref[...] - sin0 * sint_ref[...]
    sin = sin0 * cost_ref[...] + cos0 * sint_ref[...]
    lane = lax.broadcasted_iota(jnp.int32, (1, RET_QK), 1)
    first_half = lane % RET_DK < RET_DK // 2
    cos2 = jnp.concatenate([cos, cos], axis=-1)
    sin2 = jnp.where(first_half, -1.0, 1.0) * jnp.concatenate([sin, sin], axis=-1)

    def rotary(t):
        partner = jnp.where(first_half, pltpu.roll(t, RET_QK - RET_DK // 2, 1), pltpu.roll(t, RET_DK // 2, 1))
        return t * cos2 + partner * sin2

    qr = rotary(_dot(hn, win_ref[:, OFF_Q:OFF_Q + RET_QK])) * (RET_DK ** -0.5)
    kr = rotary(_dot(hn, win_ref[:, OFF_K:OFF_K + RET_QK]))
    v = _dot(hn, win_ref[:, OFF_V:OFF_V + RET_V])
    g = _dot(hn, win_ref[:, OFF_G:OFF_G + RET_V])
    kt = kr.T.astype(_bf16)
    vb = v.astype(_bf16)

    yx = _dot(qr.astype(_bf16), state[...].astype(_bf16)) * xi_ref[...]

    q_head = lane // RET_DK
    lo = lax.broadcasted_iota(jnp.int32, (1, LANES), 1) < RET_DV
    for p in range(RET_HEADS // 2):
        cols = slice(p * LANES, (p + 1) * LANES)
        vp = vb[:, cols]
        ys = []
        for h in (2 * p, 2 * p + 1):
            qm = jnp.where(q_head == h, qr, 0.0).astype(_bf16)
            s = _dot(qm, kt)
            ys.append(_dot((s * dmat_ref[h]).astype(_bf16), vp))
        y = jnp.where(lo, ys[0], ys[1]) + yx[:, cols]
        yn = _group_norm_halves(y, lo) * gnw_ref[:, cols]
        gp = g[:, cols]
        mbuf[:, CONV_CH + p * LANES:CONV_CH + (p + 1) * LANES] = (gp * _sigmoid(gp) * yn).astype(_bf16)

    kv = _dot(kt, (v * zeta_ref[...]).astype(_bf16))
    row_head = lax.broadcasted_iota(jnp.int32, (RET_QK, RET_V), 0) // RET_DK
    col_head = lax.broadcasted_iota(jnp.int32, (RET_QK, RET_V), 1) // RET_DV
    state[...] = gl_ref[...] * state[...] + jnp.where(row_head == col_head, kv, 0.0)

    o_ref[0] = x + _dot(mbuf[...], wout_ref[...])


def _retention_tables(L):
    h = np.arange(RET_HEADS, dtype=np.float64)
    log_g = np.log1p(-np.exp2(-5.0 - h))
    idx = np.arange(L, dtype=np.float64)
    dist = np.abs(idx[:, None] - idx[None, :])
    visible = (idx[None, :] // CHUNK) <= (idx[:, None] // CHUNK)
    dmat = np.where(visible[None], np.exp(log_g[:, None, None] * dist[None]), 0.0)
    xi = np.exp(log_g[None, :] * (idx[:, None] + 1.0))
    zeta = np.exp(log_g[None, :] * (L - 1.0 - idx[:, None]))
    gl = np.exp(log_g * L)
    rep = lambda t: np.repeat(t, RET_DV, axis=-1)
    inv_freq = ROPE_BASE ** (-np.arange(RET_DK // 2, dtype=np.float32) / np.float32(RET_DK // 2))
    invf = np.tile(inv_freq.astype(np.float32), LANES // (RET_DK // 2))[None, :]
    rel = idx[:, None] * invf.astype(np.float64)
    f = lambda t: jnp.asarray(t, dtype=_f32)
    return f(dmat), f(rep(xi)), f(rep(zeta)), f(rep(gl[None, :])), f(invf), f(np.cos(rel)), f(np.sin(rel))


def _conv_dft_tables():
    n = np.arange(DFT_N, dtype=np.float64)
    f = np.arange(N_FREQ, dtype=np.float64)[:, None]
    w = 2.0 * np.pi / DFT_N
    fwd = np.zeros((2 * FREQ_PAD, DFT_N))
    fwd[:N_FREQ] = np.cos(w * f * n[None, :])
    fwd[FREQ_PAD:FREQ_PAD + N_FREQ] = -np.sin(w * f * n[None, :])
    delay = (CONV_WIDTH - 1) - np.arange(CONV_WIDTH, dtype=np.float64)[None, :]
    tap = np.zeros((2 * FREQ_PAD, CONV_HALO))
    tap[:N_FREQ, :CONV_WIDTH] = np.cos(w * f * delay)
    tap[FREQ_PAD:FREQ_PAD + N_FREQ, :CONV_WIDTH] = -np.sin(w * f * delay)
    weight = np.full((1, N_FREQ), 2.0)
    weight[0, 0] = weight[0, -1] = 1.0
    out = n[CONV_HALO:, None]
    inv = np.zeros((CONV_SEG, 2 * FREQ_PAD))
    inv[:, :N_FREQ] = weight * np.cos(w * out * f.T) / DFT_N
    inv[:, FREQ_PAD:FREQ_PAD + N_FREQ] = -weight * np.sin(w * out * f.T) / DFT_N
    return [jnp.asarray(t, dtype=_f32) for t in (fwd, inv, tap)]


def _mixer(x, positions, norm_w, w_in, conv_w, conv_b, ln_w, ln_b, gn_w, w_out, later_weights):
    B, S, D = x.shape
    L = MIX_BLOCK
    n = S // L
    steps = B * n
    dmat, xi, zeta, gl, invf, cost, sint = _retention_tables(L)
    fwd, inv, tap = _conv_dft_tables()
    taps = jnp.pad(conv_w, ((0, CONV_HALO - CONV_WIDTH), (0, 0)))
    const = lambda shape: pl.BlockSpec(shape, lambda b, j: (0,) * len(shape))
    once = lambda shape: pl.BlockSpec(shape, lambda b, j: (0,) * len(shape), pipeline_mode=pl.Buffered(1))
    row = lambda n: pl.BlockSpec((1, n), lambda b, j: (0, 0))
    slabs = [w.reshape(steps, w.shape[0] // steps, w.shape[1]) for w in later_weights]
    slab_spec = lambda w: pl.BlockSpec((1,) + w.shape[1:], lambda b, j: (b * n + j, 0, 0))
    outs = pl.pallas_call(
        _mixer_kernel,
        grid=(B, n),
        in_specs=[
            pl.BlockSpec((1, L, D), lambda b, j: (b, j, 0)),
            pl.BlockSpec((1, 1, L), lambda b, j: (b * n + j, 0, 0)),
            row(D),
            once((D, D_IN)),
            const((CONV_HALO, CONV_CH)),
            row(CONV_CH), row(CONV_CH), row(CONV_CH), row(RET_V),
            once((D, D)),
            const((RET_HEADS, L, L)),
            const((L, RET_V)), const((L, RET_V)), row(RET_V), row(LANES),
            const((L, LANES)), const((L, LANES)),
            const(fwd.shape), const(inv.shape), const(tap.shape),
        ] + [slab_spec(w) for w in slabs],
        out_specs=[pl.BlockSpec((1, L, D), lambda b, j: (b, j, 0))] + [slab_spec(w) for w in slabs],
        out_shape=[jax.ShapeDtypeStruct((B, S, D), _f32)] + [jax.ShapeDtypeStruct(w.shape, _bf16) for w in slabs],
        scratch_shapes=[
            pltpu.VMEM((D, D_IN), _bf16),
            pltpu.VMEM((D, D), _bf16),
            pltpu.VMEM(fwd.shape, _bf16),
            pltpu.VMEM(inv.shape, _bf16),
            pltpu.VMEM((2 * FREQ_PAD, CONV_CH), _f32),
            pltpu.VMEM((CONV_HALO + L, CONV_CH), _bf16),
            pltpu.VMEM((RET_QK, RET_V), _f32),
            pltpu.VMEM((L, D), _bf16),
        ],
        compiler_params=pltpu.CompilerParams(
            dimension_semantics=("arbitrary", "arbitrary"), vmem_limit_bytes=VMEM_LIMIT),
        name="mixer",
    )(x, positions.reshape(steps, 1, L), norm_w.reshape(1, D), w_in, taps, conv_b.reshape(1, -1),
      ln_w.reshape(1, -1), ln_b.reshape(1, -1), gn_w.reshape(1, -1), w_out, dmat, xi, zeta, gl, invf, cost, sint,
      fwd, inv, tap, *slabs)
    return outs[0], [o.reshape(w.shape) for o, w in zip(outs[1:], later_weights)]


def _xattn_kernel(h_ref, nw_ref, wq_ref, kt_ref, v_ref, wo_ref, o_ref, obuf):
    h = h_ref[0]
    hn = _rms(h, nw_ref[...]).astype(_bf16)
    q = (_dot(hn, wq_ref[...]) * (XATTN_HEAD_DIM ** -0.5)).astype(_bf16)
    for i in range(XATTN_HEADS):
        cols = slice(i * XATTN_HEAD_DIM, (i + 1) * XATTN_HEAD_DIM)
        s = _dot(q[:, cols], kt_ref[0, cols, :])
        e = jnp.exp(s - jnp.max(s, axis=-1, keepdims=True))
        o = _dot(e.astype(_bf16), v_ref[0, :, cols])
        obuf[:, cols] = (o * (1.0 / jnp.sum(e, axis=-1, keepdims=True))).astype(_bf16)
    o_ref[0] = h + _dot(obuf[...], wo_ref[...])


def _xattn(h, norm_w, xq_w, kt, v, xo_w):
    B, S, D = h.shape
    T = TOK_BLOCK
    return pl.pallas_call(
        _xattn_kernel,
        grid=(B, S // T),
        in_specs=[
            pl.BlockSpec((1, T, D), lambda b, j: (b, j, 0)),
            pl.BlockSpec((1, D), lambda b, j: (0, 0)),
            pl.BlockSpec((D, D), lambda b, j: (0, 0)),
            pl.BlockSpec((1, D, N_MEM), lambda b, j: (b, 0, 0)),
            pl.BlockSpec((1, N_MEM, D), lambda b, j: (b, 0, 0)),
            pl.BlockSpec((D, D), lambda b, j: (0, 0)),
        ],
        out_specs=pl.BlockSpec((1, T, D), lambda b, j: (b, j, 0)),
        out_shape=jax.ShapeDtypeStruct((B, S, D), _f32),
        scratch_shapes=[pltpu.VMEM((T, D), _bf16)],
        compiler_params=pltpu.CompilerParams(
            dimension_semantics=("arbitrary", "arbitrary"), vmem_limit_bytes=VMEM_LIMIT),
        name="xattn",
    )(h, norm_w.reshape(1, D), xq_w, kt, v, xo_w)


def _mlp_kernel(h_ref, nw_ref, wu_ref, wd_ref, fw_ref, o_ref, abuf):
    h = h_ref[...]
    hn = _rms(h, nw_ref[...]).astype(_bf16)
    for c in range(0, D_FF, FF_CHUNK):
        u = jnp.maximum(_dot(hn, wu_ref[:, c:c + FF_CHUNK]), 0.0)
        abuf[:, c:c + FF_CHUNK] = (u * u).astype(_bf16)
    o_ref[...] = _rms(h + _dot(abuf[...], wd_ref[...]), fw_ref[...])


def _mlp(h, norm_w, up_w, down_w, norm_f_w):
    B, S, D = h.shape
    T = TOK_BLOCK
    h2 = h.reshape(B * S, D)
    out = pl.pallas_call(
        _mlp_kernel,
        grid=(B * S // T,),
        in_specs=[
            pl.BlockSpec((T, D), lambda i: (i, 0)),
            pl.BlockSpec((1, D), lambda i: (0, 0)),
            pl.BlockSpec((D, D_FF), lambda i: (0, 0)),
            pl.BlockSpec((D_FF, D), lambda i: (0, 0)),
            pl.BlockSpec((1, D), lambda i: (0, 0)),
        ],
        out_specs=pl.BlockSpec((T, D), lambda i: (i, 0)),
        out_shape=jax.ShapeDtypeStruct((B * S, D), _f32),
        scratch_shapes=[pltpu.VMEM((T, D_FF), _bf16)],
        compiler_params=pltpu.CompilerParams(
            dimension_semantics=("arbitrary",), vmem_limit_bytes=VMEM_LIMIT),
        name="mlp",
    )(h2, norm_w.reshape(1, D), up_w, down_w, norm_f_w.reshape(1, D))
    return out.reshape(B, S, D)


def kernel(x, mem, positions, norm_mix_w, w_in, conv_w, conv_b, conv_ln_w, conv_ln_b, ret_gn_w, w_out,
           norm_xattn_w, norm_mem_w, xq_w, xkv_w, xo_w, norm_mlp_w, mlp_up_w, mlp_down_w, norm_f_w):
    kt, v = _mem_kv(mem, norm_mem_w, xkv_w)
    h, (xq_b, xo_b, up_b, down_b) = _mixer(x, positions, norm_mix_w, w_in, conv_w, conv_b, conv_ln_w, conv_ln_b,
                                           ret_gn_w, w_out, (xq_w, xo_w, mlp_up_w, mlp_down_w))
    h = _xattn(h, norm_xattn_w, xq_b, kt, v, xo_b)
    return _mlp(h, norm_mlp_w, up_b, down_b, norm_f_w)
```

```python
import numpy as np
import jax
import jax.numpy as jnp
from jax import lax
from jax.experimental import pallas as pl
from jax.experimental.pallas import tpu as pltpu

D_MODEL = 1024
CHUNK = 64
CONV_CH = 512
CONV_WIDTH = 31
RET_HEADS = 8
RET_DV = 64
RET_DK = 32
RET_QK = RET_HEADS * RET_DK
RET_V = RET_HEADS * RET_DV
N_MEM = 256
XATTN_HEADS = 4
XATTN_HEAD_DIM = 256
D_FF = 4096
ROPE_BASE = 10000.0
EPS = 1e-6

OFF_A, OFF_B, OFF_Q, OFF_K, OFF_V, OFF_G = 0, 512, 1024, 1280, 1536, 2048
D_IN = 2560

LANES = 128
SUBLANES = 8
MXU_TILE = 256
MIX_BLOCK = 256
CONV_HALO = 32
CONV_SEG = 128
DFT_N = CONV_HALO + CONV_SEG
N_FREQ = DFT_N // 2 + 1
FREQ_PAD = -(-N_FREQ // SUBLANES) * SUBLANES
CONV_TAIL = MXU_TILE - DFT_N
CONV_ROWS = 32
TOK_BLOCK = 512
FF_CHUNK = 1024
VMEM_LIMIT = 56 * 1024 * 1024

_f32 = jnp.float32
_bf16 = jnp.bfloat16


def _dot(a, b):
    return jnp.dot(a, b, preferred_element_type=_f32)


def _rms(x, w):
    return x * lax.rsqrt(jnp.mean(x * x, axis=-1, keepdims=True) + EPS) * w


def _sigmoid(x):
    return 1.0 / (1.0 + jnp.exp(-x))


def _cast_columns(src_ref, dst_ref):
    for c in range(0, src_ref.shape[-1], MXU_TILE):
        dst_ref[:, c:c + MXU_TILE] = src_ref[:, c:c + MXU_TILE].astype(_bf16)


def _mem_kv_kernel(mem_ref, nw_ref, wkv_ref, kt_ref, v_ref):
    m = _rms(mem_ref[0], nw_ref[...]).astype(_bf16)
    k = _dot(m, wkv_ref[:, :D_MODEL].astype(_bf16))
    kt_ref[0] = k.T.astype(_bf16)
    v_ref[0] = _dot(m, wkv_ref[:, D_MODEL:].astype(_bf16)).astype(_bf16)


def _mem_kv(mem, norm_mem_w, xkv_w):
    B = mem.shape[0]
    return pl.pallas_call(
        _mem_kv_kernel,
        grid=(B,),
        in_specs=[
            pl.BlockSpec((1, N_MEM, D_MODEL), lambda b: (b, 0, 0)),
            pl.BlockSpec((1, D_MODEL), lambda b: (0, 0)),
            pl.BlockSpec((D_MODEL, 2 * D_MODEL), lambda b: (0, 0)),
        ],
        out_specs=[
            pl.BlockSpec((1, D_MODEL, N_MEM), lambda b: (b, 0, 0)),
            pl.BlockSpec((1, N_MEM, D_MODEL), lambda b: (b, 0, 0)),
        ],
        out_shape=[
            jax.ShapeDtypeStruct((B, D_MODEL, N_MEM), _bf16),
            jax.ShapeDtypeStruct((B, N_MEM, D_MODEL), _bf16),
        ],
        compiler_params=pltpu.CompilerParams(
            dimension_semantics=("arbitrary",), vmem_limit_bytes=VMEM_LIMIT),
        name="mem_kv",
    )(mem, norm_mem_w.reshape(1, D_MODEL), xkv_w)


def _group_norm_halves(y, lo):
    inv = 1.0 / RET_DV
    s_lo = jnp.sum(jnp.where(lo, y, 0.0), axis=-1, keepdims=True)
    s_hi = jnp.sum(jnp.where(lo, 0.0, y), axis=-1, keepdims=True)
    d = y - jnp.where(lo, s_lo, s_hi) * inv
    d2 = d * d
    v_lo = jnp.sum(jnp.where(lo, d2, 0.0), axis=-1, keepdims=True)
    v_hi = jnp.sum(jnp.where(lo, 0.0, d2), axis=-1, keepdims=True)
    return d * lax.rsqrt(jnp.where(lo, v_lo, v_hi) * inv + EPS)


N_CAST = 4
CAST_SCALES = (XATTN_HEAD_DIM ** -0.5, 1.0, 1.0, 1.0)


def _mixer_kernel(*refs):
    n_in = 20
    (x_ref, pos_ref, nw_ref, win32_ref, cw_ref, cb_ref, lnw_ref, lnb_ref, gnw_ref, wout32_ref,
     dmat_ref, xi_ref, zeta_ref, gl_ref, invf_ref, cost_ref, sint_ref, fwd32_ref, inv32_ref, tap_ref) = refs[:n_in]
    cast_src = refs[n_in:n_in + N_CAST]
    o_ref = refs[n_in + N_CAST]
    cast_dst = refs[n_in + 1 + N_CAST:n_in + 1 + 2 * N_CAST]
    win_ref, wout_ref, fwd_ref, inv_ref, hspec, ubuf, state, mbuf = refs[n_in + 1 + 2 * N_CAST:]
    L = MIX_BLOCK

    @pl.when((pl.program_id(0) == 0) & (pl.program_id(1) == 0))
    def _():
        _cast_columns(win32_ref, win_ref)
        _cast_columns(wout32_ref, wout_ref)
        fwd_ref[...] = fwd32_ref[...].astype(_bf16)
        inv_ref[...] = inv32_ref[...].astype(_bf16)
        ubuf[CONV_HALO + L:, :] = jnp.zeros((CONV_TAIL, CONV_CH), _bf16)
        hspec[...] = jnp.dot(tap_ref[...], cw_ref[...], preferred_element_type=_f32, precision=lax.Precision.HIGHEST)

    @pl.when(pl.program_id(1) == 0)
    def _():
        ubuf[0:CONV_HALO, :] = jnp.zeros((CONV_HALO, CONV_CH), _bf16)
        state[...] = jnp.zeros_like(state)

    for src, dst, scale in zip(cast_src, cast_dst, CAST_SCALES):
        dst[...] = (src[...] if scale == 1.0 else src[...] * scale).astype(_bf16)

    x = x_ref[0]
    hn = _rms(x, nw_ref[...]).astype(_bf16)

    a = _dot(hn, win_ref[:, OFF_A:OFF_A + CONV_CH])
    b = _dot(hn, win_ref[:, OFF_B:OFF_B + CONV_CH])
    ubuf[CONV_HALO:CONV_HALO + L, :] = (a * _sigmoid(b)).astype(_bf16)
    h_re, h_im = hspec[0:FREQ_PAD, :], hspec[FREQ_PAD:2 * FREQ_PAD, :]

    def conv_segment(row0):
        spec = _dot(fwd_ref[...], ubuf[row0:row0 + MXU_TILE, :])
        s_re, s_im = spec[0:FREQ_PAD, :], spec[FREQ_PAD:2 * FREQ_PAD, :]
        prod = jnp.concatenate([s_re * h_re - s_im * h_im, s_re * h_im + s_im * h_re,
                                jnp.zeros((MXU_TILE - 2 * FREQ_PAD, CONV_CH), _f32)], axis=0)
        y = _dot(inv_ref[...], prod.astype(_bf16))
        for r in range(0, CONV_SEG, CONV_ROWS):
            acc = y[r:r + CONV_ROWS, :] + cb_ref[...]
            mu = jnp.mean(acc, axis=-1, keepdims=True)
            d = acc - mu
            var = jnp.mean(d * d, axis=-1, keepdims=True)
            yn = d * lax.rsqrt(var + EPS) * lnw_ref[...] + lnb_ref[...]
            mbuf[row0 + r:row0 + r + CONV_ROWS, 0:CONV_CH] = (yn * _sigmoid(yn)).astype(_bf16)

    for i in range(L // CONV_SEG):
        conv_segment(i * CONV_SEG)
    ubuf[0:CONV_HALO, :] = ubuf[L:L + CONV_HALO, :]

    ang0 = pos_ref[0, :, 0:1].astype(_f32) * invf_ref[...]
    cos0, sin0 = jnp.cos(ang0), jnp.sin(ang0)
    cos = cos0 * cost_ref[...] - sin0 * sint_ref[...]
    sin = sin0 * cost_ref[...] + cos0 * sint_ref[...]
    lane = lax.broadcasted_iota(jnp.int32, (1, RET_QK), 1)
    first_half = lane % RET_DK < RET_DK // 2
    cos2 = jnp.concatenate([cos, cos], axis=-1)
    sin2 = jnp.where(first_half, -1.0, 1.0) * jnp.concatenate([sin, sin], axis=-1)

    def rotary(t):
        partner = jnp.where(first_half, pltpu.roll(t, RET_QK - RET_DK // 2, 1), pltpu.roll(t, RET_DK // 2, 1))
        return t * cos2 + partner * sin2

    qr = rotary(_dot(hn, win_ref[:, OFF_Q:OFF_Q + RET_QK]))
    kr = rotary(_dot(hn, win_ref[:, OFF_K:OFF_K + RET_QK]))
    v = _dot(hn, win_ref[:, OFF_V:OFF_V + RET_V])
    g = _dot(hn, win_ref[:, OFF_G:OFF_G + RET_V])
    kt = kr.T.astype(_bf16)
    vb = v.astype(_bf16)

    yx = _dot(qr.astype(_bf16), state[...].astype(_bf16)) * xi_ref[...]

    q_head = lane // RET_DK
    lo = lax.broadcasted_iota(jnp.int32, (1, LANES), 1) < RET_DV

    def head_pair(p):
        cols = slice(p * LANES, (p + 1) * LANES)
        vp = vb[:, cols]
        ys = []
        for h in (2 * p, 2 * p + 1):
            qm = jnp.where(q_head == h, qr, 0.0).astype(_bf16)
            s = _dot(qm, kt)
            ys.append(_dot((s * dmat_ref[h]).astype(_bf16), vp))
        y = jnp.where(lo, ys[0], ys[1]) + yx[:, cols]
        yn = _group_norm_halves(y, lo) * gnw_ref[:, cols]
        gp = g[:, cols]
        mbuf[:, CONV_CH + p * LANES:CONV_CH + (p + 1) * LANES] = (gp * _sigmoid(gp) * yn).astype(_bf16)

    for p in range(RET_HEADS // 2):
        head_pair(p)

    kv = _dot(kt, (v * zeta_ref[...]).astype(_bf16))
    row_head = lax.broadcasted_iota(jnp.int32, (RET_QK, RET_V), 0) // RET_DK
    col_head = lax.broadcasted_iota(jnp.int32, (RET_QK, RET_V), 1) // RET_DV
    state[...] = gl_ref[...] * state[...] + jnp.where(row_head == col_head, kv, 0.0)

    o_ref[0] = x + _dot(mbuf[...], wout_ref[...])


def _retention_tables(L):
    h = np.arange(RET_HEADS, dtype=np.float64)
    log_g = np.log1p(-np.exp2(-5.0 - h))
    idx = np.arange(L, dtype=np.float64)
    dist = np.abs(idx[:, None] - idx[None, :])
    visible = (idx[None, :] // CHUNK) <= (idx[:, None] // CHUNK)
    dmat = np.where(visible[None], np.exp(log_g[:, None, None] * dist[None]), 0.0)
    xi = np.exp(log_g[None, :] * (idx[:, None] + 1.0))
    zeta = np.exp(log_g[None, :] * (L - 1.0 - idx[:, None]))
    gl = np.exp(log_g * L)
    rep = lambda t: np.repeat(t, RET_DV, axis=-1)
    inv_freq = ROPE_BASE ** (-np.arange(RET_DK // 2, dtype=np.float32) / np.float32(RET_DK // 2))
    invf = np.tile(inv_freq.astype(np.float32), LANES // (RET_DK // 2))[None, :]
    rel = idx[:, None] * invf.astype(np.float64)
    f = lambda t: jnp.asarray(t, dtype=_f32)
    scale = RET_DK ** -0.5
    return (f(dmat * scale), f(rep(xi) * scale), f(rep(zeta)), f(rep(gl[None, :])), f(invf),
            f(np.cos(rel)), f(np.sin(rel)))


def _conv_dft_tables():
    n = np.arange(DFT_N, dtype=np.float64)
    f = np.arange(N_FREQ, dtype=np.float64)[:, None]
    w = 2.0 * np.pi / DFT_N
    fwd = np.zeros((2 * FREQ_PAD, MXU_TILE))
    fwd[:N_FREQ, :DFT_N] = np.cos(w * f * n[None, :])
    fwd[FREQ_PAD:FREQ_PAD + N_FREQ, :DFT_N] = -np.sin(w * f * n[None, :])
    delay = (CONV_WIDTH - 1) - np.arange(CONV_WIDTH, dtype=np.float64)[None, :]
    tap = np.zeros((2 * FREQ_PAD, CONV_HALO))
    tap[:N_FREQ, :CONV_WIDTH] = np.cos(w * f * delay)
    tap[FREQ_PAD:FREQ_PAD + N_FREQ, :CONV_WIDTH] = -np.sin(w * f * delay)
    weight = np.full((1, N_FREQ), 2.0)
    weight[0, 0] = weight[0, -1] = 1.0
    out = n[CONV_HALO:, None]
    inv = np.zeros((CONV_SEG, MXU_TILE))
    inv[:, :N_FREQ] = weight * np.cos(w * out * f.T) / DFT_N
    inv[:, FREQ_PAD:FREQ_PAD + N_FREQ] = -weight * np.sin(w * out * f.T) / DFT_N
    return [jnp.asarray(t, dtype=_f32) for t in (fwd, inv, tap)]


def _mixer(x, positions, norm_w, w_in, conv_w, conv_b, ln_w, ln_b, gn_w, w_out, later_weights):
    B, S, D = x.shape
    L = MIX_BLOCK
    n = S // L
    steps = B * n
    dmat, xi, zeta, gl, invf, cost, sint = _retention_tables(L)
    fwd, inv, tap = _conv_dft_tables()
    taps = jnp.pad(conv_w, ((0, CONV_HALO - CONV_WIDTH), (0, 0)))
    const = lambda shape: pl.BlockSpec(shape, lambda b, j: (0,) * len(shape))
    once = lambda shape: pl.BlockSpec(shape, lambda b, j: (0,) * len(shape), pipeline_mode=pl.Buffered(1))
    row = lambda n: pl.BlockSpec((1, n), lambda b, j: (0, 0))
    slabs = [w.reshape(steps, w.shape[0] // steps, w.shape[1]) for w in later_weights]
    slab_spec = lambda w: pl.BlockSpec((1,) + w.shape[1:], lambda b, j: (b * n + j, 0, 0))
    outs = pl.pallas_call(
        _mixer_kernel,
        grid=(B, n),
        in_specs=[
            pl.BlockSpec((1, L, D), lambda b, j: (b, j, 0)),
            pl.BlockSpec((1, 1, L), lambda b, j: (b * n + j, 0, 0)),
            row(D),
            once((D, D_IN)),
            const((CONV_HALO, CONV_CH)),
            row(CONV_CH), row(CONV_CH), row(CONV_CH), row(RET_V),
            once((D, D)),
            const((RET_HEADS, L, L)),
            const((L, RET_V)), const((L, RET_V)), row(RET_V), row(LANES),
            const((L, LANES)), const((L, LANES)),
            const(fwd.shape), const(inv.shape), const(tap.shape),
        ] + [slab_spec(w) for w in slabs],
        out_specs=[pl.BlockSpec((1, L, D), lambda b, j: (b, j, 0))] + [slab_spec(w) for w in slabs],
        out_shape=[jax.ShapeDtypeStruct((B, S, D), _f32)] + [jax.ShapeDtypeStruct(w.shape, _bf16) for w in slabs],
        scratch_shapes=[
            pltpu.VMEM((D, D_IN), _bf16),
            pltpu.VMEM((D, D), _bf16),
            pltpu.VMEM(fwd.shape, _bf16),
            pltpu.VMEM(inv.shape, _bf16),
            pltpu.VMEM((2 * FREQ_PAD, CONV_CH), _f32),
            pltpu.VMEM((CONV_HALO + L + CONV_TAIL, CONV_CH), _bf16),
            pltpu.VMEM((RET_QK, RET_V), _f32),
            pltpu.VMEM((L, D), _bf16),
        ],
        compiler_params=pltpu.CompilerParams(
            dimension_semantics=("arbitrary", "arbitrary"), vmem_limit_bytes=VMEM_LIMIT),
        name="mixer",
    )(x, positions.reshape(steps, 1, L), norm_w.reshape(1, D), w_in, taps, conv_b.reshape(1, -1),
      ln_w.reshape(1, -1), ln_b.reshape(1, -1), gn_w.reshape(1, -1), w_out, dmat, xi, zeta, gl, invf, cost, sint,
      fwd, inv, tap, *slabs)
    return outs[0], [o.reshape(w.shape) for o, w in zip(outs[1:], later_weights)]


def _xattn_kernel(h_ref, nw_ref, wq_ref, kt_ref, v_ref, wo_ref, o_ref, obuf):
    h = h_ref[0]
    hn = _rms(h, nw_ref[...]).astype(_bf16)
    q = _dot(hn, wq_ref[...]).astype(_bf16)
    for i in range(XATTN_HEADS):
        cols = slice(i * XATTN_HEAD_DIM, (i + 1) * XATTN_HEAD_DIM)
        s = _dot(q[:, cols], kt_ref[0, cols, :])
        e = jnp.exp(s - jnp.max(s, axis=-1, keepdims=True))
        o = _dot(e.astype(_bf16), v_ref[0, :, cols])
        obuf[:, cols] = (o * (1.0 / jnp.sum(e, axis=-1, keepdims=True))).astype(_bf16)
    o_ref[0] = h + _dot(obuf[...], wo_ref[...])


def _xattn(h, norm_w, xq_w, kt, v, xo_w):
    B, S, D = h.shape
    T = TOK_BLOCK
    return pl.pallas_call(
        _xattn_kernel,
        grid=(B, S // T),
        in_specs=[
            pl.BlockSpec((1, T, D), lambda b, j: (b, j, 0)),
            pl.BlockSpec((1, D), lambda b, j: (0, 0)),
            pl.BlockSpec((D, D), lambda b, j: (0, 0)),
            pl.BlockSpec((1, D, N_MEM), lambda b, j: (b, 0, 0)),
            pl.BlockSpec((1, N_MEM, D), lambda b, j: (b, 0, 0)),
            pl.BlockSpec((D, D), lambda b, j: (0, 0)),
        ],
        out_specs=pl.BlockSpec((1, T, D), lambda b, j: (b, j, 0)),
        out_shape=jax.ShapeDtypeStruct((B, S, D), _f32),
        scratch_shapes=[pltpu.VMEM((T, D), _bf16)],
        compiler_params=pltpu.CompilerParams(
            dimension_semantics=("arbitrary", "arbitrary"), vmem_limit_bytes=VMEM_LIMIT),
        name="xattn",
    )(h, norm_w.reshape(1, D), xq_w, kt, v, xo_w)


def _mlp_kernel(h_ref, nw_ref, wu_ref, wd_ref, fw_ref, o_ref, abuf):
    h = h_ref[...]
    hn = _rms(h, nw_ref[...]).astype(_bf16)
    for c in range(0, D_FF, FF_CHUNK):
        u = jnp.maximum(_dot(hn, wu_ref[:, c:c + FF_CHUNK]), 0.0)
        abuf[:, c:c + FF_CHUNK] = (u * u).astype(_bf16)
    o_ref[...] = _rms(h + _dot(abuf[...], wd_ref[...]), fw_ref[...])


def _mlp(h, norm_w, up_w, down_w, norm_f_w):
    B, S, D = h.shape
    T = TOK_BLOCK
    h2 = h.reshape(B * S, D)
    out = pl.pallas_call(
        _mlp_kernel,
        grid=(B * S // T,),
        in_specs=[
            pl.BlockSpec((T, D), lambda i: (i, 0)),
            pl.BlockSpec((1, D), lambda i: (0, 0)),
            pl.BlockSpec((D, D_FF), lambda i: (0, 0)),
            pl.BlockSpec((D_FF, D), lambda i: (0, 0)),
            pl.BlockSpec((1, D), lambda i: (0, 0)),
        ],
        out_specs=pl.BlockSpec((T, D), lambda i: (i, 0)),
        out_shape=jax.ShapeDtypeStruct((B * S, D), _f32),
        scratch_shapes=[pltpu.VMEM((T, D_FF), _bf16)],
        compiler_params=pltpu.CompilerParams(
            dimension_semantics=("arbitrary",), vmem_limit_bytes=VMEM_LIMIT),
        name="mlp",
    )(h2, norm_w.reshape(1, D), up_w, down_w, norm_f_w.reshape(1, D))
    return out.reshape(B, S, D)


def kernel(x, mem, positions, norm_mix_w, w_in, conv_w, conv_b, conv_ln_w, conv_ln_b, ret_gn_w, w_out,
           norm_xattn_w, norm_mem_w, xq_w, xkv_w, xo_w, norm_mlp_w, mlp_up_w, mlp_down_w, norm_f_w):
    kt, v = _mem_kv(mem, norm_mem_w, xkv_w)
    h, (xq_b, xo_b, up_b, down_b) = _mixer(x, positions, norm_mix_w, w_in, conv_w, conv_b, conv_ln_w, conv_ln_b,
                                           ret_gn_w, w_out, (xq_w, xo_w, mlp_up_w, mlp_down_w))
    h = _xattn(h, norm_xattn_w, xq_b, kt, v, xo_b)
    return _mlp(h, norm_mlp_w, up_b, down_b, norm_f_w)
```

```python
import numpy as np
import jax
import jax.numpy as jnp
from jax import lax
from jax.experimental import pallas as pl
from jax.experimental.pallas import tpu as pltpu

D_MODEL = 1024
CHUNK = 64
CONV_CH = 512
CONV_WIDTH = 31
RET_HEADS = 8
RET_DV = 64
RET_DK = 32
RET_QK = RET_HEADS * RET_DK
RET_V = RET_HEADS * RET_DV
N_MEM = 256
XATTN_HEADS = 4
XATTN_HEAD_DIM = 256
D_FF = 4096
ROPE_BASE = 10000.0
EPS = 1e-6

OFF_A, OFF_B, OFF_Q, OFF_K, OFF_V, OFF_G = 0, 512, 1024, 1280, 1536, 2048
D_IN = 2560

LANES = 128
SUBLANES = 8
MXU_TILE = 256
MIX_BLOCK = 512
RET_BLOCK = 256
CONV_HALO = 32
CONV_SEG = 128
DFT_N = CONV_HALO + CONV_SEG
N_FREQ = DFT_N // 2 + 1
FREQ_PAD = -(-N_FREQ // SUBLANES) * SUBLANES
CONV_TAIL = MXU_TILE - DFT_N
CONV_ROWS = 32
TOK_BLOCK = 512
FF_CHUNK = 1024
VMEM_LIMIT = 56 * 1024 * 1024

_f32 = jnp.float32
_bf16 = jnp.bfloat16


def _dot(a, b):
    return jnp.dot(a, b, preferred_element_type=_f32)


def _rms(x, w):
    return x * lax.rsqrt(jnp.mean(x * x, axis=-1, keepdims=True) + EPS) * w


def _sigmoid(x):
    return 1.0 / (1.0 + jnp.exp(-x))


def _cast_columns(src_ref, dst_ref):
    for c in range(0, src_ref.shape[-1], MXU_TILE):
        dst_ref[:, c:c + MXU_TILE] = src_ref[:, c:c + MXU_TILE].astype(_bf16)


def _mem_kv_kernel(mem_ref, nw_ref, wkv_ref, kt_ref, v_ref):
    m = _rms(mem_ref[0], nw_ref[...]).astype(_bf16)
    k = _dot(m, wkv_ref[:, :D_MODEL].astype(_bf16))
    kt_ref[0] = k.T.astype(_bf16)
    v_ref[0] = _dot(m, wkv_ref[:, D_MODEL:].astype(_bf16)).astype(_bf16)


def _mem_kv(mem, norm_mem_w, xkv_w):
    B = mem.shape[0]
    return pl.pallas_call(
        _mem_kv_kernel,
        grid=(B,),
        in_specs=[
            pl.BlockSpec((1, N_MEM, D_MODEL), lambda b: (b, 0, 0)),
            pl.BlockSpec((1, D_MODEL), lambda b: (0, 0)),
            pl.BlockSpec((D_MODEL, 2 * D_MODEL), lambda b: (0, 0)),
        ],
        out_specs=[
            pl.BlockSpec((1, D_MODEL, N_MEM), lambda b: (b, 0, 0)),
            pl.BlockSpec((1, N_MEM, D_MODEL), lambda b: (b, 0, 0)),
        ],
        out_shape=[
            jax.ShapeDtypeStruct((B, D_MODEL, N_MEM), _bf16),
            jax.ShapeDtypeStruct((B, N_MEM, D_MODEL), _bf16),
        ],
        compiler_params=pltpu.CompilerParams(
            dimension_semantics=("arbitrary",), vmem_limit_bytes=VMEM_LIMIT),
        name="mem_kv",
    )(mem, norm_mem_w.reshape(1, D_MODEL), xkv_w)


def _group_norm_halves(y, lo):
    inv = 1.0 / RET_DV
    s_lo = jnp.sum(jnp.where(lo, y, 0.0), axis=-1, keepdims=True)
    s_hi = jnp.sum(jnp.where(lo, 0.0, y), axis=-1, keepdims=True)
    d = y - jnp.where(lo, s_lo, s_hi) * inv
    d2 = d * d
    v_lo = jnp.sum(jnp.where(lo, d2, 0.0), axis=-1, keepdims=True)
    v_hi = jnp.sum(jnp.where(lo, 0.0, d2), axis=-1, keepdims=True)
    return d * lax.rsqrt(jnp.where(lo, v_lo, v_hi) * inv + EPS)


N_CAST = 4
CAST_SCALES = (XATTN_HEAD_DIM ** -0.5, 1.0, 1.0, 1.0)


def _mixer_kernel(*refs):
    n_in = 20
    (x_ref, pos_ref, nw_ref, win32_ref, cw_ref, cb_ref, lnw_ref, lnb_ref, gnw_ref, wout32_ref,
     dmat_ref, xi_ref, zeta_ref, gl_ref, invf_ref, cost_ref, sint_ref, fwd32_ref, inv32_ref, tap_ref) = refs[:n_in]
    cast_src = refs[n_in:n_in + N_CAST]
    o_ref = refs[n_in + N_CAST]
    cast_dst = refs[n_in + 1 + N_CAST:n_in + 1 + 2 * N_CAST]
    win_ref, wout_ref, fwd_ref, inv_ref, hspec, ubuf, pbuf, state, mbuf = refs[n_in + 1 + 2 * N_CAST:]
    L = MIX_BLOCK

    @pl.when((pl.program_id(0) == 0) & (pl.program_id(1) == 0))
    def _():
        _cast_columns(win32_ref, win_ref)
        _cast_columns(wout32_ref, wout_ref)
        fwd_ref[...] = fwd32_ref[...].astype(_bf16)
        inv_ref[...] = inv32_ref[...].astype(_bf16)
        ubuf[CONV_HALO + L:, :] = jnp.zeros((CONV_TAIL, CONV_CH), _bf16)
        hspec[...] = jnp.dot(tap_ref[...], cw_ref[...], preferred_element_type=_f32, precision=lax.Precision.HIGHEST)

    @pl.when(pl.program_id(1) == 0)
    def _():
        ubuf[0:CONV_HALO, :] = jnp.zeros((CONV_HALO, CONV_CH), _bf16)
        state[...] = jnp.zeros_like(state)

    for src, dst, scale in zip(cast_src, cast_dst, CAST_SCALES):
        dst[...] = (src[...] if scale == 1.0 else src[...] * scale).astype(_bf16)

    x = x_ref[0]
    hn = _rms(x, nw_ref[...]).astype(_bf16)

    a = _dot(hn, win_ref[:, OFF_A:OFF_A + CONV_CH])
    b = _dot(hn, win_ref[:, OFF_B:OFF_B + CONV_CH])
    ubuf[CONV_HALO:CONV_HALO + L, :] = (a * _sigmoid(b)).astype(_bf16)
    h_re, h_im = hspec[0:FREQ_PAD, :], hspec[FREQ_PAD:2 * FREQ_PAD, :]

    def conv_segment(row0):
        spec = _dot(fwd_ref[...], ubuf[row0:row0 + MXU_TILE, :])
        s_re, s_im = spec[0:FREQ_PAD, :], spec[FREQ_PAD:2 * FREQ_PAD, :]
        prod = jnp.concatenate([s_re * h_re - s_im * h_im, s_re * h_im + s_im * h_re,
                                jnp.zeros((MXU_TILE - 2 * FREQ_PAD, CONV_CH), _f32)], axis=0)
        y = _dot(inv_ref[...], prod.astype(_bf16))
        for r in range(0, CONV_SEG, CONV_ROWS):
            acc = y[r:r + CONV_ROWS, :] + cb_ref[...]
            mu = jnp.mean(acc, axis=-1, keepdims=True)
            d = acc - mu
            var = jnp.mean(d * d, axis=-1, keepdims=True)
            yn = d * lax.rsqrt(var + EPS) * lnw_ref[...] + lnb_ref[...]
            mbuf[row0 + r:row0 + r + CONV_ROWS, 0:CONV_CH] = (yn * _sigmoid(yn)).astype(_bf16)

    for i in range(L // CONV_SEG):
        conv_segment(i * CONV_SEG)
    ubuf[0:CONV_HALO, :] = ubuf[L:L + CONV_HALO, :]

    for c in range(OFF_Q, D_IN, MXU_TILE):
        pbuf[:, c - OFF_Q:c - OFF_Q + MXU_TILE] = _dot(hn, win_ref[:, c:c + MXU_TILE])
    ang0 = pos_ref[0, :, 0:1].astype(_f32) * invf_ref[...]
    cos0, sin0 = jnp.cos(ang0), jnp.sin(ang0)
    lane = lax.broadcasted_iota(jnp.int32, (1, RET_QK), 1)
    first_half = lane % RET_DK < RET_DK // 2
    q_head = lane // RET_DK
    lo = lax.broadcasted_iota(jnp.int32, (1, LANES), 1) < RET_DV
    row_head = lax.broadcasted_iota(jnp.int32, (RET_QK, RET_V), 0) // RET_DK
    col_head = lax.broadcasted_iota(jnp.int32, (RET_QK, RET_V), 1) // RET_DV

    for r0 in range(0, L, RET_BLOCK):
        rows = slice(r0, r0 + RET_BLOCK)
        cos = cos0 * cost_ref[rows, :] - sin0 * sint_ref[rows, :]
        sin = sin0 * cost_ref[rows, :] + cos0 * sint_ref[rows, :]
        cos2 = jnp.concatenate([cos, cos], axis=-1)
        sin2 = jnp.where(first_half, -1.0, 1.0) * jnp.concatenate([sin, sin], axis=-1)

        def rotary(t):
            partner = jnp.where(first_half, pltpu.roll(t, RET_QK - RET_DK // 2, 1), pltpu.roll(t, RET_DK // 2, 1))
            return t * cos2 + partner * sin2

        qr = rotary(pbuf[rows, 0:OFF_K - OFF_Q])
        kr = rotary(pbuf[rows, OFF_K - OFF_Q:OFF_V - OFF_Q])
        v = pbuf[rows, OFF_V - OFF_Q:OFF_G - OFF_Q]
        kt = kr.T.astype(_bf16)
        vb = v.astype(_bf16)

        yx = _dot(qr.astype(_bf16), state[...].astype(_bf16)) * xi_ref[...]

        for p in range(RET_HEADS // 2):
            cols = slice(p * LANES, (p + 1) * LANES)
            vp = vb[:, cols]
            ys = []
            for h in (2 * p, 2 * p + 1):
                qm = jnp.where(q_head == h, qr, 0.0).astype(_bf16)
                s = _dot(qm, kt)
                ys.append(_dot((s * dmat_ref[h]).astype(_bf16), vp))
            y = jnp.where(lo, ys[0], ys[1]) + yx[:, cols]
            yn = _group_norm_halves(y, lo) * gnw_ref[:, cols]
            gp = pbuf[rows, OFF_G - OFF_Q + p * LANES:OFF_G - OFF_Q + (p + 1) * LANES]
            mbuf[rows, CONV_CH + p * LANES:CONV_CH + (p + 1) * LANES] = (gp * _sigmoid(gp) * yn).astype(_bf16)

        kv = _dot(kt, (v * zeta_ref[...]).astype(_bf16))
        state[...] = gl_ref[...] * state[...] + jnp.where(row_head == col_head, kv, 0.0)

    o_ref[0] = x + _dot(mbuf[...], wout_ref[...])


def _retention_tables(L, step_rows):
    h = np.arange(RET_HEADS, dtype=np.float64)
    log_g = np.log1p(-np.exp2(-5.0 - h))
    idx = np.arange(L, dtype=np.float64)
    dist = np.abs(idx[:, None] - idx[None, :])
    visible = (idx[None, :] // CHUNK) <= (idx[:, None] // CHUNK)
    dmat = np.where(visible[None], np.exp(log_g[:, None, None] * dist[None]), 0.0)
    xi = np.exp(log_g[None, :] * (idx[:, None] + 1.0))
    zeta = np.exp(log_g[None, :] * (L - 1.0 - idx[:, None]))
    gl = np.exp(log_g * L)
    rep = lambda t: np.repeat(t, RET_DV, axis=-1)
    inv_freq = ROPE_BASE ** (-np.arange(RET_DK // 2, dtype=np.float32) / np.float32(RET_DK // 2))
    invf = np.tile(inv_freq.astype(np.float32), LANES // (RET_DK // 2))[None, :]
    rel = np.arange(step_rows, dtype=np.float64)[:, None] * invf.astype(np.float64)
    f = lambda t: jnp.asarray(t, dtype=_f32)
    scale = RET_DK ** -0.5
    return (f(dmat * scale), f(rep(xi) * scale), f(rep(zeta)), f(rep(gl[None, :])), f(invf),
            f(np.cos(rel)), f(np.sin(rel)))


def _conv_dft_tables():
    n = np.arange(DFT_N, dtype=np.float64)
    f = np.arange(N_FREQ, dtype=np.float64)[:, None]
    w = 2.0 * np.pi / DFT_N
    fwd = np.zeros((2 * FREQ_PAD, MXU_TILE))
    fwd[:N_FREQ, :DFT_N] = np.cos(w * f * n[None, :])
    fwd[FREQ_PAD:FREQ_PAD + N_FREQ, :DFT_N] = -np.sin(w * f * n[None, :])
    delay = (CONV_WIDTH - 1) - np.arange(CONV_WIDTH, dtype=np.float64)[None, :]
    tap = np.zeros((2 * FREQ_PAD, CONV_HALO))
    tap[:N_FREQ, :CONV_WIDTH] = np.cos(w * f * delay)
    tap[FREQ_PAD:FREQ_PAD + N_FREQ, :CONV_WIDTH] = -np.sin(w * f * delay)
    weight = np.full((1, N_FREQ), 2.0)
    weight[0, 0] = weight[0, -1] = 1.0
    out = n[CONV_HALO:, None]
    inv = np.zeros((CONV_SEG, MXU_TILE))
    inv[:, :N_FREQ] = weight * np.cos(w * out * f.T) / DFT_N
    inv[:, FREQ_PAD:FREQ_PAD + N_FREQ] = -weight * np.sin(w * out * f.T) / DFT_N
    return [jnp.asarray(t, dtype=_f32) for t in (fwd, inv, tap)]


def _mixer(x, positions, norm_w, w_in, conv_w, conv_b, ln_w, ln_b, gn_w, w_out, later_weights):
    B, S, D = x.shape
    L = MIX_BLOCK
    n = S // L
    steps = B * n
    R = RET_BLOCK
    dmat, xi, zeta, gl, invf, cost, sint = _retention_tables(R, L)
    fwd, inv, tap = _conv_dft_tables()
    taps = jnp.pad(conv_w, ((0, CONV_HALO - CONV_WIDTH), (0, 0)))
    const = lambda shape: pl.BlockSpec(shape, lambda b, j: (0,) * len(shape))
    once = lambda shape: pl.BlockSpec(shape, lambda b, j: (0,) * len(shape), pipeline_mode=pl.Buffered(1))
    row = lambda n: pl.BlockSpec((1, n), lambda b, j: (0, 0))
    slabs = [w.reshape(steps, w.shape[0] // steps, w.shape[1]) for w in later_weights]
    slab_spec = lambda w: pl.BlockSpec((1,) + w.shape[1:], lambda b, j: (b * n + j, 0, 0))
    outs = pl.pallas_call(
        _mixer_kernel,
        grid=(B, n),
        in_specs=[
            pl.BlockSpec((1, L, D), lambda b, j: (b, j, 0)),
            pl.BlockSpec((1, 1, L), lambda b, j: (b * n + j, 0, 0)),
            row(D),
            once((D, D_IN)),
            const((CONV_HALO, CONV_CH)),
            row(CONV_CH), row(CONV_CH), row(CONV_CH), row(RET_V),
            once((D, D)),
            const((RET_HEADS, R, R)),
            const((R, RET_V)), const((R, RET_V)), row(RET_V), row(LANES),
            const((L, LANES)), const((L, LANES)),
            const(fwd.shape), const(inv.shape), const(tap.shape),
        ] + [slab_spec(w) for w in slabs],
        out_specs=[pl.BlockSpec((1, L, D), lambda b, j: (b, j, 0))] + [slab_spec(w) for w in slabs],
        out_shape=[jax.ShapeDtypeStruct((B, S, D), _f32)] + [jax.ShapeDtypeStruct(w.shape, _bf16) for w in slabs],
        scratch_shapes=[
            pltpu.VMEM((D, D_IN), _bf16),
            pltpu.VMEM((D, D), _bf16),
            pltpu.VMEM(fwd.shape, _bf16),
            pltpu.VMEM(inv.shape, _bf16),
            pltpu.VMEM((2 * FREQ_PAD, CONV_CH), _f32),
            pltpu.VMEM((CONV_HALO + L + CONV_TAIL, CONV_CH), _bf16),
            pltpu.VMEM((L, D_IN - OFF_Q), _f32),
            pltpu.VMEM((RET_QK, RET_V), _f32),
            pltpu.VMEM((L, D), _bf16),
        ],
        compiler_params=pltpu.CompilerParams(
            dimension_semantics=("arbitrary", "arbitrary"), vmem_limit_bytes=VMEM_LIMIT),
        name="mixer",
    )(x, positions.reshape(steps, 1, L), norm_w.reshape(1, D), w_in, taps, conv_b.reshape(1, -1),
      ln_w.reshape(1, -1), ln_b.reshape(1, -1), gn_w.reshape(1, -1), w_out, dmat, xi, zeta, gl, invf, cost, sint,
      fwd, inv, tap, *slabs)
    return outs[0], [o.reshape(w.shape) for o, w in zip(outs[1:], later_weights)]


def _xattn_kernel(h_ref, nw_ref, wq_ref, kt_ref, v_ref, wo_ref, o_ref, obuf):
    h = h_ref[0]
    hn = _rms(h, nw_ref[...]).astype(_bf16)
    q = _dot(hn, wq_ref[...]).astype(_bf16)
    for i in range(XATTN_HEADS):
        cols = slice(i * XATTN_HEAD_DIM, (i + 1) * XATTN_HEAD_DIM)
        s = _dot(q[:, cols], kt_ref[0, cols, :])
        e = jnp.exp(s - jnp.max(s, axis=-1, keepdims=True))
        o = _dot(e.astype(_bf16), v_ref[0, :, cols])
        obuf[:, cols] = (o * (1.0 / jnp.sum(e, axis=-1, keepdims=True))).astype(_bf16)
    o_ref[0] = h + _dot(obuf[...], wo_ref[...])


def _xattn(h, norm_w, xq_w, kt, v, xo_w):
    B, S, D = h.shape
    T = TOK_BLOCK
    return pl.pallas_call(
        _xattn_kernel,
        grid=(B, S // T),
        in_specs=[
            pl.BlockSpec((1, T, D), lambda b, j: (b, j, 0)),
            pl.BlockSpec((1, D), lambda b, j: (0, 0)),
            pl.BlockSpec((D, D), lambda b, j: (0, 0)),
            pl.BlockSpec((1, D, N_MEM), lambda b, j: (b, 0, 0)),
            pl.BlockSpec((1, N_MEM, D), lambda b, j: (b, 0, 0)),
            pl.BlockSpec((D, D), lambda b, j: (0, 0)),
        ],
        out_specs=pl.BlockSpec((1, T, D), lambda b, j: (b, j, 0)),
        out_shape=jax.ShapeDtypeStruct((B, S, D), _f32),
        scratch_shapes=[pltpu.VMEM((T, D), _bf16)],
        compiler_params=pltpu.CompilerParams(
            dimension_semantics=("arbitrary", "arbitrary"), vmem_limit_bytes=VMEM_LIMIT),
        name="xattn",
    )(h, norm_w.reshape(1, D), xq_w, kt, v, xo_w)


def _mlp_kernel(h_ref, nw_ref, wu_ref, wd_ref, fw_ref, o_ref, abuf):
    h = h_ref[...]
    hn = _rms(h, nw_ref[...]).astype(_bf16)
    for c in range(0, D_FF, FF_CHUNK):
        u = jnp.maximum(_dot(hn, wu_ref[:, c:c + FF_CHUNK]), 0.0)
        abuf[:, c:c + FF_CHUNK] = (u * u).astype(_bf16)
    o_ref[...] = _rms(h + _dot(abuf[...], wd_ref[...]), fw_ref[...])


def _mlp(h, norm_w, up_w, down_w, norm_f_w):
    B, S, D = h.shape
    T = TOK_BLOCK
    h2 = h.reshape(B * S, D)
    out = pl.pallas_call(
        _mlp_kernel,
        grid=(B * S // T,),
        in_specs=[
            pl.BlockSpec((T, D), lambda i: (i, 0)),
            pl.BlockSpec((1, D), lambda i: (0, 0)),
            pl.BlockSpec((D, D_FF), lambda i: (0, 0)),
            pl.BlockSpec((D_FF, D), lambda i: (0, 0)),
            pl.BlockSpec((1, D), lambda i: (0, 0)),
        ],
        out_specs=pl.BlockSpec((T, D), lambda i: (i, 0)),
        out_shape=jax.ShapeDtypeStruct((B * S, D), _f32),
        scratch_shapes=[pltpu.VMEM((T, D_FF), _bf16)],
        compiler_params=pltpu.CompilerParams(
            dimension_semantics=("arbitrary",), vmem_limit_bytes=VMEM_LIMIT),
        name="mlp",
    )(h2, norm_w.reshape(1, D), up_w, down_w, norm_f_w.reshape(1, D))
    return out.reshape(B, S, D)


def kernel(x, mem, positions, norm_mix_w, w_in, conv_w, conv_b, conv_ln_w, conv_ln_b, ret_gn_w, w_out,
           norm_xattn_w, norm_mem_w, xq_w, xkv_w, xo_w, norm_mlp_w, mlp_up_w, mlp_down_w, norm_f_w):
    kt, v = _mem_kv(mem, norm_mem_w, xkv_w)
    h, (xq_b, xo_b, up_b, down_b) = _mixer(x, positions, norm_mix_w, w_in, conv_w, conv_b, conv_ln_w, conv_ln_b,
                                           ret_gn_w, w_out, (xq_w, xo_w, mlp_up_w, mlp_down_w))
    h = _xattn(h, norm_xattn_w, xq_b, kt, v, xo_b)
    return _mlp(h, norm_mlp_w, up_b, down_b, norm_f_w)
```

```python
import numpy as np
import jax
import jax.numpy as jnp
from jax import lax
from jax.experimental import pallas as pl
from jax.experimental.pallas import tpu as pltpu

D_MODEL = 1024
CHUNK = 64
CONV_CH = 512
CONV_WIDTH = 31
RET_HEADS = 8
RET_DV = 64
RET_DK = 32
RET_QK = RET_HEADS * RET_DK
RET_V = RET_HEADS * RET_DV
N_MEM = 256
XATTN_HEADS = 4
XATTN_HEAD_DIM = 256
D_FF = 4096
ROPE_BASE = 10000.0
EPS = 1e-6

OFF_A, OFF_B, OFF_Q, OFF_K, OFF_V, OFF_G = 0, 512, 1024, 1280, 1536, 2048
D_IN = 2560

LANES = 128
SUBLANES = 8
MXU_TILE = 256
MIX_BLOCK = 512
RET_BLOCK = 256
CONV_HALO = 32
CONV_SEG = 128
DFT_N = CONV_HALO + CONV_SEG
N_FREQ = DFT_N // 2 + 1
FREQ_PAD = -(-N_FREQ // SUBLANES) * SUBLANES
CONV_TAIL = MXU_TILE - DFT_N
CONV_ROWS = 32
TOK_BLOCK = 1024
FF_CHUNK = 1024
VMEM_LIMIT = 56 * 1024 * 1024

_f32 = jnp.float32
_bf16 = jnp.bfloat16


def _dot(a, b):
    return jnp.dot(a, b, preferred_element_type=_f32)


def _rms(x, w):
    return x * lax.rsqrt(jnp.mean(x * x, axis=-1, keepdims=True) + EPS) * w


def _sigmoid(x):
    return 1.0 / (1.0 + jnp.exp(-x))


def _cast_columns(src_ref, dst_ref):
    for c in range(0, src_ref.shape[-1], MXU_TILE):
        dst_ref[:, c:c + MXU_TILE] = src_ref[:, c:c + MXU_TILE].astype(_bf16)


def _mem_kv_kernel(mem_ref, nw_ref, wkv_ref, kt_ref, v_ref):
    m = _rms(mem_ref[0], nw_ref[...]).astype(_bf16)
    k = _dot(m, wkv_ref[:, :D_MODEL].astype(_bf16))
    kt_ref[0] = k.T.astype(_bf16)
    v_ref[0] = _dot(m, wkv_ref[:, D_MODEL:].astype(_bf16)).astype(_bf16)


def _mem_kv(mem, norm_mem_w, xkv_w):
    B = mem.shape[0]
    return pl.pallas_call(
        _mem_kv_kernel,
        grid=(B,),
        in_specs=[
            pl.BlockSpec((1, N_MEM, D_MODEL), lambda b: (b, 0, 0)),
            pl.BlockSpec((1, D_MODEL), lambda b: (0, 0)),
            pl.BlockSpec((D_MODEL, 2 * D_MODEL), lambda b: (0, 0)),
        ],
        out_specs=[
            pl.BlockSpec((1, D_MODEL, N_MEM), lambda b: (b, 0, 0)),
            pl.BlockSpec((1, N_MEM, D_MODEL), lambda b: (b, 0, 0)),
        ],
        out_shape=[
            jax.ShapeDtypeStruct((B, D_MODEL, N_MEM), _bf16),
            jax.ShapeDtypeStruct((B, N_MEM, D_MODEL), _bf16),
        ],
        compiler_params=pltpu.CompilerParams(
            dimension_semantics=("arbitrary",), vmem_limit_bytes=VMEM_LIMIT),
        name="mem_kv",
    )(mem, norm_mem_w.reshape(1, D_MODEL), xkv_w)


def _group_norm_halves(y, lo):
    inv = 1.0 / RET_DV
    s_lo = jnp.sum(jnp.where(lo, y, 0.0), axis=-1, keepdims=True)
    s_hi = jnp.sum(jnp.where(lo, 0.0, y), axis=-1, keepdims=True)
    d = y - jnp.where(lo, s_lo, s_hi) * inv
    d2 = d * d
    v_lo = jnp.sum(jnp.where(lo, d2, 0.0), axis=-1, keepdims=True)
    v_hi = jnp.sum(jnp.where(lo, 0.0, d2), axis=-1, keepdims=True)
    return d * lax.rsqrt(jnp.where(lo, v_lo, v_hi) * inv + EPS)


N_CAST = 4
CAST_SCALES = (XATTN_HEAD_DIM ** -0.5, 1.0, 1.0, 1.0)


def _mixer_kernel(*refs):
    n_in = 20
    (x_ref, pos_ref, nw_ref, win32_ref, cw_ref, cb_ref, lnw_ref, lnb_ref, gnw_ref, wout32_ref,
     dmat_ref, xi_ref, zeta_ref, gl_ref, invf_ref, cost_ref, sint_ref, fwd32_ref, inv32_ref, tap_ref) = refs[:n_in]
    cast_src = refs[n_in:n_in + N_CAST]
    o_ref = refs[n_in + N_CAST]
    cast_dst = refs[n_in + 1 + N_CAST:n_in + 1 + 2 * N_CAST]
    win_ref, wout_ref, fwd_ref, inv_ref, hspec, ubuf, pbuf, state, mbuf = refs[n_in + 1 + 2 * N_CAST:]
    L = MIX_BLOCK

    @pl.when((pl.program_id(0) == 0) & (pl.program_id(1) == 0))
    def _():
        _cast_columns(win32_ref, win_ref)
        _cast_columns(wout32_ref, wout_ref)
        fwd_ref[...] = fwd32_ref[...].astype(_bf16)
        inv_ref[...] = inv32_ref[...].astype(_bf16)
        ubuf[CONV_HALO + L:, :] = jnp.zeros((CONV_TAIL, CONV_CH), _bf16)
        hspec[...] = jnp.dot(tap_ref[...], cw_ref[...], preferred_element_type=_f32, precision=lax.Precision.HIGHEST)

    @pl.when(pl.program_id(1) == 0)
    def _():
        ubuf[0:CONV_HALO, :] = jnp.zeros((CONV_HALO, CONV_CH), _bf16)
        state[...] = jnp.zeros_like(state)

    for src, dst, scale in zip(cast_src, cast_dst, CAST_SCALES):
        dst[...] = (src[...] if scale == 1.0 else src[...] * scale).astype(_bf16)

    x = x_ref[0]
    hn = _rms(x, nw_ref[...]).astype(_bf16)

    a = _dot(hn, win_ref[:, OFF_A:OFF_A + CONV_CH])
    b = _dot(hn, win_ref[:, OFF_B:OFF_B + CONV_CH])
    ubuf[CONV_HALO:CONV_HALO + L, :] = (a * _sigmoid(b)).astype(_bf16)
    h_re, h_im = hspec[0:FREQ_PAD, :], hspec[FREQ_PAD:2 * FREQ_PAD, :]

    def conv_segment(row0):
        spec = _dot(fwd_ref[...], ubuf[row0:row0 + MXU_TILE, :])
        s_re, s_im = spec[0:FREQ_PAD, :], spec[FREQ_PAD:2 * FREQ_PAD, :]
        prod = jnp.concatenate([s_re * h_re - s_im * h_im, s_re * h_im + s_im * h_re,
                                jnp.zeros((MXU_TILE - 2 * FREQ_PAD, CONV_CH), _f32)], axis=0)
        y = _dot(inv_ref[...], prod.astype(_bf16))
        for r in range(0, CONV_SEG, CONV_ROWS):
            acc = y[r:r + CONV_ROWS, :] + cb_ref[...]
            mu = jnp.mean(acc, axis=-1, keepdims=True)
            d = acc - mu
            var = jnp.mean(d * d, axis=-1, keepdims=True)
            yn = d * lax.rsqrt(var + EPS) * lnw_ref[...] + lnb_ref[...]
            mbuf[row0 + r:row0 + r + CONV_ROWS, 0:CONV_CH] = (yn * _sigmoid(yn)).astype(_bf16)

    for i in range(L // CONV_SEG):
        conv_segment(i * CONV_SEG)
    ubuf[0:CONV_HALO, :] = ubuf[L:L + CONV_HALO, :]

    for c in range(OFF_Q, D_IN, MXU_TILE):
        pbuf[:, c - OFF_Q:c - OFF_Q + MXU_TILE] = _dot(hn, win_ref[:, c:c + MXU_TILE])
    ang0 = pos_ref[0, :, 0:1].astype(_f32) * invf_ref[...]
    cos0, sin0 = jnp.cos(ang0), jnp.sin(ang0)
    lane = lax.broadcasted_iota(jnp.int32, (1, RET_QK), 1)
    first_half = lane % RET_DK < RET_DK // 2
    q_head = lane // RET_DK
    lo = lax.broadcasted_iota(jnp.int32, (1, LANES), 1) < RET_DV
    row_head = lax.broadcasted_iota(jnp.int32, (RET_QK, RET_V), 0) // RET_DK
    col_head = lax.broadcasted_iota(jnp.int32, (RET_QK, RET_V), 1) // RET_DV

    for r0 in range(0, L, RET_BLOCK):
        rows = slice(r0, r0 + RET_BLOCK)
        cos = cos0 * cost_ref[rows, :] - sin0 * sint_ref[rows, :]
        sin = sin0 * cost_ref[rows, :] + cos0 * sint_ref[rows, :]
        cos2 = jnp.concatenate([cos, cos], axis=-1)
        sin2 = jnp.where(first_half, -1.0, 1.0) * jnp.concatenate([sin, sin], axis=-1)

        def rotary(t):
            partner = jnp.where(first_half, pltpu.roll(t, RET_QK - RET_DK // 2, 1), pltpu.roll(t, RET_DK // 2, 1))
            return t * cos2 + partner * sin2

        qr = rotary(pbuf[rows, 0:OFF_K - OFF_Q])
        kr = rotary(pbuf[rows, OFF_K - OFF_Q:OFF_V - OFF_Q])
        v = pbuf[rows, OFF_V - OFF_Q:OFF_G - OFF_Q]
        kt = kr.T.astype(_bf16)
        vb = v.astype(_bf16)

        yx = _dot(qr.astype(_bf16), state[...].astype(_bf16)) * xi_ref[...]

        for p in range(RET_HEADS // 2):
            cols = slice(p * LANES, (p + 1) * LANES)
            vp = vb[:, cols]
            ys = []
            for h in (2 * p, 2 * p + 1):
                qm = jnp.where(q_head == h, qr, 0.0).astype(_bf16)
                s = _dot(qm, kt)
                ys.append(_dot((s * dmat_ref[h]).astype(_bf16), vp))
            y = jnp.where(lo, ys[0], ys[1]) + yx[:, cols]
            yn = _group_norm_halves(y, lo) * gnw_ref[:, cols]
            gp = pbuf[rows, OFF_G - OFF_Q + p * LANES:OFF_G - OFF_Q + (p + 1) * LANES]
            mbuf[rows, CONV_CH + p * LANES:CONV_CH + (p + 1) * LANES] = (gp * _sigmoid(gp) * yn).astype(_bf16)

        kv = _dot(kt, (v * zeta_ref[...]).astype(_bf16))
        state[...] = gl_ref[...] * state[...] + jnp.where(row_head == col_head, kv, 0.0)

    o_ref[0] = x + _dot(mbuf[...], wout_ref[...])


def _retention_tables(L, step_rows):
    h = np.arange(RET_HEADS, dtype=np.float64)
    log_g = np.log1p(-np.exp2(-5.0 - h))
    idx = np.arange(L, dtype=np.float64)
    dist = np.abs(idx[:, None] - idx[None, :])
    visible = (idx[None, :] // CHUNK) <= (idx[:, None] // CHUNK)
    dmat = np.where(visible[None], np.exp(log_g[:, None, None] * dist[None]), 0.0)
    xi = np.exp(log_g[None, :] * (idx[:, None] + 1.0))
    zeta = np.exp(log_g[None, :] * (L - 1.0 - idx[:, None]))
    gl = np.exp(log_g * L)
    rep = lambda t: np.repeat(t, RET_DV, axis=-1)
    inv_freq = ROPE_BASE ** (-np.arange(RET_DK // 2, dtype=np.float32) / np.float32(RET_DK // 2))
    invf = np.tile(inv_freq.astype(np.float32), LANES // (RET_DK // 2))[None, :]
    rel = np.arange(step_rows, dtype=np.float64)[:, None] * invf.astype(np.float64)
    f = lambda t: jnp.asarray(t, dtype=_f32)
    scale = RET_DK ** -0.5
    return (f(dmat * scale), f(rep(xi) * scale), f(rep(zeta)), f(rep(gl[None, :])), f(invf),
            f(np.cos(rel)), f(np.sin(rel)))


def _conv_dft_tables():
    n = np.arange(DFT_N, dtype=np.float64)
    f = np.arange(N_FREQ, dtype=np.float64)[:, None]
    w = 2.0 * np.pi / DFT_N
    fwd = np.zeros((2 * FREQ_PAD, MXU_TILE))
    fwd[:N_FREQ, :DFT_N] = np.cos(w * f * n[None, :])
    fwd[FREQ_PAD:FREQ_PAD + N_FREQ, :DFT_N] = -np.sin(w * f * n[None, :])
    delay = (CONV_WIDTH - 1) - np.arange(CONV_WIDTH, dtype=np.float64)[None, :]
    tap = np.zeros((2 * FREQ_PAD, CONV_HALO))
    tap[:N_FREQ, :CONV_WIDTH] = np.cos(w * f * delay)
    tap[FREQ_PAD:FREQ_PAD + N_FREQ, :CONV_WIDTH] = -np.sin(w * f * delay)
    weight = np.full((1, N_FREQ), 2.0)
    weight[0, 0] = weight[0, -1] = 1.0
    out = n[CONV_HALO:, None]
    inv = np.zeros((CONV_SEG, MXU_TILE))
    inv[:, :N_FREQ] = weight * np.cos(w * out * f.T) / DFT_N
    inv[:, FREQ_PAD:FREQ_PAD + N_FREQ] = -weight * np.sin(w * out * f.T) / DFT_N
    return [jnp.asarray(t, dtype=_f32) for t in (fwd, inv, tap)]


def _mixer(x, positions, norm_w, w_in, conv_w, conv_b, ln_w, ln_b, gn_w, w_out, later_weights):
    B, S, D = x.shape
    L = MIX_BLOCK
    n = S // L
    steps = B * n
    R = RET_BLOCK
    dmat, xi, zeta, gl, invf, cost, sint = _retention_tables(R, L)
    fwd, inv, tap = _conv_dft_tables()
    taps = jnp.pad(conv_w, ((0, CONV_HALO - CONV_WIDTH), (0, 0)))
    const = lambda shape: pl.BlockSpec(shape, lambda b, j: (0,) * len(shape))
    once = lambda shape: pl.BlockSpec(shape, lambda b, j: (0,) * len(shape), pipeline_mode=pl.Buffered(1))
    row = lambda n: pl.BlockSpec((1, n), lambda b, j: (0, 0))
    slabs = [w.reshape(steps, w.shape[0] // steps, w.shape[1]) for w in later_weights]
    slab_spec = lambda w: pl.BlockSpec((1,) + w.shape[1:], lambda b, j: (b * n + j, 0, 0))
    outs = pl.pallas_call(
        _mixer_kernel,
        grid=(B, n),
        in_specs=[
            pl.BlockSpec((1, L, D), lambda b, j: (b, j, 0)),
            pl.BlockSpec((1, 1, L), lambda b, j: (b * n + j, 0, 0)),
            row(D),
            once((D, D_IN)),
            const((CONV_HALO, CONV_CH)),
            row(CONV_CH), row(CONV_CH), row(CONV_CH), row(RET_V),
            once((D, D)),
            const((RET_HEADS, R, R)),
            const((R, RET_V)), const((R, RET_V)), row(RET_V), row(LANES),
            const((L, LANES)), const((L, LANES)),
            const(fwd.shape), const(inv.shape), const(tap.shape),
        ] + [slab_spec(w) for w in slabs],
        out_specs=[pl.BlockSpec((1, L, D), lambda b, j: (b, j, 0))] + [slab_spec(w) for w in slabs],
        out_shape=[jax.ShapeDtypeStruct((B, S, D), _f32)] + [jax.ShapeDtypeStruct(w.shape, _bf16) for w in slabs],
        scratch_shapes=[
            pltpu.VMEM((D, D_IN), _bf16),
            pltpu.VMEM((D, D), _bf16),
            pltpu.VMEM(fwd.shape, _bf16),
            pltpu.VMEM(inv.shape, _bf16),
            pltpu.VMEM((2 * FREQ_PAD, CONV_CH), _f32),
            pltpu.VMEM((CONV_HALO + L + CONV_TAIL, CONV_CH), _bf16),
            pltpu.VMEM((L, D_IN - OFF_Q), _f32),
            pltpu.VMEM((RET_QK, RET_V), _f32),
            pltpu.VMEM((L, D), _bf16),
        ],
        compiler_params=pltpu.CompilerParams(
            dimension_semantics=("arbitrary", "arbitrary"), vmem_limit_bytes=VMEM_LIMIT),
        name="mixer",
    )(x, positions.reshape(steps, 1, L), norm_w.reshape(1, D), w_in, taps, conv_b.reshape(1, -1),
      ln_w.reshape(1, -1), ln_b.reshape(1, -1), gn_w.reshape(1, -1), w_out, dmat, xi, zeta, gl, invf, cost, sint,
      fwd, inv, tap, *slabs)
    return outs[0], [o.reshape(w.shape) for o, w in zip(outs[1:], later_weights)]


def _xattn_kernel(h_ref, nw_ref, wq_ref, kt_ref, v_ref, wo_ref, o_ref, obuf):
    h = h_ref[0]
    hn = _rms(h, nw_ref[...]).astype(_bf16)
    q = _dot(hn, wq_ref[...]).astype(_bf16)
    for i in range(XATTN_HEADS):
        cols = slice(i * XATTN_HEAD_DIM, (i + 1) * XATTN_HEAD_DIM)
        s = _dot(q[:, cols], kt_ref[0, cols, :])
        e = jnp.exp(s - jnp.max(s, axis=-1, keepdims=True))
        o = _dot(e.astype(_bf16), v_ref[0, :, cols])
        obuf[:, cols] = (o * (1.0 / jnp.sum(e, axis=-1, keepdims=True))).astype(_bf16)
    o_ref[0] = h + _dot(obuf[...], wo_ref[...])


def _xattn(h, norm_w, xq_w, kt, v, xo_w):
    B, S, D = h.shape
    T = TOK_BLOCK
    return pl.pallas_call(
        _xattn_kernel,
        grid=(B, S // T),
        in_specs=[
            pl.BlockSpec((1, T, D), lambda b, j: (b, j, 0)),
            pl.BlockSpec((1, D), lambda b, j: (0, 0)),
            pl.BlockSpec((D, D), lambda b, j: (0, 0)),
            pl.BlockSpec((1, D, N_MEM), lambda b, j: (b, 0, 0)),
            pl.BlockSpec((1, N_MEM, D), lambda b, j: (b, 0, 0)),
            pl.BlockSpec((D, D), lambda b, j: (0, 0)),
        ],
        out_specs=pl.BlockSpec((1, T, D), lambda b, j: (b, j, 0)),
        out_shape=jax.ShapeDtypeStruct((B, S, D), _f32),
        scratch_shapes=[pltpu.VMEM((T, D), _bf16)],
        compiler_params=pltpu.CompilerParams(
            dimension_semantics=("arbitrary", "arbitrary"), vmem_limit_bytes=VMEM_LIMIT),
        name="xattn",
    )(h, norm_w.reshape(1, D), xq_w, kt, v, xo_w)


def _mlp_kernel(h_ref, nw_ref, wu_ref, wd_ref, fw_ref, o_ref, abuf):
    h = h_ref[...]
    hn = _rms(h, nw_ref[...]).astype(_bf16)
    for c in range(0, D_FF, FF_CHUNK):
        u = jnp.maximum(_dot(hn, wu_ref[:, c:c + FF_CHUNK]), 0.0)
        abuf[:, c:c + FF_CHUNK] = (u * u).astype(_bf16)
    o_ref[...] = _rms(h + _dot(abuf[...], wd_ref[...]), fw_ref[...])


def _mlp(h, norm_w, up_w, down_w, norm_f_w):
    B, S, D = h.shape
    T = TOK_BLOCK
    h2 = h.reshape(B * S, D)
    out = pl.pallas_call(
        _mlp_kernel,
        grid=(B * S // T,),
        in_specs=[
            pl.BlockSpec((T, D), lambda i: (i, 0)),
            pl.BlockSpec((1, D), lambda i: (0, 0)),
            pl.BlockSpec((D, D_FF), lambda i: (0, 0), pipeline_mode=pl.Buffered(1)),
            pl.BlockSpec((D_FF, D), lambda i: (0, 0), pipeline_mode=pl.Buffered(1)),
            pl.BlockSpec((1, D), lambda i: (0, 0)),
        ],
        out_specs=pl.BlockSpec((T, D), lambda i: (i, 0)),
        out_shape=jax.ShapeDtypeStruct((B * S, D), _f32),
        scratch_shapes=[pltpu.VMEM((T, D_FF), _bf16)],
        compiler_params=pltpu.CompilerParams(
            dimension_semantics=("arbitrary",), vmem_limit_bytes=VMEM_LIMIT),
        name="mlp",
    )(h2, norm_w.reshape(1, D), up_w, down_w, norm_f_w.reshape(1, D))
    return out.reshape(B, S, D)


def kernel(x, mem, positions, norm_mix_w, w_in, conv_w, conv_b, conv_ln_w, conv_ln_b, ret_gn_w, w_out,
           norm_xattn_w, norm_mem_w, xq_w, xkv_w, xo_w, norm_mlp_w, mlp_up_w, mlp_down_w, norm_f_w):
    kt, v = _mem_kv(mem, norm_mem_w, xkv_w)
    h, (xq_b, xo_b, up_b, down_b) = _mixer(x, positions, norm_mix_w, w_in, conv_w, conv_b, conv_ln_w, conv_ln_b,
                                           ret_gn_w, w_out, (xq_w, xo_w, mlp_up_w, mlp_down_w))
    h = _xattn(h, norm_xattn_w, xq_b, kt, v, xo_b)
    return _mlp(h, norm_mlp_w, up_b, down_b, norm_f_w)
```

```python
import numpy as np
import jax
import jax.numpy as jnp
from jax import lax
from jax.experimental import pallas as pl
from jax.experimental.pallas import tpu as pltpu

D_MODEL = 1024
CHUNK = 64
CONV_CH = 512
CONV_WIDTH = 31
RET_HEADS = 8
RET_DV = 64
RET_DK = 32
RET_QK = RET_HEADS * RET_DK
RET_V = RET_HEADS * RET_DV
N_MEM = 256
XATTN_HEADS = 4
XATTN_HEAD_DIM = 256
D_FF = 4096
ROPE_BASE = 10000.0
EPS = 1e-6

OFF_A, OFF_B, OFF_Q, OFF_K, OFF_V, OFF_G = 0, 512, 1024, 1280, 1536, 2048
D_IN = 2560

LANES = 128
SUBLANES = 8
MXU_TILE = 256
MIX_BLOCK = 1024
RET_BLOCK = 256
CONV_HALO = 32
CONV_SEG = 128
DFT_N = CONV_HALO + CONV_SEG
N_FREQ = DFT_N // 2 + 1
FREQ_PAD = -(-N_FREQ // SUBLANES) * SUBLANES
CONV_TAIL = MXU_TILE - DFT_N
CONV_ROWS = 32
TOK_BLOCK = 1024
FF_CHUNK = 1024
VMEM_LIMIT = 56 * 1024 * 1024

_f32 = jnp.float32
_bf16 = jnp.bfloat16


def _dot(a, b):
    return jnp.dot(a, b, preferred_element_type=_f32)


def _rms(x, w):
    return x * lax.rsqrt(jnp.mean(x * x, axis=-1, keepdims=True) + EPS) * w


def _sigmoid(x):
    return 1.0 / (1.0 + jnp.exp(-x))


def _cast_columns(src_ref, dst_ref):
    for c in range(0, src_ref.shape[-1], MXU_TILE):
        dst_ref[:, c:c + MXU_TILE] = src_ref[:, c:c + MXU_TILE].astype(_bf16)


def _mem_kv_kernel(mem_ref, nw_ref, wkv_ref, win32_ref, wout32_ref, kt_ref, v_ref, win_ref, wout_ref):
    m = _rms(mem_ref[0], nw_ref[...]).astype(_bf16)
    k = _dot(m, wkv_ref[:, :D_MODEL].astype(_bf16))
    kt_ref[0] = k.T.astype(_bf16)
    v_ref[0] = _dot(m, wkv_ref[:, D_MODEL:].astype(_bf16)).astype(_bf16)
    _cast_columns(win32_ref.at[0], win_ref.at[0])
    _cast_columns(wout32_ref.at[0], wout_ref.at[0])


def _mem_kv(mem, norm_mem_w, xkv_w, w_in, w_out):
    B = mem.shape[0]
    slab = lambda w: w.reshape(B, w.shape[0] // B, w.shape[1])
    slab_spec = lambda w: pl.BlockSpec((1, w.shape[0] // B, w.shape[1]), lambda b: (b, 0, 0))
    kt, v, win_b, wout_b = pl.pallas_call(
        _mem_kv_kernel,
        grid=(B,),
        in_specs=[
            pl.BlockSpec((1, N_MEM, D_MODEL), lambda b: (b, 0, 0)),
            pl.BlockSpec((1, D_MODEL), lambda b: (0, 0)),
            pl.BlockSpec((D_MODEL, 2 * D_MODEL), lambda b: (0, 0)),
            slab_spec(w_in), slab_spec(w_out),
        ],
        out_specs=[
            pl.BlockSpec((1, D_MODEL, N_MEM), lambda b: (b, 0, 0)),
            pl.BlockSpec((1, N_MEM, D_MODEL), lambda b: (b, 0, 0)),
            slab_spec(w_in), slab_spec(w_out),
        ],
        out_shape=[
            jax.ShapeDtypeStruct((B, D_MODEL, N_MEM), _bf16),
            jax.ShapeDtypeStruct((B, N_MEM, D_MODEL), _bf16),
            jax.ShapeDtypeStruct(slab(w_in).shape, _bf16),
            jax.ShapeDtypeStruct(slab(w_out).shape, _bf16),
        ],
        compiler_params=pltpu.CompilerParams(
            dimension_semantics=("arbitrary",), vmem_limit_bytes=VMEM_LIMIT),
        name="mem_kv",
    )(mem, norm_mem_w.reshape(1, D_MODEL), xkv_w, slab(w_in), slab(w_out))
    return kt, v, win_b.reshape(w_in.shape), wout_b.reshape(w_out.shape)


def _group_norm_halves(y, lo):
    inv = 1.0 / RET_DV
    s_lo = jnp.sum(jnp.where(lo, y, 0.0), axis=-1, keepdims=True)
    s_hi = jnp.sum(jnp.where(lo, 0.0, y), axis=-1, keepdims=True)
    d = y - jnp.where(lo, s_lo, s_hi) * inv
    d2 = d * d
    v_lo = jnp.sum(jnp.where(lo, d2, 0.0), axis=-1, keepdims=True)
    v_hi = jnp.sum(jnp.where(lo, 0.0, d2), axis=-1, keepdims=True)
    return d * lax.rsqrt(jnp.where(lo, v_lo, v_hi) * inv + EPS)


N_CAST = 4
CAST_SCALES = (XATTN_HEAD_DIM ** -0.5, 1.0, 1.0, 1.0)


def _mixer_kernel(*refs):
    n_in = 20
    (x_ref, pos_ref, nw_ref, win_ref, cw_ref, cb_ref, lnw_ref, lnb_ref, gnw_ref, wout_ref,
     dmat_ref, xi_ref, zeta_ref, gl_ref, invf_ref, cost_ref, sint_ref, fwd32_ref, inv32_ref, tap_ref) = refs[:n_in]
    cast_src = refs[n_in:n_in + N_CAST]
    o_ref = refs[n_in + N_CAST]
    cast_dst = refs[n_in + 1 + N_CAST:n_in + 1 + 2 * N_CAST]
    fwd_ref, inv_ref, hspec, ubuf, pbuf, state, mbuf = refs[n_in + 1 + 2 * N_CAST:]
    L = MIX_BLOCK

    @pl.when((pl.program_id(0) == 0) & (pl.program_id(1) == 0))
    def _():
        fwd_ref[...] = fwd32_ref[...].astype(_bf16)
        inv_ref[...] = inv32_ref[...].astype(_bf16)
        ubuf[CONV_HALO + L:, :] = jnp.zeros((CONV_TAIL, CONV_CH), _bf16)
        hspec[...] = jnp.dot(tap_ref[...], cw_ref[...], preferred_element_type=_f32, precision=lax.Precision.HIGHEST)

    @pl.when(pl.program_id(1) == 0)
    def _():
        ubuf[0:CONV_HALO, :] = jnp.zeros((CONV_HALO, CONV_CH), _bf16)
        state[...] = jnp.zeros_like(state)

    for src, dst, scale in zip(cast_src, cast_dst, CAST_SCALES):
        dst[...] = (src[...] if scale == 1.0 else src[...] * scale).astype(_bf16)

    x = x_ref[0]
    hn = _rms(x, nw_ref[...]).astype(_bf16)

    a = _dot(hn, win_ref[:, OFF_A:OFF_A + CONV_CH])
    b = _dot(hn, win_ref[:, OFF_B:OFF_B + CONV_CH])
    ubuf[CONV_HALO:CONV_HALO + L, :] = (a * _sigmoid(b)).astype(_bf16)
    h_re, h_im = hspec[0:FREQ_PAD, :], hspec[FREQ_PAD:2 * FREQ_PAD, :]

    def conv_segment(row0):
        spec = _dot(fwd_ref[...], ubuf[row0:row0 + MXU_TILE, :])
        s_re, s_im = spec[0:FREQ_PAD, :], spec[FREQ_PAD:2 * FREQ_PAD, :]
        prod = jnp.concatenate([s_re * h_re - s_im * h_im, s_re * h_im + s_im * h_re,
                                jnp.zeros((MXU_TILE - 2 * FREQ_PAD, CONV_CH), _f32)], axis=0)
        y = _dot(inv_ref[...], prod.astype(_bf16))
        for r in range(0, CONV_SEG, CONV_ROWS):
            acc = y[r:r + CONV_ROWS, :] + cb_ref[...]
            mu = jnp.mean(acc, axis=-1, keepdims=True)
            d = acc - mu
            var = jnp.mean(d * d, axis=-1, keepdims=True)
            yn = d * lax.rsqrt(var + EPS) * lnw_ref[...] + lnb_ref[...]
            mbuf[row0 + r:row0 + r + CONV_ROWS, 0:CONV_CH] = (yn * _sigmoid(yn)).astype(_bf16)

    for i in range(L // CONV_SEG):
        conv_segment(i * CONV_SEG)
    ubuf[0:CONV_HALO, :] = ubuf[L:L + CONV_HALO, :]

    for c in range(OFF_Q, D_IN, MXU_TILE):
        pbuf[:, c - OFF_Q:c - OFF_Q + MXU_TILE] = _dot(hn, win_ref[:, c:c + MXU_TILE])
    ang0 = pos_ref[0, :, 0:1].astype(_f32) * invf_ref[...]
    cos0, sin0 = jnp.cos(ang0), jnp.sin(ang0)
    lane = lax.broadcasted_iota(jnp.int32, (1, RET_QK), 1)
    first_half = lane % RET_DK < RET_DK // 2
    q_head = lane // RET_DK
    lo = lax.broadcasted_iota(jnp.int32, (1, LANES), 1) < RET_DV
    row_head = lax.broadcasted_iota(jnp.int32, (RET_QK, RET_V), 0) // RET_DK
    col_head = lax.broadcasted_iota(jnp.int32, (RET_QK, RET_V), 1) // RET_DV

    for r0 in range(0, L, RET_BLOCK):
        rows = slice(r0, r0 + RET_BLOCK)
        cos = cos0 * cost_ref[rows, :] - sin0 * sint_ref[rows, :]
        sin = sin0 * cost_ref[rows, :] + cos0 * sint_ref[rows, :]
        cos2 = jnp.concatenate([cos, cos], axis=-1)
        sin2 = jnp.where(first_half, -1.0, 1.0) * jnp.concatenate([sin, sin], axis=-1)

        def rotary(t):
            partner = jnp.where(first_half, pltpu.roll(t, RET_QK - RET_DK // 2, 1), pltpu.roll(t, RET_DK // 2, 1))
            return t * cos2 + partner * sin2

        qr = rotary(pbuf[rows, 0:OFF_K - OFF_Q])
        kr = rotary(pbuf[rows, OFF_K - OFF_Q:OFF_V - OFF_Q])
        v = pbuf[rows, OFF_V - OFF_Q:OFF_G - OFF_Q]
        kt = kr.T.astype(_bf16)
        vb = v.astype(_bf16)

        yx = _dot(qr.astype(_bf16), state[...].astype(_bf16)) * xi_ref[...]

        for p in range(RET_HEADS // 2):
            cols = slice(p * LANES, (p + 1) * LANES)
            vp = vb[:, cols]
            ys = []
            for h in (2 * p, 2 * p + 1):
                qm = jnp.where(q_head == h, qr, 0.0).astype(_bf16)
                s = _dot(qm, kt)
                ys.append(_dot((s * dmat_ref[h]).astype(_bf16), vp))
            y = jnp.where(lo, ys[0], ys[1]) + yx[:, cols]
            yn = _group_norm_halves(y, lo) * gnw_ref[:, cols]
            gp = pbuf[rows, OFF_G - OFF_Q + p * LANES:OFF_G - OFF_Q + (p + 1) * LANES]
            mbuf[rows, CONV_CH + p * LANES:CONV_CH + (p + 1) * LANES] = (gp * _sigmoid(gp) * yn).astype(_bf16)

        kv = _dot(kt, (v * zeta_ref[...]).astype(_bf16))
        state[...] = gl_ref[...] * state[...] + jnp.where(row_head == col_head, kv, 0.0)

    o_ref[0] = x + _dot(mbuf[...], wout_ref[...])


def _retention_tables(L, step_rows):
    h = np.arange(RET_HEADS, dtype=np.float64)
    log_g = np.log1p(-np.exp2(-5.0 - h))
    idx = np.arange(L, dtype=np.float64)
    dist = np.abs(idx[:, None] - idx[None, :])
    visible = (idx[None, :] // CHUNK) <= (idx[:, None] // CHUNK)
    dmat = np.where(visible[None], np.exp(log_g[:, None, None] * dist[None]), 0.0)
    xi = np.exp(log_g[None, :] * (idx[:, None] + 1.0))
    zeta = np.exp(log_g[None, :] * (L - 1.0 - idx[:, None]))
    gl = np.exp(log_g * L)
    rep = lambda t: np.repeat(t, RET_DV, axis=-1)
    inv_freq = ROPE_BASE ** (-np.arange(RET_DK // 2, dtype=np.float32) / np.float32(RET_DK // 2))
    invf = np.tile(inv_freq.astype(np.float32), LANES // (RET_DK // 2))[None, :]
    rel = np.arange(step_rows, dtype=np.float64)[:, None] * invf.astype(np.float64)
    f = lambda t: jnp.asarray(t, dtype=_f32)
    scale = RET_DK ** -0.5
    return (f(dmat * scale), f(rep(xi) * scale), f(rep(zeta)), f(rep(gl[None, :])), f(invf),
            f(np.cos(rel)), f(np.sin(rel)))


def _conv_dft_tables():
    n = np.arange(DFT_N, dtype=np.float64)
    f = np.arange(N_FREQ, dtype=np.float64)[:, None]
    w = 2.0 * np.pi / DFT_N
    fwd = np.zeros((2 * FREQ_PAD, MXU_TILE))
    fwd[:N_FREQ, :DFT_N] = np.cos(w * f * n[None, :])
    fwd[FREQ_PAD:FREQ_PAD + N_FREQ, :DFT_N] = -np.sin(w * f * n[None, :])
    delay = (CONV_WIDTH - 1) - np.arange(CONV_WIDTH, dtype=np.float64)[None, :]
    tap = np.zeros((2 * FREQ_PAD, CONV_HALO))
    tap[:N_FREQ, :CONV_WIDTH] = np.cos(w * f * delay)
    tap[FREQ_PAD:FREQ_PAD + N_FREQ, :CONV_WIDTH] = -np.sin(w * f * delay)
    weight = np.full((1, N_FREQ), 2.0)
    weight[0, 0] = weight[0, -1] = 1.0
    out = n[CONV_HALO:, None]
    inv = np.zeros((CONV_SEG, MXU_TILE))
    inv[:, :N_FREQ] = weight * np.cos(w * out * f.T) / DFT_N
    inv[:, FREQ_PAD:FREQ_PAD + N_FREQ] = -weight * np.sin(w * out * f.T) / DFT_N
    return [jnp.asarray(t, dtype=_f32) for t in (fwd, inv, tap)]


def _mixer(x, positions, norm_w, w_in, conv_w, conv_b, ln_w, ln_b, gn_w, w_out, later_weights):
    B, S, D = x.shape
    L = MIX_BLOCK
    n = S // L
    steps = B * n
    R = RET_BLOCK
    dmat, xi, zeta, gl, invf, cost, sint = _retention_tables(R, L)
    fwd, inv, tap = _conv_dft_tables()
    taps = jnp.pad(conv_w, ((0, CONV_HALO - CONV_WIDTH), (0, 0)))
    const = lambda shape: pl.BlockSpec(shape, lambda b, j: (0,) * len(shape))
    once = lambda shape: pl.BlockSpec(shape, lambda b, j: (0,) * len(shape), pipeline_mode=pl.Buffered(1))
    row = lambda n: pl.BlockSpec((1, n), lambda b, j: (0, 0))
    slabs = [w.reshape(steps, w.shape[0] // steps, w.shape[1]) for w in later_weights]
    slab_spec = lambda w: pl.BlockSpec((1,) + w.shape[1:], lambda b, j: (b * n + j, 0, 0))
    outs = pl.pallas_call(
        _mixer_kernel,
        grid=(B, n),
        in_specs=[
            pl.BlockSpec((1, L, D), lambda b, j: (b, j, 0)),
            pl.BlockSpec((1, 1, L), lambda b, j: (b * n + j, 0, 0)),
            row(D),
            once((D, D_IN)),
            const((CONV_HALO, CONV_CH)),
            row(CONV_CH), row(CONV_CH), row(CONV_CH), row(RET_V),
            once((D, D)),
            const((RET_HEADS, R, R)),
            const((R, RET_V)), const((R, RET_V)), row(RET_V), row(LANES),
            const((L, LANES)), const((L, LANES)),
            const(fwd.shape), const(inv.shape), const(tap.shape),
        ] + [slab_spec(w) for w in slabs],
        out_specs=[pl.BlockSpec((1, L, D), lambda b, j: (b, j, 0))] + [slab_spec(w) for w in slabs],
        out_shape=[jax.ShapeDtypeStruct((B, S, D), _f32)] + [jax.ShapeDtypeStruct(w.shape, _bf16) for w in slabs],
        scratch_shapes=[
            pltpu.VMEM(fwd.shape, _bf16),
            pltpu.VMEM(inv.shape, _bf16),
            pltpu.VMEM((2 * FREQ_PAD, CONV_CH), _f32),
            pltpu.VMEM((CONV_HALO + L + CONV_TAIL, CONV_CH), _bf16),
            pltpu.VMEM((L, D_IN - OFF_Q), _f32),
            pltpu.VMEM((RET_QK, RET_V), _f32),
            pltpu.VMEM((L, D), _bf16),
        ],
        compiler_params=pltpu.CompilerParams(
            dimension_semantics=("arbitrary", "arbitrary"), vmem_limit_bytes=VMEM_LIMIT),
        name="mixer",
    )(x, positions.reshape(steps, 1, L), norm_w.reshape(1, D), w_in, taps, conv_b.reshape(1, -1),
      ln_w.reshape(1, -1), ln_b.reshape(1, -1), gn_w.reshape(1, -1), w_out, dmat, xi, zeta, gl, invf, cost, sint,
      fwd, inv, tap, *slabs)
    return outs[0], [o.reshape(w.shape) for o, w in zip(outs[1:], later_weights)]


def _xattn_kernel(h_ref, nw_ref, wq_ref, kt_ref, v_ref, wo_ref, o_ref, obuf):
    h = h_ref[0]
    hn = _rms(h, nw_ref[...]).astype(_bf16)
    q = _dot(hn, wq_ref[...]).astype(_bf16)
    for i in range(XATTN_HEADS):
        cols = slice(i * XATTN_HEAD_DIM, (i + 1) * XATTN_HEAD_DIM)
        s = _dot(q[:, cols], kt_ref[0, cols, :])
        e = jnp.exp(s - jnp.max(s, axis=-1, keepdims=True))
        o = _dot(e.astype(_bf16), v_ref[0, :, cols])
        obuf[:, cols] = (o * (1.0 / jnp.sum(e, axis=-1, keepdims=True))).astype(_bf16)
    o_ref[0] = h + _dot(obuf[...], wo_ref[...])


def _xattn(h, norm_w, xq_w, kt, v, xo_w):
    B, S, D = h.shape
    T = TOK_BLOCK
    return pl.pallas_call(
        _xattn_kernel,
        grid=(B, S // T),
        in_specs=[
            pl.BlockSpec((1, T, D), lambda b, j: (b, j, 0)),
            pl.BlockSpec((1, D), lambda b, j: (0, 0)),
            pl.BlockSpec((D, D), lambda b, j: (0, 0)),
            pl.BlockSpec((1, D, N_MEM), lambda b, j: (b, 0, 0)),
            pl.BlockSpec((1, N_MEM, D), lambda b, j: (b, 0, 0)),
            pl.BlockSpec((D, D), lambda b, j: (0, 0)),
        ],
        out_specs=pl.BlockSpec((1, T, D), lambda b, j: (b, j, 0)),
        out_shape=jax.ShapeDtypeStruct((B, S, D), _f32),
        scratch_shapes=[pltpu.VMEM((T, D), _bf16)],
        compiler_params=pltpu.CompilerParams(
            dimension_semantics=("arbitrary", "arbitrary"), vmem_limit_bytes=VMEM_LIMIT),
        name="xattn",
    )(h, norm_w.reshape(1, D), xq_w, kt, v, xo_w)


def _mlp_kernel(h_ref, nw_ref, wu_ref, wd_ref, fw_ref, o_ref, abuf):
    h = h_ref[...]
    hn = _rms(h, nw_ref[...]).astype(_bf16)
    for c in range(0, D_FF, FF_CHUNK):
        u = jnp.maximum(_dot(hn, wu_ref[:, c:c + FF_CHUNK]), 0.0)
        abuf[:, c:c + FF_CHUNK] = (u * u).astype(_bf16)
    o_ref[...] = _rms(h + _dot(abuf[...], wd_ref[...]), fw_ref[...])


def _mlp(h, norm_w, up_w, down_w, norm_f_w):
    B, S, D = h.shape
    T = TOK_BLOCK
    h2 = h.reshape(B * S, D)
    out = pl.pallas_call(
        _mlp_kernel,
        grid=(B * S // T,),
        in_specs=[
            pl.BlockSpec((T, D), lambda i: (i, 0)),
            pl.BlockSpec((1, D), lambda i: (0, 0)),
            pl.BlockSpec((D, D_FF), lambda i: (0, 0), pipeline_mode=pl.Buffered(1)),
            pl.BlockSpec((D_FF, D), lambda i: (0, 0), pipeline_mode=pl.Buffered(1)),
            pl.BlockSpec((1, D), lambda i: (0, 0)),
        ],
        out_specs=pl.BlockSpec((T, D), lambda i: (i, 0)),
        out_shape=jax.ShapeDtypeStruct((B * S, D), _f32),
        scratch_shapes=[pltpu.VMEM((T, D_FF), _bf16)],
        compiler_params=pltpu.CompilerParams(
            dimension_semantics=("arbitrary",), vmem_limit_bytes=VMEM_LIMIT),
        name="mlp",
    )(h2, norm_w.reshape(1, D), up_w, down_w, norm_f_w.reshape(1, D))
    return out.reshape(B, S, D)


def kernel(x, mem, positions, norm_mix_w, w_in, conv_w, conv_b, conv_ln_w, conv_ln_b, ret_gn_w, w_out,
           norm_xattn_w, norm_mem_w, xq_w, xkv_w, xo_w, norm_mlp_w, mlp_up_w, mlp_down_w, norm_f_w):
    kt, v, w_in_b, w_out_b = _mem_kv(mem, norm_mem_w, xkv_w, w_in, w_out)
    h, (xq_b, xo_b, up_b, down_b) = _mixer(x, positions, norm_mix_w, w_in_b, conv_w, conv_b, conv_ln_w, conv_ln_b,
                                           ret_gn_w, w_out_b, (xq_w, xo_w, mlp_up_w, mlp_down_w))
    h = _xattn(h, norm_xattn_w, xq_b, kt, v, xo_b)
    return _mlp(h, norm_mlp_w, up_b, down_b, norm_f_w)
```

```python
import numpy as np
import jax
import jax.numpy as jnp
from jax import lax
from jax.experimental import pallas as pl
from jax.experimental.pallas import tpu as pltpu

D_MODEL = 1024
CHUNK = 64
CONV_CH = 512
CONV_WIDTH = 31
RET_HEADS = 8
RET_DV = 64
RET_DK = 32
RET_QK = RET_HEADS * RET_DK
RET_V = RET_HEADS * RET_DV
N_MEM = 256
XATTN_HEADS = 4
XATTN_HEAD_DIM = 256
D_FF = 4096
ROPE_BASE = 10000.0
EPS = 1e-6

OFF_A, OFF_B, OFF_Q, OFF_K, OFF_V, OFF_G = 0, 512, 1024, 1280, 1536, 2048
D_IN = 2560

LANES = 128
SUBLANES = 8
MXU_TILE = 256
MIX_BLOCK = 512
RET_BLOCK = 256
CONV_HALO = 32
CONV_SEG = 128
DFT_N = CONV_HALO + CONV_SEG
N_FREQ = DFT_N // 2 + 1
FREQ_PAD = -(-N_FREQ // SUBLANES) * SUBLANES
CONV_TAIL = MXU_TILE - DFT_N
CONV_ROWS = 32
TOK_BLOCK = 1024
FF_CHUNK = 1024
VMEM_LIMIT = 56 * 1024 * 1024

_f32 = jnp.float32
_bf16 = jnp.bfloat16


def _dot(a, b):
    return jnp.dot(a, b, preferred_element_type=_f32)


def _rms(x, w):
    return x * lax.rsqrt(jnp.mean(x * x, axis=-1, keepdims=True) + EPS) * w


def _sigmoid(x):
    return 1.0 / (1.0 + jnp.exp(-x))


def _cast_columns(src_ref, dst_ref):
    for c in range(0, src_ref.shape[-1], MXU_TILE):
        dst_ref[:, c:c + MXU_TILE] = src_ref[:, c:c + MXU_TILE].astype(_bf16)


def _mem_kv_kernel(mem_ref, nw_ref, wkv_ref, win32_ref, wout32_ref, kt_ref, v_ref, win_ref, wout_ref):
    m = _rms(mem_ref[0], nw_ref[...]).astype(_bf16)
    k = _dot(m, wkv_ref[:, :D_MODEL].astype(_bf16))
    kt_ref[0] = k.T.astype(_bf16)
    v_ref[0] = _dot(m, wkv_ref[:, D_MODEL:].astype(_bf16)).astype(_bf16)
    _cast_columns(win32_ref.at[0], win_ref.at[0])
    _cast_columns(wout32_ref.at[0], wout_ref.at[0])


def _mem_kv(mem, norm_mem_w, xkv_w, w_in, w_out):
    B = mem.shape[0]
    slab = lambda w: w.reshape(B, w.shape[0] // B, w.shape[1])
    slab_spec = lambda w: pl.BlockSpec((1, w.shape[0] // B, w.shape[1]), lambda b: (b, 0, 0))
    kt, v, win_b, wout_b = pl.pallas_call(
        _mem_kv_kernel,
        grid=(B,),
        in_specs=[
            pl.BlockSpec((1, N_MEM, D_MODEL), lambda b: (b, 0, 0)),
            pl.BlockSpec((1, D_MODEL), lambda b: (0, 0)),
            pl.BlockSpec((D_MODEL, 2 * D_MODEL), lambda b: (0, 0)),
            slab_spec(w_in), slab_spec(w_out),
        ],
        out_specs=[
            pl.BlockSpec((1, D_MODEL, N_MEM), lambda b: (b, 0, 0)),
            pl.BlockSpec((1, N_MEM, D_MODEL), lambda b: (b, 0, 0)),
            slab_spec(w_in), slab_spec(w_out),
        ],
        out_shape=[
            jax.ShapeDtypeStruct((B, D_MODEL, N_MEM), _bf16),
            jax.ShapeDtypeStruct((B, N_MEM, D_MODEL), _bf16),
            jax.ShapeDtypeStruct(slab(w_in).shape, _bf16),
            jax.ShapeDtypeStruct(slab(w_out).shape, _bf16),
        ],
        compiler_params=pltpu.CompilerParams(
            dimension_semantics=("arbitrary",), vmem_limit_bytes=VMEM_LIMIT),
        name="mem_kv",
    )(mem, norm_mem_w.reshape(1, D_MODEL), xkv_w, slab(w_in), slab(w_out))
    return kt, v, win_b.reshape(w_in.shape), wout_b.reshape(w_out.shape)


def _group_norm_halves(y, lo):
    inv = 1.0 / RET_DV
    s_lo = jnp.sum(jnp.where(lo, y, 0.0), axis=-1, keepdims=True)
    s_hi = jnp.sum(jnp.where(lo, 0.0, y), axis=-1, keepdims=True)
    d = y - jnp.where(lo, s_lo, s_hi) * inv
    d2 = d * d
    v_lo = jnp.sum(jnp.where(lo, d2, 0.0), axis=-1, keepdims=True)
    v_hi = jnp.sum(jnp.where(lo, 0.0, d2), axis=-1, keepdims=True)
    return d * lax.rsqrt(jnp.where(lo, v_lo, v_hi) * inv + EPS)


N_CAST = 4
CAST_SCALES = (XATTN_HEAD_DIM ** -0.5, 1.0, 1.0, 1.0)


def _mixer_kernel(*refs):
    n_in = 20
    (x_ref, pos_ref, nw_ref, win_ref, cw_ref, cb_ref, lnw_ref, lnb_ref, gnw_ref, wout_ref,
     dmat_ref, xi_ref, zeta_ref, gl_ref, invf_ref, cost_ref, sint_ref, fwd32_ref, inv32_ref, tap_ref) = refs[:n_in]
    cast_src = refs[n_in:n_in + N_CAST]
    o_ref = refs[n_in + N_CAST]
    cast_dst = refs[n_in + 1 + N_CAST:n_in + 1 + 2 * N_CAST]
    fwd_ref, inv_ref, hspec, ubuf, ybuf, pbuf, state, mbuf = refs[n_in + 1 + 2 * N_CAST:]
    L = MIX_BLOCK

    @pl.when((pl.program_id(0) == 0) & (pl.program_id(1) == 0))
    def _():
        fwd_ref[...] = fwd32_ref[...].astype(_bf16)
        inv_ref[...] = inv32_ref[...].astype(_bf16)
        ubuf[CONV_HALO + L:, :] = jnp.zeros((CONV_TAIL, CONV_CH), _bf16)
        hspec[...] = jnp.dot(tap_ref[...], cw_ref[...], preferred_element_type=_f32, precision=lax.Precision.HIGHEST)

    @pl.when(pl.program_id(1) == 0)
    def _():
        ubuf[0:CONV_HALO, :] = jnp.zeros((CONV_HALO, CONV_CH), _bf16)
        state[...] = jnp.zeros_like(state)

    for src, dst, scale in zip(cast_src, cast_dst, CAST_SCALES):
        dst[...] = (src[...] if scale == 1.0 else src[...] * scale).astype(_bf16)

    x = x_ref[0]
    hn = _rms(x, nw_ref[...]).astype(_bf16)

    a = _dot(hn, win_ref[:, OFF_A:OFF_A + CONV_CH])
    b = _dot(hn, win_ref[:, OFF_B:OFF_B + CONV_CH])
    ubuf[CONV_HALO:CONV_HALO + L, :] = (a * _sigmoid(b)).astype(_bf16)
    h_re, h_im = hspec[0:FREQ_PAD, :], hspec[FREQ_PAD:2 * FREQ_PAD, :]

    seg_rows = range(0, L, CONV_SEG)
    specs = [_dot(fwd_ref[...], ubuf[row0:row0 + MXU_TILE, :]) for row0 in seg_rows]
    prods = []
    for spec in specs:
        s_re, s_im = spec[0:FREQ_PAD, :], spec[FREQ_PAD:2 * FREQ_PAD, :]
        prods.append(jnp.concatenate([s_re * h_re - s_im * h_im, s_re * h_im + s_im * h_re,
                                      jnp.zeros((MXU_TILE - 2 * FREQ_PAD, CONV_CH), _f32)], axis=0).astype(_bf16))
    for row0, prod in zip(seg_rows, prods):
        ybuf[row0:row0 + CONV_SEG, :] = _dot(inv_ref[...], prod)
    ubuf[0:CONV_HALO, :] = ubuf[L:L + CONV_HALO, :]
    for r in range(0, L, CONV_ROWS):
        acc = ybuf[r:r + CONV_ROWS, :] + cb_ref[...]
        mu = jnp.mean(acc, axis=-1, keepdims=True)
        d = acc - mu
        var = jnp.mean(d * d, axis=-1, keepdims=True)
        yn = d * lax.rsqrt(var + EPS) * lnw_ref[...] + lnb_ref[...]
        mbuf[r:r + CONV_ROWS, 0:CONV_CH] = (yn * _sigmoid(yn)).astype(_bf16)

    for c in range(OFF_Q, D_IN, MXU_TILE):
        pbuf[:, c - OFF_Q:c - OFF_Q + MXU_TILE] = _dot(hn, win_ref[:, c:c + MXU_TILE])
    ang0 = pos_ref[0, :, 0:1].astype(_f32) * invf_ref[...]
    cos0, sin0 = jnp.cos(ang0), jnp.sin(ang0)
    lane = lax.broadcasted_iota(jnp.int32, (1, RET_QK), 1)
    first_half = lane % RET_DK < RET_DK // 2
    q_head = lane // RET_DK
    lo = lax.broadcasted_iota(jnp.int32, (1, LANES), 1) < RET_DV
    row_head = lax.broadcasted_iota(jnp.int32, (RET_QK, RET_V), 0) // RET_DK
    col_head = lax.broadcasted_iota(jnp.int32, (RET_QK, RET_V), 1) // RET_DV

    for r0 in range(0, L, RET_BLOCK):
        rows = slice(r0, r0 + RET_BLOCK)
        cos = cos0 * cost_ref[rows, :] - sin0 * sint_ref[rows, :]
        sin = sin0 * cost_ref[rows, :] + cos0 * sint_ref[rows, :]
        cos2 = jnp.concatenate([cos, cos], axis=-1)
        sin2 = jnp.where(first_half, -1.0, 1.0) * jnp.concatenate([sin, sin], axis=-1)

        def rotary(t):
            partner = jnp.where(first_half, pltpu.roll(t, RET_QK - RET_DK // 2, 1), pltpu.roll(t, RET_DK // 2, 1))
            return t * cos2 + partner * sin2

        qr = rotary(pbuf[rows, 0:OFF_K - OFF_Q])
        kr = rotary(pbuf[rows, OFF_K - OFF_Q:OFF_V - OFF_Q])
        v = pbuf[rows, OFF_V - OFF_Q:OFF_G - OFF_Q]
        kt = kr.T.astype(_bf16)
        vb = v.astype(_bf16)

        yx = _dot(qr.astype(_bf16), state[...].astype(_bf16)) * xi_ref[...]

        for p in range(RET_HEADS // 2):
            cols = slice(p * LANES, (p + 1) * LANES)
            vp = vb[:, cols]
            ys = []
            for h in (2 * p, 2 * p + 1):
                qm = jnp.where(q_head == h, qr, 0.0).astype(_bf16)
                s = _dot(qm, kt)
                ys.append(_dot((s * dmat_ref[h]).astype(_bf16), vp))
            y = jnp.where(lo, ys[0], ys[1]) + yx[:, cols]
            yn = _group_norm_halves(y, lo) * gnw_ref[:, cols]
            gp = pbuf[rows, OFF_G - OFF_Q + p * LANES:OFF_G - OFF_Q + (p + 1) * LANES]
            mbuf[rows, CONV_CH + p * LANES:CONV_CH + (p + 1) * LANES] = (gp * _sigmoid(gp) * yn).astype(_bf16)

        kv = _dot(kt, (v * zeta_ref[...]).astype(_bf16))
        state[...] = gl_ref[...] * state[...] + jnp.where(row_head == col_head, kv, 0.0)

    o_ref[0] = x + _dot(mbuf[...], wout_ref[...])


def _retention_tables(L, step_rows):
    h = np.arange(RET_HEADS, dtype=np.float64)
    log_g = np.log1p(-np.exp2(-5.0 - h))
    idx = np.arange(L, dtype=np.float64)
    dist = np.abs(idx[:, None] - idx[None, :])
    visible = (idx[None, :] // CHUNK) <= (idx[:, None] // CHUNK)
    dmat = np.where(visible[None], np.exp(log_g[:, None, None] * dist[None]), 0.0)
    xi = np.exp(log_g[None, :] * (idx[:, None] + 1.0))
    zeta = np.exp(log_g[None, :] * (L - 1.0 - idx[:, None]))
    gl = np.exp(log_g * L)
    rep = lambda t: np.repeat(t, RET_DV, axis=-1)
    inv_freq = ROPE_BASE ** (-np.arange(RET_DK // 2, dtype=np.float32) / np.float32(RET_DK // 2))
    invf = np.tile(inv_freq.astype(np.float32), LANES // (RET_DK // 2))[None, :]
    rel = np.arange(step_rows, dtype=np.float64)[:, None] * invf.astype(np.float64)
    f = lambda t: jnp.asarray(t, dtype=_f32)
    scale = RET_DK ** -0.5
    return (f(dmat * scale), f(rep(xi) * scale), f(rep(zeta)), f(rep(gl[None, :])), f(invf),
            f(np.cos(rel)), f(np.sin(rel)))


def _conv_dft_tables():
    n = np.arange(DFT_N, dtype=np.float64)
    f = np.arange(N_FREQ, dtype=np.float64)[:, None]
    w = 2.0 * np.pi / DFT_N
    fwd = np.zeros((2 * FREQ_PAD, MXU_TILE))
    fwd[:N_FREQ, :DFT_N] = np.cos(w * f * n[None, :])
    fwd[FREQ_PAD:FREQ_PAD + N_FREQ, :DFT_N] = -np.sin(w * f * n[None, :])
    delay = (CONV_WIDTH - 1) - np.arange(CONV_WIDTH, dtype=np.float64)[None, :]
    tap = np.zeros((2 * FREQ_PAD, CONV_HALO))
    tap[:N_FREQ, :CONV_WIDTH] = np.cos(w * f * delay)
    tap[FREQ_PAD:FREQ_PAD + N_FREQ, :CONV_WIDTH] = -np.sin(w * f * delay)
    weight = np.full((1, N_FREQ), 2.0)
    weight[0, 0] = weight[0, -1] = 1.0
    out = n[CONV_HALO:, None]
    inv = np.zeros((CONV_SEG, MXU_TILE))
    inv[:, :N_FREQ] = weight * np.cos(w * out * f.T) / DFT_N
    inv[:, FREQ_PAD:FREQ_PAD + N_FREQ] = -weight * np.sin(w * out * f.T) / DFT_N
    return [jnp.asarray(t, dtype=_f32) for t in (fwd, inv, tap)]


def _mixer(x, positions, norm_w, w_in, conv_w, conv_b, ln_w, ln_b, gn_w, w_out, later_weights):
    B, S, D = x.shape
    L = MIX_BLOCK
    n = S // L
    steps = B * n
    R = RET_BLOCK
    dmat, xi, zeta, gl, invf, cost, sint = _retention_tables(R, L)
    fwd, inv, tap = _conv_dft_tables()
    taps = jnp.pad(conv_w, ((0, CONV_HALO - CONV_WIDTH), (0, 0)))
    const = lambda shape: pl.BlockSpec(shape, lambda b, j: (0,) * len(shape))
    once = lambda shape: pl.BlockSpec(shape, lambda b, j: (0,) * len(shape), pipeline_mode=pl.Buffered(1))
    row = lambda n: pl.BlockSpec((1, n), lambda b, j: (0, 0))
    slabs = [w.reshape(steps, w.shape[0] // steps, w.shape[1]) for w in later_weights]
    slab_spec = lambda w: pl.BlockSpec((1,) + w.shape[1:], lambda b, j: (b * n + j, 0, 0))
    outs = pl.pallas_call(
        _mixer_kernel,
        grid=(B, n),
        in_specs=[
            pl.BlockSpec((1, L, D), lambda b, j: (b, j, 0)),
            pl.BlockSpec((1, 1, L), lambda b, j: (b * n + j, 0, 0)),
            row(D),
            once((D, D_IN)),
            const((CONV_HALO, CONV_CH)),
            row(CONV_CH), row(CONV_CH), row(CONV_CH), row(RET_V),
            once((D, D)),
            const((RET_HEADS, R, R)),
            const((R, RET_V)), const((R, RET_V)), row(RET_V), row(LANES),
            const((L, LANES)), const((L, LANES)),
            const(fwd.shape), const(inv.shape), const(tap.shape),
        ] + [slab_spec(w) for w in slabs],
        out_specs=[pl.BlockSpec((1, L, D), lambda b, j: (b, j, 0))] + [slab_spec(w) for w in slabs],
        out_shape=[jax.ShapeDtypeStruct((B, S, D), _f32)] + [jax.ShapeDtypeStruct(w.shape, _bf16) for w in slabs],
        scratch_shapes=[
            pltpu.VMEM(fwd.shape, _bf16),
            pltpu.VMEM(inv.shape, _bf16),
            pltpu.VMEM((2 * FREQ_PAD, CONV_CH), _f32),
            pltpu.VMEM((CONV_HALO + L + CONV_TAIL, CONV_CH), _bf16),
            pltpu.VMEM((L, CONV_CH), _f32),
            pltpu.VMEM((L, D_IN - OFF_Q), _f32),
            pltpu.VMEM((RET_QK, RET_V), _f32),
            pltpu.VMEM((L, D), _bf16),
        ],
        compiler_params=pltpu.CompilerParams(
            dimension_semantics=("arbitrary", "arbitrary"), vmem_limit_bytes=VMEM_LIMIT),
        name="mixer",
    )(x, positions.reshape(steps, 1, L), norm_w.reshape(1, D), w_in, taps, conv_b.reshape(1, -1),
      ln_w.reshape(1, -1), ln_b.reshape(1, -1), gn_w.reshape(1, -1), w_out, dmat, xi, zeta, gl, invf, cost, sint,
      fwd, inv, tap, *slabs)
    return outs[0], [o.reshape(w.shape) for o, w in zip(outs[1:], later_weights)]


def _xattn_kernel(h_ref, nw_ref, wq_ref, kt_ref, v_ref, wo_ref, o_ref, obuf):
    h = h_ref[0]
    hn = _rms(h, nw_ref[...]).astype(_bf16)
    q = _dot(hn, wq_ref[...]).astype(_bf16)
    for i in range(XATTN_HEADS):
        cols = slice(i * XATTN_HEAD_DIM, (i + 1) * XATTN_HEAD_DIM)
        s = _dot(q[:, cols], kt_ref[0, cols, :])
        e = jnp.exp(s - jnp.max(s, axis=-1, keepdims=True))
        o = _dot(e.astype(_bf16), v_ref[0, :, cols])
        obuf[:, cols] = (o * (1.0 / jnp.sum(e, axis=-1, keepdims=True))).astype(_bf16)
    o_ref[0] = h + _dot(obuf[...], wo_ref[...])


def _xattn(h, norm_w, xq_w, kt, v, xo_w):
    B, S, D = h.shape
    T = TOK_BLOCK
    return pl.pallas_call(
        _xattn_kernel,
        grid=(B, S // T),
        in_specs=[
            pl.BlockSpec((1, T, D), lambda b, j: (b, j, 0)),
            pl.BlockSpec((1, D), lambda b, j: (0, 0)),
            pl.BlockSpec((D, D), lambda b, j: (0, 0)),
            pl.BlockSpec((1, D, N_MEM), lambda b, j: (b, 0, 0)),
            pl.BlockSpec((1, N_MEM, D), lambda b, j: (b, 0, 0)),
            pl.BlockSpec((D, D), lambda b, j: (0, 0)),
        ],
        out_specs=pl.BlockSpec((1, T, D), lambda b, j: (b, j, 0)),
        out_shape=jax.ShapeDtypeStruct((B, S, D), _f32),
        scratch_shapes=[pltpu.VMEM((T, D), _bf16)],
        compiler_params=pltpu.CompilerParams(
            dimension_semantics=("arbitrary", "arbitrary"), vmem_limit_bytes=VMEM_LIMIT),
        name="xattn",
    )(h, norm_w.reshape(1, D), xq_w, kt, v, xo_w)


def _mlp_kernel(h_ref, nw_ref, wu_ref, wd_ref, fw_ref, o_ref, abuf):
    h = h_ref[...]
    hn = _rms(h, nw_ref[...]).astype(_bf16)
    for c in range(0, D_FF, FF_CHUNK):
        u = jnp.maximum(_dot(hn, wu_ref[:, c:c + FF_CHUNK]), 0.0)
        abuf[:, c:c + FF_CHUNK] = (u * u).astype(_bf16)
    o_ref[...] = _rms(h + _dot(abuf[...], wd_ref[...]), fw_ref[...])


def _mlp(h, norm_w, up_w, down_w, norm_f_w):
    B, S, D = h.shape
    T = TOK_BLOCK
    h2 = h.reshape(B * S, D)
    out = pl.pallas_call(
        _mlp_kernel,
        grid=(B * S // T,),
        in_specs=[
            pl.BlockSpec((T, D), lambda i: (i, 0)),
            pl.BlockSpec((1, D), lambda i: (0, 0)),
            pl.BlockSpec((D, D_FF), lambda i: (0, 0), pipeline_mode=pl.Buffered(1)),
            pl.BlockSpec((D_FF, D), lambda i: (0, 0), pipeline_mode=pl.Buffered(1)),
            pl.BlockSpec((1, D), lambda i: (0, 0)),
        ],
        out_specs=pl.BlockSpec((T, D), lambda i: (i, 0)),
        out_shape=jax.ShapeDtypeStruct((B * S, D), _f32),
        scratch_shapes=[pltpu.VMEM((T, D_FF), _bf16)],
        compiler_params=pltpu.CompilerParams(
            dimension_semantics=("arbitrary",), vmem_limit_bytes=VMEM_LIMIT),
        name="mlp",
    )(h2, norm_w.reshape(1, D), up_w, down_w, norm_f_w.reshape(1, D))
    return out.reshape(B, S, D)


def kernel(x, mem, positions, norm_mix_w, w_in, conv_w, conv_b, conv_ln_w, conv_ln_b, ret_gn_w, w_out,
           norm_xattn_w, norm_mem_w, xq_w, xkv_w, xo_w, norm_mlp_w, mlp_up_w, mlp_down_w, norm_f_w):
    kt, v, w_in_b, w_out_b = _mem_kv(mem, norm_mem_w, xkv_w, w_in, w_out)
    h, (xq_b, xo_b, up_b, down_b) = _mixer(x, positions, norm_mix_w, w_in_b, conv_w, conv_b, conv_ln_w, conv_ln_b,
                                           ret_gn_w, w_out_b, (xq_w, xo_w, mlp_up_w, mlp_down_w))
    h = _xattn(h, norm_xattn_w, xq_b, kt, v, xo_b)
    return _mlp(h, norm_mlp_w, up_b, down_b, norm_f_w)
```

```python
import numpy as np
import jax
import jax.numpy as jnp
from jax import lax
from jax.experimental import pallas as pl
from jax.experimental.pallas import tpu as pltpu

D_MODEL = 1024
CHUNK = 64
CONV_CH = 512
CONV_WIDTH = 31
RET_HEADS = 8
RET_DV = 64
RET_DK = 32
RET_QK = RET_HEADS * RET_DK
RET_V = RET_HEADS * RET_DV
N_MEM = 256
XATTN_HEADS = 4
XATTN_HEAD_DIM = 256
D_FF = 4096
ROPE_BASE = 10000.0
EPS = 1e-6

OFF_A, OFF_B, OFF_Q, OFF_K, OFF_V, OFF_G = 0, 512, 1024, 1280, 1536, 2048
D_IN = 2560

LANES = 128
SUBLANES = 8
MXU_TILE = 256
MIX_BLOCK = 512
RET_BLOCK = 128
CONV_HALO = 32
CONV_SEG = 128
DFT_N = CONV_HALO + CONV_SEG
N_FREQ = DFT_N // 2 + 1
FREQ_PAD = -(-N_FREQ // SUBLANES) * SUBLANES
CONV_TAIL = MXU_TILE - DFT_N
CONV_ROWS = 32
TOK_BLOCK = 1024
FF_CHUNK = 1024
VMEM_LIMIT = 56 * 1024 * 1024

_f32 = jnp.float32
_bf16 = jnp.bfloat16


def _dot(a, b):
    return jnp.dot(a, b, preferred_element_type=_f32)


def _rms(x, w):
    return x * lax.rsqrt(jnp.mean(x * x, axis=-1, keepdims=True) + EPS) * w


def _sigmoid(x):
    return 1.0 / (1.0 + jnp.exp(-x))


def _cast_columns(src_ref, dst_ref):
    for c in range(0, src_ref.shape[-1], MXU_TILE):
        dst_ref[:, c:c + MXU_TILE] = src_ref[:, c:c + MXU_TILE].astype(_bf16)


def _mem_kv_kernel(mem_ref, nw_ref, wkv_ref, win32_ref, wout32_ref, kt_ref, v_ref, win_ref, wout_ref):
    m = _rms(mem_ref[0], nw_ref[...]).astype(_bf16)
    k = _dot(m, wkv_ref[:, :D_MODEL].astype(_bf16))
    kt_ref[0] = k.T.astype(_bf16)
    v_ref[0] = _dot(m, wkv_ref[:, D_MODEL:].astype(_bf16)).astype(_bf16)
    _cast_columns(win32_ref.at[0], win_ref.at[0])
    _cast_columns(wout32_ref.at[0], wout_ref.at[0])


def _mem_kv(mem, norm_mem_w, xkv_w, w_in, w_out):
    B = mem.shape[0]
    slab = lambda w: w.reshape(B, w.shape[0] // B, w.shape[1])
    slab_spec = lambda w: pl.BlockSpec((1, w.shape[0] // B, w.shape[1]), lambda b: (b, 0, 0))
    kt, v, win_b, wout_b = pl.pallas_call(
        _mem_kv_kernel,
        grid=(B,),
        in_specs=[
            pl.BlockSpec((1, N_MEM, D_MODEL), lambda b: (b, 0, 0)),
            pl.BlockSpec((1, D_MODEL), lambda b: (0, 0)),
            pl.BlockSpec((D_MODEL, 2 * D_MODEL), lambda b: (0, 0)),
            slab_spec(w_in), slab_spec(w_out),
        ],
        out_specs=[
            pl.BlockSpec((1, D_MODEL, N_MEM), lambda b: (b, 0, 0)),
            pl.BlockSpec((1, N_MEM, D_MODEL), lambda b: (b, 0, 0)),
            slab_spec(w_in), slab_spec(w_out),
        ],
        out_shape=[
            jax.ShapeDtypeStruct((B, D_MODEL, N_MEM), _bf16),
            jax.ShapeDtypeStruct((B, N_MEM, D_MODEL), _bf16),
            jax.ShapeDtypeStruct(slab(w_in).shape, _bf16),
            jax.ShapeDtypeStruct(slab(w_out).shape, _bf16),
        ],
        compiler_params=pltpu.CompilerParams(
            dimension_semantics=("arbitrary",), vmem_limit_bytes=VMEM_LIMIT),
        name="mem_kv",
    )(mem, norm_mem_w.reshape(1, D_MODEL), xkv_w, slab(w_in), slab(w_out))
    return kt, v, win_b.reshape(w_in.shape), wout_b.reshape(w_out.shape)


def _group_norm_halves(y, lo):
    inv = 1.0 / RET_DV
    s_lo = jnp.sum(jnp.where(lo, y, 0.0), axis=-1, keepdims=True)
    s_hi = jnp.sum(jnp.where(lo, 0.0, y), axis=-1, keepdims=True)
    d = y - jnp.where(lo, s_lo, s_hi) * inv
    d2 = d * d
    v_lo = jnp.sum(jnp.where(lo, d2, 0.0), axis=-1, keepdims=True)
    v_hi = jnp.sum(jnp.where(lo, 0.0, d2), axis=-1, keepdims=True)
    return d * lax.rsqrt(jnp.where(lo, v_lo, v_hi) * inv + EPS)


N_CAST = 4
CAST_SCALES = (XATTN_HEAD_DIM ** -0.5, 1.0, 1.0, 1.0)


def _mixer_kernel(*refs):
    n_in = 20
    (x_ref, pos_ref, nw_ref, win_ref, cw_ref, cb_ref, lnw_ref, lnb_ref, gnw_ref, wout_ref,
     dmat_ref, xi_ref, zeta_ref, gl_ref, invf_ref, cost_ref, sint_ref, fwd32_ref, inv32_ref, tap_ref) = refs[:n_in]
    cast_src = refs[n_in:n_in + N_CAST]
    o_ref = refs[n_in + N_CAST]
    cast_dst = refs[n_in + 1 + N_CAST:n_in + 1 + 2 * N_CAST]
    fwd_ref, inv_ref, hspec, ubuf, ybuf, pbuf, state, mbuf = refs[n_in + 1 + 2 * N_CAST:]
    L = MIX_BLOCK

    @pl.when((pl.program_id(0) == 0) & (pl.program_id(1) == 0))
    def _():
        fwd_ref[...] = fwd32_ref[...].astype(_bf16)
        inv_ref[...] = inv32_ref[...].astype(_bf16)
        ubuf[CONV_HALO + L:, :] = jnp.zeros((CONV_TAIL, CONV_CH), _bf16)
        hspec[...] = jnp.dot(tap_ref[...], cw_ref[...], preferred_element_type=_f32, precision=lax.Precision.HIGHEST)

    @pl.when(pl.program_id(1) == 0)
    def _():
        ubuf[0:CONV_HALO, :] = jnp.zeros((CONV_HALO, CONV_CH), _bf16)
        state[...] = jnp.zeros_like(state)

    for src, dst, scale in zip(cast_src, cast_dst, CAST_SCALES):
        dst[...] = (src[...] if scale == 1.0 else src[...] * scale).astype(_bf16)

    x = x_ref[0]
    hn = _rms(x, nw_ref[...]).astype(_bf16)

    a = _dot(hn, win_ref[:, OFF_A:OFF_A + CONV_CH])
    b = _dot(hn, win_ref[:, OFF_B:OFF_B + CONV_CH])
    ubuf[CONV_HALO:CONV_HALO + L, :] = (a * _sigmoid(b)).astype(_bf16)
    h_re, h_im = hspec[0:FREQ_PAD, :], hspec[FREQ_PAD:2 * FREQ_PAD, :]

    seg_rows = range(0, L, CONV_SEG)
    specs = [_dot(fwd_ref[...], ubuf[row0:row0 + MXU_TILE, :]) for row0 in seg_rows]
    prods = []
    for spec in specs:
        s_re, s_im = spec[0:FREQ_PAD, :], spec[FREQ_PAD:2 * FREQ_PAD, :]
        prods.append(jnp.concatenate([s_re * h_re - s_im * h_im, s_re * h_im + s_im * h_re,
                                      jnp.zeros((MXU_TILE - 2 * FREQ_PAD, CONV_CH), _f32)], axis=0).astype(_bf16))
    for row0, prod in zip(seg_rows, prods):
        ybuf[row0:row0 + CONV_SEG, :] = _dot(inv_ref[...], prod)
    ubuf[0:CONV_HALO, :] = ubuf[L:L + CONV_HALO, :]
    for r in range(0, L, CONV_ROWS):
        acc = ybuf[r:r + CONV_ROWS, :] + cb_ref[...]
        mu = jnp.mean(acc, axis=-1, keepdims=True)
        d = acc - mu
        var = jnp.mean(d * d, axis=-1, keepdims=True)
        yn = d * lax.rsqrt(var + EPS) * lnw_ref[...] + lnb_ref[...]
        mbuf[r:r + CONV_ROWS, 0:CONV_CH] = (yn * _sigmoid(yn)).astype(_bf16)

    for c in range(OFF_Q, D_IN, MXU_TILE):
        pbuf[:, c - OFF_Q:c - OFF_Q + MXU_TILE] = _dot(hn, win_ref[:, c:c + MXU_TILE])
    ang0 = pos_ref[0, :, 0:1].astype(_f32) * invf_ref[...]
    cos0, sin0 = jnp.cos(ang0), jnp.sin(ang0)
    lane = lax.broadcasted_iota(jnp.int32, (1, RET_QK), 1)
    first_half = lane % RET_DK < RET_DK // 2
    lo =lax.broadcasted_iota(jnp.int32, (1, LANES), 1) < RET_DV
    row_head = lax.broadcasted_iota(jnp.int32, (RET_QK, RET_V), 0) // RET_DK
    col_head = lax.broadcasted_iota(jnp.int32, (RET_QK, RET_V), 1) // RET_DV

    for r0 in range(0, L, RET_BLOCK):
        rows = slice(r0, r0 + RET_BLOCK)
        cos = cos0 * cost_ref[rows, :] - sin0 * sint_ref[rows, :]
        sin = sin0 * cost_ref[rows, :] + cos0 * sint_ref[rows, :]
        cos2 = jnp.concatenate([cos, cos], axis=-1)
        sin2 = jnp.where(first_half, -1.0, 1.0) * jnp.concatenate([sin, sin], axis=-1)

        def rotary(t):
            partner = jnp.where(first_half, pltpu.roll(t, RET_QK - RET_DK // 2, 1), pltpu.roll(t, RET_DK // 2, 1))
            return t * cos2 + partner * sin2

        qb = rotary(pbuf[rows, 0:OFF_K - OFF_Q]).astype(_bf16)
        kr = rotary(pbuf[rows, OFF_K - OFF_Q:OFF_V - OFF_Q])
        v = pbuf[rows, OFF_V - OFF_Q:OFF_G - OFF_Q]
        kt = kr.T.astype(_bf16)
        vb = v.astype(_bf16)

        yx = _dot(qb, state[...].astype(_bf16)) * xi_ref[...]

        for p in range(RET_HEADS // 2):
            cols = slice(p * LANES, (p + 1) * LANES)
            blank = jnp.zeros((RET_DK, RET_BLOCK), _bf16)
            k_pair = jnp.concatenate(
                [jnp.concatenate([kt[h * RET_DK:(h + 1) * RET_DK] if h == 2 * p else blank,
                                  kt[h * RET_DK:(h + 1) * RET_DK] if h == 2 * p + 1 else blank], axis=1)
                 for h in range(RET_HEADS)], axis=0)
            s = _dot(qb, k_pair)
            vp = vb[:, cols]
            off = jnp.zeros_like(vp)
            v_pair = jnp.concatenate([jnp.where(lo, vp, off), jnp.where(lo, off, vp)], axis=0)
            y = _dot((s * dmat_ref[p]).astype(_bf16), v_pair) + yx[:, cols]
            yn = _group_norm_halves(y, lo) * gnw_ref[:, cols]
            gp = pbuf[rows, OFF_G - OFF_Q + p * LANES:OFF_G - OFF_Q + (p + 1) * LANES]
            mbuf[rows, CONV_CH + p * LANES:CONV_CH + (p + 1) * LANES] = (gp * _sigmoid(gp) * yn).astype(_bf16)

        kv = _dot(kt, (v * zeta_ref[...]).astype(_bf16))
        state[...] = gl_ref[...] * state[...] + jnp.where(row_head == col_head, kv, 0.0)

    o_ref[0] = x + _dot(mbuf[...], wout_ref[...])


def _retention_tables(L, step_rows):
    h = np.arange(RET_HEADS, dtype=np.float64)
    log_g = np.log1p(-np.exp2(-5.0 - h))
    idx = np.arange(L, dtype=np.float64)
    dist = np.abs(idx[:, None] - idx[None, :])
    visible = (idx[None, :] // CHUNK) <= (idx[:, None] // CHUNK)
    dmat = np.where(visible[None], np.exp(log_g[:, None, None] * dist[None]), 0.0)
    dmat = np.concatenate([dmat[0::2], dmat[1::2]], axis=-1)
    xi = np.exp(log_g[None, :] * (idx[:, None] + 1.0))
    zeta = np.exp(log_g[None, :] * (L - 1.0 - idx[:, None]))
    gl = np.exp(log_g * L)
    rep = lambda t: np.repeat(t, RET_DV, axis=-1)
    inv_freq = ROPE_BASE ** (-np.arange(RET_DK // 2, dtype=np.float32) / np.float32(RET_DK // 2))
    invf = np.tile(inv_freq.astype(np.float32), LANES // (RET_DK // 2))[None, :]
    rel = np.arange(step_rows, dtype=np.float64)[:, None] * invf.astype(np.float64)
    f = lambda t: jnp.asarray(t, dtype=_f32)
    scale = RET_DK ** -0.5
    return (f(dmat * scale), f(rep(xi) * scale), f(rep(zeta)), f(rep(gl[None, :])), f(invf),
            f(np.cos(rel)), f(np.sin(rel)))


def _conv_dft_tables():
    n = np.arange(DFT_N, dtype=np.float64)
    f = np.arange(N_FREQ, dtype=np.float64)[:, None]
    w = 2.0 * np.pi / DFT_N
    fwd = np.zeros((2 * FREQ_PAD, MXU_TILE))
    fwd[:N_FREQ, :DFT_N] = np.cos(w * f * n[None, :])
    fwd[FREQ_PAD:FREQ_PAD + N_FREQ, :DFT_N] = -np.sin(w * f * n[None, :])
    delay = (CONV_WIDTH - 1) - np.arange(CONV_WIDTH, dtype=np.float64)[None, :]
    tap = np.zeros((2 * FREQ_PAD, CONV_HALO))
    tap[:N_FREQ, :CONV_WIDTH] = np.cos(w * f * delay)
    tap[FREQ_PAD:FREQ_PAD + N_FREQ, :CONV_WIDTH] = -np.sin(w * f * delay)
    weight = np.full((1, N_FREQ), 2.0)
    weight[0, 0] = weight[0, -1] = 1.0
    out = n[CONV_HALO:, None]
    inv = np.zeros((CONV_SEG, MXU_TILE))
    inv[:, :N_FREQ] = weight * np.cos(w * out * f.T) / DFT_N
    inv[:, FREQ_PAD:FREQ_PAD + N_FREQ] = -weight * np.sin(w * out * f.T) / DFT_N
    return [jnp.asarray(t, dtype=_f32) for t in (fwd, inv, tap)]


def _mixer(x, positions, norm_w, w_in, conv_w, conv_b, ln_w, ln_b, gn_w, w_out, later_weights):
    B, S, D = x.shape
    L = MIX_BLOCK
    n = S // L
    steps = B * n
    R = RET_BLOCK
    dmat, xi, zeta, gl, invf, cost, sint = _retention_tables(R, L)
    fwd, inv, tap = _conv_dft_tables()
    taps = jnp.pad(conv_w, ((0, CONV_HALO - CONV_WIDTH), (0, 0)))
    const = lambda shape: pl.BlockSpec(shape, lambda b, j: (0,) * len(shape))
    once = lambda shape: pl.BlockSpec(shape, lambda b, j: (0,) * len(shape), pipeline_mode=pl.Buffered(1))
    row = lambda n: pl.BlockSpec((1, n), lambda b, j: (0, 0))
    slabs = [w.reshape(steps, w.shape[0] // steps, w.shape[1]) for w in later_weights]
    slab_spec = lambda w: pl.BlockSpec((1,) + w.shape[1:], lambda b, j: (b * n + j, 0, 0))
    outs = pl.pallas_call(
        _mixer_kernel,
        grid=(B, n),
        in_specs=[
            pl.BlockSpec((1, L, D), lambda b, j: (b, j, 0)),
            pl.BlockSpec((1, 1, L), lambda b, j: (b * n + j, 0, 0)),
            row(D),
            once((D, D_IN)),
            const((CONV_HALO, CONV_CH)),
            row(CONV_CH), row(CONV_CH), row(CONV_CH), row(RET_V),
            once((D, D)),
            const((RET_HEADS // 2, R, 2 * R)),
            const((R, RET_V)), const((R, RET_V)), row(RET_V), row(LANES),
            const((L, LANES)), const((L, LANES)),
            const(fwd.shape), const(inv.shape), const(tap.shape),
        ] + [slab_spec(w) for w in slabs],
        out_specs=[pl.BlockSpec((1, L, D), lambda b, j: (b, j, 0))] + [slab_spec(w) for w in slabs],
        out_shape=[jax.ShapeDtypeStruct((B, S, D), _f32)] + [jax.ShapeDtypeStruct(w.shape, _bf16) for w in slabs],
        scratch_shapes=[
            pltpu.VMEM(fwd.shape, _bf16),
            pltpu.VMEM(inv.shape, _bf16),
            pltpu.VMEM((2 * FREQ_PAD, CONV_CH), _f32),
            pltpu.VMEM((CONV_HALO + L + CONV_TAIL, CONV_CH), _bf16),
            pltpu.VMEM((L, CONV_CH), _f32),
            pltpu.VMEM((L, D_IN - OFF_Q), _f32),
            pltpu.VMEM((RET_QK, RET_V), _f32),
            pltpu.VMEM((L, D), _bf16),
        ],
        compiler_params=pltpu.CompilerParams(
            dimension_semantics=("arbitrary", "arbitrary"), vmem_limit_bytes=VMEM_LIMIT),
        name="mixer",
    )(x, positions.reshape(steps, 1, L), norm_w.reshape(1, D), w_in, taps, conv_b.reshape(1, -1),
      ln_w.reshape(1, -1), ln_b.reshape(1, -1), gn_w.reshape(1, -1), w_out, dmat, xi, zeta, gl, invf, cost, sint,
      fwd, inv, tap, *slabs)
    return outs[0], [o.reshape(w.shape) for o, w in zip(outs[1:], later_weights)]


def _xattn_kernel(h_ref, nw_ref, wq_ref, kt_ref, v_ref, wo_ref, o_ref, obuf):
    h = h_ref[0]
    hn = _rms(h, nw_ref[...]).astype(_bf16)
    q = _dot(hn, wq_ref[...]).astype(_bf16)
    for i in range(XATTN_HEADS):
        cols = slice(i * XATTN_HEAD_DIM, (i + 1) * XATTN_HEAD_DIM)
        s = _dot(q[:, cols], kt_ref[0, cols, :])
        e = jnp.exp(s - jnp.max(s, axis=-1, keepdims=True))
        o = _dot(e.astype(_bf16), v_ref[0, :, cols])
        obuf[:, cols] = (o * (1.0 / jnp.sum(e, axis=-1, keepdims=True))).astype(_bf16)
    o_ref[0] = h + _dot(obuf[...], wo_ref[...])


def _xattn(h, norm_w, xq_w, kt, v, xo_w):
    B, S, D = h.shape
    T = TOK_BLOCK
    return pl.pallas_call(
        _xattn_kernel,
        grid=(B, S // T),
        in_specs=[
            pl.BlockSpec((1, T, D), lambda b, j: (b, j, 0)),
            pl.BlockSpec((1, D), lambda b, j: (0, 0)),
            pl.BlockSpec((D, D), lambda b, j: (0, 0)),
            pl.BlockSpec((1, D, N_MEM), lambda b, j: (b, 0, 0)),
            pl.BlockSpec((1, N_MEM, D), lambda b, j: (b, 0, 0)),
            pl.BlockSpec((D, D), lambda b, j: (0, 0)),
        ],
        out_specs=pl.BlockSpec((1, T, D), lambda b, j: (b, j, 0)),
        out_shape=jax.ShapeDtypeStruct((B, S, D), _f32),
        scratch_shapes=[pltpu.VMEM((T, D), _bf16)],
        compiler_params=pltpu.CompilerParams(
            dimension_semantics=("arbitrary", "arbitrary"), vmem_limit_bytes=VMEM_LIMIT),
        name="xattn",
    )(h, norm_w.reshape(1, D), xq_w, kt, v, xo_w)


def _mlp_kernel(h_ref, nw_ref, wu_ref, wd_ref, fw_ref, o_ref, abuf):
    h = h_ref[...]
    hn = _rms(h, nw_ref[...]).astype(_bf16)
    for c in range(0, D_FF, FF_CHUNK):
        u = jnp.maximum(_dot(hn, wu_ref[:, c:c + FF_CHUNK]), 0.0)
        abuf[:, c:c + FF_CHUNK] = (u * u).astype(_bf16)
    o_ref[...] = _rms(h + _dot(abuf[...], wd_ref[...]), fw_ref[...])


def _mlp(h, norm_w, up_w, down_w, norm_f_w):
    B, S, D = h.shape
    T = TOK_BLOCK
    h2 = h.reshape(B * S, D)
    out = pl.pallas_call(
        _mlp_kernel,
        grid=(B * S // T,),
        in_specs=[
            pl.BlockSpec((T, D), lambda i: (i, 0)),
            pl.BlockSpec((1, D), lambda i: (0, 0)),
            pl.BlockSpec((D, D_FF), lambda i: (0, 0), pipeline_mode=pl.Buffered(1)),
            pl.BlockSpec((D_FF, D), lambda i: (0, 0), pipeline_mode=pl.Buffered(1)),
            pl.BlockSpec((1, D), lambda i: (0, 0)),
        ],
        out_specs=pl.BlockSpec((T, D), lambda i: (i, 0)),
        out_shape=jax.ShapeDtypeStruct((B * S, D), _f32),
        scratch_shapes=[pltpu.VMEM((T, D_FF), _bf16)],
        compiler_params=pltpu.CompilerParams(
            dimension_semantics=("arbitrary",), vmem_limit_bytes=VMEM_LIMIT),
        name="mlp",
    )(h2, norm_w.reshape(1, D), up_w, down_w, norm_f_w.reshape(1, D))
    return out.reshape(B, S, D)


def kernel(x, mem, positions, norm_mix_w, w_in, conv_w, conv_b, conv_ln_w, conv_ln_b, ret_gn_w, w_out,
           norm_xattn_w, norm_mem_w, xq_w, xkv_w, xo_w, norm_mlp_w, mlp_up_w, mlp_down_w, norm_f_w):
    kt, v, w_in_b, w_out_b = _mem_kv(mem, norm_mem_w, xkv_w, w_in, w_out)
    h, (xq_b, xo_b, up_b, down_b) = _mixer(x, positions, norm_mix_w, w_in_b, conv_w, conv_b, conv_ln_w, conv_ln_b,
                                           ret_gn_w, w_out_b, (xq_w, xo_w, mlp_up_w, mlp_down_w))
    h = _xattn(h, norm_xattn_w, xq_b, kt, v, xo_b)
    return _mlp(h, norm_mlp_w, up_b, down_b, norm_f_w)
```

```python
import numpy as np
import jax
import jax.numpy as jnp
from jax import lax
from jax.experimental import pallas as pl
from jax.experimental.pallas import tpu as pltpu

D_MODEL = 1024
CHUNK = 64
CONV_CH = 512
CONV_WIDTH = 31
RET_HEADS = 8
RET_DV = 64
RET_DK = 32
RET_QK = RET_HEADS * RET_DK
RET_V = RET_HEADS * RET_DV
N_MEM = 256
XATTN_HEADS = 4
XATTN_HEAD_DIM = 256
D_FF = 4096
ROPE_BASE = 10000.0
EPS = 1e-6

OFF_A, OFF_B, OFF_Q, OFF_K, OFF_V, OFF_G = 0, 512, 1024, 1280, 1536, 2048
D_IN = 2560

LANES = 128
SUBLANES = 8
MXU_TILE = 256
MIX_BLOCK = 512
RET_BLOCK = 128
CONV_HALO = 32
CONV_SEG = 128
DFT_N = CONV_HALO + CONV_SEG
N_FREQ = DFT_N // 2 + 1
FREQ_PAD = -(-N_FREQ // SUBLANES) * SUBLANES
CONV_TAIL = MXU_TILE - DFT_N
CONV_ROWS = 32
TOK_BLOCK = 1024
ROW_CHUNK = 256
MLP_ROW_CHUNK = 256
FF_CHUNK = 1024
VMEM_LIMIT = 56 * 1024 * 1024

_f32 = jnp.float32
_bf16 = jnp.bfloat16


def _dot(a, b):
    return jnp.dot(a, b, preferred_element_type=_f32)


def _rms(x, w):
    return x * lax.rsqrt(jnp.mean(x * x, axis=-1, keepdims=True) + EPS) * w


def _sigmoid(x):
    return 1.0 / (1.0 + jnp.exp(-x))


def _swish(x):
    h = 0.5 * x
    return h + h * jnp.tanh(h)


def _cast_columns(src_ref, dst_ref):
    for c in range(0, src_ref.shape[-1], MXU_TILE):
        dst_ref[:, c:c + MXU_TILE] = src_ref[:, c:c + MXU_TILE].astype(_bf16)


def _mem_kv_kernel(mem_ref, nw_ref, wkv_ref, win32_ref, wout32_ref, kt_ref, v_ref, win_ref, wout_ref):
    m = _rms(mem_ref[0], nw_ref[...]).astype(_bf16)
    k = _dot(m, wkv_ref[:, :D_MODEL].astype(_bf16))
    kt_ref[0] = k.T.astype(_bf16)
    v_ref[0] = _dot(m, wkv_ref[:, D_MODEL:].astype(_bf16)).astype(_bf16)
    _cast_columns(win32_ref.at[0], win_ref.at[0])
    _cast_columns(wout32_ref.at[0], wout_ref.at[0])


def _mem_kv(mem, norm_mem_w, xkv_w, w_in, w_out):
    B = mem.shape[0]
    slab = lambda w: w.reshape(B, w.shape[0] // B, w.shape[1])
    slab_spec = lambda w: pl.BlockSpec((1, w.shape[0] // B, w.shape[1]), lambda b: (b, 0, 0))
    kt, v, win_b, wout_b = pl.pallas_call(
        _mem_kv_kernel,
        grid=(B,),
        in_specs=[
            pl.BlockSpec((1, N_MEM, D_MODEL), lambda b: (b, 0, 0)),
            pl.BlockSpec((1, D_MODEL), lambda b: (0, 0)),
            pl.BlockSpec((D_MODEL, 2 * D_MODEL), lambda b: (0, 0)),
            slab_spec(w_in), slab_spec(w_out),
        ],
        out_specs=[
            pl.BlockSpec((1, D_MODEL, N_MEM), lambda b: (b, 0, 0)),
            pl.BlockSpec((1, N_MEM, D_MODEL), lambda b: (b, 0, 0)),
            slab_spec(w_in), slab_spec(w_out),
        ],
        out_shape=[
            jax.ShapeDtypeStruct((B, D_MODEL, N_MEM), _bf16),
            jax.ShapeDtypeStruct((B, N_MEM, D_MODEL), _bf16),
            jax.ShapeDtypeStruct(slab(w_in).shape, _bf16),
            jax.ShapeDtypeStruct(slab(w_out).shape, _bf16),
        ],
        compiler_params=pltpu.CompilerParams(
            dimension_semantics=("arbitrary",), vmem_limit_bytes=VMEM_LIMIT),
        name="mem_kv",
    )(mem, norm_mem_w.reshape(1, D_MODEL), xkv_w, slab(w_in), slab(w_out))
    return kt, v, win_b.reshape(w_in.shape), wout_b.reshape(w_out.shape)


def _group_norm_halves(y, lo):
    inv = 1.0 / RET_DV
    s_lo = jnp.sum(jnp.where(lo, y, 0.0), axis=-1, keepdims=True)
    s_hi = jnp.sum(jnp.where(lo, 0.0, y), axis=-1, keepdims=True)
    d = y - jnp.where(lo, s_lo, s_hi) * inv
    d2 = d * d
    v_lo = jnp.sum(jnp.where(lo, d2, 0.0), axis=-1, keepdims=True)
    v_hi = jnp.sum(jnp.where(lo, 0.0, d2), axis=-1, keepdims=True)
    return d * lax.rsqrt(jnp.where(lo, v_lo, v_hi) * inv + EPS)


N_CAST = 4
CAST_SCALES = (XATTN_HEAD_DIM ** -0.5, 1.0, 1.0, 1.0)


def _mixer_kernel(*refs):
    n_in = 20
    (x_ref, pos_ref, nw_ref, win_ref, cw_ref, cb_ref, lnw_ref, lnb_ref, gnw_ref, wout_ref,
     dmat_ref, xi_ref, zeta_ref, gl_ref, invf_ref, cost_ref, sint_ref, fwd32_ref, inv32_ref, tap_ref) = refs[:n_in]
    cast_src = refs[n_in:n_in + N_CAST]
    o_ref = refs[n_in + N_CAST]
    cast_dst = refs[n_in + 1 + N_CAST:n_in + 1 + 2 * N_CAST]
    (fwd_ref, inv_ref, hspec, hbuf, hist, ubuf, ybuf, qbuf, ktbuf, vbuf, vzbuf, gbuf, state,
     mbuf) = refs[n_in + 1 + 2 * N_CAST:]
    L = MIX_BLOCK

    @pl.when((pl.program_id(0) == 0) & (pl.program_id(1) == 0))
    def _():
        fwd_ref[...] = fwd32_ref[...].astype(_bf16)
        inv_ref[...] = inv32_ref[...].astype(_bf16)
        ubuf[CONV_HALO + L:, :] = jnp.zeros((CONV_TAIL, CONV_CH), _bf16)
        hspec[...] = jnp.dot(tap_ref[...], cw_ref[...], preferred_element_type=_f32, precision=lax.Precision.HIGHEST)

    @pl.when(pl.program_id(1) == 0)
    def _():
        hist[...] = jnp.zeros((CONV_HALO, CONV_CH), _bf16)
        state[...] = jnp.zeros_like(state)

    for src, dst, scale in zip(cast_src, cast_dst, CAST_SCALES):
        dst[...] = (src[...] if scale == 1.0 else src[...] * scale).astype(_bf16)

    x = x_ref[0]
    ubuf[0:CONV_HALO, :] = hist[...]

    for m in range(0, L, ROW_CHUNK):
        hn_m = _rms(x_ref[0, m:m + ROW_CHUNK, :], nw_ref[...]).astype(_bf16)
        hbuf[m:m + ROW_CHUNK, :] = hn_m
        a = _dot(hn_m, win_ref[:, OFF_A:OFF_A + CONV_CH])
        b = _dot(hn_m, win_ref[:, OFF_B:OFF_B + CONV_CH])
        ubuf[CONV_HALO + m:CONV_HALO + m + ROW_CHUNK, :] = (a * _sigmoid(b)).astype(_bf16)
    hn = hbuf[...]

    ang0 = pos_ref[0, :, 0:1].astype(_f32) * invf_ref[...]
    cos0, sin0 = jnp.cos(ang0), jnp.sin(ang0)
    lane = lax.broadcasted_iota(jnp.int32, (1, RET_QK), 1)
    first_half = lane % RET_DK < RET_DK // 2
    sub_blocks = [slice(r0, r0 + RET_BLOCK) for r0 in range(0, L, RET_BLOCK)]

    def rotary(t, rows):
        cos = cos0 * cost_ref[rows, :] - sin0 * sint_ref[rows, :]
        sin = sin0 * cost_ref[rows, :] + cos0 * sint_ref[rows, :]
        cos2 = jnp.concatenate([cos, cos], axis=-1)
        sin2 = jnp.where(first_half, -1.0, 1.0) * jnp.concatenate([sin, sin], axis=-1)
        partner = jnp.where(first_half, pltpu.roll(t, RET_QK - RET_DK // 2, 1), pltpu.roll(t, RET_DK // 2, 1))
        return t * cos2 + partner * sin2

    q_all = _dot(hn, win_ref[:, OFF_Q:OFF_K])
    for rows in sub_blocks:
        qbuf[rows, :] = rotary(q_all[rows, :], rows).astype(_bf16)
    k_all = _dot(hn, win_ref[:, OFF_K:OFF_V])
    for i, rows in enumerate(sub_blocks):
        ktbuf[i] = rotary(k_all[rows, :], rows).T.astype(_bf16)
    v_all = _dot(hn, win_ref[:, OFF_V:OFF_G])
    vbuf[...] = v_all.astype(_bf16)
    for rows in sub_blocks:
        vzbuf[rows, :] = (v_all[rows, :] * zeta_ref[...]).astype(_bf16)
    g_all = _dot(hn, win_ref[:, OFF_G:D_IN])
    gbuf[...] = _swish(g_all)

    h_re, h_im = hspec[0:FREQ_PAD, :], hspec[FREQ_PAD:2 * FREQ_PAD, :]

    seg_rows = range(0, L, CONV_SEG)
    specs = [_dot(fwd_ref[...], ubuf[row0:row0 + MXU_TILE, :]) for row0 in seg_rows]
    prods = []
    for spec in specs:
        s_re, s_im = spec[0:FREQ_PAD, :], spec[FREQ_PAD:2 * FREQ_PAD, :]
        prods.append(jnp.concatenate([s_re * h_re - s_im * h_im, s_re * h_im + s_im * h_re,
                                      jnp.zeros((MXU_TILE - 2 * FREQ_PAD, CONV_CH), _f32)], axis=0).astype(_bf16))
    for row0, prod in zip(seg_rows, prods):
        ybuf[row0:row0 + CONV_SEG, :] = _dot(inv_ref[...], prod)
    hist[...] = ubuf[L:L + CONV_HALO, :]
    for r in range(0, L, CONV_ROWS):
        acc = ybuf[r:r + CONV_ROWS, :] + cb_ref[...]
        mu = jnp.mean(acc, axis=-1, keepdims=True)
        d = acc - mu
        var = jnp.mean(d * d, axis=-1, keepdims=True)
        yn = d * lax.rsqrt(var + EPS) * lnw_ref[...] + lnb_ref[...]
        mbuf[r:r + CONV_ROWS, 0:CONV_CH] = _swish(yn).astype(_bf16)

    lo = lax.broadcasted_iota(jnp.int32, (1, LANES), 1) < RET_DV
    row_head = lax.broadcasted_iota(jnp.int32, (RET_QK, RET_V), 0) // RET_DK
    col_head = lax.broadcasted_iota(jnp.int32, (RET_QK, RET_V), 1) // RET_DV

    for i, rows in enumerate(sub_blocks):
        qb = qbuf[rows, :]
        kt = ktbuf[i]
        vb = vbuf[rows, :]

        yx = _dot(qb, state[...].astype(_bf16)) * xi_ref[...]

        for p in range(RET_HEADS // 2):
            cols = slice(p * LANES, (p + 1) * LANES)
            blank = jnp.zeros((RET_DK, RET_BLOCK), _bf16)
            k_pair = jnp.concatenate(
                [jnp.concatenate([kt[h * RET_DK:(h + 1) * RET_DK] if h == 2 * p else blank,
                                  kt[h * RET_DK:(h + 1) * RET_DK] if h == 2 * p + 1 else blank], axis=1)
                 for h in range(RET_HEADS)], axis=0)
            s = _dot(qb, k_pair)
            vp = vb[:, cols]
            off = jnp.zeros_like(vp)
            v_pair = jnp.concatenate([jnp.where(lo, vp, off), jnp.where(lo, off, vp)], axis=0)
            y = _dot((s * dmat_ref[p]).astype(_bf16), v_pair) + yx[:, cols]
            yn = _group_norm_halves(y, lo) * gnw_ref[:, cols]
            mbuf[rows, CONV_CH + p * LANES:CONV_CH + (p + 1) * LANES] = (gbuf[rows, cols] * yn).astype(_bf16)

        kv = _dot(kt, vzbuf[rows, :])
        state[...] = gl_ref[...] * state[...] + jnp.where(row_head == col_head, kv, 0.0)

    o_ref[0] = x + _dot(mbuf[...], wout_ref[...])


def _retention_tables(L, step_rows):
    h = np.arange(RET_HEADS, dtype=np.float64)
    log_g = np.log1p(-np.exp2(-5.0 - h))
    idx = np.arange(L, dtype=np.float64)
    dist = np.abs(idx[:, None] - idx[None, :])
    visible = (idx[None, :] // CHUNK) <= (idx[:, None] // CHUNK)
    dmat = np.where(visible[None], np.exp(log_g[:, None, None] * dist[None]), 0.0)
    dmat = np.concatenate([dmat[0::2], dmat[1::2]], axis=-1)
    xi = np.exp(log_g[None, :] * (idx[:, None] + 1.0))
    zeta = np.exp(log_g[None, :] * (L - 1.0 - idx[:, None]))
    gl = np.exp(log_g * L)
    rep = lambda t: np.repeat(t, RET_DV, axis=-1)
    inv_freq = ROPE_BASE ** (-np.arange(RET_DK // 2, dtype=np.float32) / np.float32(RET_DK // 2))
    invf = np.tile(inv_freq.astype(np.float32), LANES // (RET_DK // 2))[None, :]
    rel = np.arange(step_rows, dtype=np.float64)[:, None] * invf.astype(np.float64)
    f = lambda t: jnp.asarray(t, dtype=_f32)
    scale = RET_DK ** -0.5
    return (f(dmat * scale), f(rep(xi) * scale), f(rep(zeta)), f(rep(gl[None, :])), f(invf),
            f(np.cos(rel)), f(np.sin(rel)))


def _conv_dft_tables():
    n = np.arange(DFT_N, dtype=np.float64)
    f = np.arange(N_FREQ, dtype=np.float64)[:, None]
    w = 2.0 * np.pi / DFT_N
    fwd = np.zeros((2 * FREQ_PAD, MXU_TILE))
    fwd[:N_FREQ, :DFT_N] = np.cos(w * f * n[None, :])
    fwd[FREQ_PAD:FREQ_PAD + N_FREQ, :DFT_N] = -np.sin(w * f * n[None, :])
    delay = (CONV_WIDTH - 1) - np.arange(CONV_WIDTH, dtype=np.float64)[None, :]
    tap = np.zeros((2 * FREQ_PAD, CONV_HALO))
    tap[:N_FREQ, :CONV_WIDTH] = np.cos(w * f * delay)
    tap[FREQ_PAD:FREQ_PAD + N_FREQ, :CONV_WIDTH] = -np.sin(w * f * delay)
    weight = np.full((1, N_FREQ), 2.0)
    weight[0, 0] = weight[0, -1] = 1.0
    out = n[CONV_HALO:, None]
    inv = np.zeros((CONV_SEG, MXU_TILE))
    inv[:, :N_FREQ] = weight * np.cos(w * out * f.T) / DFT_N
    inv[:, FREQ_PAD:FREQ_PAD + N_FREQ] = -weight * np.sin(w * out * f.T) / DFT_N
    return [jnp.asarray(t, dtype=_f32) for t in (fwd, inv, tap)]


def _mixer(x, positions, norm_w, w_in, conv_w, conv_b, ln_w, ln_b, gn_w, w_out, later_weights):
    B, S, D = x.shape
    L = MIX_BLOCK
    n = S // L
    steps = B * n
    R = RET_BLOCK
    dmat, xi, zeta, gl, invf, cost, sint = _retention_tables(R, L)
    fwd, inv, tap = _conv_dft_tables()
    taps = jnp.pad(conv_w, ((0, CONV_HALO - CONV_WIDTH), (0, 0)))
    const = lambda shape: pl.BlockSpec(shape, lambda b, j: (0,) * len(shape))
    once = lambda shape: pl.BlockSpec(shape, lambda b, j: (0,) * len(shape), pipeline_mode=pl.Buffered(1))
    row = lambda n: pl.BlockSpec((1, n), lambda b, j: (0, 0))
    slabs = [w.reshape(steps, w.shape[0] // steps, w.shape[1]) for w in later_weights]
    slab_spec = lambda w: pl.BlockSpec((1,) + w.shape[1:], lambda b, j: (b * n + j, 0, 0))
    outs = pl.pallas_call(
        _mixer_kernel,
        grid=(B, n),
        in_specs=[
            pl.BlockSpec((1, L, D), lambda b, j: (b, j, 0)),
            pl.BlockSpec((1, 1, L), lambda b, j: (b * n + j, 0, 0)),
            row(D),
            once((D, D_IN)),
            const((CONV_HALO, CONV_CH)),
            row(CONV_CH), row(CONV_CH), row(CONV_CH), row(RET_V),
            once((D, D)),
            const((RET_HEADS // 2, R, 2 * R)),
            const((R, RET_V)), const((R, RET_V)), row(RET_V), row(LANES),
            const((L, LANES)), const((L, LANES)),
            const(fwd.shape), const(inv.shape), const(tap.shape),
        ] + [slab_spec(w) for w in slabs],
        out_specs=[pl.BlockSpec((1, L, D), lambda b, j: (b, j, 0))] + [slab_spec(w) for w in slabs],
        out_shape=[jax.ShapeDtypeStruct((B, S, D), _f32)] + [jax.ShapeDtypeStruct(w.shape, _bf16) for w in slabs],
        scratch_shapes=[
            pltpu.VMEM(fwd.shape, _bf16),
            pltpu.VMEM(inv.shape, _bf16),
            pltpu.VMEM((2 * FREQ_PAD, CONV_CH), _f32),
            pltpu.VMEM((L, D), _bf16),
            pltpu.VMEM((CONV_HALO, CONV_CH), _bf16),
            pltpu.VMEM((CONV_HALO + L + CONV_TAIL, CONV_CH), _bf16),
            pltpu.VMEM((L, CONV_CH), _f32),
            pltpu.VMEM((L, RET_QK), _bf16),
            pltpu.VMEM((L // R, RET_QK, R), _bf16),
            pltpu.VMEM((L, RET_V), _bf16),
            pltpu.VMEM((L, RET_V), _bf16),
            pltpu.VMEM((L, RET_V), _f32),
            pltpu.VMEM((RET_QK, RET_V), _f32),
            pltpu.VMEM((L, D), _bf16),
        ],
        compiler_params=pltpu.CompilerParams(
            dimension_semantics=("arbitrary", "arbitrary"), vmem_limit_bytes=VMEM_LIMIT),
        name="mixer",
    )(x, positions.reshape(steps, 1, L), norm_w.reshape(1, D), w_in, taps, conv_b.reshape(1, -1),
      ln_w.reshape(1, -1), ln_b.reshape(1, -1), gn_w.reshape(1, -1), w_out, dmat, xi, zeta, gl, invf, cost, sint,
      fwd, inv, tap, *slabs)
    return outs[0], [o.reshape(w.shape) for o, w in zip(outs[1:], later_weights)]


def _xattn_kernel(h_ref, nw_ref, wq_ref, kt_ref, v_ref, wo_ref, o_ref, qbuf, obuf):
    for m in range(0, h_ref.shape[1], ROW_CHUNK):
        hn = _rms(h_ref[0, m:m + ROW_CHUNK, :], nw_ref[...]).astype(_bf16)
        qbuf[m:m + ROW_CHUNK, :] = _dot(hn, wq_ref[...]).astype(_bf16)
    for i in range(XATTN_HEADS):
        cols = slice(i * XATTN_HEAD_DIM, (i + 1) * XATTN_HEAD_DIM)
        s = _dot(qbuf[:, cols], kt_ref[0, cols, :])
        e = jnp.exp(s - jnp.max(s, axis=-1, keepdims=True))
        o = _dot(e.astype(_bf16), v_ref[0, :, cols])
        obuf[:, cols] = (o * (1.0 / jnp.sum(e, axis=-1, keepdims=True))).astype(_bf16)
    o_ref[0] = h_ref[0] + _dot(obuf[...], wo_ref[...])


def _xattn(h, norm_w, xq_w, kt, v, xo_w):
    B, S, D = h.shape
    T = TOK_BLOCK
    return pl.pallas_call(
        _xattn_kernel,
        grid=(B, S // T),
        in_specs=[
            pl.BlockSpec((1, T, D), lambda b, j: (b, j, 0)),
            pl.BlockSpec((1, D), lambda b, j: (0, 0)),
            pl.BlockSpec((D, D), lambda b, j: (0, 0)),
            pl.BlockSpec((1, D, N_MEM), lambda b, j: (b, 0, 0)),
            pl.BlockSpec((1, N_MEM, D), lambda b, j: (b, 0, 0)),
            pl.BlockSpec((D, D), lambda b, j: (0, 0)),
        ],
        out_specs=pl.BlockSpec((1, T, D), lambda b, j: (b, j, 0)),
        out_shape=jax.ShapeDtypeStruct((B, S, D), _f32),
        scratch_shapes=[pltpu.VMEM((T, D), _bf16), pltpu.VMEM((T, D), _bf16)],
        compiler_params=pltpu.CompilerParams(
            dimension_semantics=("arbitrary", "arbitrary"), vmem_limit_bytes=VMEM_LIMIT),
        name="xattn",
    )(h, norm_w.reshape(1, D), xq_w, kt, v, xo_w)


def _mlp_kernel(h_ref, nw_ref, wu_ref, wd_ref, fw_ref, o_ref, abuf):
    for m in range(0, h_ref.shape[0], MLP_ROW_CHUNK):
        hn = _rms(h_ref[m:m + MLP_ROW_CHUNK, :], nw_ref[...]).astype(_bf16)
        for c in range(0, D_FF, FF_CHUNK):
            u = jnp.maximum(_dot(hn, wu_ref[:, c:c + FF_CHUNK]), 0.0)
            abuf[m:m + MLP_ROW_CHUNK, c:c + FF_CHUNK] = (u * u).astype(_bf16)
    o_ref[...] = _rms(h_ref[...] + _dot(abuf[...], wd_ref[...]), fw_ref[...])


def _mlp(h, norm_w, up_w, down_w, norm_f_w):
    B, S, D = h.shape
    T = TOK_BLOCK
    h2 = h.reshape(B * S, D)
    out = pl.pallas_call(
        _mlp_kernel,
        grid=(B * S // T,),
        in_specs=[
            pl.BlockSpec((T, D), lambda i: (i, 0)),
            pl.BlockSpec((1, D), lambda i: (0, 0)),
            pl.BlockSpec((D, D_FF), lambda i: (0, 0), pipeline_mode=pl.Buffered(1)),
            pl.BlockSpec((D_FF, D), lambda i: (0, 0), pipeline_mode=pl.Buffered(1)),
            pl.BlockSpec((1, D), lambda i: (0, 0)),
        ],
        out_specs=pl.BlockSpec((T, D), lambda i: (i, 0)),
        out_shape=jax.ShapeDtypeStruct((B * S, D), _f32),
        scratch_shapes=[pltpu.VMEM((T, D_FF), _bf16)],
        compiler_params=pltpu.CompilerParams(
            dimension_semantics=("arbitrary",), vmem_limit_bytes=VMEM_LIMIT),
        name="mlp",
    )(h2, norm_w.reshape(1, D), up_w, down_w, norm_f_w.reshape(1, D))
    return out.reshape(B, S, D)


def kernel(x, mem, positions, norm_mix_w, w_in, conv_w, conv_b, conv_ln_w, conv_ln_b, ret_gn_w, w_out,
           norm_xattn_w, norm_mem_w, xq_w, xkv_w, xo_w, norm_mlp_w, mlp_up_w, mlp_down_w, norm_f_w):
    kt, v, w_in_b, w_out_b = _mem_kv(mem, norm_mem_w, xkv_w, w_in, w_out)
    h, (xq_b, xo_b, up_b, down_b) = _mixer(x, positions, norm_mix_w, w_in_b, conv_w, conv_b, conv_ln_w, conv_ln_b,
                                           ret_gn_w, w_out_b, (xq_w, xo_w, mlp_up_w, mlp_down_w))
    h = _xattn(h, norm_xattn_w, xq_b, kt, v, xo_b)
    return _mlp(h, norm_mlp_w, up_b, down_b, norm_f_w)
```

```python
import numpy as np
import jax
import jax.numpy as jnp
from jax import lax
from jax.experimental import pallas as pl
from jax.experimental.pallas import tpu as pltpu

D_MODEL = 1024
CHUNK = 64
CONV_CH = 512
CONV_WIDTH = 31
RET_HEADS = 8
RET_DV = 64
RET_DK = 32
RET_QK = RET_HEADS * RET_DK
RET_V = RET_HEADS * RET_DV
N_MEM = 256
XATTN_HEADS = 4
XATTN_HEAD_DIM = 256
D_FF = 4096
ROPE_BASE = 10000.0
EPS = 1e-6

OFF_A, OFF_B, OFF_Q, OFF_K, OFF_V, OFF_G = 0, 512, 1024, 1280, 1536, 2048
D_IN = 2560

LANES = 128
SUBLANES = 8
MXU_TILE = 256
MIX_BLOCK = 512
RET_BLOCK = 128
CONV_HALO = 32
CONV_SEG = 128
DFT_N = CONV_HALO + CONV_SEG
N_FREQ = DFT_N // 2 + 1
FREQ_PAD = -(-N_FREQ // SUBLANES) * SUBLANES
CONV_TAIL = MXU_TILE - DFT_N
CONV_ROWS = 32
TOK_BLOCK = 1024
ROW_CHUNK = 256
MLP_ROW_CHUNK = 256
FF_CHUNK = 1024
VMEM_LIMIT = 56 * 1024 * 1024

_f32 = jnp.float32
_bf16 = jnp.bfloat16


def _dot(a, b):
    return jnp.dot(a, b, preferred_element_type=_f32)


def _rms(x, w):
    return x * lax.rsqrt(jnp.mean(x * x, axis=-1, keepdims=True) + EPS) * w


def _sigmoid(x):
    return 1.0 / (1.0 + jnp.exp(-x))


def _swish(x):
    h = 0.5 * x
    return h + h * jnp.tanh(h)


def _cast_columns(src_ref, dst_ref):
    for c in range(0, src_ref.shape[-1], MXU_TILE):
        dst_ref[:, c:c + MXU_TILE] = src_ref[:, c:c + MXU_TILE].astype(_bf16)


def _mem_kv_kernel(mem_ref, nw_ref, wkv_ref, win32_ref, wout32_ref, kt_ref, v_ref, win_ref, wout_ref):
    m = _rms(mem_ref[0], nw_ref[...]).astype(_bf16)
    k = _dot(m, wkv_ref[:, :D_MODEL].astype(_bf16))
    kt_ref[0] = k.T.astype(_bf16)
    v_ref[0] = _dot(m, wkv_ref[:, D_MODEL:].astype(_bf16)).astype(_bf16)
    _cast_columns(win32_ref.at[0], win_ref.at[0])
    _cast_columns(wout32_ref.at[0], wout_ref.at[0])


def _mem_kv(mem, norm_mem_w, xkv_w, w_in, w_out):
    B = mem.shape[0]
    slab = lambda w: w.reshape(B, w.shape[0] // B, w.shape[1])
    slab_spec = lambda w: pl.BlockSpec((1, w.shape[0] // B, w.shape[1]), lambda b: (b, 0, 0))
    kt, v, win_b, wout_b = pl.pallas_call(
        _mem_kv_kernel,
        grid=(B,),
        in_specs=[
            pl.BlockSpec((1, N_MEM, D_MODEL), lambda b: (b, 0, 0)),
            pl.BlockSpec((1, D_MODEL), lambda b: (0, 0)),
            pl.BlockSpec((D_MODEL, 2 * D_MODEL), lambda b: (0, 0)),
            slab_spec(w_in), slab_spec(w_out),
        ],
        out_specs=[
            pl.BlockSpec((1, D_MODEL, N_MEM), lambda b: (b, 0, 0)),
            pl.BlockSpec((1, N_MEM, D_MODEL), lambda b: (b, 0, 0)),
            slab_spec(w_in), slab_spec(w_out),
        ],
        out_shape=[
            jax.ShapeDtypeStruct((B, D_MODEL, N_MEM), _bf16),
            jax.ShapeDtypeStruct((B, N_MEM, D_MODEL), _bf16),
            jax.ShapeDtypeStruct(slab(w_in).shape, _bf16),
            jax.ShapeDtypeStruct(slab(w_out).shape, _bf16),
        ],
        compiler_params=pltpu.CompilerParams(
            dimension_semantics=("arbitrary",), vmem_limit_bytes=VMEM_LIMIT),
        name="mem_kv",
    )(mem, norm_mem_w.reshape(1, D_MODEL), xkv_w, slab(w_in), slab(w_out))
    return kt, v, win_b.reshape(w_in.shape), wout_b.reshape(w_out.shape)


def _group_norm_halves(y, lo):
    inv = 1.0 / RET_DV
    s_lo = jnp.sum(jnp.where(lo, y, 0.0), axis=-1, keepdims=True)
    s_hi = jnp.sum(jnp.where(lo, 0.0, y), axis=-1, keepdims=True)
    d = y - jnp.where(lo, s_lo, s_hi) * inv
    d2 = d * d
    v_lo = jnp.sum(jnp.where(lo, d2, 0.0), axis=-1, keepdims=True)
    v_hi = jnp.sum(jnp.where(lo, 0.0, d2), axis=-1, keepdims=True)
    return d * lax.rsqrt(jnp.where(lo, v_lo, v_hi) * inv + EPS)


N_CAST = 4
CAST_SCALES = (XATTN_HEAD_DIM ** -0.5, 1.0, 1.0, 1.0)


def _mixer_kernel(*refs):
    n_in = 20
    (x_ref, pos_ref, nw_ref, win_ref, cw_ref, cb_ref, lnw_ref, lnb_ref, gnw_ref, wout_ref,
     dmat_ref, xiq_ref, zeta_ref, gl_ref, invf_ref, cost_ref, sint_ref, fwd32_ref, inv32_ref, tap_ref) = refs[:n_in]
    cast_src = refs[n_in:n_in + N_CAST]
    o_ref = refs[n_in + N_CAST]
    cast_dst = refs[n_in + 1 + N_CAST:n_in + 1 + 2 * N_CAST]
    (fwd_ref, inv_ref, hspec, hbuf, hist, ubuf, ybuf, qbuf, qxbuf, ktbuf, vbuf, vzbuf, gbuf, state,
     mbuf) = refs[n_in + 1 + 2 * N_CAST:]
    L = MIX_BLOCK

    @pl.when((pl.program_id(0) == 0) & (pl.program_id(1) == 0))
    def _():
        fwd_ref[...] = fwd32_ref[...].astype(_bf16)
        inv_ref[...] = inv32_ref[...].astype(_bf16)
        ubuf[CONV_HALO + L:, :] = jnp.zeros((CONV_TAIL, CONV_CH), _bf16)
        hspec[...] = jnp.dot(tap_ref[...], cw_ref[...], preferred_element_type=_f32, precision=lax.Precision.HIGHEST)

    @pl.when(pl.program_id(1) == 0)
    def _():
        hist[...] = jnp.zeros((CONV_HALO, CONV_CH), _bf16)
        state[...] = jnp.zeros_like(state)

    for src, dst, scale in zip(cast_src, cast_dst, CAST_SCALES):
        dst[...] = (src[...] if scale == 1.0 else src[...] * scale).astype(_bf16)

    x = x_ref[0]
    ubuf[0:CONV_HALO, :] = hist[...]

    for m in range(0, L, ROW_CHUNK):
        hn_m = _rms(x_ref[0, m:m + ROW_CHUNK, :], nw_ref[...]).astype(_bf16)
        hbuf[m:m + ROW_CHUNK, :] = hn_m
        a = _dot(hn_m, win_ref[:, OFF_A:OFF_A + CONV_CH])
        b = _dot(hn_m, win_ref[:, OFF_B:OFF_B + CONV_CH])
        ubuf[CONV_HALO + m:CONV_HALO + m + ROW_CHUNK, :] = (a * _sigmoid(b)).astype(_bf16)
    hn = hbuf[...]

    ang0 = pos_ref[0, :, 0:1].astype(_f32) * invf_ref[...]
    cos0, sin0 = jnp.cos(ang0), jnp.sin(ang0)
    lane = lax.broadcasted_iota(jnp.int32, (1, RET_QK), 1)
    first_half = lane % RET_DK < RET_DK // 2
    sub_blocks = [slice(r0, r0 + RET_BLOCK) for r0 in range(0, L, RET_BLOCK)]

    def rotary(t, rows):
        cos = cos0 * cost_ref[rows, :] - sin0 * sint_ref[rows, :]
        sin = sin0 * cost_ref[rows, :] + cos0 * sint_ref[rows, :]
        cos2 = jnp.concatenate([cos, cos], axis=-1)
        sin2 = jnp.where(first_half, -1.0, 1.0) * jnp.concatenate([sin, sin], axis=-1)
        partner = jnp.where(first_half, pltpu.roll(t, RET_QK - RET_DK // 2, 1), pltpu.roll(t, RET_DK // 2, 1))
        return t * cos2 + partner * sin2

    q_all = _dot(hn, win_ref[:, OFF_Q:OFF_K])
    for rows in sub_blocks:
        qr = rotary(q_all[rows, :], rows)
        qbuf[rows, :] = qr.astype(_bf16)
        qxbuf[rows, :] = (qr * xiq_ref[...]).astype(_bf16)
    k_all = _dot(hn, win_ref[:, OFF_K:OFF_V])
    for i, rows in enumerate(sub_blocks):
        ktbuf[i] = rotary(k_all[rows, :], rows).T.astype(_bf16)
    v_all = _dot(hn, win_ref[:, OFF_V:OFF_G])
    vbuf[...] = v_all.astype(_bf16)
    for rows in sub_blocks:
        vzbuf[rows, :] = (v_all[rows, :] * zeta_ref[...]).astype(_bf16)
    g_all = _dot(hn, win_ref[:, OFF_G:D_IN])
    gbuf[...] = _swish(g_all) * gnw_ref[...]

    h_re, h_im = hspec[0:FREQ_PAD, :], hspec[FREQ_PAD:2 * FREQ_PAD, :]

    seg_rows = range(0, L, CONV_SEG)
    specs = [_dot(fwd_ref[...], ubuf[row0:row0 + MXU_TILE, :]) for row0 in seg_rows]
    prods = []
    for spec in specs:
        s_re, s_im = spec[0:FREQ_PAD, :], spec[FREQ_PAD:2 * FREQ_PAD, :]
        prods.append(jnp.concatenate([s_re * h_re - s_im * h_im, s_re * h_im + s_im * h_re,
                                      jnp.zeros((MXU_TILE - 2 * FREQ_PAD, CONV_CH), _f32)], axis=0).astype(_bf16))
    for row0, prod in zip(seg_rows, prods):
        ybuf[row0:row0 + CONV_SEG, :] = _dot(inv_ref[...], prod)
    hist[...] = ubuf[L:L + CONV_HALO, :]
    half_lnw, half_lnb = 0.5 * lnw_ref[...], 0.5 * lnb_ref[...]
    for r in range(0, L, CONV_ROWS):
        acc = ybuf[r:r + CONV_ROWS, :] + cb_ref[...]
        mu = jnp.mean(acc, axis=-1, keepdims=True)
        d = acc - mu
        var = jnp.mean(d * d, axis=-1, keepdims=True)
        half = d * lax.rsqrt(var + EPS) * half_lnw + half_lnb
        mbuf[r:r + CONV_ROWS, 0:CONV_CH] = (half + half * jnp.tanh(half)).astype(_bf16)

    lo = lax.broadcasted_iota(jnp.int32, (1, LANES), 1) < RET_DV
    row_head = lax.broadcasted_iota(jnp.int32, (RET_QK, RET_V), 0) // RET_DK
    col_head = lax.broadcasted_iota(jnp.int32, (RET_QK, RET_V), 1) // RET_DV

    for i, rows in enumerate(sub_blocks):
        qb = qbuf[rows, :]
        kt = ktbuf[i]
        vb = vbuf[rows, :]

        yx = _dot(qxbuf[rows, :], state[...].astype(_bf16))

        for p in range(RET_HEADS // 2):
            cols = slice(p * LANES, (p + 1) * LANES)
            blank = jnp.zeros((RET_DK, RET_BLOCK), _bf16)
            k_pair = jnp.concatenate(
                [jnp.concatenate([kt[h * RET_DK:(h + 1) * RET_DK] if h == 2 * p else blank,
                                  kt[h * RET_DK:(h + 1) * RET_DK] if h == 2 * p + 1 else blank], axis=1)
                 for h in range(RET_HEADS)], axis=0)
            s = _dot(qb, k_pair)
            vp = vb[:, cols]
            off = jnp.zeros_like(vp)
            v_pair = jnp.concatenate([jnp.where(lo, vp, off), jnp.where(lo, off, vp)], axis=0)
            y = _dot((s * dmat_ref[p]).astype(_bf16), v_pair) + yx[:, cols]
            mbuf[rows, CONV_CH + p * LANES:CONV_CH + (p + 1) * LANES] = (
                gbuf[rows, cols] * _group_norm_halves(y, lo)).astype(_bf16)

        kv = _dot(kt, vzbuf[rows, :])
        state[...] = gl_ref[...] * state[...] + jnp.where(row_head == col_head, kv, 0.0)

    o_ref[0] = x + _dot(mbuf[...], wout_ref[...])


def _retention_tables(L, step_rows):
    h = np.arange(RET_HEADS, dtype=np.float64)
    log_g = np.log1p(-np.exp2(-5.0 - h))
    idx = np.arange(L, dtype=np.float64)
    dist = np.abs(idx[:, None] - idx[None, :])
    visible = (idx[None, :] // CHUNK) <= (idx[:, None] // CHUNK)
    dmat = np.where(visible[None], np.exp(log_g[:, None, None] * dist[None]), 0.0)
    dmat = np.concatenate([dmat[0::2], dmat[1::2]], axis=-1)
    xi = np.exp(log_g[None, :] * (idx[:, None] + 1.0))
    zeta = np.exp(log_g[None, :] * (L - 1.0 - idx[:, None]))
    gl = np.exp(log_g * L)
    rep = lambda t: np.repeat(t, RET_DV, axis=-1)
    inv_freq = ROPE_BASE ** (-np.arange(RET_DK // 2, dtype=np.float32) / np.float32(RET_DK // 2))
    invf = np.tile(inv_freq.astype(np.float32), LANES // (RET_DK // 2))[None, :]
    rel = np.arange(step_rows, dtype=np.float64)[:, None] * invf.astype(np.float64)
    f = lambda t: jnp.asarray(t, dtype=_f32)
    scale = RET_DK ** -0.5
    xi_q = np.repeat(xi, RET_DK, axis=-1) * scale
    return (f(dmat * scale), f(xi_q), f(rep(zeta)), f(rep(gl[None, :])), f(invf),
            f(np.cos(rel)), f(np.sin(rel)))


def _conv_dft_tables():
    n = np.arange(DFT_N, dtype=np.float64)
    f = np.arange(N_FREQ, dtype=np.float64)[:, None]
    w = 2.0 * np.pi / DFT_N
    fwd = np.zeros((2 * FREQ_PAD, MXU_TILE))
    fwd[:N_FREQ, :DFT_N] = np.cos(w * f * n[None, :])
    fwd[FREQ_PAD:FREQ_PAD + N_FREQ, :DFT_N] = -np.sin(w * f * n[None, :])
    delay = (CONV_WIDTH - 1) - np.arange(CONV_WIDTH, dtype=np.float64)[None, :]
    tap = np.zeros((2 * FREQ_PAD, CONV_HALO))
    tap[:N_FREQ, :CONV_WIDTH] = np.cos(w * f * delay)
    tap[FREQ_PAD:FREQ_PAD + N_FREQ, :CONV_WIDTH] = -np.sin(w * f * delay)
    weight = np.full((1, N_FREQ), 2.0)
    weight[0, 0] = weight[0, -1] = 1.0
    out = n[CONV_HALO:, None]
    inv = np.zeros((CONV_SEG, MXU_TILE))
    inv[:, :N_FREQ] = weight * np.cos(w * out * f.T) / DFT_N
    inv[:, FREQ_PAD:FREQ_PAD + N_FREQ] = -weight * np.sin(w * out * f.T) / DFT_N
    return [jnp.asarray(t, dtype=_f32) for t in (fwd, inv, tap)]


def _mixer(x, positions, norm_w, w_in, conv_w, conv_b, ln_w, ln_b, gn_w, w_out, later_weights):
    B, S, D = x.shape
    L = MIX_BLOCK
    n = S // L
    steps = B * n
    R = RET_BLOCK
    dmat, xi, zeta, gl, invf, cost, sint = _retention_tables(R, L)
    fwd, inv, tap = _conv_dft_tables()
    taps = jnp.pad(conv_w, ((0, CONV_HALO - CONV_WIDTH), (0, 0)))
    const = lambda shape: pl.BlockSpec(shape, lambda b, j: (0,) * len(shape))
    once = lambda shape: pl.BlockSpec(shape, lambda b, j: (0,) * len(shape), pipeline_mode=pl.Buffered(1))
    row = lambda n: pl.BlockSpec((1, n), lambda b, j: (0, 0))
    slabs = [w.reshape(steps, w.shape[0] // steps, w.shape[1]) for w in later_weights]
    slab_spec = lambda w: pl.BlockSpec((1,) + w.shape[1:], lambda b, j: (b * n + j, 0, 0))
    outs = pl.pallas_call(
        _mixer_kernel,
        grid=(B, n),
        in_specs=[
            pl.BlockSpec((1, L, D), lambda b, j: (b, j, 0)),
            pl.BlockSpec((1, 1, L), lambda b, j: (b * n + j, 0, 0)),
            row(D),
            once((D, D_IN)),
            const((CONV_HALO, CONV_CH)),
            row(CONV_CH), row(CONV_CH), row(CONV_CH), row(RET_V),
            once((D, D)),
            const((RET_HEADS // 2, R, 2 * R)),
            const((R, RET_QK)), const((R, RET_V)), row(RET_V), row(LANES),
            const((L, LANES)), const((L, LANES)),
            const(fwd.shape), const(inv.shape), const(tap.shape),
        ] + [slab_spec(w) for w in slabs],
        out_specs=[pl.BlockSpec((1, L, D), lambda b, j: (b, j, 0))] + [slab_spec(w) for w in slabs],
        out_shape=[jax.ShapeDtypeStruct((B, S, D), _f32)] + [jax.ShapeDtypeStruct(w.shape, _bf16) for w in slabs],
        scratch_shapes=[
            pltpu.VMEM(fwd.shape, _bf16),
            pltpu.VMEM(inv.shape, _bf16),
            pltpu.VMEM((2 * FREQ_PAD, CONV_CH), _f32),
            pltpu.VMEM((L, D), _bf16),
            pltpu.VMEM((CONV_HALO, CONV_CH), _bf16),
            pltpu.VMEM((CONV_HALO + L + CONV_TAIL, CONV_CH), _bf16),
            pltpu.VMEM((L, CONV_CH), _f32),
            pltpu.VMEM((L, RET_QK), _bf16),
            pltpu.VMEM((L, RET_QK), _bf16),
            pltpu.VMEM((L // R, RET_QK, R), _bf16),
            pltpu.VMEM((L, RET_V), _bf16),
            pltpu.VMEM((L, RET_V), _bf16),
            pltpu.VMEM((L, RET_V), _f32),
            pltpu.VMEM((RET_QK, RET_V), _f32),
            pltpu.VMEM((L, D), _bf16),
        ],
        compiler_params=pltpu.CompilerParams(
            dimension_semantics=("arbitrary", "arbitrary"), vmem_limit_bytes=VMEM_LIMIT),
        name="mixer",
    )(x, positions.reshape(steps, 1, L), norm_w.reshape(1, D), w_in, taps, conv_b.reshape(1, -1),
      ln_w.reshape(1, -1), ln_b.reshape(1, -1), gn_w.reshape(1, -1), w_out, dmat, xi, zeta, gl, invf, cost, sint,
      fwd, inv, tap, *slabs)
    return outs[0], [o.reshape(w.shape) for o, w in zip(outs[1:], later_weights)]


def _xattn_kernel(h_ref, nw_ref, wq_ref, kt_ref, v_ref, wo_ref, o_ref, qbuf, obuf):
    for m in range(0, h_ref.shape[1], ROW_CHUNK):
        hn = _rms(h_ref[0, m:m + ROW_CHUNK, :], nw_ref[...]).astype(_bf16)
        qbuf[m:m + ROW_CHUNK, :] = _dot(hn, wq_ref[...]).astype(_bf16)
    for i in range(XATTN_HEADS):
        cols = slice(i * XATTN_HEAD_DIM, (i + 1) * XATTN_HEAD_DIM)
        s = _dot(qbuf[:, cols], kt_ref[0, cols, :])
        e = jnp.exp(s - jnp.max(s, axis=-1, keepdims=True))
        o = _dot(e.astype(_bf16), v_ref[0, :, cols])
        obuf[:, cols] = (o * (1.0 / jnp.sum(e, axis=-1, keepdims=True))).astype(_bf16)
    o_ref[0] = h_ref[0] + _dot(obuf[...], wo_ref[...])


def _xattn(h, norm_w, xq_w, kt, v, xo_w):
    B, S, D = h.shape
    T = TOK_BLOCK
    return pl.pallas_call(
        _xattn_kernel,
        grid=(B, S // T),
        in_specs=[
            pl.BlockSpec((1, T, D), lambda b, j: (b, j, 0)),
            pl.BlockSpec((1, D), lambda b, j: (0, 0)),
            pl.BlockSpec((D, D), lambda b, j: (0, 0)),
            pl.BlockSpec((1, D, N_MEM), lambda b, j: (b, 0, 0)),
            pl.BlockSpec((1, N_MEM, D), lambda b, j: (b, 0, 0)),
            pl.BlockSpec((D, D), lambda b, j: (0, 0)),
        ],
        out_specs=pl.BlockSpec((1, T, D), lambda b, j: (b, j, 0)),
        out_shape=jax.ShapeDtypeStruct((B, S, D), _f32),
        scratch_shapes=[pltpu.VMEM((T, D), _bf16), pltpu.VMEM((T, D), _bf16)],
        compiler_params=pltpu.CompilerParams(
            dimension_semantics=("arbitrary", "arbitrary"), vmem_limit_bytes=VMEM_LIMIT),
        name="xattn",
    )(h, norm_w.reshape(1, D), xq_w, kt, v, xo_w)


def _mlp_kernel(h_ref, nw_ref, wu_ref, wd_ref, fw_ref, o_ref, abuf):
    for m in range(0, h_ref.shape[0], MLP_ROW_CHUNK):
        hn = _rms(h_ref[m:m + MLP_ROW_CHUNK, :], nw_ref[...]).astype(_bf16)
        for c in range(0, D_FF, FF_CHUNK):
            u = jnp.maximum(_dot(hn, wu_ref[:, c:c + FF_CHUNK]), 0.0)
            abuf[m:m + MLP_ROW_CHUNK, c:c + FF_CHUNK] = (u * u).astype(_bf16)
    o_ref[...] = _rms(h_ref[...] + _dot(abuf[...], wd_ref[...]), fw_ref[...])


def _mlp(h, norm_w, up_w, down_w, norm_f_w):
    B, S, D = h.shape
    T = TOK_BLOCK
    h2 = h.reshape(B * S, D)
    out = pl.pallas_call(
        _mlp_kernel,
        grid=(B * S // T,),
        in_specs=[
            pl.BlockSpec((T, D), lambda i: (i, 0)),
            pl.BlockSpec((1, D), lambda i: (0, 0)),
            pl.BlockSpec((D, D_FF), lambda i: (0, 0), pipeline_mode=pl.Buffered(1)),
            pl.BlockSpec((D_FF, D), lambda i: (0, 0), pipeline_mode=pl.Buffered(1)),
            pl.BlockSpec((1, D), lambda i: (0, 0)),
        ],
        out_specs=pl.BlockSpec((T, D), lambda i: (i, 0)),
        out_shape=jax.ShapeDtypeStruct((B * S, D), _f32),
        scratch_shapes=[pltpu.VMEM((T, D_FF), _bf16)],
        compiler_params=pltpu.CompilerParams(
            dimension_semantics=("arbitrary",), vmem_limit_bytes=VMEM_LIMIT),
        name="mlp",
    )(h2, norm_w.reshape(1, D), up_w, down_w, norm_f_w.reshape(1, D))
    return out.reshape(B, S, D)


def kernel(x, mem, positions, norm_mix_w, w_in, conv_w, conv_b, conv_ln_w, conv_ln_b, ret_gn_w, w_out,
           norm_xattn_w, norm_mem_w, xq_w, xkv_w, xo_w, norm_mlp_w, mlp_up_w, mlp_down_w, norm_f_w):
    kt, v, w_in_b, w_out_b = _mem_kv(mem, norm_mem_w, xkv_w, w_in, w_out)
    h, (xq_b, xo_b, up_b, down_b) = _mixer(x, positions, norm_mix_w, w_in_b, conv_w, conv_b, conv_ln_w, conv_ln_b,
                                           ret_gn_w, w_out_b, (xq_w, xo_w, mlp_up_w, mlp_down_w))
    h = _xattn(h, norm_xattn_w, xq_b, kt, v, xo_b)
    return _mlp(h, norm_mlp_w, up_b, down_b, norm_f_w)
```

```python
import numpy as np
import jax
import jax.numpy as jnp
from jax import lax
from jax.experimental import pallas as pl
from jax.experimental.pallas import tpu as pltpu

D_MODEL = 1024
CHUNK = 64
CONV_CH = 512
CONV_WIDTH = 31
RET_HEADS = 8
RET_DV = 64
RET_DK = 32
RET_QK = RET_HEADS * RET_DK
RET_V = RET_HEADS * RET_DV
N_MEM = 256
XATTN_HEADS = 4
XATTN_HEAD_DIM = 256
D_FF = 4096
ROPE_BASE = 10000.0
EPS = 1e-6

OFF_A, OFF_B, OFF_Q, OFF_K, OFF_V, OFF_G = 0, 512, 1024, 1280, 1536, 2048
D_IN = 2560

LANES = 128
SUBLANES = 8
MXU_TILE = 256
MIX_BLOCK = 512
RET_BLOCK = 128
CONV_HALO = 32
CONV_SEG = 128
DFT_N = CONV_HALO + CONV_SEG
N_FREQ = DFT_N // 2 + 1
FREQ_PAD = -(-N_FREQ // SUBLANES) * SUBLANES
CONV_TAIL = MXU_TILE - DFT_N
CONV_ROWS = 32
TOK_BLOCK = 1024
KV_STEPS = 4
ROW_CHUNK = 256
MLP_ROW_CHUNK = 256
FF_CHUNK = 1024
VMEM_LIMIT = 56 * 1024 * 1024

_f32 = jnp.float32
_bf16 = jnp.bfloat16


def _dot(a, b):
    return jnp.dot(a, b, preferred_element_type=_f32)


def _rms(x, w):
    return x * lax.rsqrt(jnp.mean(x * x, axis=-1, keepdims=True) + EPS) * w


def _sigmoid(x):
    return 1.0 / (1.0 + jnp.exp(-x))


def _swish(x):
    h = 0.5 * x
    return h + h * jnp.tanh(h)


def _cast_columns(src_ref, dst_ref):
    for c in range(0, src_ref.shape[-1], MXU_TILE):
        dst_ref[:, c:c + MXU_TILE] = src_ref[:, c:c + MXU_TILE].astype(_bf16)


def _mem_kv_kernel(mem_ref, nw_ref, wkv_ref, win32_ref, wout32_ref, kt_ref, v_ref, win_ref, wout_ref):
    @pl.when(pl.program_id(0) % KV_STEPS == 0)
    def _():
        m = _rms(mem_ref[0], nw_ref[...]).astype(_bf16)
        k = _dot(m, wkv_ref[:, :D_MODEL].astype(_bf16))
        kt_ref[0] = k.T.astype(_bf16)
        v_ref[0] = _dot(m, wkv_ref[:, D_MODEL:].astype(_bf16)).astype(_bf16)

    _cast_columns(win32_ref.at[0], win_ref.at[0])
    _cast_columns(wout32_ref.at[0], wout_ref.at[0])


def _mem_kv(mem, norm_mem_w, xkv_w, w_in, w_out):
    B = mem.shape[0]
    steps = B * KV_STEPS
    slab = lambda w: w.reshape(steps, w.shape[0] // steps, w.shape[1])
    slab_spec = lambda w: pl.BlockSpec((1, w.shape[0] // steps, w.shape[1]), lambda i: (i, 0, 0))
    kt, v, win_b, wout_b = pl.pallas_call(
        _mem_kv_kernel,
        grid=(steps,),
        in_specs=[
            pl.BlockSpec((1, N_MEM, D_MODEL), lambda i: (i // KV_STEPS, 0, 0)),
            pl.BlockSpec((1, D_MODEL), lambda i: (0, 0)),
            pl.BlockSpec((D_MODEL, 2 * D_MODEL), lambda i: (0, 0)),
            slab_spec(w_in), slab_spec(w_out),
        ],
        out_specs=[
            pl.BlockSpec((1, D_MODEL, N_MEM), lambda i: (i // KV_STEPS, 0, 0)),
            pl.BlockSpec((1, N_MEM, D_MODEL), lambda i: (i // KV_STEPS, 0, 0)),
            slab_spec(w_in), slab_spec(w_out),
        ],
        out_shape=[
            jax.ShapeDtypeStruct((B, D_MODEL, N_MEM), _bf16),
            jax.ShapeDtypeStruct((B, N_MEM, D_MODEL), _bf16),
            jax.ShapeDtypeStruct(slab(w_in).shape, _bf16),
            jax.ShapeDtypeStruct(slab(w_out).shape, _bf16),
        ],
        compiler_params=pltpu.CompilerParams(
            dimension_semantics=("arbitrary",), vmem_limit_bytes=VMEM_LIMIT),
        name="mem_kv",
    )(mem, norm_mem_w.reshape(1, D_MODEL), xkv_w, slab(w_in), slab(w_out))
    return kt, v, win_b.reshape(w_in.shape), wout_b.reshape(w_out.shape)


def _group_norm_halves(y, lo):
    inv = 1.0 / RET_DV
    s_lo = jnp.sum(jnp.where(lo, y, 0.0), axis=-1, keepdims=True)
    s_hi = jnp.sum(jnp.where(lo, 0.0, y), axis=-1, keepdims=True)
    d = y - jnp.where(lo, s_lo, s_hi) * inv
    d2 = d * d
    v_lo = jnp.sum(jnp.where(lo, d2, 0.0), axis=-1, keepdims=True)
    v_hi = jnp.sum(jnp.where(lo, 0.0, d2), axis=-1, keepdims=True)
    return d * lax.rsqrt(jnp.where(lo, v_lo, v_hi) * inv + EPS)


N_CAST = 4
CAST_SCALES = (XATTN_HEAD_DIM ** -0.5, 1.0, 1.0, 1.0)


def _mixer_kernel(*refs):
    n_in = 20
    (x_ref, pos_ref, nw_ref, win_ref, cw_ref, cb_ref, lnw_ref, lnb_ref, gnw_ref, wout_ref,
     dmat_ref, xi_ref, zeta_ref, gl_ref, invf_ref, cost_ref, sint_ref, fwd32_ref, inv32_ref, tap_ref) = refs[:n_in]
    cast_src = refs[n_in:n_in + N_CAST]
    o_ref = refs[n_in + N_CAST]
    cast_dst = refs[n_in + 1 + N_CAST:n_in + 1 + 2 * N_CAST]
    (fwd_ref, inv_ref, hspec, hbuf, hist, ubuf, ybuf, qbuf, ktbuf, vbuf, vzbuf, gbuf, state,
     mbuf) = refs[n_in + 1 + 2 * N_CAST:]
    L = MIX_BLOCK

    @pl.when((pl.program_id(0) == 0) & (pl.program_id(1) == 0))
    def _():
        fwd_ref[...] = fwd32_ref[...].astype(_bf16)
        inv_ref[...] = inv32_ref[...].astype(_bf16)
        ubuf[CONV_HALO + L:, :] = jnp.zeros((CONV_TAIL, CONV_CH), _bf16)
        hspec[...] = jnp.dot(tap_ref[...], cw_ref[...], preferred_element_type=_f32, precision=lax.Precision.HIGHEST)

    @pl.when(pl.program_id(1) == 0)
    def _():
        hist[...] = jnp.zeros((CONV_HALO, CONV_CH), _bf16)
        state[...] = jnp.zeros_like(state)

    for src, dst, scale in zip(cast_src, cast_dst, CAST_SCALES):
        dst[...] = (src[...] if scale == 1.0 else src[...] * scale).astype(_bf16)

    x = x_ref[0]
    ubuf[0:CONV_HALO, :] = hist[...]

    for m in range(0, L, ROW_CHUNK):
        hn_m = _rms(x_ref[0, m:m + ROW_CHUNK, :], nw_ref[...]).astype(_bf16)
        hbuf[m:m + ROW_CHUNK, :] = hn_m
        a = _dot(hn_m, win_ref[:, OFF_A:OFF_A + CONV_CH])
        b = _dot(hn_m, win_ref[:, OFF_B:OFF_B + CONV_CH])
        ubuf[CONV_HALO + m:CONV_HALO + m + ROW_CHUNK, :] = (a * _sigmoid(b)).astype(_bf16)
    hn = hbuf[...]

    ang0 = pos_ref[0, :, 0:1].astype(_f32) * invf_ref[...]
    cos0, sin0 = jnp.cos(ang0), jnp.sin(ang0)
    lane = lax.broadcasted_iota(jnp.int32, (1, RET_QK), 1)
    first_half = lane % RET_DK < RET_DK // 2
    sub_blocks = [slice(r0, r0 + RET_BLOCK) for r0 in range(0, L, RET_BLOCK)]

    def rotary(t, rows):
        cos = cos0 * cost_ref[rows, :] - sin0 * sint_ref[rows, :]
        sin = sin0 * cost_ref[rows, :] + cos0 * sint_ref[rows, :]
        cos2 = jnp.concatenate([cos, cos], axis=-1)
        sin2 = jnp.where(first_half, -1.0, 1.0) * jnp.concatenate([sin, sin], axis=-1)
        partner = jnp.where(first_half, pltpu.roll(t, RET_QK - RET_DK // 2, 1), pltpu.roll(t, RET_DK // 2, 1))
        return t * cos2 + partner * sin2

    def project_q():
        q_all = _dot(hn, win_ref[:, OFF_Q:OFF_K])
        for rows in sub_blocks:
            qbuf[rows, :] = rotary(q_all[rows, :], rows).astype(_bf16)

    def project_k():
        k_all = _dot(hn, win_ref[:, OFF_K:OFF_V])
        for i, rows in enumerate(sub_blocks):
            ktbuf[i] = rotary(k_all[rows, :], rows).T.astype(_bf16)

    def project_v():
        v_all = _dot(hn, win_ref[:, OFF_V:OFF_G])
        vbuf[...] = v_all.astype(_bf16)
        for rows in sub_blocks:
            vzbuf[rows, :] = (v_all[rows, :] * zeta_ref[...]).astype(_bf16)

    def project_g():
        gbuf[...] = _swish(_dot(hn, win_ref[:, OFF_G:D_IN]))

    h_re, h_im = hspec[0:FREQ_PAD, :], hspec[FREQ_PAD:2 * FREQ_PAD, :]
    seg_rows = range(0, L, CONV_SEG)

    def conv_forward():
        return [_dot(fwd_ref[...], ubuf[row0:row0 + MXU_TILE, :]) for row0 in seg_rows]

    def conv_product(specs):
        prods = []
        for spec in specs:
            s_re, s_im = spec[0:FREQ_PAD, :], spec[FREQ_PAD:2 * FREQ_PAD, :]
            prods.append(jnp.concatenate([s_re * h_re - s_im * h_im, s_re * h_im + s_im * h_re,
                                          jnp.zeros((MXU_TILE - 2 * FREQ_PAD, CONV_CH), _f32)],
                                         axis=0).astype(_bf16))
        return prods

    def conv_inverse(prods):
        for row0, prod in zip(seg_rows, prods):
            ybuf[row0:row0 + CONV_SEG, :] = _dot(inv_ref[...], prod)
        hist[...] = ubuf[L:L + CONV_HALO, :]

    def conv_norm(row_lo, row_hi):
        half_lnw, half_lnb = 0.5 * lnw_ref[...], 0.5 * lnb_ref[...]
        for r in range(row_lo, row_hi, CONV_ROWS):
            acc = ybuf[r:r + CONV_ROWS, :] + cb_ref[...]
            mu = jnp.mean(acc, axis=-1, keepdims=True)
            d = acc - mu
            var = jnp.mean(d * d, axis=-1, keepdims=True)
            half = d * lax.rsqrt(var + EPS) * half_lnw + half_lnb
            mbuf[r:r + CONV_ROWS, 0:CONV_CH] = (half + half * jnp.tanh(half)).astype(_bf16)

    project_q()
    project_k()
    project_v()
    project_g()
    conv_inverse(conv_product(conv_forward()))
    conv_norm(0, L)

    lo = lax.broadcasted_iota(jnp.int32, (1, LANES), 1) < RET_DV
    row_head = lax.broadcasted_iota(jnp.int32, (RET_QK, RET_V), 0) // RET_DK
    col_head = lax.broadcasted_iota(jnp.int32, (RET_QK, RET_V), 1) // RET_DV

    for i, rows in enumerate(sub_blocks):
        qb = qbuf[rows, :]
        kt = ktbuf[i]
        vb = vbuf[rows, :]

        yx = _dot(qb, state[...].astype(_bf16)) * xi_ref[...]

        for p in range(RET_HEADS // 2):
            cols = slice(p * LANES, (p + 1) * LANES)
            blank = jnp.zeros((RET_DK, RET_BLOCK), _bf16)
            k_pair = jnp.concatenate(
                [jnp.concatenate([kt[h * RET_DK:(h + 1) * RET_DK] if h == 2 * p else blank,
                                  kt[h * RET_DK:(h + 1) * RET_DK] if h == 2 * p + 1 else blank], axis=1)
                 for h in range(RET_HEADS)], axis=0)
            s = _dot(qb, k_pair)
            vp = vb[:, cols]
            off = jnp.zeros_like(vp)
            v_pair = jnp.concatenate([jnp.where(lo, vp, off), jnp.where(lo, off, vp)], axis=0)
            y = _dot((s * dmat_ref[p]).astype(_bf16), v_pair) + yx[:, cols]
            yn = _group_norm_halves(y, lo) * gnw_ref[:, cols]
            mbuf[rows, CONV_CH + p * LANES:CONV_CH + (p + 1) * LANES] = (gbuf[rows, cols] * yn).astype(_bf16)

        kv = _dot(kt, vzbuf[rows, :])
        state[...] = gl_ref[...] * state[...] + jnp.where(row_head == col_head, kv, 0.0)

    o_ref[0] = x + _dot(mbuf[...], wout_ref[...])


def _retention_tables(L, step_rows):
    h = np.arange(RET_HEADS, dtype=np.float64)
    log_g = np.log1p(-np.exp2(-5.0 - h))
    idx = np.arange(L, dtype=np.float64)
    dist = np.abs(idx[:, None] - idx[None, :])
    visible = (idx[None, :] // CHUNK) <= (idx[:, None] // CHUNK)
    dmat = np.where(visible[None], np.exp(log_g[:, None, None] * dist[None]), 0.0)
    dmat = np.concatenate([dmat[0::2], dmat[1::2]], axis=-1)
    xi = np.exp(log_g[None, :] * (idx[:, None] + 1.0))
    zeta = np.exp(log_g[None, :] * (L - 1.0 - idx[:, None]))
    gl = np.exp(log_g * L)
    rep = lambda t: np.repeat(t, RET_DV, axis=-1)
    inv_freq = ROPE_BASE ** (-np.arange(RET_DK // 2, dtype=np.float32) / np.float32(RET_DK // 2))
    invf = np.tile(inv_freq.astype(np.float32), LANES // (RET_DK // 2))[None, :]
    rel = np.arange(step_rows, dtype=np.float64)[:, None] * invf.astype(np.float64)
    f = lambda t: jnp.asarray(t, dtype=_f32)
    scale = RET_DK ** -0.5
    return (f(dmat * scale), f(rep(xi) * scale), f(rep(zeta)), f(rep(gl[None, :])), f(invf),
            f(np.cos(rel)), f(np.sin(rel)))


def _conv_dft_tables():
    n = np.arange(DFT_N, dtype=np.float64)
    f = np.arange(N_FREQ, dtype=np.float64)[:, None]
    w = 2.0 * np.pi / DFT_N
    fwd = np.zeros((2 * FREQ_PAD, MXU_TILE))
    fwd[:N_FREQ, :DFT_N] = np.cos(w * f * n[None, :])
    fwd[FREQ_PAD:FREQ_PAD + N_FREQ, :DFT_N] = -np.sin(w * f * n[None, :])
    delay = (CONV_WIDTH - 1) - np.arange(CONV_WIDTH, dtype=np.float64)[None, :]
    tap = np.zeros((2 * FREQ_PAD, CONV_HALO))
    tap[:N_FREQ, :CONV_WIDTH] = np.cos(w * f * delay)
    tap[FREQ_PAD:FREQ_PAD + N_FREQ, :CONV_WIDTH] = -np.sin(w * f * delay)
    weight = np.full((1, N_FREQ), 2.0)
    weight[0, 0] = weight[0, -1] = 1.0
    out = n[CONV_HALO:, None]
    inv = np.zeros((CONV_SEG, MXU_TILE))
    inv[:, :N_FREQ] = weight * np.cos(w * out * f.T) / DFT_N
    inv[:, FREQ_PAD:FREQ_PAD + N_FREQ] = -weight * np.sin(w * out * f.T) / DFT_N
    return [jnp.asarray(t, dtype=_f32) for t in (fwd, inv, tap)]


def _mixer(x, positions, norm_w, w_in, conv_w, conv_b, ln_w, ln_b, gn_w, w_out, later_weights):
    B, S, D = x.shape
    L = MIX_BLOCK
    n = S // L
    steps = B * n
    R = RET_BLOCK
    dmat, xi, zeta, gl, invf, cost, sint = _retention_tables(R, L)
    fwd, inv, tap = _conv_dft_tables()
    taps = jnp.pad(conv_w, ((0, CONV_HALO - CONV_WIDTH), (0, 0)))
    const = lambda shape: pl.BlockSpec(shape, lambda b, j: (0,) * len(shape))
    once = lambda shape: pl.BlockSpec(shape, lambda b, j: (0,) * len(shape), pipeline_mode=pl.Buffered(1))
    row = lambda n: pl.BlockSpec((1, n), lambda b, j: (0, 0))
    slabs = [w.reshape(steps, w.shape[0] // steps, w.shape[1]) for w in later_weights]
    slab_spec = lambda w: pl.BlockSpec((1,) + w.shape[1:], lambda b, j: (b * n + j, 0, 0))
    outs = pl.pallas_call(
        _mixer_kernel,
        grid=(B, n),
        in_specs=[
            pl.BlockSpec((1, L, D), lambda b, j: (b, j, 0)),
            pl.BlockSpec((1, 1, L), lambda b, j: (b * n + j, 0, 0)),
            row(D),
            once((D, D_IN)),
            const((CONV_HALO, CONV_CH)),
            row(CONV_CH), row(CONV_CH), row(CONV_CH), row(RET_V),
            once((D, D)),
            const((RET_HEADS // 2, R, 2 * R)),
            const((R, RET_V)), const((R, RET_V)), row(RET_V), row(LANES),
            const((L, LANES)), const((L, LANES)),
            const(fwd.shape), const(inv.shape), const(tap.shape),
        ] + [slab_spec(w) for w in slabs],
        out_specs=[pl.BlockSpec((1, L, D), lambda b, j: (b, j, 0))] + [slab_spec(w) for w in slabs],
        out_shape=[jax.ShapeDtypeStruct((B, S, D), _f32)] + [jax.ShapeDtypeStruct(w.shape, _bf16) for w in slabs],
        scratch_shapes=[
            pltpu.VMEM(fwd.shape, _bf16),
            pltpu.VMEM(inv.shape, _bf16),
            pltpu.VMEM((2 * FREQ_PAD, CONV_CH), _f32),
            pltpu.VMEM((L, D), _bf16),
            pltpu.VMEM((CONV_HALO, CONV_CH), _bf16),
            pltpu.VMEM((CONV_HALO + L + CONV_TAIL, CONV_CH), _bf16),
            pltpu.VMEM((L, CONV_CH), _f32),
            pltpu.VMEM((L, RET_QK), _bf16),
            pltpu.VMEM((L // R, RET_QK, R), _bf16),
            pltpu.VMEM((L, RET_V), _bf16),
            pltpu.VMEM((L, RET_V), _bf16),
            pltpu.VMEM((L, RET_V), _f32),
            pltpu.VMEM((RET_QK, RET_V), _f32),
            pltpu.VMEM((L, D), _bf16),
        ],
        compiler_params=pltpu.CompilerParams(
            dimension_semantics=("arbitrary", "arbitrary"), vmem_limit_bytes=VMEM_LIMIT),
        name="mixer",
    )(x, positions.reshape(steps, 1, L), norm_w.reshape(1, D), w_in, taps, conv_b.reshape(1, -1),
      ln_w.reshape(1, -1), ln_b.reshape(1, -1), gn_w.reshape(1, -1), w_out, dmat, xi, zeta, gl, invf, cost, sint,
      fwd, inv, tap, *slabs)
    return outs[0], [o.reshape(w.shape) for o, w in zip(outs[1:], later_weights)]


def _xattn_kernel(h_ref, nw_ref, wq_ref, kt_ref, v_ref, wo_ref, o_ref, qbuf, obuf):
    for m in range(0, h_ref.shape[1], ROW_CHUNK):
        hn = _rms(h_ref[0, m:m + ROW_CHUNK, :], nw_ref[...]).astype(_bf16)
        qbuf[m:m + ROW_CHUNK, :] = _dot(hn, wq_ref[...]).astype(_bf16)
    for i in range(XATTN_HEADS):
        cols = slice(i * XATTN_HEAD_DIM, (i + 1) * XATTN_HEAD_DIM)
        s = _dot(qbuf[:, cols], kt_ref[0, cols, :])
        e = jnp.exp(s - jnp.max(s, axis=-1, keepdims=True))
        o = _dot(e.astype(_bf16), v_ref[0, :, cols])
        obuf[:, cols] = (o * (1.0 / jnp.sum(e, axis=-1, keepdims=True))).astype(_bf16)
    o_ref[0] = h_ref[0] + _dot(obuf[...], wo_ref[...])


def _xattn(h, norm_w, xq_w, kt, v, xo_w):
    B, S, D = h.shape
    T = TOK_BLOCK
    return pl.pallas_call(
        _xattn_kernel,
        grid=(B, S // T),
        in_specs=[
            pl.BlockSpec((1, T, D), lambda b, j: (b, j, 0)),
            pl.BlockSpec((1, D), lambda b, j: (0, 0)),
            pl.BlockSpec((D, D), lambda b, j: (0, 0)),
            pl.BlockSpec((1, D, N_MEM), lambda b, j: (b, 0, 0)),
            pl.BlockSpec((1, N_MEM, D), lambda b, j: (b, 0, 0)),
            pl.BlockSpec((D, D), lambda b, j: (0, 0)),
        ],
        out_specs=pl.BlockSpec((1, T, D), lambda b, j: (b, j, 0)),
        out_shape=jax.ShapeDtypeStruct((B, S, D), _f32),
        scratch_shapes=[pltpu.VMEM((T, D), _bf16), pltpu.VMEM((T, D), _bf16)],
        compiler_params=pltpu.CompilerParams(
            dimension_semantics=("arbitrary", "arbitrary"), vmem_limit_bytes=VMEM_LIMIT),
        name="xattn",
    )(h, norm_w.reshape(1, D), xq_w, kt, v, xo_w)


def _mlp_kernel(h_ref, nw_ref, wu_ref, wd_ref, fw_ref, o_ref, abuf):
    for m in range(0, h_ref.shape[0], MLP_ROW_CHUNK):
        hn = _rms(h_ref[m:m + MLP_ROW_CHUNK, :], nw_ref[...]).astype(_bf16)
        for c in range(0, D_FF, FF_CHUNK):
            u = jnp.maximum(_dot(hn, wu_ref[:, c:c + FF_CHUNK]), 0.0)
            abuf[m:m + MLP_ROW_CHUNK, c:c + FF_CHUNK] = (u * u).astype(_bf16)
    o_ref[...] = _rms(h_ref[...] + _dot(abuf[...], wd_ref[...]), fw_ref[...])


def _mlp(h, norm_w, up_w, down_w, norm_f_w):
    B, S, D = h.shape
    T = TOK_BLOCK
    h2 = h.reshape(B * S, D)
    out = pl.pallas_call(
        _mlp_kernel,
        grid=(B * S // T,),
        in_specs=[
            pl.BlockSpec((T, D), lambda i: (i, 0)),
            pl.BlockSpec((1, D), lambda i: (0, 0)),
            pl.BlockSpec((D, D_FF), lambda i: (0, 0), pipeline_mode=pl.Buffered(1)),
            pl.BlockSpec((D_FF, D), lambda i: (0, 0), pipeline_mode=pl.Buffered(1)),
            pl.BlockSpec((1, D), lambda i: (0, 0)),
        ],
        out_specs=pl.BlockSpec((T, D), lambda i: (i, 0)),
        out_shape=jax.ShapeDtypeStruct((B * S, D), _f32),
        scratch_shapes=[pltpu.VMEM((T, D_FF), _bf16)],
        compiler_params=pltpu.CompilerParams(
            dimension_semantics=("arbitrary",), vmem_limit_bytes=VMEM_LIMIT),
        name="mlp",
    )(h2, norm_w.reshape(1, D), up_w, down_w, norm_f_w.reshape(1, D))
    return out.reshape(B, S, D)


def kernel(x, mem, positions, norm_mix_w, w_in, conv_w, conv_b, conv_ln_w, conv_ln_b, ret_gn_w, w_out,
           norm_xattn_w, norm_mem_w, xq_w, xkv_w, xo_w, norm_mlp_w, mlp_up_w, mlp_down_w, norm_f_w):
    kt, v, w_in_b, w_out_b = _mem_kv(mem, norm_mem_w, xkv_w, w_in, w_out)
    h, (xq_b, xo_b, up_b, down_b) = _mixer(x, positions, norm_mix_w, w_in_b, conv_w, conv_b, conv_ln_w, conv_ln_b,
                                           ret_gn_w, w_out_b, (xq_w, xo_w, mlp_up_w, mlp_down_w))
    h = _xattn(h, norm_xattn_w, xq_b, kt, v, xo_b)
    return _mlp(h, norm_mlp_w, up_b, down_b, norm_f_w)
```

```python
import numpy as np
import jax
import jax.numpy as jnp
from jax import lax
from jax.experimental import pallas as pl
from jax.experimental.pallas import tpu as pltpu

D_MODEL = 1024
CHUNK = 64
CONV_CH = 512
CONV_WIDTH = 31
RET_HEADS = 8
RET_DV = 64
RET_DK = 32
RET_QK = RET_HEADS * RET_DK
RET_V = RET_HEADS * RET_DV
N_MEM = 256
XATTN_HEADS = 4
XATTN_HEAD_DIM = 256
D_FF = 4096
ROPE_BASE = 10000.0
EPS = 1e-6

OFF_A, OFF_B, OFF_Q, OFF_K, OFF_V, OFF_G = 0, 512, 1024, 1280, 1536, 2048
D_IN = 2560

LANES = 128
SUBLANES = 8
MXU_TILE = 256
MIX_BLOCK = 512
RET_BLOCK = 128
CONV_HALO = 32
CONV_SEG = 128
DFT_N = CONV_HALO + CONV_SEG
N_FREQ = DFT_N // 2 + 1
FREQ_PAD = -(-N_FREQ // SUBLANES) * SUBLANES
CONV_TAIL = MXU_TILE - DFT_N
CONV_ROWS = 32
TOK_BLOCK = 1024
KV_STEPS = 1
ROW_CHUNK = 256
MLP_ROW_CHUNK = 256
FF_CHUNK = 1024
VMEM_LIMIT = 56 * 1024 * 1024

_f32 = jnp.float32
_bf16 = jnp.bfloat16


def _dot(a, b):
    return jnp.dot(a, b, preferred_element_type=_f32)


def _rms(x, w):
    return x * lax.rsqrt(jnp.mean(x * x, axis=-1, keepdims=True) + EPS) * w


def _sigmoid(x):
    return 1.0 / (1.0 + jnp.exp(-x))


def _swish(x):
    h = 0.5 * x
    return h + h * jnp.tanh(h)


def _cast_columns(src_ref, dst_ref):
    for c in range(0, src_ref.shape[-1], MXU_TILE):
        dst_ref[:, c:c + MXU_TILE] = src_ref[:, c:c + MXU_TILE].astype(_bf16)


def _mem_kv_kernel(mem_ref, nw_ref, wkv_ref, win32_ref, wout32_ref, kt_ref, v_ref, win_ref, wout_ref):
    @pl.when(pl.program_id(0) % KV_STEPS == 0)
    def _():
        m = _rms(mem_ref[0], nw_ref[...]).astype(_bf16)
        k = _dot(m, wkv_ref[:, :D_MODEL].astype(_bf16))
        kt_ref[0] = k.T.astype(_bf16)
        v_ref[0] = _dot(m, wkv_ref[:, D_MODEL:].astype(_bf16)).astype(_bf16)

    _cast_columns(win32_ref.at[0], win_ref.at[0])
    _cast_columns(wout32_ref.at[0], wout_ref.at[0])


def _mem_kv(mem, norm_mem_w, xkv_w, w_in, w_out):
    B = mem.shape[0]
    steps = B * KV_STEPS
    slab = lambda w: w.reshape(steps, w.shape[0] // steps, w.shape[1])
    slab_spec = lambda w: pl.BlockSpec((1, w.shape[0] // steps, w.shape[1]), lambda i: (i, 0, 0))
    kt, v, win_b, wout_b = pl.pallas_call(
        _mem_kv_kernel,
        grid=(steps,),
        in_specs=[
            pl.BlockSpec((1, N_MEM, D_MODEL), lambda i: (i // KV_STEPS, 0, 0)),
            pl.BlockSpec((1, D_MODEL), lambda i: (0, 0)),
            pl.BlockSpec((D_MODEL, 2 * D_MODEL), lambda i: (0, 0)),
            slab_spec(w_in), slab_spec(w_out),
        ],
        out_specs=[
            pl.BlockSpec((1, D_MODEL, N_MEM), lambda i: (i // KV_STEPS, 0, 0)),
            pl.BlockSpec((1, N_MEM, D_MODEL), lambda i: (i // KV_STEPS, 0, 0)),
            slab_spec(w_in), slab_spec(w_out),
        ],
        out_shape=[
            jax.ShapeDtypeStruct((B, D_MODEL, N_MEM), _bf16),
            jax.ShapeDtypeStruct((B, N_MEM, D_MODEL), _bf16),
            jax.ShapeDtypeStruct(slab(w_in).shape, _bf16),
            jax.ShapeDtypeStruct(slab(w_out).shape, _bf16),
        ],
        compiler_params=pltpu.CompilerParams(
            dimension_semantics=("arbitrary",), vmem_limit_bytes=VMEM_LIMIT),
        name="mem_kv",
    )(mem, norm_mem_w.reshape(1, D_MODEL), xkv_w, slab(w_in), slab(w_out))
    return kt, v, win_b.reshape(w_in.shape), wout_b.reshape(w_out.shape)


def _group_norm_halves(y, lo):
    inv = 1.0 / RET_DV
    s_lo = jnp.sum(jnp.where(lo, y, 0.0), axis=-1, keepdims=True)
    s_hi = jnp.sum(jnp.where(lo, 0.0, y), axis=-1, keepdims=True)
    d = y - jnp.where(lo, s_lo, s_hi) * inv
    d2 = d * d
    v_lo = jnp.sum(jnp.where(lo, d2, 0.0), axis=-1, keepdims=True)
    v_hi = jnp.sum(jnp.where(lo, 0.0, d2), axis=-1, keepdims=True)
    return d * lax.rsqrt(jnp.where(lo, v_lo, v_hi) * inv + EPS)


N_CAST = 4
CAST_SCALES = (XATTN_HEAD_DIM ** -0.5, 1.0, 1.0, 1.0)


def _mixer_kernel(*refs):
    n_in = 20
    (x_ref, pos_ref, nw_ref, win_ref, cw_ref, cb_ref, lnw_ref, lnb_ref, gnw_ref, wout_ref,
     dmat_ref, xi_ref, zeta_ref, gl_ref, invf_ref, cost_ref, sint_ref, fwd32_ref, inv32_ref, tap_ref) = refs[:n_in]
    cast_src = refs[n_in:n_in + N_CAST]
    o_ref = refs[n_in + N_CAST]
    cast_dst = refs[n_in + 1 + N_CAST:n_in + 1 + 2 * N_CAST]
    (fwd_ref, inv_ref, hspec, hbuf, hist, ubuf, ybuf, qbuf, ktbuf, vbuf, vzbuf, gbuf, state,
     mbuf) = refs[n_in + 1 + 2 * N_CAST:]
    L = MIX_BLOCK

    @pl.when((pl.program_id(0) == 0) & (pl.program_id(1) == 0))
    def _():
        fwd_ref[...] = fwd32_ref[...].astype(_bf16)
        inv_ref[...] = inv32_ref[...].astype(_bf16)
        ubuf[...] = jnp.zeros(ubuf.shape, _bf16)
        hspec[...] = jnp.dot(tap_ref[...], cw_ref[...], preferred_element_type=_f32, precision=lax.Precision.HIGHEST)

    @pl.when(pl.program_id(1) == 0)
    def _():
        hist[...] = jnp.zeros((CONV_HALO, CONV_CH), _bf16)
        state[...] = jnp.zeros_like(state)

    for src, dst, scale in zip(cast_src, cast_dst, CAST_SCALES):
        dst[...] = (src[...] if scale == 1.0 else src[...] * scale).astype(_bf16)

    x = x_ref[0]
    ubuf[0:CONV_HALO, :] = hist[...]

    def glu_rows(row_lo, row_hi):
        for m in range(row_lo, row_hi, ROW_CHUNK):
            hn_m = _rms(x_ref[0, m:m + ROW_CHUNK, :], nw_ref[...]).astype(_bf16)
            hbuf[m:m + ROW_CHUNK, :] = hn_m
            a = _dot(hn_m, win_ref[:, OFF_A:OFF_A + CONV_CH])
            b = _dot(hn_m, win_ref[:, OFF_B:OFF_B + CONV_CH])
            ubuf[CONV_HALO + m:CONV_HALO + m + ROW_CHUNK, :] = (a * _sigmoid(b)).astype(_bf16)

    ang0 = pos_ref[0, :, 0:1].astype(_f32) * invf_ref[...]
    cos0, sin0 = jnp.cos(ang0), jnp.sin(ang0)
    lane = lax.broadcasted_iota(jnp.int32, (1, RET_QK), 1)
    first_half = lane % RET_DK < RET_DK // 2
    sub_blocks = [slice(r0, r0 + RET_BLOCK) for r0 in range(0, L, RET_BLOCK)]

    def rotary(t, rows):
        cos = cos0 * cost_ref[rows, :] - sin0 * sint_ref[rows, :]
        sin = sin0 * cost_ref[rows, :] + cos0 * sint_ref[rows, :]
        cos2 = jnp.concatenate([cos, cos], axis=-1)
        sin2 = jnp.where(first_half, -1.0, 1.0) * jnp.concatenate([sin, sin], axis=-1)
        partner = jnp.where(first_half, pltpu.roll(t, RET_QK - RET_DK // 2, 1), pltpu.roll(t, RET_DK // 2, 1))
        return t * cos2 + partner * sin2

    def project_rows(row_lo, row_hi):
        hn = hbuf[row_lo:row_hi, :]
        mine = [(i, rows, slice(rows.start - row_lo, rows.stop - row_lo)) for i, rows in enumerate(sub_blocks)
                if row_lo <= rows.start < row_hi]
        q_all = _dot(hn, win_ref[:, OFF_Q:OFF_K])
        for _, rows, local in mine:
            qbuf[rows, :] = rotary(q_all[local, :], rows).astype(_bf16)
        k_all = _dot(hn, win_ref[:, OFF_K:OFF_V])
        for i, rows, local in mine:
            ktbuf[i] = rotary(k_all[local, :], rows).T.astype(_bf16)
        v_all = _dot(hn, win_ref[:, OFF_V:OFF_G])
        vbuf[row_lo:row_hi, :] = v_all.astype(_bf16)
        for _, rows, local in mine:
            vzbuf[rows, :] = (v_all[local, :] * zeta_ref[...]).astype(_bf16)
        gbuf[row_lo:row_hi, :] = _swish(_dot(hn, win_ref[:, OFF_G:D_IN]))

    h_re, h_im = hspec[0:FREQ_PAD, :], hspec[FREQ_PAD:2 * FREQ_PAD, :]

    def conv_rows(row_lo, row_hi):
        seg_rows = range(row_lo, row_hi, CONV_SEG)
        specs = [_dot(fwd_ref[...], ubuf[row0:row0 + MXU_TILE, :]) for row0 in seg_rows]
        prods = []
        for spec in specs:
            s_re, s_im = spec[0:FREQ_PAD, :], spec[FREQ_PAD:2 * FREQ_PAD, :]
            prods.append(jnp.concatenate([s_re * h_re - s_im * h_im, s_re * h_im + s_im * h_re,
                                          jnp.zeros((MXU_TILE - 2 * FREQ_PAD, CONV_CH), _f32)],
                                         axis=0).astype(_bf16))
        for row0, prod in zip(seg_rows, prods):
            ybuf[row0:row0 + CONV_SEG, :] = _dot(inv_ref[...], prod)
        half_lnw, half_lnb = 0.5 * lnw_ref[...], 0.5 * lnb_ref[...]
        for r in range(row_lo, row_hi, CONV_ROWS):
            acc = ybuf[r:r + CONV_ROWS, :] + cb_ref[...]
            mu = jnp.mean(acc, axis=-1, keepdims=True)
            d = acc - mu
            var = jnp.mean(d * d, axis=-1, keepdims=True)
            half = d * lax.rsqrt(var + EPS) * half_lnw + half_lnb
            mbuf[r:r + CONV_ROWS, 0:CONV_CH] = (half + half * jnp.tanh(half)).astype(_bf16)

    lo = lax.broadcasted_iota(jnp.int32, (1, LANES), 1) < RET_DV
    row_head = lax.broadcasted_iota(jnp.int32, (RET_QK, RET_V), 0) // RET_DK
    col_head = lax.broadcasted_iota(jnp.int32, (RET_QK, RET_V), 1) // RET_DV

    def retention(i):
        rows = sub_blocks[i]
        qb = qbuf[rows, :]
        kt = ktbuf[i]
        vb = vbuf[rows, :]

        yx = _dot(qb, state[...].astype(_bf16)) * xi_ref[...]

        for p in range(RET_HEADS // 2):
            cols = slice(p * LANES, (p + 1) * LANES)
            blank = jnp.zeros((RET_DK, RET_BLOCK), _bf16)
            k_pair = jnp.concatenate(
                [jnp.concatenate([kt[h * RET_DK:(h + 1) * RET_DK] if h == 2 * p else blank,
                                  kt[h * RET_DK:(h + 1) * RET_DK] if h == 2 * p + 1 else blank], axis=1)
                 for h in range(RET_HEADS)], axis=0)
            s = _dot(qb, k_pair)
            vp = vb[:, cols]
            off = jnp.zeros_like(vp)
            v_pair = jnp.concatenate([jnp.where(lo, vp, off), jnp.where(lo, off, vp)], axis=0)
            y = _dot((s * dmat_ref[p]).astype(_bf16), v_pair) + yx[:, cols]
            yn = _group_norm_halves(y, lo) * gnw_ref[:, cols]
            mbuf[rows, CONV_CH + p * LANES:CONV_CH + (p + 1) * LANES] = (gbuf[rows, cols] * yn).astype(_bf16)

        kv = _dot(kt, vzbuf[rows, :])
        state[...] = gl_ref[...] * state[...] + jnp.where(row_head == col_head, kv, 0.0)

    half_rows = L // 2
    per_half = len(sub_blocks) // 2
    glu_rows(0, half_rows)
    project_rows(0, half_rows)
    conv_rows(0, half_rows)
    glu_rows(half_rows, L)
    hist[...] = ubuf[L:L + CONV_HALO, :]
    retention(0)
    project_rows(half_rows, L)
    for i in range(1, per_half):
        retention(i)
    conv_rows(half_rows, L)
    for i in range(per_half, len(sub_blocks)):
        retention(i)

    o_ref[0] = x + _dot(mbuf[...], wout_ref[...])


def _retention_tables(L, step_rows):
    h = np.arange(RET_HEADS, dtype=np.float64)
    log_g = np.log1p(-np.exp2(-5.0 - h))
    idx = np.arange(L, dtype=np.float64)
    dist = np.abs(idx[:, None] - idx[None, :])
    visible = (idx[None, :] // CHUNK) <= (idx[:, None] // CHUNK)
    dmat = np.where(visible[None], np.exp(log_g[:, None, None] * dist[None]), 0.0)
    dmat = np.concatenate([dmat[0::2], dmat[1::2]], axis=-1)
    xi = np.exp(log_g[None, :] * (idx[:, None] + 1.0))
    zeta = np.exp(log_g[None, :] * (L - 1.0 - idx[:, None]))
    gl = np.exp(log_g * L)
    rep = lambda t: np.repeat(t, RET_DV, axis=-1)
    inv_freq = ROPE_BASE ** (-np.arange(RET_DK // 2, dtype=np.float32) / np.float32(RET_DK // 2))
    invf = np.tile(inv_freq.astype(np.float32), LANES // (RET_DK // 2))[None, :]
    rel = np.arange(step_rows, dtype=np.float64)[:, None] * invf.astype(np.float64)
    f = lambda t: jnp.asarray(t, dtype=_f32)
    scale = RET_DK ** -0.5
    return (f(dmat * scale), f(rep(xi) * scale), f(rep(zeta)), f(rep(gl[None, :])), f(invf),
            f(np.cos(rel)), f(np.sin(rel)))


def _conv_dft_tables():
    n = np.arange(DFT_N, dtype=np.float64)
    f = np.arange(N_FREQ, dtype=np.float64)[:, None]
    w = 2.0 * np.pi / DFT_N
    fwd = np.zeros((2 * FREQ_PAD, MXU_TILE))
    fwd[:N_FREQ, :DFT_N] = np.cos(w * f * n[None, :])
    fwd[FREQ_PAD:FREQ_PAD + N_FREQ, :DFT_N] = -np.sin(w * f * n[None, :])
    delay = (CONV_WIDTH - 1) - np.arange(CONV_WIDTH, dtype=np.float64)[None, :]
    tap = np.zeros((2 * FREQ_PAD, CONV_HALO))
    tap[:N_FREQ, :CONV_WIDTH] = np.cos(w * f * delay)
    tap[FREQ_PAD:FREQ_PAD + N_FREQ, :CONV_WIDTH] = -np.sin(w * f * delay)
    weight = np.full((1, N_FREQ), 2.0)
    weight[0, 0] = weight[0, -1] = 1.0
    out = n[CONV_HALO:, None]
    inv = np.zeros((CONV_SEG, MXU_TILE))
    inv[:, :N_FREQ] = weight * np.cos(w * out * f.T) / DFT_N
    inv[:, FREQ_PAD:FREQ_PAD + N_FREQ] = -weight * np.sin(w * out * f.T) / DFT_N
    return [jnp.asarray(t, dtype=_f32) for t in (fwd, inv, tap)]


def _mixer(x, positions, norm_w, w_in, conv_w, conv_b, ln_w, ln_b, gn_w, w_out, later_weights):
    B, S, D = x.shape
    L = MIX_BLOCK
    n = S // L
    steps = B * n
    R = RET_BLOCK
    dmat, xi, zeta, gl, invf, cost, sint = _retention_tables(R, L)
    fwd, inv, tap = _conv_dft_tables()
    taps = jnp.pad(conv_w, ((0, CONV_HALO - CONV_WIDTH), (0, 0)))
    const = lambda shape: pl.BlockSpec(shape, lambda b, j: (0,) * len(shape))
    once = lambda shape: pl.BlockSpec(shape, lambda b, j: (0,) * len(shape), pipeline_mode=pl.Buffered(1))
    row = lambda n: pl.BlockSpec((1, n), lambda b, j: (0, 0))
    slabs = [w.reshape(steps, w.shape[0] // steps, w.shape[1]) for w in later_weights]
    slab_spec = lambda w: pl.BlockSpec((1,) + w.shape[1:], lambda b, j: (b * n + j, 0, 0))
    outs = pl.pallas_call(
        _mixer_kernel,
        grid=(B, n),
        in_specs=[
            pl.BlockSpec((1, L, D), lambda b, j: (b, j, 0)),
            pl.BlockSpec((1, 1, L), lambda b, j: (b * n + j, 0, 0)),
            row(D),
            once((D, D_IN)),
            const((CONV_HALO, CONV_CH)),
            row(CONV_CH), row(CONV_CH), row(CONV_CH), row(RET_V),
            once((D, D)),
            const((RET_HEADS // 2, R, 2 * R)),
            const((R, RET_V)), const((R, RET_V)), row(RET_V), row(LANES),
            const((L, LANES)), const((L, LANES)),
            const(fwd.shape), const(inv.shape), const(tap.shape),
        ] + [slab_spec(w) for w in slabs],
        out_specs=[pl.BlockSpec((1, L, D), lambda b, j: (b, j, 0))] + [slab_spec(w) for w in slabs],
        out_shape=[jax.ShapeDtypeStruct((B, S, D), _f32)] + [jax.ShapeDtypeStruct(w.shape, _bf16) for w in slabs],
        scratch_shapes=[
            pltpu.VMEM(fwd.shape, _bf16),
            pltpu.VMEM(inv.shape, _bf16),
            pltpu.VMEM((2 * FREQ_PAD, CONV_CH), _f32),
            pltpu.VMEM((L, D), _bf16),
            pltpu.VMEM((CONV_HALO, CONV_CH), _bf16),
            pltpu.VMEM((CONV_HALO + L + CONV_TAIL, CONV_CH), _bf16),
            pltpu.VMEM((L, CONV_CH), _f32),
            pltpu.VMEM((L, RET_QK), _bf16),
            pltpu.VMEM((L // R, RET_QK, R), _bf16),
            pltpu.VMEM((L, RET_V), _bf16),
            pltpu.VMEM((L, RET_V), _bf16),
            pltpu.VMEM((L, RET_V), _f32),
            pltpu.VMEM((RET_QK, RET_V), _f32),
            pltpu.VMEM((L, D), _bf16),
        ],
        compiler_params=pltpu.CompilerParams(
            dimension_semantics=("arbitrary", "arbitrary"), vmem_limit_bytes=VMEM_LIMIT),
        name="mixer",
    )(x, positions.reshape(steps, 1, L), norm_w.reshape(1, D), w_in, taps, conv_b.reshape(1, -1),
      ln_w.reshape(1, -1), ln_b.reshape(1, -1), gn_w.reshape(1, -1), w_out, dmat, xi, zeta, gl, invf, cost, sint,
      fwd, inv, tap, *slabs)
    return outs[0], [o.reshape(w.shape) for o, w in zip(outs[1:], later_weights)]


def _xattn_kernel(h_ref, nw_ref, wq_ref, kt_ref, v_ref, wo_ref, o_ref, qbuf, obuf):
    for m in range(0, h_ref.shape[1], ROW_CHUNK):
        hn = _rms(h_ref[0, m:m + ROW_CHUNK, :], nw_ref[...]).astype(_bf16)
        qbuf[m:m + ROW_CHUNK, :] = _dot(hn, wq_ref[...]).astype(_bf16)
    for i in range(XATTN_HEADS):
        cols = slice(i * XATTN_HEAD_DIM, (i + 1) * XATTN_HEAD_DIM)
        s = _dot(qbuf[:, cols], kt_ref[0, cols, :])
        e = jnp.exp(s - jnp.max(s, axis=-1, keepdims=True))
        o = _dot(e.astype(_bf16), v_ref[0, :, cols])
        obuf[:, cols] = (o * (1.0 / jnp.sum(e, axis=-1, keepdims=True))).astype(_bf16)
    o_ref[0] = h_ref[0] + _dot(obuf[...], wo_ref[...])


def _xattn(h, norm_w, xq_w, kt, v, xo_w):
    B, S, D = h.shape
    T = TOK_BLOCK
    return pl.pallas_call(
        _xattn_kernel,
        grid=(B, S // T),
        in_specs=[
            pl.BlockSpec((1, T, D), lambda b, j: (b, j, 0)),
            pl.BlockSpec((1, D), lambda b, j: (0, 0)),
            pl.BlockSpec((D, D), lambda b, j: (0, 0)),
            pl.BlockSpec((1, D, N_MEM), lambda b, j: (b, 0, 0)),
            pl.BlockSpec((1, N_MEM, D), lambda b, j: (b, 0, 0)),
            pl.BlockSpec((D, D), lambda b, j: (0, 0)),
        ],
        out_specs=pl.BlockSpec((1, T, D), lambda b, j: (b, j, 0)),
        out_shape=jax.ShapeDtypeStruct((B, S, D), _f32),
        scratch_shapes=[pltpu.VMEM((T, D), _bf16), pltpu.VMEM((T, D), _bf16)],
        compiler_params=pltpu.CompilerParams(
            dimension_semantics=("arbitrary", "arbitrary"), vmem_limit_bytes=VMEM_LIMIT),
        name="xattn",
    )(h, norm_w.reshape(1, D), xq_w, kt, v, xo_w)


def _mlp_kernel(h_ref, nw_ref, wu_ref, wd_ref, fw_ref, o_ref, abuf):
    for m in range(0, h_ref.shape[0], MLP_ROW_CHUNK):
        hn = _rms(h_ref[m:m + MLP_ROW_CHUNK, :], nw_ref[...]).astype(_bf16)
        for c in range(0, D_FF, FF_CHUNK):
            u = jnp.maximum(_dot(hn, wu_ref[:, c:c + FF_CHUNK]), 0.0)
            abuf[m:m + MLP_ROW_CHUNK, c:c + FF_CHUNK] = (u * u).astype(_bf16)
    o_ref[...] = _rms(h_ref[...] + _dot(abuf[...], wd_ref[...]), fw_ref[...])


def _mlp(h, norm_w, up_w, down_w, norm_f_w):
    B, S, D = h.shape
    T = TOK_BLOCK
    h2 = h.reshape(B * S, D)
    out = pl.pallas_call(
        _mlp_kernel,
        grid=(B * S // T,),
        in_specs=[
            pl.BlockSpec((T, D), lambda i: (i, 0)),
            pl.BlockSpec((1, D), lambda i: (0, 0)),
            pl.BlockSpec((D, D_FF), lambda i: (0, 0), pipeline_mode=pl.Buffered(1)),
            pl.BlockSpec((D_FF, D), lambda i: (0, 0), pipeline_mode=pl.Buffered(1)),
            pl.BlockSpec((1, D), lambda i: (0, 0)),
        ],
        out_specs=pl.BlockSpec((T, D), lambda i: (i, 0)),
        out_shape=jax.ShapeDtypeStruct((B * S, D), _f32),
        scratch_shapes=[pltpu.VMEM((T, D_FF), _bf16)],
        compiler_params=pltpu.CompilerParams(
            dimension_semantics=("arbitrary",), vmem_limit_bytes=VMEM_LIMIT),
        name="mlp",
    )(h2, norm_w.reshape(1, D), up_w, down_w, norm_f_w.reshape(1, D))
    return out.reshape(B, S, D)


def kernel(x, mem, positions, norm_mix_w, w_in, conv_w, conv_b, conv_ln_w, conv_ln_b, ret_gn_w, w_out,
           norm_xattn_w, norm_mem_w, xq_w, xkv_w, xo_w, norm_mlp_w, mlp_up_w, mlp_down_w, norm_f_w):
    kt, v, w_in_b, w_out_b = _mem_kv(mem, norm_mem_w, xkv_w, w_in, w_out)
    h, (xq_b, xo_b, up_b, down_b) = _mixer(x, positions, norm_mix_w, w_in_b, conv_w, conv_b, conv_ln_w, conv_ln_b,
                                           ret_gn_w, w_out_b, (xq_w, xo_w, mlp_up_w, mlp_down_w))
    h = _xattn(h, norm_xattn_w, xq_b, kt, v, xo_b)
    return _mlp(h, norm_mlp_w, up_b, down_b, norm_f_w)
```

```python
import numpy as np
import jax
import jax.numpy as jnp
from jax import lax
from jax.experimental import pallas as pl
from jax.experimental.pallas import tpu as pltpu

D_MODEL = 1024
CHUNK = 64
CONV_CH = 512
CONV_WIDTH = 31
RET_HEADS = 8
RET_DV = 64
RET_DK = 32
RET_QK = RET_HEADS * RET_DK
RET_V = RET_HEADS * RET_DV
N_MEM = 256
XATTN_HEADS = 4
XATTN_HEAD_DIM = 256
D_FF = 4096
ROPE_BASE = 10000.0
EPS = 1e-6

OFF_A, OFF_B, OFF_Q, OFF_K, OFF_V, OFF_G = 0, 512, 1024, 1280, 1536, 2048
D_IN = 2560

LANES = 128
SUBLANES = 8
MXU_TILE = 256
MIX_BLOCK = 512
STATE_COLS = MXU_TILE
STATE_ROWS = STATE_COLS // RET_DV * RET_DK
RET_BLOCK = 128
CONV_HALO = 32
CONV_SEG = 128
DFT_N = CONV_HALO + CONV_SEG
N_FREQ = DFT_N // 2 + 1
FREQ_PAD = -(-N_FREQ // SUBLANES) * SUBLANES
CONV_TAIL = MXU_TILE - DFT_N
CONV_ROWS = 32
TOK_BLOCK = 1024
KV_STEPS = 1
ROW_CHUNK = 256
MLP_ROW_CHUNK = 256
FF_CHUNK = 1024
VMEM_LIMIT = 56 * 1024 * 1024

_f32 = jnp.float32
_bf16 = jnp.bfloat16


def _dot(a, b):
    return jnp.dot(a, b, preferred_element_type=_f32)


def _rms(x, w):
    return x * lax.rsqrt(jnp.mean(x * x, axis=-1, keepdims=True) + EPS) * w


def _sigmoid(x):
    return 1.0 / (1.0 + jnp.exp(-x))


def _swish(x):
    h = 0.5 * x
    return h + h * jnp.tanh(h)


def _cast_columns(src_ref, dst_ref):
    for c in range(0, src_ref.shape[-1], MXU_TILE):
        dst_ref[:, c:c + MXU_TILE] = src_ref[:, c:c + MXU_TILE].astype(_bf16)


def _mem_kv_kernel(mem_ref, nw_ref, wkv_ref, win32_ref, wout32_ref, kt_ref, v_ref, win_ref, wout_ref):
    @pl.when(pl.program_id(0) % KV_STEPS == 0)
    def _():
        m = _rms(mem_ref[0], nw_ref[...]).astype(_bf16)
        k = _dot(m, wkv_ref[:, :D_MODEL].astype(_bf16))
        kt_ref[0] = k.T.astype(_bf16)
        v_ref[0] = _dot(m, wkv_ref[:, D_MODEL:].astype(_bf16)).astype(_bf16)

    _cast_columns(win32_ref.at[0], win_ref.at[0])
    _cast_columns(wout32_ref.at[0], wout_ref.at[0])


def _mem_kv(mem, norm_mem_w, xkv_w, w_in, w_out):
    B = mem.shape[0]
    steps = B * KV_STEPS
    slab = lambda w: w.reshape(steps, w.shape[0] // steps, w.shape[1])
    slab_spec = lambda w: pl.BlockSpec((1, w.shape[0] // steps, w.shape[1]), lambda i: (i, 0, 0))
    kt, v, win_b, wout_b = pl.pallas_call(
        _mem_kv_kernel,
        grid=(steps,),
        in_specs=[
            pl.BlockSpec((1, N_MEM, D_MODEL), lambda i: (i // KV_STEPS, 0, 0)),
            pl.BlockSpec((1, D_MODEL), lambda i: (0, 0)),
            pl.BlockSpec((D_MODEL, 2 * D_MODEL), lambda i: (0, 0)),
            slab_spec(w_in), slab_spec(w_out),
        ],
        out_specs=[
            pl.BlockSpec((1, D_MODEL, N_MEM), lambda i: (i // KV_STEPS, 0, 0)),
            pl.BlockSpec((1, N_MEM, D_MODEL), lambda i: (i // KV_STEPS, 0, 0)),
            slab_spec(w_in), slab_spec(w_out),
        ],
        out_shape=[
            jax.ShapeDtypeStruct((B, D_MODEL, N_MEM), _bf16),
            jax.ShapeDtypeStruct((B, N_MEM, D_MODEL), _bf16),
            jax.ShapeDtypeStruct(slab(w_in).shape, _bf16),
            jax.ShapeDtypeStruct(slab(w_out).shape, _bf16),
        ],
        compiler_params=pltpu.CompilerParams(
            dimension_semantics=("arbitrary",), vmem_limit_bytes=VMEM_LIMIT),
        name="mem_kv",
    )(mem, norm_mem_w.reshape(1, D_MODEL), xkv_w, slab(w_in), slab(w_out))
    return kt, v, win_b.reshape(w_in.shape), wout_b.reshape(w_out.shape)


def _group_norm_halves(y, lo):
    inv = 1.0 / RET_DV
    s_lo = jnp.sum(jnp.where(lo, y, 0.0), axis=-1, keepdims=True)
    s_hi = jnp.sum(jnp.where(lo, 0.0, y), axis=-1, keepdims=True)
    d = y - jnp.where(lo, s_lo, s_hi) * inv
    d2 = d * d
    v_lo = jnp.sum(jnp.where(lo, d2, 0.0), axis=-1, keepdims=True)
    v_hi = jnp.sum(jnp.where(lo, 0.0, d2), axis=-1, keepdims=True)
    return d * lax.rsqrt(jnp.where(lo, v_lo, v_hi) * inv + EPS)


N_CAST = 4
CAST_SCALES = (XATTN_HEAD_DIM ** -0.5, 1.0, 1.0, 1.0)


def _mixer_kernel(*refs):
    n_in = 20
    (x_ref, pos_ref, nw_ref, win_ref, cw_ref, cb_ref, lnw_ref, lnb_ref, gnw_ref, wout_ref,
     dmat_ref, xi_ref, zeta_ref, gl_ref, invf_ref, cost_ref, sint_ref, fwd32_ref, inv32_ref, tap_ref) = refs[:n_in]
    cast_src = refs[n_in:n_in + N_CAST]
    o_ref = refs[n_in + N_CAST]
    cast_dst = refs[n_in + 1 + N_CAST:n_in + 1 + 2 * N_CAST]
    (fwd_ref, inv_ref, hspec, hbuf, hist, ubuf, ybuf, qbuf, ktbuf, vbuf, vzbuf, gbuf, state, state_b,
     mbuf) = refs[n_in + 1 + 2 * N_CAST:]
    L = MIX_BLOCK

    @pl.when((pl.program_id(0) == 0) & (pl.program_id(1) == 0))
    def _():
        fwd_ref[...] = fwd32_ref[...].astype(_bf16)
        inv_ref[...] = inv32_ref[...].astype(_bf16)
        ubuf[...] = jnp.zeros(ubuf.shape, _bf16)
        hspec[...] = jnp.dot(tap_ref[...], cw_ref[...], preferred_element_type=_f32, precision=lax.Precision.HIGHEST)

    @pl.when(pl.program_id(1) == 0)
    def _():
        hist[...] = jnp.zeros((CONV_HALO, CONV_CH), _bf16)
        state[...] = jnp.zeros_like(state)
        state_b[...] = jnp.zeros_like(state_b)

    for src, dst, scale in zip(cast_src, cast_dst, CAST_SCALES):
        dst[...] = (src[...] if scale == 1.0 else src[...] * scale).astype(_bf16)

    x = x_ref[0]
    ubuf[0:CONV_HALO, :] = hist[...]

    def glu_rows(row_lo, row_hi):
        for m in range(row_lo, row_hi, ROW_CHUNK):
            hn_m = _rms(x_ref[0, m:m + ROW_CHUNK, :], nw_ref[...]).astype(_bf16)
            hbuf[m:m + ROW_CHUNK, :] = hn_m
            a = _dot(hn_m, win_ref[:, OFF_A:OFF_A + CONV_CH])
            b = _dot(hn_m, win_ref[:, OFF_B:OFF_B + CONV_CH])
            ubuf[CONV_HALO + m:CONV_HALO + m + ROW_CHUNK, :] = (a * _sigmoid(b)).astype(_bf16)

    ang0 = pos_ref[0, :, 0:1].astype(_f32) * invf_ref[...]
    cos0, sin0 = jnp.cos(ang0), jnp.sin(ang0)
    lane = lax.broadcasted_iota(jnp.int32, (1, RET_QK), 1)
    first_half = lane % RET_DK < RET_DK // 2
    sub_blocks = [slice(r0, r0 + RET_BLOCK) for r0 in range(0, L, RET_BLOCK)]

    def rotary(t, rows):
        cos = cos0 * cost_ref[rows, :] - sin0 * sint_ref[rows, :]
        sin = sin0 * cost_ref[rows, :] + cos0 * sint_ref[rows, :]
        cos2 = jnp.concatenate([cos, cos], axis=-1)
        sin2 = jnp.where(first_half, -1.0, 1.0) * jnp.concatenate([sin, sin], axis=-1)
        partner = jnp.where(first_half, pltpu.roll(t, RET_QK - RET_DK // 2, 1), pltpu.roll(t, RET_DK // 2, 1))
        return t * cos2 + partner * sin2

    def project_rows(row_lo, row_hi):
        hn = hbuf[row_lo:row_hi, :]
        mine = [(i, rows, slice(rows.start - row_lo, rows.stop - row_lo)) for i, rows in enumerate(sub_blocks)
                if row_lo <= rows.start < row_hi]
        q_all = _dot(hn, win_ref[:, OFF_Q:OFF_K])
        for _, rows, local in mine:
            qbuf[rows, :] = rotary(q_all[local, :], rows).astype(_bf16)
        k_all = _dot(hn, win_ref[:, OFF_K:OFF_V])
        for i, rows, local in mine:
            ktbuf[i] = rotary(k_all[local, :], rows).T.astype(_bf16)
        v_all = _dot(hn, win_ref[:, OFF_V:OFF_G])
        vbuf[row_lo:row_hi, :] = v_all.astype(_bf16)
        for _, rows, local in mine:
            vzbuf[rows, :] = (v_all[local, :] * zeta_ref[...]).astype(_bf16)
        gbuf[row_lo:row_hi, :] = _swish(_dot(hn, win_ref[:, OFF_G:D_IN]))

    h_re, h_im = hspec[0:FREQ_PAD, :], hspec[FREQ_PAD:2 * FREQ_PAD, :]

    def conv_rows(row_lo, row_hi):
        seg_rows = range(row_lo, row_hi, CONV_SEG)
        specs = [_dot(fwd_ref[...], ubuf[row0:row0 + MXU_TILE, :]) for row0 in seg_rows]
        prods = []
        for spec in specs:
            s_re, s_im = spec[0:FREQ_PAD, :], spec[FREQ_PAD:2 * FREQ_PAD, :]
            prods.append(jnp.concatenate([s_re * h_re - s_im * h_im, s_re * h_im + s_im * h_re,
                                          jnp.zeros((MXU_TILE - 2 * FREQ_PAD, CONV_CH), _f32)],
                                         axis=0).astype(_bf16))
        for row0, prod in zip(seg_rows, prods):
            ybuf[row0:row0 + CONV_SEG, :] = _dot(inv_ref[...], prod)
        half_lnw, half_lnb = 0.5 * lnw_ref[...], 0.5 * lnb_ref[...]
        for r in range(row_lo, row_hi, CONV_ROWS):
            acc = ybuf[r:r + CONV_ROWS, :] + cb_ref[...]
            mu = jnp.mean(acc, axis=-1, keepdims=True)
            d = acc - mu
            var = jnp.mean(d * d, axis=-1, keepdims=True)
            half = d * lax.rsqrt(var + EPS) * half_lnw + half_lnb
            mbuf[r:r + CONV_ROWS, 0:CONV_CH] = (half + half * jnp.tanh(half)).astype(_bf16)

    lo = lax.broadcasted_iota(jnp.int32, (1, LANES), 1) < RET_DV
    same_head = (lax.broadcasted_iota(jnp.int32, (STATE_ROWS, STATE_COLS), 0) // RET_DK
                 == lax.broadcasted_iota(jnp.int32, (STATE_ROWS, STATE_COLS), 1) // RET_DV)

    def retention(i):
        rows = sub_blocks[i]
        qb = qbuf[rows, :]
        kt = ktbuf[i]
        vb = vbuf[rows, :]

        yx = _dot(qb, state_b[...]) * xi_ref[...]

        for p in range(RET_HEADS // 2):
            cols = slice(p * LANES, (p + 1) * LANES)
            blank = jnp.zeros((RET_DK, RET_BLOCK), _bf16)
            k_pair = jnp.concatenate(
                [jnp.concatenate([kt[h * RET_DK:(h + 1) * RET_DK] if h == 2 * p else blank,
                                  kt[h * RET_DK:(h + 1) * RET_DK] if h == 2 * p + 1 else blank], axis=1)
                 for h in range(RET_HEADS)], axis=0)
            s = _dot(qb, k_pair)
            vp = vb[:, cols]
            off = jnp.zeros_like(vp)
            v_pair = jnp.concatenate([jnp.where(lo, vp, off), jnp.where(lo, off, vp)], axis=0)
            y = _dot((s * dmat_ref[p]).astype(_bf16), v_pair) + yx[:, cols]
            yn = _group_norm_halves(y, lo) * gnw_ref[:, cols]
            mbuf[rows, CONV_CH + p * LANES:CONV_CH + (p + 1) * LANES] = (gbuf[rows, cols] * yn).astype(_bf16)

        for q in range(RET_QK // STATE_ROWS):
            srows = slice(q * STATE_ROWS, (q + 1) * STATE_ROWS)
            scols = slice(q * STATE_COLS, (q + 1) * STATE_COLS)
            kv = _dot(kt[srows, :], vzbuf[rows, scols])
            new = gl_ref[:, scols] * state[srows, scols] + jnp.where(same_head, kv, 0.0)
            state[srows, scols] = new
            state_b[srows, scols] = new.astype(_bf16)

    half_rows = L // 2
    per_half = len(sub_blocks) // 2
    glu_rows(0, half_rows)
    project_rows(0, half_rows)
    conv_rows(0, half_rows)
    glu_rows(half_rows, L)
    hist[...] = ubuf[L:L + CONV_HALO, :]
    retention(0)
    project_rows(half_rows, L)
    for i in range(1, per_half):
        retention(i)
    conv_rows(half_rows, L)
    for i in range(per_half, len(sub_blocks)):
        retention(i)

    o_ref[0] = x + _dot(mbuf[...], wout_ref[...])


def _retention_tables(L, step_rows):
    h = np.arange(RET_HEADS, dtype=np.float64)
    log_g = np.log1p(-np.exp2(-5.0 - h))
    idx = np.arange(L, dtype=np.float64)
    dist = np.abs(idx[:, None] - idx[None, :])
    visible = (idx[None, :] // CHUNK) <= (idx[:, None] // CHUNK)
    dmat = np.where(visible[None], np.exp(log_g[:, None, None] * dist[None]), 0.0)
    dmat = np.concatenate([dmat[0::2], dmat[1::2]], axis=-1)
    xi = np.exp(log_g[None, :] * (idx[:, None] + 1.0))
    zeta = np.exp(log_g[None, :] * (L - 1.0 - idx[:, None]))
    gl = np.exp(log_g * L)
    rep = lambda t: np.repeat(t, RET_DV, axis=-1)
    inv_freq = ROPE_BASE ** (-np.arange(RET_DK // 2, dtype=np.float32) / np.float32(RET_DK // 2))
    invf = np.tile(inv_freq.astype(np.float32), LANES // (RET_DK // 2))[None, :]
    rel = np.arange(step_rows, dtype=np.float64)[:, None] * invf.astype(np.float64)
    f = lambda t: jnp.asarray(t, dtype=_f32)
    scale = RET_DK ** -0.5
    return (f(dmat * scale), f(rep(xi) * scale), f(rep(zeta)), f(rep(gl[None, :])), f(invf),
            f(np.cos(rel)), f(np.sin(rel)))


def _conv_dft_tables():
    n = np.arange(DFT_N, dtype=np.float64)
    f = np.arange(N_FREQ, dtype=np.float64)[:, None]
    w = 2.0 * np.pi / DFT_N
    fwd = np.zeros((2 * FREQ_PAD, MXU_TILE))
    fwd[:N_FREQ, :DFT_N] = np.cos(w * f * n[None, :])
    fwd[FREQ_PAD:FREQ_PAD + N_FREQ, :DFT_N] = -np.sin(w * f * n[None, :])
    delay = (CONV_WIDTH - 1) - np.arange(CONV_WIDTH, dtype=np.float64)[None, :]
    tap = np.zeros((2 * FREQ_PAD, CONV_HALO))
    tap[:N_FREQ, :CONV_WIDTH] = np.cos(w * f * delay)
    tap[FREQ_PAD:FREQ_PAD + N_FREQ, :CONV_WIDTH] = -np.sin(w * f * delay)
    weight = np.full((1, N_FREQ), 2.0)
    weight[0, 0] = weight[0, -1] = 1.0
    out = n[CONV_HALO:, None]
    inv = np.zeros((CONV_SEG, MXU_TILE))
    inv[:, :N_FREQ] = weight * np.cos(w * out * f.T) / DFT_N
    inv[:, FREQ_PAD:FREQ_PAD + N_FREQ] = -weight * np.sin(w * out * f.T) / DFT_N
    return [jnp.asarray(t, dtype=_f32) for t in (fwd, inv, tap)]


def _mixer(x, positions, norm_w, w_in, conv_w, conv_b, ln_w, ln_b, gn_w, w_out, later_weights):
    B, S, D = x.shape
    L = MIX_BLOCK
    n = S // L
    steps = B * n
    R = RET_BLOCK
    dmat, xi, zeta, gl, invf, cost, sint = _retention_tables(R, L)
    fwd, inv, tap = _conv_dft_tables()
    taps = jnp.pad(conv_w, ((0, CONV_HALO - CONV_WIDTH), (0, 0)))
    const = lambda shape: pl.BlockSpec(shape, lambda b, j: (0,) * len(shape))
    once = lambda shape: pl.BlockSpec(shape, lambda b, j: (0,) * len(shape), pipeline_mode=pl.Buffered(1))
    row = lambda n: pl.BlockSpec((1, n), lambda b, j: (0, 0))
    slabs = [w.reshape(steps, w.shape[0] // steps, w.shape[1]) for w in later_weights]
    slab_spec = lambda w: pl.BlockSpec((1,) + w.shape[1:], lambda b, j: (b * n + j, 0, 0))
    outs = pl.pallas_call(
        _mixer_kernel,
        grid=(B, n),
        in_specs=[
            pl.BlockSpec((1, L, D), lambda b, j: (b, j, 0)),
            pl.BlockSpec((1, 1, L), lambda b, j: (b * n + j, 0, 0)),
            row(D),
            once((D, D_IN)),
            const((CONV_HALO, CONV_CH)),
            row(CONV_CH), row(CONV_CH), row(CONV_CH), row(RET_V),
            once((D, D)),
            const((RET_HEADS // 2, R, 2 * R)),
            const((R, RET_V)), const((R, RET_V)), row(RET_V), row(LANES),
            const((L, LANES)), const((L, LANES)),
            const(fwd.shape), const(inv.shape), const(tap.shape),
        ] + [slab_spec(w) for w in slabs],
        out_specs=[pl.BlockSpec((1, L, D), lambda b, j: (b, j, 0))] + [slab_spec(w) for w in slabs],
        out_shape=[jax.ShapeDtypeStruct((B, S, D), _f32)] + [jax.ShapeDtypeStruct(w.shape, _bf16) for w in slabs],
        scratch_shapes=[
            pltpu.VMEM(fwd.shape, _bf16),
            pltpu.VMEM(inv.shape, _bf16),
            pltpu.VMEM((2 * FREQ_PAD, CONV_CH), _f32),
            pltpu.VMEM((L, D), _bf16),
            pltpu.VMEM((CONV_HALO, CONV_CH), _bf16),
            pltpu.VMEM((CONV_HALO + L + CONV_TAIL, CONV_CH), _bf16),
            pltpu.VMEM((L, CONV_CH), _f32),
            pltpu.VMEM((L, RET_QK), _bf16),
            pltpu.VMEM((L // R, RET_QK, R), _bf16),
            pltpu.VMEM((L, RET_V), _bf16),
            pltpu.VMEM((L, RET_V), _bf16),
            pltpu.VMEM((L, RET_V), _f32),
            pltpu.VMEM((RET_QK, RET_V), _f32),
            pltpu.VMEM((RET_QK, RET_V), _bf16),
            pltpu.VMEM((L, D), _bf16),
        ],
        compiler_params=pltpu.CompilerParams(
            dimension_semantics=("arbitrary", "arbitrary"), vmem_limit_bytes=VMEM_LIMIT),
        name="mixer",
    )(x, positions.reshape(steps, 1, L), norm_w.reshape(1, D), w_in, taps, conv_b.reshape(1, -1),
      ln_w.reshape(1, -1), ln_b.reshape(1, -1), gn_w.reshape(1, -1), w_out, dmat, xi, zeta, gl, invf, cost, sint,
      fwd, inv, tap, *slabs)
    return outs[0], [o.reshape(w.shape) for o, w in zip(outs[1:], later_weights)]


def _xattn_kernel(h_ref, nw_ref, wq_ref, kt_ref, v_ref, wo_ref, o_ref, qbuf, obuf):
    for m in range(0, h_ref.shape[1], ROW_CHUNK):
        hn = _rms(h_ref[0, m:m + ROW_CHUNK, :], nw_ref[...]).astype(_bf16)
        qbuf[m:m + ROW_CHUNK, :] = _dot(hn, wq_ref[...]).astype(_bf16)
    for i in range(XATTN_HEADS):
        cols = slice(i * XATTN_HEAD_DIM, (i + 1) * XATTN_HEAD_DIM)
        s = _dot(qbuf[:, cols], kt_ref[0, cols, :])
        e = jnp.exp(s - jnp.max(s, axis=-1, keepdims=True))
        o = _dot(e.astype(_bf16), v_ref[0, :, cols])
        obuf[:, cols] = (o * (1.0 / jnp.sum(e, axis=-1, keepdims=True))).astype(_bf16)
    o_ref[0] = h_ref[0] + _dot(obuf[...], wo_ref[...])


def _xattn(h, norm_w, xq_w, kt, v, xo_w):
    B, S, D = h.shape
    T = TOK_BLOCK
    return pl.pallas_call(
        _xattn_kernel,
        grid=(B, S // T),
        in_specs=[
            pl.BlockSpec((1, T, D), lambda b, j: (b, j, 0)),
            pl.BlockSpec((1, D), lambda b, j: (0, 0)),
            pl.BlockSpec((D, D), lambda b, j: (0, 0)),
            pl.BlockSpec((1, D, N_MEM), lambda b, j: (b, 0, 0)),
            pl.BlockSpec((1, N_MEM, D), lambda b, j: (b, 0, 0)),
            pl.BlockSpec((D, D), lambda b, j: (0, 0)),
        ],
        out_specs=pl.BlockSpec((1, T, D), lambda b, j: (b, j, 0)),
        out_shape=jax.ShapeDtypeStruct((B, S, D), _f32),
        scratch_shapes=[pltpu.VMEM((T, D), _bf16), pltpu.VMEM((T, D), _bf16)],
        compiler_params=pltpu.CompilerParams(
            dimension_semantics=("arbitrary", "arbitrary"), vmem_limit_bytes=VMEM_LIMIT),
        name="xattn",
    )(h, norm_w.reshape(1, D), xq_w, kt, v, xo_w)


def _mlp_kernel(h_ref, nw_ref, wu_ref, wd_ref, fw_ref, o_ref, abuf):
    for m in range(0, h_ref.shape[0], MLP_ROW_CHUNK):
        hn = _rms(h_ref[m:m + MLP_ROW_CHUNK, :], nw_ref[...]).astype(_bf16)
        for c in range(0, D_FF, FF_CHUNK):
            u = jnp.maximum(_dot(hn, wu_ref[:, c:c + FF_CHUNK]), 0.0)
            abuf[m:m + MLP_ROW_CHUNK, c:c + FF_CHUNK] = (u * u).astype(_bf16)
    o_ref[...] = _rms(h_ref[...] + _dot(abuf[...], wd_ref[...]), fw_ref[...])


def _mlp(h, norm_w, up_w, down_w, norm_f_w):
    B, S, D = h.shape
    T = TOK_BLOCK
    h2 = h.reshape(B * S, D)
    out = pl.pallas_call(
        _mlp_kernel,
        grid=(B * S // T,),
        in_specs=[
            pl.BlockSpec((T, D), lambda i: (i, 0)),
            pl.BlockSpec((1, D), lambda i: (0, 0)),
            pl.BlockSpec((D, D_FF), lambda i: (0, 0), pipeline_mode=pl.Buffered(1)),
            pl.BlockSpec((D_FF, D), lambda i: (0, 0), pipeline_mode=pl.Buffered(1)),
            pl.BlockSpec((1, D), lambda i: (0, 0)),
        ],
        out_specs=pl.BlockSpec((T, D), lambda i: (i, 0)),
        out_shape=jax.ShapeDtypeStruct((B * S, D), _f32),
        scratch_shapes=[pltpu.VMEM((T, D_FF), _bf16)],
        compiler_params=pltpu.CompilerParams(
            dimension_semantics=("arbitrary",), vmem_limit_bytes=VMEM_LIMIT),
        name="mlp",
    )(h2, norm_w.reshape(1, D), up_w, down_w, norm_f_w.reshape(1, D))
    return out.reshape(B, S, D)


def kernel(x, mem, positions, norm_mix_w, w_in, conv_w, conv_b, conv_ln_w, conv_ln_b, ret_gn_w, w_out,
           norm_xattn_w, norm_mem_w, xq_w, xkv_w, xo_w, norm_mlp_w, mlp_up_w, mlp_down_w, norm_f_w):
    kt, v, w_in_b, w_out_b = _mem_kv(mem, norm_mem_w, xkv_w, w_in, w_out)
    h, (xq_b, xo_b, up_b, down_b) = _mixer(x, positions, norm_mix_w, w_in_b, conv_w, conv_b, conv_ln_w, conv_ln_b,
                                           ret_gn_w, w_out_b, (xq_w, xo_w, mlp_up_w, mlp_down_w))
    h = _xattn(h, norm_xattn_w, xq_b, kt, v, xo_b)
    return _mlp(h, norm_mlp_w, up_b, down_b, norm_f_w)
```

```python
import numpy as np
import jax
import jax.numpy as jnp
from jax import lax
from jax.experimental import pallas as pl
from jax.experimental.pallas import tpu as pltpu

D_MODEL = 1024
CHUNK = 64
CONV_CH = 512
CONV_WIDTH = 31
RET_HEADS = 8
RET_DV = 64
RET_DK = 32
RET_QK = RET_HEADS * RET_DK
RET_V = RET_HEADS * RET_DV
N_MEM = 256
XATTN_HEADS = 4
XATTN_HEAD_DIM = 256
D_FF = 4096
ROPE_BASE = 10000.0
EPS = 1e-6

OFF_A, OFF_B, OFF_Q, OFF_K, OFF_V, OFF_G = 0, 512, 1024, 1280, 1536, 2048
D_IN = 2560

LANES = 128
SUBLANES = 8
MXU_TILE = 256
MIX_BLOCK = 512
STATE_COLS = MXU_TILE
STATE_ROWS = STATE_COLS // RET_DV * RET_DK
RET_BLOCK = 128
CONV_HALO = 32
CONV_SEG = 128
DFT_N = CONV_HALO + CONV_SEG
N_FREQ = DFT_N // 2 + 1
FREQ_PAD = -(-N_FREQ // SUBLANES) * SUBLANES
CONV_TAIL = MXU_TILE - DFT_N
CONV_ROWS = 32
TOK_BLOCK = 1024
XATTN_BLOCK = 2048
KV_STEPS = 1
ROW_CHUNK = 256
MLP_ROW_CHUNK = 256
FF_CHUNK = 1024
VMEM_LIMIT = 56 * 1024 * 1024

_f32 = jnp.float32
_bf16 = jnp.bfloat16


def _dot(a, b):
    return jnp.dot(a, b, preferred_element_type=_f32)


def _rms(x, w):
    return x * lax.rsqrt(jnp.mean(x * x, axis=-1, keepdims=True) + EPS) * w


def _sigmoid(x):
    return 1.0 / (1.0 + jnp.exp(-x))


def _swish(x):
    h = 0.5 * x
    return h + h * jnp.tanh(h)


def _cast_columns(src_ref, dst_ref):
    for c in range(0, src_ref.shape[-1], MXU_TILE):
        dst_ref[:, c:c + MXU_TILE] = src_ref[:, c:c + MXU_TILE].astype(_bf16)


def _mem_kv_kernel(mem_ref, nw_ref, wkv_ref, win32_ref, wout32_ref, kt_ref, v_ref, win_ref, wout_ref):
    @pl.when(pl.program_id(0) % KV_STEPS == 0)
    def _():
        m = _rms(mem_ref[0], nw_ref[...]).astype(_bf16)
        k = _dot(m, wkv_ref[:, :D_MODEL].astype(_bf16))
        kt_ref[0] = k.T.astype(_bf16)
        v_ref[0] = _dot(m, wkv_ref[:, D_MODEL:].astype(_bf16)).astype(_bf16)

    _cast_columns(win32_ref.at[0], win_ref.at[0])
    _cast_columns(wout32_ref.at[0], wout_ref.at[0])


def _mem_kv(mem, norm_mem_w, xkv_w, w_in, w_out):
    B = mem.shape[0]
    steps = B * KV_STEPS
    slab = lambda w: w.reshape(steps, w.shape[0] // steps, w.shape[1])
    slab_spec = lambda w: pl.BlockSpec((1, w.shape[0] // steps, w.shape[1]), lambda i: (i, 0, 0))
    kt, v, win_b, wout_b = pl.pallas_call(
        _mem_kv_kernel,
        grid=(steps,),
        in_specs=[
            pl.BlockSpec((1, N_MEM, D_MODEL), lambda i: (i // KV_STEPS, 0, 0)),
            pl.BlockSpec((1, D_MODEL), lambda i: (0, 0)),
            pl.BlockSpec((D_MODEL, 2 * D_MODEL), lambda i: (0, 0)),
            slab_spec(w_in), slab_spec(w_out),
        ],
        out_specs=[
            pl.BlockSpec((1, D_MODEL, N_MEM), lambda i: (i // KV_STEPS, 0, 0)),
            pl.BlockSpec((1, N_MEM, D_MODEL), lambda i: (i // KV_STEPS, 0, 0)),
            slab_spec(w_in), slab_spec(w_out),
        ],
        out_shape=[
            jax.ShapeDtypeStruct((B, D_MODEL, N_MEM), _bf16),
            jax.ShapeDtypeStruct((B, N_MEM, D_MODEL), _bf16),
            jax.ShapeDtypeStruct(slab(w_in).shape, _bf16),
            jax.ShapeDtypeStruct(slab(w_out).shape, _bf16),
        ],
        compiler_params=pltpu.CompilerParams(
            dimension_semantics=("arbitrary",), vmem_limit_bytes=VMEM_LIMIT),
        name="mem_kv",
    )(mem, norm_mem_w.reshape(1, D_MODEL), xkv_w, slab(w_in), slab(w_out))
    return kt, v, win_b.reshape(w_in.shape), wout_b.reshape(w_out.shape)


def _group_norm_halves(y, lo):
    inv = 1.0 / RET_DV
    s_lo = jnp.sum(jnp.where(lo, y, 0.0), axis=-1, keepdims=True)
    s_hi = jnp.sum(jnp.where(lo, 0.0, y), axis=-1, keepdims=True)
    d = y - jnp.where(lo, s_lo, s_hi) * inv
    d2 = d * d
    v_lo = jnp.sum(jnp.where(lo, d2, 0.0), axis=-1, keepdims=True)
    v_hi = jnp.sum(jnp.where(lo, 0.0, d2), axis=-1, keepdims=True)
    return d * lax.rsqrt(jnp.where(lo, v_lo, v_hi) * inv + EPS)


N_CAST = 4
CAST_SCALES = (XATTN_HEAD_DIM ** -0.5, 1.0, 1.0, 1.0)


def _mixer_kernel(*refs):
    n_in = 20
    (x_ref, pos_ref, nw_ref, win_ref, cw_ref, cb_ref, lnw_ref, lnb_ref, gnw_ref, wout_ref,
     dmat_ref, xi_ref, zeta_ref, gl_ref, invf_ref, cost_ref, sint_ref, fwd32_ref, inv32_ref, tap_ref) = refs[:n_in]
    cast_src = refs[n_in:n_in + N_CAST]
    o_ref = refs[n_in + N_CAST]
    cast_dst = refs[n_in + 1 + N_CAST:n_in + 1 + 2 * N_CAST]
    (fwd_ref, inv_ref, hspec, hbuf, hist, ubuf, ybuf, qbuf, ktbuf, vbuf, vzbuf, gbuf, state, state_b,
     mbuf) = refs[n_in + 1 + 2 * N_CAST:]
    L = MIX_BLOCK

    @pl.when((pl.program_id(0) == 0) & (pl.program_id(1) == 0))
    def _():
        fwd_ref[...] = fwd32_ref[...].astype(_bf16)
        inv_ref[...] = inv32_ref[...].astype(_bf16)
        ubuf[...] = jnp.zeros(ubuf.shape, _bf16)
        hspec[...] = jnp.dot(tap_ref[...], cw_ref[...], preferred_element_type=_f32, precision=lax.Precision.HIGHEST)

    @pl.when(pl.program_id(1) == 0)
    def _():
        hist[...] = jnp.zeros((CONV_HALO, CONV_CH), _bf16)
        state[...] = jnp.zeros_like(state)
        state_b[...] = jnp.zeros_like(state_b)

    for src, dst, scale in zip(cast_src, cast_dst, CAST_SCALES):
        dst[...] = (src[...] if scale == 1.0 else src[...] * scale).astype(_bf16)

    x = x_ref[0]
    ubuf[0:CONV_HALO, :] = hist[...]

    def glu_rows(row_lo, row_hi):
        for m in range(row_lo, row_hi, ROW_CHUNK):
            hn_m = _rms(x_ref[0, m:m + ROW_CHUNK, :], nw_ref[...]).astype(_bf16)
            hbuf[m:m + ROW_CHUNK, :] = hn_m
            a = _dot(hn_m, win_ref[:, OFF_A:OFF_A + CONV_CH])
            b = _dot(hn_m, win_ref[:, OFF_B:OFF_B + CONV_CH])
            ubuf[CONV_HALO + m:CONV_HALO + m + ROW_CHUNK, :] = (a * _sigmoid(b)).astype(_bf16)

    ang0 = pos_ref[0, :, 0:1].astype(_f32) * invf_ref[...]
    cos0, sin0 = jnp.cos(ang0), jnp.sin(ang0)
    lane = lax.broadcasted_iota(jnp.int32, (1, RET_QK), 1)
    first_half = lane % RET_DK < RET_DK // 2
    sub_blocks = [slice(r0, r0 + RET_BLOCK) for r0 in range(0, L, RET_BLOCK)]

    def rotary(t, rows):
        cos = cos0 * cost_ref[rows, :] - sin0 * sint_ref[rows, :]
        sin = sin0 * cost_ref[rows, :] + cos0 * sint_ref[rows, :]
        cos2 = jnp.concatenate([cos, cos], axis=-1)
        sin2 = jnp.where(first_half, -1.0, 1.0) * jnp.concatenate([sin, sin], axis=-1)
        partner = jnp.where(first_half, pltpu.roll(t, RET_QK - RET_DK // 2, 1), pltpu.roll(t, RET_DK // 2, 1))
        return t * cos2 + partner * sin2

    def project_rows(row_lo, row_hi):
        hn = hbuf[row_lo:row_hi, :]
        mine = [(i, rows, slice(rows.start - row_lo, rows.stop - row_lo)) for i, rows in enumerate(sub_blocks)
                if row_lo <= rows.start < row_hi]
        q_all = _dot(hn, win_ref[:, OFF_Q:OFF_K])
        for _, rows, local in mine:
            qbuf[rows, :] = rotary(q_all[local, :], rows).astype(_bf16)
        k_all = _dot(hn, win_ref[:, OFF_K:OFF_V])
        for i, rows, local in mine:
            ktbuf[i] = rotary(k_all[local, :], rows).T.astype(_bf16)
        v_all = _dot(hn, win_ref[:, OFF_V:OFF_G])
        vbuf[row_lo:row_hi, :] = v_all.astype(_bf16)
        for _, rows, local in mine:
            vzbuf[rows, :] = (v_all[local, :] * zeta_ref[...]).astype(_bf16)
        gbuf[row_lo:row_hi, :] = _swish(_dot(hn, win_ref[:, OFF_G:D_IN]))

    h_re, h_im = hspec[0:FREQ_PAD, :], hspec[FREQ_PAD:2 * FREQ_PAD, :]

    def conv_rows(row_lo, row_hi):
        seg_rows = range(row_lo, row_hi, CONV_SEG)
        specs = [_dot(fwd_ref[...], ubuf[row0:row0 + MXU_TILE, :]) for row0 in seg_rows]
        prods = []
        for spec in specs:
            s_re, s_im = spec[0:FREQ_PAD, :], spec[FREQ_PAD:2 * FREQ_PAD, :]
            prods.append(jnp.concatenate([s_re * h_re - s_im * h_im, s_re * h_im + s_im * h_re,
                                          jnp.zeros((MXU_TILE - 2 * FREQ_PAD, CONV_CH), _f32)],
                                         axis=0).astype(_bf16))
        for row0, prod in zip(seg_rows, prods):
            ybuf[row0:row0 + CONV_SEG, :] = _dot(inv_ref[...], prod)
        half_lnw, half_lnb = 0.5 * lnw_ref[...], 0.5 * lnb_ref[...]
        for r in range(row_lo, row_hi, CONV_ROWS):
            acc = ybuf[r:r + CONV_ROWS, :] + cb_ref[...]
            mu = jnp.mean(acc, axis=-1, keepdims=True)
            d = acc - mu
            var = jnp.mean(d * d, axis=-1, keepdims=True)
            half = d * lax.rsqrt(var + EPS) * half_lnw + half_lnb
            mbuf[r:r + CONV_ROWS, 0:CONV_CH] = (half + half * jnp.tanh(half)).astype(_bf16)

    lo = lax.broadcasted_iota(jnp.int32, (1, LANES), 1) < RET_DV
    same_head = (lax.broadcasted_iota(jnp.int32, (STATE_ROWS, STATE_COLS), 0) // RET_DK
                 == lax.broadcasted_iota(jnp.int32, (STATE_ROWS, STATE_COLS), 1) // RET_DV)

    def retention(i):
        rows = sub_blocks[i]
        qb = qbuf[rows, :]
        kt = ktbuf[i]
        vb = vbuf[rows, :]

        yx = _dot(qb, state_b[...]) * xi_ref[...]

        for p in range(RET_HEADS // 2):
            cols = slice(p * LANES, (p + 1) * LANES)
            blank = jnp.zeros((RET_DK, RET_BLOCK), _bf16)
            k_pair = jnp.concatenate(
                [jnp.concatenate([kt[h * RET_DK:(h + 1) * RET_DK] if h == 2 * p else blank,
                                  kt[h * RET_DK:(h + 1) * RET_DK] if h == 2 * p + 1 else blank], axis=1)
                 for h in range(RET_HEADS)], axis=0)
            s = _dot(qb, k_pair)
            vp = vb[:, cols]
            off = jnp.zeros_like(vp)
            v_pair = jnp.concatenate([jnp.where(lo, vp, off), jnp.where(lo, off, vp)], axis=0)
            y = _dot((s * dmat_ref[p]).astype(_bf16), v_pair) + yx[:, cols]
            yn = _group_norm_halves(y, lo) * gnw_ref[:, cols]
            mbuf[rows, CONV_CH + p * LANES:CONV_CH + (p + 1) * LANES] = (gbuf[rows, cols] * yn).astype(_bf16)

        for q in range(RET_QK // STATE_ROWS):
            srows = slice(q * STATE_ROWS, (q + 1) * STATE_ROWS)
            scols = slice(q * STATE_COLS, (q + 1) * STATE_COLS)
            kv = _dot(kt[srows, :], vzbuf[rows, scols])
            new = gl_ref[:, scols] * state[srows, scols] + jnp.where(same_head, kv, 0.0)
            state[srows, scols] = new
            state_b[srows, scols] = new.astype(_bf16)

    half_rows = L // 2
    per_half = len(sub_blocks) // 2
    glu_rows(0, half_rows)
    project_rows(0, half_rows)
    conv_rows(0, half_rows)
    glu_rows(half_rows, L)
    hist[...] = ubuf[L:L + CONV_HALO, :]
    retention(0)
    project_rows(half_rows, L)
    for i in range(1, per_half):
        retention(i)
    conv_rows(half_rows, L)
    for i in range(per_half, len(sub_blocks)):
        retention(i)

    o_ref[0] = x + _dot(mbuf[...], wout_ref[...])


def _retention_tables(L, step_rows):
    h = np.arange(RET_HEADS, dtype=np.float64)
    log_g = np.log1p(-np.exp2(-5.0 - h))
    idx = np.arange(L, dtype=np.float64)
    dist = np.abs(idx[:, None] - idx[None, :])
    visible = (idx[None, :] // CHUNK) <= (idx[:, None] // CHUNK)
    dmat = np.where(visible[None], np.exp(log_g[:, None, None] * dist[None]), 0.0)
    dmat = np.concatenate([dmat[0::2], dmat[1::2]], axis=-1)
    xi = np.exp(log_g[None, :] * (idx[:, None] + 1.0))
    zeta = np.exp(log_g[None, :] * (L - 1.0 - idx[:, None]))
    gl = np.exp(log_g * L)
    rep = lambda t: np.repeat(t, RET_DV, axis=-1)
    inv_freq = ROPE_BASE ** (-np.arange(RET_DK // 2, dtype=np.float32) / np.float32(RET_DK // 2))
    invf = np.tile(inv_freq.astype(np.float32), LANES // (RET_DK // 2))[None, :]
    rel = np.arange(step_rows, dtype=np.float64)[:, None] * invf.astype(np.float64)
    f = lambda t: jnp.asarray(t, dtype=_f32)
    scale = RET_DK ** -0.5
    return (f(dmat * scale), f(rep(xi) * scale), f(rep(zeta)), f(rep(gl[None, :])), f(invf),
            f(np.cos(rel)), f(np.sin(rel)))


def _conv_dft_tables():
    n = np.arange(DFT_N, dtype=np.float64)
    f = np.arange(N_FREQ, dtype=np.float64)[:, None]
    w = 2.0 * np.pi / DFT_N
    fwd = np.zeros((2 * FREQ_PAD, MXU_TILE))
    fwd[:N_FREQ, :DFT_N] = np.cos(w * f * n[None, :])
    fwd[FREQ_PAD:FREQ_PAD + N_FREQ, :DFT_N] = -np.sin(w * f * n[None, :])
    delay = (CONV_WIDTH - 1) - np.arange(CONV_WIDTH, dtype=np.float64)[None, :]
    tap = np.zeros((2 * FREQ_PAD, CONV_HALO))
    tap[:N_FREQ, :CONV_WIDTH] = np.cos(w * f * delay)
    tap[FREQ_PAD:FREQ_PAD + N_FREQ, :CONV_WIDTH] = -np.sin(w * f * delay)
    weight = np.full((1, N_FREQ), 2.0)
    weight[0, 0] = weight[0, -1] = 1.0
    out = n[CONV_HALO:, None]
    inv = np.zeros((CONV_SEG, MXU_TILE))
    inv[:, :N_FREQ] = weight * np.cos(w * out * f.T) / DFT_N
    inv[:, FREQ_PAD:FREQ_PAD + N_FREQ] = -weight * np.sin(w * out * f.T) / DFT_N
    return [jnp.asarray(t, dtype=_f32) for t in (fwd, inv, tap)]


def _mixer(x, positions, norm_w, w_in, conv_w, conv_b, ln_w, ln_b, gn_w, w_out, later_weights):
    B, S, D = x.shape
    L = MIX_BLOCK
    n = S // L
    steps = B * n
    R = RET_BLOCK
    dmat, xi, zeta, gl, invf, cost, sint = _retention_tables(R, L)
    fwd, inv, tap = _conv_dft_tables()
    taps = jnp.pad(conv_w, ((0, CONV_HALO - CONV_WIDTH), (0, 0)))
    const = lambda shape: pl.BlockSpec(shape, lambda b, j: (0,) * len(shape))
    once = lambda shape: pl.BlockSpec(shape, lambda b, j: (0,) * len(shape), pipeline_mode=pl.Buffered(1))
    row = lambda n: pl.BlockSpec((1, n), lambda b, j: (0, 0))
    slabs = [w.reshape(steps, w.shape[0] // steps, w.shape[1]) for w in later_weights]
    slab_spec = lambda w: pl.BlockSpec((1,) + w.shape[1:], lambda b, j: (b * n + j, 0, 0))
    outs = pl.pallas_call(
        _mixer_kernel,
        grid=(B, n),
        in_specs=[
            pl.BlockSpec((1, L, D), lambda b, j: (b, j, 0)),
            pl.BlockSpec((1, 1, L), lambda b, j: (b * n + j, 0, 0)),
            row(D),
            once((D, D_IN)),
            const((CONV_HALO, CONV_CH)),
            row(CONV_CH), row(CONV_CH), row(CONV_CH), row(RET_V),
            once((D, D)),
            const((RET_HEADS // 2, R, 2 * R)),
            const((R, RET_V)), const((R, RET_V)), row(RET_V), row(LANES),
            const((L, LANES)), const((L, LANES)),
            const(fwd.shape), const(inv.shape), const(tap.shape),
        ] + [slab_spec(w) for w in slabs],
        out_specs=[pl.BlockSpec((1, L, D), lambda b, j: (b, j, 0))] + [slab_spec(w) for w in slabs],
        out_shape=[jax.ShapeDtypeStruct((B, S, D), _f32)] + [jax.ShapeDtypeStruct(w.shape, _bf16) for w in slabs],
        scratch_shapes=[
            pltpu.VMEM(fwd.shape, _bf16),
            pltpu.VMEM(inv.shape, _bf16),
            pltpu.VMEM((2 * FREQ_PAD, CONV_CH), _f32),
            pltpu.VMEM((L, D), _bf16),
            pltpu.VMEM((CONV_HALO, CONV_CH), _bf16),
            pltpu.VMEM((CONV_HALO + L + CONV_TAIL, CONV_CH), _bf16),
            pltpu.VMEM((L, CONV_CH), _f32),
            pltpu.VMEM((L, RET_QK), _bf16),
            pltpu.VMEM((L // R, RET_QK, R), _bf16),
            pltpu.VMEM((L, RET_V), _bf16),
            pltpu.VMEM((L, RET_V), _bf16),
            pltpu.VMEM((L, RET_V), _f32),
            pltpu.VMEM((RET_QK, RET_V), _f32),
            pltpu.VMEM((RET_QK, RET_V), _bf16),
            pltpu.VMEM((L, D), _bf16),
        ],
        compiler_params=pltpu.CompilerParams(
            dimension_semantics=("arbitrary", "arbitrary"), vmem_limit_bytes=VMEM_LIMIT),
        name="mixer",
    )(x, positions.reshape(steps, 1, L), norm_w.reshape(1, D), w_in, taps, conv_b.reshape(1, -1),
      ln_w.reshape(1, -1), ln_b.reshape(1, -1), gn_w.reshape(1, -1), w_out, dmat, xi, zeta, gl, invf, cost, sint,
      fwd, inv, tap, *slabs)
    return outs[0], [o.reshape(w.shape) for o, w in zip(outs[1:], later_weights)]


def _xattn_kernel(h_ref, nw_ref, wq_ref, kt_ref, v_ref, wo_ref, o_ref, qbuf, obuf):
    for m in range(0, h_ref.shape[1], ROW_CHUNK):
        hn = _rms(h_ref[0, m:m + ROW_CHUNK, :], nw_ref[...]).astype(_bf16)
        qbuf[m:m + ROW_CHUNK, :] = _dot(hn, wq_ref[...]).astype(_bf16)
    for i in range(XATTN_HEADS):
        cols = slice(i * XATTN_HEAD_DIM, (i + 1) * XATTN_HEAD_DIM)
        s = _dot(qbuf[:, cols], kt_ref[0, cols, :])
        e = jnp.exp(s - jnp.max(s, axis=-1, keepdims=True))
        o = _dot(e.astype(_bf16), v_ref[0, :, cols])
        obuf[:, cols] = (o * (1.0 / jnp.sum(e, axis=-1, keepdims=True))).astype(_bf16)
    o_ref[0] = h_ref[0] + _dot(obuf[...], wo_ref[...])


def _xattn(h, norm_w, xq_w, kt, v, xo_w):
    B, S, D = h.shape
    T = XATTN_BLOCK
    return pl.pallas_call(
        _xattn_kernel,
        grid=(B, S // T),
        in_specs=[
            pl.BlockSpec((1, T, D), lambda b, j: (b, j, 0)),
            pl.BlockSpec((1, D), lambda b, j: (0, 0)),
            pl.BlockSpec((D, D), lambda b, j: (0, 0)),
            pl.BlockSpec((1, D, N_MEM), lambda b, j: (b, 0, 0)),
            pl.BlockSpec((1, N_MEM, D), lambda b, j: (b, 0, 0)),
            pl.BlockSpec((D, D), lambda b, j: (0, 0)),
        ],
        out_specs=pl.BlockSpec((1, T, D), lambda b, j: (b, j, 0)),
        out_shape=jax.ShapeDtypeStruct((B, S, D), _f32),
        scratch_shapes=[pltpu.VMEM((T, D), _bf16), pltpu.VMEM((T, D), _bf16)],
        compiler_params=pltpu.CompilerParams(
            dimension_semantics=("arbitrary", "arbitrary"), vmem_limit_bytes=VMEM_LIMIT),
        name="xattn",
    )(h, norm_w.reshape(1, D), xq_w, kt, v, xo_w)


def _mlp_kernel(h_ref, nw_ref, wu_ref, wd_ref, fw_ref, o_ref, abuf):
    for m in range(0, h_ref.shape[0], MLP_ROW_CHUNK):
        hn = _rms(h_ref[m:m + MLP_ROW_CHUNK, :], nw_ref[...]).astype(_bf16)
        for c in range(0, D_FF, FF_CHUNK):
            u = jnp.maximum(_dot(hn, wu_ref[:, c:c + FF_CHUNK]), 0.0)
            abuf[m:m + MLP_ROW_CHUNK, c:c + FF_CHUNK] = (u * u).astype(_bf16)
    o_ref[...] = _rms(h_ref[...] + _dot(abuf[...], wd_ref[...]), fw_ref[...])


def _mlp(h, norm_w, up_w, down_w, norm_f_w):
    B, S, D = h.shape
    T = TOK_BLOCK
    h2 = h.reshape(B * S, D)
    out = pl.pallas_call(
        _mlp_kernel,
        grid=(B * S // T,),
        in_specs=[
            pl.BlockSpec((T, D), lambda i: (i, 0)),
            pl.BlockSpec((1, D), lambda i: (0, 0)),
            pl.BlockSpec((D, D_FF), lambda i: (0, 0), pipeline_mode=pl.Buffered(1)),
            pl.BlockSpec((D_FF, D), lambda i: (0, 0), pipeline_mode=pl.Buffered(1)),
            pl.BlockSpec((1, D), lambda i: (0, 0)),
        ],
        out_specs=pl.BlockSpec((T, D), lambda i: (i, 0)),
        out_shape=jax.ShapeDtypeStruct((B * S, D), _f32),
        scratch_shapes=[pltpu.VMEM((T, D_FF), _bf16)],
        compiler_params=pltpu.CompilerParams(
            dimension_semantics=("arbitrary",), vmem_limit_bytes=VMEM_LIMIT),
        name="mlp",
    )(h2, norm_w.reshape(1, D), up_w, down_w, norm_f_w.reshape(1, D))
    return out.reshape(B, S, D)


def kernel(x, mem, positions, norm_mix_w, w_in, conv_w, conv_b, conv_ln_w, conv_ln_b, ret_gn_w, w_out,
           norm_xattn_w, norm_mem_w, xq_w, xkv_w, xo_w, norm_mlp_w, mlp_up_w, mlp_down_w, norm_f_w):
    kt, v, w_in_b, w_out_b = _mem_kv(mem, norm_mem_w, xkv_w, w_in, w_out)
    h, (xq_b, xo_b, up_b, down_b) = _mixer(x, positions, norm_mix_w, w_in_b, conv_w, conv_b, conv_ln_w, conv_ln_b,
                                           ret_gn_w, w_out_b, (xq_w, xo_w, mlp_up_w, mlp_down_w))
    h = _xattn(h, norm_xattn_w, xq_b, kt, v, xo_b)
    return _mlp(h, norm_mlp_w, up_b, down_b, norm_f_w)
```

```python
import numpy as np
import jax
import jax.numpy as jnp
from jax import lax
from jax.experimental import pallas as pl
from jax.experimental.pallas import tpu as pltpu

D_MODEL = 1024
CHUNK = 64
CONV_CH = 512
CONV_WIDTH = 31
RET_HEADS = 8
RET_DV = 64
RET_DK = 32
RET_QK = RET_HEADS * RET_DK
RET_V = RET_HEADS * RET_DV
N_MEM = 256
XATTN_HEADS = 4
XATTN_HEAD_DIM = 256
D_FF = 4096
ROPE_BASE = 10000.0
EPS = 1e-6

OFF_A, OFF_B, OFF_Q, OFF_K, OFF_V, OFF_G = 0, 512, 1024, 1280, 1536, 2048
D_IN = 2560

LANES = 128
SUBLANES = 8
MXU_TILE = 256
MIX_BLOCK = 512
STATE_COLS = MXU_TILE
STATE_ROWS = STATE_COLS // RET_DV * RET_DK
RET_BLOCK = 128
CONV_HALO = 32
CONV_SEG = 128
DFT_N = CONV_HALO + CONV_SEG
N_FREQ = DFT_N // 2 + 1
FREQ_PAD = -(-N_FREQ // SUBLANES) * SUBLANES
CONV_TAIL = MXU_TILE - DFT_N
CONV_ROWS = 32
TOK_BLOCK = 1024
XATTN_BLOCK = 2048
KV_STEPS = 1
ROW_CHUNK = 256
MLP_ROW_CHUNK = 256
XATTN_TAIL_CHUNK = 512
MLP_TAIL_CHUNK = 256
FF_CHUNK = 1024
VMEM_LIMIT = 56 * 1024 * 1024

_f32 = jnp.float32
_bf16 = jnp.bfloat16


def _dot(a, b):
    return jnp.dot(a, b, preferred_element_type=_f32)


def _rms(x, w):
    return x * lax.rsqrt(jnp.mean(x * x, axis=-1, keepdims=True) + EPS) * w


def _sigmoid(x):
    return 1.0 / (1.0 + jnp.exp(-x))


def _swish(x):
    h = 0.5 * x
    return h + h * jnp.tanh(h)


def _cast_columns(src_ref, dst_ref):
    for c in range(0, src_ref.shape[-1], MXU_TILE):
        dst_ref[:, c:c + MXU_TILE] = src_ref[:, c:c + MXU_TILE].astype(_bf16)


def _mem_kv_kernel(mem_ref, nw_ref, wkv_ref, win32_ref, wout32_ref, kt_ref, v_ref, win_ref, wout_ref):
    @pl.when(pl.program_id(0) % KV_STEPS == 0)
    def _():
        m = _rms(mem_ref[0], nw_ref[...]).astype(_bf16)
        k = _dot(m, wkv_ref[:, :D_MODEL].astype(_bf16))
        kt_ref[0] = k.T.astype(_bf16)
        v_ref[0] = _dot(m, wkv_ref[:, D_MODEL:].astype(_bf16)).astype(_bf16)

    _cast_columns(win32_ref.at[0], win_ref.at[0])
    _cast_columns(wout32_ref.at[0], wout_ref.at[0])


def _mem_kv(mem, norm_mem_w, xkv_w, w_in, w_out):
    B = mem.shape[0]
    steps = B * KV_STEPS
    slab = lambda w: w.reshape(steps, w.shape[0] // steps, w.shape[1])
    slab_spec = lambda w: pl.BlockSpec((1, w.shape[0] // steps, w.shape[1]), lambda i: (i, 0, 0))
    kt, v, win_b, wout_b = pl.pallas_call(
        _mem_kv_kernel,
        grid=(steps,),
        in_specs=[
            pl.BlockSpec((1, N_MEM, D_MODEL), lambda i: (i // KV_STEPS, 0, 0)),
            pl.BlockSpec((1, D_MODEL), lambda i: (0, 0)),
            pl.BlockSpec((D_MODEL, 2 * D_MODEL), lambda i: (0, 0)),
            slab_spec(w_in), slab_spec(w_out),
        ],
        out_specs=[
            pl.BlockSpec((1, D_MODEL, N_MEM), lambda i: (i // KV_STEPS, 0, 0)),
            pl.BlockSpec((1, N_MEM, D_MODEL), lambda i: (i // KV_STEPS, 0, 0)),
            slab_spec(w_in), slab_spec(w_out),
        ],
        out_shape=[
            jax.ShapeDtypeStruct((B, D_MODEL, N_MEM), _bf16),
            jax.ShapeDtypeStruct((B, N_MEM, D_MODEL), _bf16),
            jax.ShapeDtypeStruct(slab(w_in).shape, _bf16),
            jax.ShapeDtypeStruct(slab(w_out).shape, _bf16),
        ],
        compiler_params=pltpu.CompilerParams(
            dimension_semantics=("arbitrary",), vmem_limit_bytes=VMEM_LIMIT),
        name="mem_kv",
    )(mem, norm_mem_w.reshape(1, D_MODEL), xkv_w, slab(w_in), slab(w_out))
    return kt, v, win_b.reshape(w_in.shape), wout_b.reshape(w_out.shape)


def _group_norm_halves(y, lo):
    inv = 1.0 / RET_DV
    s_lo = jnp.sum(jnp.where(lo, y, 0.0), axis=-1, keepdims=True)
    s_hi = jnp.sum(jnp.where(lo, 0.0, y), axis=-1, keepdims=True)
    d = y - jnp.where(lo, s_lo, s_hi) * inv
    d2 = d * d
    v_lo = jnp.sum(jnp.where(lo, d2, 0.0), axis=-1, keepdims=True)
    v_hi = jnp.sum(jnp.where(lo, 0.0, d2), axis=-1, keepdims=True)
    return d * lax.rsqrt(jnp.where(lo, v_lo, v_hi) * inv + EPS)


N_CAST = 4
CAST_SCALES = (XATTN_HEAD_DIM ** -0.5, 1.0, 1.0, 1.0)


def _mixer_kernel(*refs):
    n_in = 20
    (x_ref, pos_ref, nw_ref, win_ref, cw_ref, cb_ref, lnw_ref, lnb_ref, gnw_ref, wout_ref,
     dmat_ref, xi_ref, zeta_ref, gl_ref, invf_ref, cost_ref, sint_ref, fwd32_ref, inv32_ref, tap_ref) = refs[:n_in]
    cast_src = refs[n_in:n_in + N_CAST]
    o_ref = refs[n_in + N_CAST]
    cast_dst = refs[n_in + 1 + N_CAST:n_in + 1 + 2 * N_CAST]
    (fwd_ref, inv_ref, hspec, hbuf, hist, ubuf, ybuf, qbuf, ktbuf, vbuf, vzbuf, gbuf, state, state_b,
     mbuf) = refs[n_in + 1 + 2 * N_CAST:]
    L = MIX_BLOCK

    @pl.when((pl.program_id(0) == 0) & (pl.program_id(1) == 0))
    def _():
        fwd_ref[...] = fwd32_ref[...].astype(_bf16)
        inv_ref[...] = inv32_ref[...].astype(_bf16)
        ubuf[...] = jnp.zeros(ubuf.shape, _bf16)
        hspec[...] = jnp.dot(tap_ref[...], cw_ref[...], preferred_element_type=_f32, precision=lax.Precision.HIGHEST)

    @pl.when(pl.program_id(1) == 0)
    def _():
        hist[...] = jnp.zeros((CONV_HALO, CONV_CH), _bf16)
        state[...] = jnp.zeros_like(state)
        state_b[...] = jnp.zeros_like(state_b)

    for src, dst, scale in zip(cast_src, cast_dst, CAST_SCALES):
        dst[...] = (src[...] if scale == 1.0 else src[...] * scale).astype(_bf16)

    x = x_ref[0]
    ubuf[0:CONV_HALO, :] = hist[...]

    def glu_rows(row_lo, row_hi):
        for m in range(row_lo, row_hi, ROW_CHUNK):
            hn_m = _rms(x_ref[0, m:m + ROW_CHUNK, :], nw_ref[...]).astype(_bf16)
            hbuf[m:m + ROW_CHUNK, :] = hn_m
            a = _dot(hn_m, win_ref[:, OFF_A:OFF_A + CONV_CH])
            b = _dot(hn_m, win_ref[:, OFF_B:OFF_B + CONV_CH])
            ubuf[CONV_HALO + m:CONV_HALO + m + ROW_CHUNK, :] = (a * _sigmoid(b)).astype(_bf16)

    ang0 = pos_ref[0, :, 0:1].astype(_f32) * invf_ref[...]
    cos0, sin0 = jnp.cos(ang0), jnp.sin(ang0)
    lane = lax.broadcasted_iota(jnp.int32, (1, RET_QK), 1)
    first_half = lane % RET_DK < RET_DK // 2
    sub_blocks = [slice(r0, r0 + RET_BLOCK) for r0 in range(0, L, RET_BLOCK)]

    def rotary(t, rows):
        cos = cos0 * cost_ref[rows, :] - sin0 * sint_ref[rows, :]
        sin = sin0 * cost_ref[rows, :] + cos0 * sint_ref[rows, :]
        cos2 = jnp.concatenate([cos, cos], axis=-1)
        sin2 = jnp.where(first_half, -1.0, 1.0) * jnp.concatenate([sin, sin], axis=-1)
        partner = jnp.where(first_half, pltpu.roll(t, RET_QK - RET_DK // 2, 1), pltpu.roll(t, RET_DK // 2, 1))
        return t * cos2 + partner * sin2

    def project_rows(row_lo, row_hi):
        hn = hbuf[row_lo:row_hi, :]
        mine = [(i, rows, slice(rows.start - row_lo, rows.stop - row_lo)) for i, rows in enumerate(sub_blocks)
                if row_lo <= rows.start < row_hi]
        q_all = _dot(hn, win_ref[:, OFF_Q:OFF_K])
        for _, rows, local in mine:
            qbuf[rows, :] = rotary(q_all[local, :], rows).astype(_bf16)
        k_all = _dot(hn, win_ref[:, OFF_K:OFF_V])
        for i, rows, local in mine:
            ktbuf[i] = rotary(k_all[local, :], rows).T.astype(_bf16)
        v_all = _dot(hn, win_ref[:, OFF_V:OFF_G])
        vbuf[row_lo:row_hi, :] = v_all.astype(_bf16)
        for _, rows, local in mine:
            vzbuf[rows, :] = (v_all[local, :] * zeta_ref[...]).astype(_bf16)
        gbuf[row_lo:row_hi, :] = _swish(_dot(hn, win_ref[:, OFF_G:D_IN]))

    h_re, h_im = hspec[0:FREQ_PAD, :], hspec[FREQ_PAD:2 * FREQ_PAD, :]

    def conv_rows(row_lo, row_hi):
        seg_rows = range(row_lo, row_hi, CONV_SEG)
        specs = [_dot(fwd_ref[...], ubuf[row0:row0 + MXU_TILE, :]) for row0 in seg_rows]
        prods = []
        for spec in specs:
            s_re, s_im = spec[0:FREQ_PAD, :], spec[FREQ_PAD:2 * FREQ_PAD, :]
            prods.append(jnp.concatenate([s_re * h_re - s_im * h_im, s_re * h_im + s_im * h_re,
                                          jnp.zeros((MXU_TILE - 2 * FREQ_PAD, CONV_CH), _f32)],
                                         axis=0).astype(_bf16))
        for row0, prod in zip(seg_rows, prods):
            ybuf[row0:row0 + CONV_SEG, :] = _dot(inv_ref[...], prod)
        half_lnw, half_lnb = 0.5 * lnw_ref[...], 0.5 * lnb_ref[...]
        for r in range(row_lo, row_hi, CONV_ROWS):
            acc = ybuf[r:r + CONV_ROWS, :] + cb_ref[...]
            mu = jnp.mean(acc, axis=-1, keepdims=True)
            d = acc - mu
            var = jnp.mean(d * d, axis=-1, keepdims=True)
            half = d * lax.rsqrt(var + EPS) * half_lnw + half_lnb
            mbuf[r:r + CONV_ROWS, 0:CONV_CH] = (half + half * jnp.tanh(half)).astype(_bf16)

    lo = lax.broadcasted_iota(jnp.int32, (1, LANES), 1) < RET_DV
    same_head = (lax.broadcasted_iota(jnp.int32, (STATE_ROWS, STATE_COLS), 0) // RET_DK
                 == lax.broadcasted_iota(jnp.int32, (STATE_ROWS, STATE_COLS), 1) // RET_DV)

    def retention(i):
        rows = sub_blocks[i]
        qb = qbuf[rows, :]
        kt = ktbuf[i]
        vb = vbuf[rows, :]

        yx = _dot(qb, state_b[...]) * xi_ref[...]

        for p in range(RET_HEADS // 2):
            cols = slice(p * LANES, (p + 1) * LANES)
            blank = jnp.zeros((RET_DK, RET_BLOCK), _bf16)
            k_pair = jnp.concatenate(
                [jnp.concatenate([kt[h * RET_DK:(h + 1) * RET_DK] if h == 2 * p else blank,
                                  kt[h * RET_DK:(h + 1) * RET_DK] if h == 2 * p + 1 else blank], axis=1)
                 for h in range(RET_HEADS)], axis=0)
            s = _dot(qb, k_pair)
            vp = vb[:, cols]
            off = jnp.zeros_like(vp)
            v_pair = jnp.concatenate([jnp.where(lo, vp, off), jnp.where(lo, off, vp)], axis=0)
            y = _dot((s * dmat_ref[p]).astype(_bf16), v_pair) + yx[:, cols]
            yn = _group_norm_halves(y, lo) * gnw_ref[:, cols]
            mbuf[rows, CONV_CH + p * LANES:CONV_CH + (p + 1) * LANES] = (gbuf[rows, cols] * yn).astype(_bf16)

        for q in range(RET_QK // STATE_ROWS):
            srows = slice(q * STATE_ROWS, (q + 1) * STATE_ROWS)
            scols = slice(q * STATE_COLS, (q + 1) * STATE_COLS)
            kv = _dot(kt[srows, :], vzbuf[rows, scols])
            new = gl_ref[:, scols] * state[srows, scols] + jnp.where(same_head, kv, 0.0)
            state[srows, scols] = new
            state_b[srows, scols] = new.astype(_bf16)

    half_rows = L // 2
    per_half = len(sub_blocks) // 2
    glu_rows(0, half_rows)
    project_rows(0, half_rows)
    conv_rows(0, half_rows)
    glu_rows(half_rows, L)
    hist[...] = ubuf[L:L + CONV_HALO, :]
    retention(0)
    project_rows(half_rows, L)
    for i in range(1, per_half):
        retention(i)
    conv_rows(half_rows, L)
    for i in range(per_half, len(sub_blocks)):
        retention(i)

    o_ref[0] = x + _dot(mbuf[...], wout_ref[...])


def _retention_tables(L, step_rows):
    h = np.arange(RET_HEADS, dtype=np.float64)
    log_g = np.log1p(-np.exp2(-5.0 - h))
    idx = np.arange(L, dtype=np.float64)
    dist = np.abs(idx[:, None] - idx[None, :])
    visible = (idx[None, :] // CHUNK) <= (idx[:, None] // CHUNK)
    dmat = np.where(visible[None], np.exp(log_g[:, None, None] * dist[None]), 0.0)
    dmat = np.concatenate([dmat[0::2], dmat[1::2]], axis=-1)
    xi = np.exp(log_g[None, :] * (idx[:, None] + 1.0))
    zeta = np.exp(log_g[None, :] * (L - 1.0 - idx[:, None]))
    gl = np.exp(log_g * L)
    rep = lambda t: np.repeat(t, RET_DV, axis=-1)
    inv_freq = ROPE_BASE ** (-np.arange(RET_DK // 2, dtype=np.float32) / np.float32(RET_DK // 2))
    invf = np.tile(inv_freq.astype(np.float32), LANES // (RET_DK // 2))[None, :]
    rel = np.arange(step_rows, dtype=np.float64)[:, None] * invf.astype(np.float64)
    f = lambda t: jnp.asarray(t, dtype=_f32)
    scale = RET_DK ** -0.5
    return (f(dmat * scale), f(rep(xi) * scale), f(rep(zeta)), f(rep(gl[None, :])), f(invf),
            f(np.cos(rel)), f(np.sin(rel)))


def _conv_dft_tables():
    n = np.arange(DFT_N, dtype=np.float64)
    f = np.arange(N_FREQ, dtype=np.float64)[:, None]
    w = 2.0 * np.pi / DFT_N
    fwd = np.zeros((2 * FREQ_PAD, MXU_TILE))
    fwd[:N_FREQ, :DFT_N] = np.cos(w * f * n[None, :])
    fwd[FREQ_PAD:FREQ_PAD + N_FREQ, :DFT_N] = -np.sin(w * f * n[None, :])
    delay = (CONV_WIDTH - 1) - np.arange(CONV_WIDTH, dtype=np.float64)[None, :]
    tap = np.zeros((2 * FREQ_PAD, CONV_HALO))
    tap[:N_FREQ, :CONV_WIDTH] = np.cos(w * f * delay)
    tap[FREQ_PAD:FREQ_PAD + N_FREQ, :CONV_WIDTH] = -np.sin(w * f * delay)
    weight = np.full((1, N_FREQ), 2.0)
    weight[0, 0] = weight[0, -1] = 1.0
    out = n[CONV_HALO:, None]
    inv = np.zeros((CONV_SEG, MXU_TILE))
    inv[:, :N_FREQ] = weight * np.cos(w * out * f.T) / DFT_N
    inv[:, FREQ_PAD:FREQ_PAD + N_FREQ] = -weight * np.sin(w * out * f.T) / DFT_N
    return [jnp.asarray(t, dtype=_f32) for t in (fwd, inv, tap)]


def _mixer(x, positions, norm_w, w_in, conv_w, conv_b, ln_w, ln_b, gn_w, w_out, later_weights):
    B, S, D = x.shape
    L = MIX_BLOCK
    n = S // L
    steps = B * n
    R = RET_BLOCK
    dmat, xi, zeta, gl, invf, cost, sint = _retention_tables(R, L)
    fwd, inv, tap = _conv_dft_tables()
    taps = jnp.pad(conv_w, ((0, CONV_HALO - CONV_WIDTH), (0, 0)))
    const = lambda shape: pl.BlockSpec(shape, lambda b, j: (0,) * len(shape))
    once = lambda shape: pl.BlockSpec(shape, lambda b, j: (0,) * len(shape), pipeline_mode=pl.Buffered(1))
    row = lambda n: pl.BlockSpec((1, n), lambda b, j: (0, 0))
    slabs = [w.reshape(steps, w.shape[0] // steps, w.shape[1]) for w in later_weights]
    slab_spec = lambda w: pl.BlockSpec((1,) + w.shape[1:], lambda b, j: (b * n + j, 0, 0))
    outs = pl.pallas_call(
        _mixer_kernel,
        grid=(B, n),
        in_specs=[
            pl.BlockSpec((1, L, D), lambda b, j: (b, j, 0)),
            pl.BlockSpec((1, 1, L), lambda b, j: (b * n + j, 0, 0)),
            row(D),
            once((D, D_IN)),
            const((CONV_HALO, CONV_CH)),
            row(CONV_CH), row(CONV_CH), row(CONV_CH), row(RET_V),
            once((D, D)),
            const((RET_HEADS // 2, R, 2 * R)),
            const((R, RET_V)), const((R, RET_V)), row(RET_V), row(LANES),
            const((L, LANES)), const((L, LANES)),
            const(fwd.shape), const(inv.shape), const(tap.shape),
        ] + [slab_spec(w) for w in slabs],
        out_specs=[pl.BlockSpec((1, L, D), lambda b, j: (b, j, 0))] + [slab_spec(w) for w in slabs],
        out_shape=[jax.ShapeDtypeStruct((B, S, D), _f32)] + [jax.ShapeDtypeStruct(w.shape, _bf16) for w in slabs],
        scratch_shapes=[
            pltpu.VMEM(fwd.shape, _bf16),
            pltpu.VMEM(inv.shape, _bf16),
            pltpu.VMEM((2 * FREQ_PAD, CONV_CH), _f32),
            pltpu.VMEM((L, D), _bf16),
            pltpu.VMEM((CONV_HALO, CONV_CH), _bf16),
            pltpu.VMEM((CONV_HALO + L + CONV_TAIL, CONV_CH), _bf16),
            pltpu.VMEM((L, CONV_CH), _f32),
            pltpu.VMEM((L, RET_QK), _bf16),
            pltpu.VMEM((L // R, RET_QK, R), _bf16),
            pltpu.VMEM((L, RET_V), _bf16),
            pltpu.VMEM((L, RET_V), _bf16),
            pltpu.VMEM((L, RET_V), _f32),
            pltpu.VMEM((RET_QK, RET_V), _f32),
            pltpu.VMEM((RET_QK, RET_V), _bf16),
            pltpu.VMEM((L, D), _bf16),
        ],
        compiler_params=pltpu.CompilerParams(
            dimension_semantics=("arbitrary", "arbitrary"), vmem_limit_bytes=VMEM_LIMIT),
        name="mixer",
    )(x, positions.reshape(steps, 1, L), norm_w.reshape(1, D), w_in, taps, conv_b.reshape(1, -1),
      ln_w.reshape(1, -1), ln_b.reshape(1, -1), gn_w.reshape(1, -1), w_out, dmat, xi, zeta, gl, invf, cost, sint,
      fwd, inv, tap, *slabs)
    return outs[0], [o.reshape(w.shape) for o, w in zip(outs[1:], later_weights)]


def _xattn_kernel(h_ref, nw_ref, wq_ref, kt_ref, v_ref, wo_ref, o_ref, qbuf, obuf):
    for m in range(0, h_ref.shape[1], ROW_CHUNK):
        hn = _rms(h_ref[0, m:m + ROW_CHUNK, :], nw_ref[...]).astype(_bf16)
        qbuf[m:m + ROW_CHUNK, :] = _dot(hn, wq_ref[...]).astype(_bf16)
    for i in range(XATTN_HEADS):
        cols = slice(i * XATTN_HEAD_DIM, (i + 1) * XATTN_HEAD_DIM)
        s = _dot(qbuf[:, cols], kt_ref[0, cols, :])
        e = jnp.exp(s - jnp.max(s, axis=-1, keepdims=True))
        o = _dot(e.astype(_bf16), v_ref[0, :, cols])
        obuf[:, cols] = (o * (1.0 / jnp.sum(e, axis=-1, keepdims=True))).astype(_bf16)
    for m in range(0, h_ref.shape[1], XATTN_TAIL_CHUNK):
        rows = slice(m, m + XATTN_TAIL_CHUNK)
        o_ref[0, rows, :] = h_ref[0, rows, :] + _dot(obuf[rows, :], wo_ref[...])


def _xattn(h, norm_w, xq_w, kt, v, xo_w):
    B, S, D = h.shape
    T = XATTN_BLOCK
    return pl.pallas_call(
        _xattn_kernel,
        grid=(B, S // T),
        in_specs=[
            pl.BlockSpec((1, T, D), lambda b, j: (b, j, 0)),
            pl.BlockSpec((1, D), lambda b, j: (0, 0)),
            pl.BlockSpec((D, D), lambda b, j: (0, 0)),
            pl.BlockSpec((1, D, N_MEM), lambda b, j: (b, 0, 0)),
            pl.BlockSpec((1, N_MEM, D), lambda b, j: (b, 0, 0)),
            pl.BlockSpec((D, D), lambda b, j: (0, 0)),
        ],
        out_specs=pl.BlockSpec((1, T, D), lambda b, j: (b, j, 0)),
        out_shape=jax.ShapeDtypeStruct((B, S, D), _f32),
        scratch_shapes=[pltpu.VMEM((T, D), _bf16), pltpu.VMEM((T, D), _bf16)],
        compiler_params=pltpu.CompilerParams(
            dimension_semantics=("arbitrary", "arbitrary"), vmem_limit_bytes=VMEM_LIMIT),
        name="xattn",
    )(h, norm_w.reshape(1, D), xq_w, kt, v, xo_w)


def _mlp_kernel(h_ref, nw_ref, wu_ref, wd_ref, fw_ref, o_ref, abuf):
    for m in range(0, h_ref.shape[0], MLP_ROW_CHUNK):
        hn = _rms(h_ref[m:m + MLP_ROW_CHUNK, :], nw_ref[...]).astype(_bf16)
        for c in range(0, D_FF, FF_CHUNK):
            u = jnp.maximum(_dot(hn, wu_ref[:, c:c + FF_CHUNK]), 0.0)
            abuf[m:m + MLP_ROW_CHUNK, c:c + FF_CHUNK] = (u * u).astype(_bf16)
    for m in range(0, h_ref.shape[0], MLP_TAIL_CHUNK):
        rows = slice(m, m + MLP_TAIL_CHUNK)
        o_ref[rows, :] = _rms(h_ref[rows, :] + _dot(abuf[rows, :], wd_ref[...]), fw_ref[...])


def _mlp(h, norm_w, up_w, down_w, norm_f_w):
    B, S, D = h.shape
    T = TOK_BLOCK
    h2 = h.reshape(B * S, D)
    out = pl.pallas_call(
        _mlp_kernel,
        grid=(B * S // T,),
        in_specs=[
            pl.BlockSpec((T, D), lambda i: (i, 0)),
            pl.BlockSpec((1, D), lambda i: (0, 0)),
            pl.BlockSpec((D, D_FF), lambda i: (0, 0), pipeline_mode=pl.Buffered(1)),
            pl.BlockSpec((D_FF, D), lambda i: (0, 0), pipeline_mode=pl.Buffered(1)),
            pl.BlockSpec((1, D), lambda i: (0, 0)),
        ],
        out_specs=pl.BlockSpec((T, D), lambda i: (i, 0)),
        out_shape=jax.ShapeDtypeStruct((B * S, D), _f32),
        scratch_shapes=[pltpu.VMEM((T, D_FF), _bf16)],
        compiler_params=pltpu.CompilerParams(
            dimension_semantics=("arbitrary",), vmem_limit_bytes=VMEM_LIMIT),
        name="mlp",
    )(h2, norm_w.reshape(1, D), up_w, down_w, norm_f_w.reshape(1, D))
    return out.reshape(B, S, D)


def kernel(x, mem, positions, norm_mix_w, w_in, conv_w, conv_b, conv_ln_w, conv_ln_b, ret_gn_w, w_out,
           norm_xattn_w, norm_mem_w, xq_w, xkv_w, xo_w, norm_mlp_w, mlp_up_w, mlp_down_w, norm_f_w):
    kt, v, w_in_b, w_out_b = _mem_kv(mem, norm_mem_w, xkv_w, w_in, w_out)
    h, (xq_b, xo_b, up_b, down_b) = _mixer(x, positions, norm_mix_w, w_in_b, conv_w, conv_b, conv_ln_w, conv_ln_b,
                                           ret_gn_w, w_out_b, (xq_w, xo_w, mlp_up_w, mlp_down_w))
    h = _xattn(h, norm_xattn_w, xq_b, kt, v, xo_b)
    return _mlp(h, norm_mlp_w, up_b, down_b, norm_f_w)
```

```python
import numpy as np
import jax
import jax.numpy as jnp
from jax import lax
from jax.experimental import pallas as pl
from jax.experimental.pallas import tpu as pltpu

D_MODEL = 1024
CHUNK = 64
CONV_CH = 512
CONV_WIDTH = 31
RET_HEADS = 8
RET_DV = 64
RET_DK = 32
RET_QK = RET_HEADS * RET_DK
RET_V = RET_HEADS * RET_DV
N_MEM = 256
XATTN_HEADS = 4
XATTN_HEAD_DIM = 256
D_FF = 4096
ROPE_BASE = 10000.0
EPS = 1e-6

OFF_A, OFF_B, OFF_Q, OFF_K, OFF_V, OFF_G = 0, 512, 1024, 1280, 1536, 2048
D_IN = 2560

LANES = 128
SUBLANES = 8
MXU_TILE = 256
MIX_BLOCK = 1024
STATE_COLS = MXU_TILE
STATE_ROWS = STATE_COLS // RET_DV * RET_DK
RET_BLOCK = 128
CONV_HALO = 32
CONV_SEG = 128
DFT_N = CONV_HALO + CONV_SEG
N_FREQ = DFT_N // 2 + 1
FREQ_PAD = -(-N_FREQ // SUBLANES) * SUBLANES
CONV_TAIL = MXU_TILE - DFT_N
CONV_ROWS = 32
TOK_BLOCK = 1024
XATTN_BLOCK = 2048
KV_STEPS = 1
ROW_CHUNK = 256
MLP_ROW_CHUNK = 256
XATTN_TAIL_CHUNK = 512
MLP_TAIL_CHUNK = 256
FF_CHUNK = 1024
VMEM_LIMIT = 56 * 1024 * 1024

_f32 = jnp.float32
_bf16 = jnp.bfloat16


def _dot(a, b):
    return jnp.dot(a, b, preferred_element_type=_f32)


def _rms(x, w):
    return x * lax.rsqrt(jnp.mean(x * x, axis=-1, keepdims=True) + EPS) * w


def _sigmoid(x):
    return 1.0 / (1.0 + jnp.exp(-x))


def _swish(x):
    h = 0.5 * x
    return h + h * jnp.tanh(h)


def _cast_columns(src_ref, dst_ref):
    for c in range(0, src_ref.shape[-1], MXU_TILE):
        dst_ref[:, c:c + MXU_TILE] = src_ref[:, c:c + MXU_TILE].astype(_bf16)


def _mem_kv_kernel(mem_ref, nw_ref, wkv_ref, win32_ref, wout32_ref, kt_ref, v_ref, win_ref, wout_ref):
    @pl.when(pl.program_id(0) % KV_STEPS == 0)
    def _():
        m = _rms(mem_ref[0], nw_ref[...]).astype(_bf16)
        k = _dot(m, wkv_ref[:, :D_MODEL].astype(_bf16))
        kt_ref[0] = k.T.astype(_bf16)
        v_ref[0] = _dot(m, wkv_ref[:, D_MODEL:].astype(_bf16)).astype(_bf16)

    _cast_columns(win32_ref.at[0], win_ref.at[0])
    _cast_columns(wout32_ref.at[0], wout_ref.at[0])


def _mem_kv(mem, norm_mem_w, xkv_w, w_in, w_out):
    B = mem.shape[0]
    steps = B * KV_STEPS
    slab = lambda w: w.reshape(steps, w.shape[0] // steps, w.shape[1])
    slab_spec = lambda w: pl.BlockSpec((1, w.shape[0] // steps, w.shape[1]), lambda i: (i, 0, 0))
    kt, v, win_b, wout_b = pl.pallas_call(
        _mem_kv_kernel,
        grid=(steps,),
        in_specs=[
            pl.BlockSpec((1, N_MEM, D_MODEL), lambda i: (i // KV_STEPS, 0, 0)),
            pl.BlockSpec((1, D_MODEL), lambda i: (0, 0)),
            pl.BlockSpec((D_MODEL, 2 * D_MODEL), lambda i: (0, 0)),
            slab_spec(w_in), slab_spec(w_out),
        ],
        out_specs=[
            pl.BlockSpec((1, D_MODEL, N_MEM), lambda i: (i // KV_STEPS, 0, 0)),
            pl.BlockSpec((1, N_MEM, D_MODEL), lambda i: (i // KV_STEPS, 0, 0)),
            slab_spec(w_in), slab_spec(w_out),
        ],
        out_shape=[
            jax.ShapeDtypeStruct((B, D_MODEL, N_MEM), _bf16),
            jax.ShapeDtypeStruct((B, N_MEM, D_MODEL), _bf16),
            jax.ShapeDtypeStruct(slab(w_in).shape, _bf16),
            jax.ShapeDtypeStruct(slab(w_out).shape, _bf16),
        ],
        compiler_params=pltpu.CompilerParams(
            dimension_semantics=("arbitrary",), vmem_limit_bytes=VMEM_LIMIT),
        name="mem_kv",
    )(mem, norm_mem_w.reshape(1, D_MODEL), xkv_w, slab(w_in), slab(w_out))
    return kt, v, win_b.reshape(w_in.shape), wout_b.reshape(w_out.shape)


def _group_norm_halves(y, lo):
    inv = 1.0 / RET_DV
    s_lo = jnp.sum(jnp.where(lo, y, 0.0), axis=-1, keepdims=True)
    s_hi = jnp.sum(jnp.where(lo, 0.0, y), axis=-1, keepdims=True)
    d = y - jnp.where(lo, s_lo, s_hi) * inv
    d2 = d * d
    v_lo = jnp.sum(jnp.where(lo, d2, 0.0), axis=-1, keepdims=True)
    v_hi = jnp.sum(jnp.where(lo, 0.0, d2), axis=-1, keepdims=True)
    return d * lax.rsqrt(jnp.where(lo, v_lo, v_hi) * inv + EPS)


N_CAST = 4
CAST_SCALES = (XATTN_HEAD_DIM ** -0.5, 1.0, 1.0, 1.0)


def _mixer_kernel(*refs):
    n_in = 20
    (x_ref, pos_ref, nw_ref, win_ref, cw_ref, cb_ref, lnw_ref, lnb_ref, gnw_ref, wout_ref,
     dmat_ref, xi_ref, zeta_ref, gl_ref, invf_ref, cost_ref, sint_ref, fwd32_ref, inv32_ref, tap_ref) = refs[:n_in]
    cast_src = refs[n_in:n_in + N_CAST]
    o_ref = refs[n_in + N_CAST]
    cast_dst = refs[n_in + 1 + N_CAST:n_in + 1 + 2 * N_CAST]
    (fwd_ref, inv_ref, hspec, hbuf, hist, ubuf, ybuf, qbuf, ktbuf, vbuf, vzbuf, gbuf, state, state_b,
     mbuf) = refs[n_in + 1 + 2 * N_CAST:]
    L = MIX_BLOCK

    @pl.when((pl.program_id(0) == 0) & (pl.program_id(1) == 0))
    def _():
        fwd_ref[...] = fwd32_ref[...].astype(_bf16)
        inv_ref[...] = inv32_ref[...].astype(_bf16)
        ubuf[...] = jnp.zeros(ubuf.shape, _bf16)
        hspec[...] = jnp.dot(tap_ref[...], cw_ref[...], preferred_element_type=_f32, precision=lax.Precision.HIGHEST)

    @pl.when(pl.program_id(1) == 0)
    def _():
        hist[...] = jnp.zeros((CONV_HALO, CONV_CH), _bf16)
        state[...] = jnp.zeros_like(state)
        state_b[...] = jnp.zeros_like(state_b)

    for src, dst, scale in zip(cast_src, cast_dst, CAST_SCALES):
        dst[...] = (src[...] if scale == 1.0 else src[...] * scale).astype(_bf16)

    x = x_ref[0]
    ubuf[0:CONV_HALO, :] = hist[...]

    def glu_rows(row_lo, row_hi):
        for m in range(row_lo, row_hi, ROW_CHUNK):
            hn_m = _rms(x_ref[0, m:m + ROW_CHUNK, :], nw_ref[...]).astype(_bf16)
            hbuf[m:m + ROW_CHUNK, :] = hn_m
            a = _dot(hn_m, win_ref[:, OFF_A:OFF_A + CONV_CH])
            b = _dot(hn_m, win_ref[:, OFF_B:OFF_B + CONV_CH])
            ubuf[CONV_HALO + m:CONV_HALO + m + ROW_CHUNK, :] = (a * _sigmoid(b)).astype(_bf16)

    ang0 = pos_ref[0, :, 0:1].astype(_f32) * invf_ref[...]
    cos0, sin0 = jnp.cos(ang0), jnp.sin(ang0)
    lane = lax.broadcasted_iota(jnp.int32, (1, RET_QK), 1)
    first_half = lane % RET_DK < RET_DK // 2
    sub_blocks = [slice(r0, r0 + RET_BLOCK) for r0 in range(0, L, RET_BLOCK)]

    def rotary(t, rows):
        cos = cos0 * cost_ref[rows, :] - sin0 * sint_ref[rows, :]
        sin = sin0 * cost_ref[rows, :] + cos0 * sint_ref[rows, :]
        cos2 = jnp.concatenate([cos, cos], axis=-1)
        sin2 = jnp.where(first_half, -1.0, 1.0) * jnp.concatenate([sin, sin], axis=-1)
        partner = jnp.where(first_half, pltpu.roll(t, RET_QK - RET_DK // 2, 1), pltpu.roll(t, RET_DK // 2, 1))
        return t * cos2 + partner * sin2

    def project_rows(row_lo, row_hi):
        hn = hbuf[row_lo:row_hi, :]
        mine = [(i, rows, slice(rows.start - row_lo, rows.stop - row_lo)) for i, rows in enumerate(sub_blocks)
                if row_lo <= rows.start < row_hi]
        q_all = _dot(hn, win_ref[:, OFF_Q:OFF_K])
        for _, rows, local in mine:
            qbuf[rows, :] = rotary(q_all[local, :], rows).astype(_bf16)
        k_all = _dot(hn, win_ref[:, OFF_K:OFF_V])
        for i, rows, local in mine:
            ktbuf[i] = rotary(k_all[local, :], rows).T.astype(_bf16)
        v_all = _dot(hn, win_ref[:, OFF_V:OFF_G])
        vbuf[row_lo:row_hi, :] = v_all.astype(_bf16)
        for _, rows, local in mine:
            vzbuf[rows, :] = (v_all[local, :] * zeta_ref[...]).astype(_bf16)
        gbuf[row_lo:row_hi, :] = _swish(_dot(hn, win_ref[:, OFF_G:D_IN]))

    h_re, h_im = hspec[0:FREQ_PAD, :], hspec[FREQ_PAD:2 * FREQ_PAD, :]

    def conv_rows(row_lo, row_hi):
        seg_rows = range(row_lo, row_hi, CONV_SEG)
        specs = [_dot(fwd_ref[...], ubuf[row0:row0 + MXU_TILE, :]) for row0 in seg_rows]
        prods = []
        for spec in specs:
            s_re, s_im = spec[0:FREQ_PAD, :], spec[FREQ_PAD:2 * FREQ_PAD, :]
            prods.append(jnp.concatenate([s_re * h_re - s_im * h_im, s_re * h_im + s_im * h_re,
                                          jnp.zeros((MXU_TILE - 2 * FREQ_PAD, CONV_CH), _f32)],
                                         axis=0).astype(_bf16))
        for row0, prod in zip(seg_rows, prods):
            ybuf[row0:row0 + CONV_SEG, :] = _dot(inv_ref[...], prod)
        half_lnw, half_lnb = 0.5 * lnw_ref[...], 0.5 * lnb_ref[...]
        for r in range(row_lo, row_hi, CONV_ROWS):
            acc = ybuf[r:r + CONV_ROWS, :] + cb_ref[...]
            mu = jnp.mean(acc, axis=-1, keepdims=True)
            d = acc - mu
            var = jnp.mean(d * d, axis=-1, keepdims=True)
            half = d * lax.rsqrt(var + EPS) * half_lnw + half_lnb
            mbuf[r:r + CONV_ROWS, 0:CONV_CH] = (half + half * jnp.tanh(half)).astype(_bf16)

    lo = lax.broadcasted_iota(jnp.int32, (1, LANES), 1) < RET_DV
    same_head = (lax.broadcasted_iota(jnp.int32, (STATE_ROWS, STATE_COLS), 0) // RET_DK
                 == lax.broadcasted_iota(jnp.int32, (STATE_ROWS, STATE_COLS), 1) // RET_DV)

    def retention(i):
        rows = sub_blocks[i]
        qb = qbuf[rows, :]
        kt = ktbuf[i]
        vb = vbuf[rows, :]

        yx = _dot(qb, state_b[...]) * xi_ref[...]

        for p in range(RET_HEADS // 2):
            cols = slice(p * LANES, (p + 1) * LANES)
            blank = jnp.zeros((RET_DK, RET_BLOCK), _bf16)
            k_pair = jnp.concatenate(
                [jnp.concatenate([kt[h * RET_DK:(h + 1) * RET_DK] if h == 2 * p else blank,
                                  kt[h * RET_DK:(h + 1) * RET_DK] if h == 2 * p + 1 else blank], axis=1)
                 for h in range(RET_HEADS)], axis=0)
            s = _dot(qb, k_pair)
            vp = vb[:, cols]
            off = jnp.zeros_like(vp)
            v_pair = jnp.concatenate([jnp.where(lo, vp, off), jnp.where(lo, off, vp)], axis=0)
            y = _dot((s * dmat_ref[p]).astype(_bf16), v_pair) + yx[:, cols]
            yn = _group_norm_halves(y, lo) * gnw_ref[:, cols]
            mbuf[rows, CONV_CH + p * LANES:CONV_CH + (p + 1) * LANES] = (gbuf[rows, cols] * yn).astype(_bf16)

        for q in range(RET_QK // STATE_ROWS):
            srows = slice(q * STATE_ROWS, (q + 1) * STATE_ROWS)
            scols = slice(q * STATE_COLS, (q + 1) * STATE_COLS)
            kv = _dot(kt[srows, :], vzbuf[rows, scols])
            new = gl_ref[:, scols] * state[srows, scols] + jnp.where(same_head, kv, 0.0)
            state[srows, scols] = new
            state_b[srows, scols] = new.astype(_bf16)

    half_rows = L // 2
    per_half = len(sub_blocks) // 2
    glu_rows(0, half_rows)
    project_rows(0, half_rows)
    conv_rows(0, half_rows)
    glu_rows(half_rows, L)
    hist[...] = ubuf[L:L + CONV_HALO, :]
    retention(0)
    project_rows(half_rows, L)
    for i in range(1, per_half):
        retention(i)
    conv_rows(half_rows, L)
    for i in range(per_half, len(sub_blocks)):
        retention(i)

    o_ref[0] = x + _dot(mbuf[...], wout_ref[...])


def _retention_tables(L, step_rows):
    h = np.arange(RET_HEADS, dtype=np.float64)
    log_g = np.log1p(-np.exp2(-5.0 - h))
    idx = np.arange(L, dtype=np.float64)
    dist = np.abs(idx[:, None] - idx[None, :])
    visible = (idx[None, :] // CHUNK) <= (idx[:, None] // CHUNK)
    dmat = np.where(visible[None], np.exp(log_g[:, None, None] * dist[None]), 0.0)
    dmat = np.concatenate([dmat[0::2], dmat[1::2]], axis=-1)
    xi = np.exp(log_g[None, :] * (idx[:, None] + 1.0))
    zeta = np.exp(log_g[None, :] * (L - 1.0 - idx[:, None]))
    gl = np.exp(log_g * L)
    rep = lambda t: np.repeat(t, RET_DV, axis=-1)
    inv_freq = ROPE_BASE ** (-np.arange(RET_DK // 2, dtype=np.float32) / np.float32(RET_DK // 2))
    invf = np.tile(inv_freq.astype(np.float32), LANES // (RET_DK // 2))[None, :]
    rel = np.arange(step_rows, dtype=np.float64)[:, None] * invf.astype(np.float64)
    f = lambda t: jnp.asarray(t, dtype=_f32)
    scale = RET_DK ** -0.5
    return (f(dmat * scale), f(rep(xi) * scale), f(rep(zeta)), f(rep(gl[None, :])), f(invf),
            f(np.cos(rel)), f(np.sin(rel)))


def _conv_dft_tables():
    n = np.arange(DFT_N, dtype=np.float64)
    f = np.arange(N_FREQ, dtype=np.float64)[:, None]
    w = 2.0 * np.pi / DFT_N
    fwd = np.zeros((2 * FREQ_PAD, MXU_TILE))
    fwd[:N_FREQ, :DFT_N] = np.cos(w * f * n[None, :])
    fwd[FREQ_PAD:FREQ_PAD + N_FREQ, :DFT_N] = -np.sin(w * f * n[None, :])
    delay = (CONV_WIDTH - 1) - np.arange(CONV_WIDTH, dtype=np.float64)[None, :]
    tap = np.zeros((2 * FREQ_PAD, CONV_HALO))
    tap[:N_FREQ, :CONV_WIDTH] = np.cos(w * f * delay)
    tap[FREQ_PAD:FREQ_PAD + N_FREQ, :CONV_WIDTH] = -np.sin(w * f * delay)
    weight = np.full((1, N_FREQ), 2.0)
    weight[0, 0] = weight[0, -1] = 1.0
    out = n[CONV_HALO:, None]
    inv = np.zeros((CONV_SEG, MXU_TILE))
    inv[:, :N_FREQ] = weight * np.cos(w * out * f.T) / DFT_N
    inv[:, FREQ_PAD:FREQ_PAD + N_FREQ] = -weight * np.sin(w * out * f.T) / DFT_N
    return [jnp.asarray(t, dtype=_f32) for t in (fwd, inv, tap)]


def _mixer(x, positions, norm_w, w_in, conv_w, conv_b, ln_w, ln_b, gn_w, w_out, later_weights):
    B, S, D = x.shape
    L = MIX_BLOCK
    n = S // L
    steps = B * n
    R = RET_BLOCK
    dmat, xi, zeta, gl, invf, cost, sint = _retention_tables(R, L)
    fwd, inv, tap = _conv_dft_tables()
    taps = jnp.pad(conv_w, ((0, CONV_HALO - CONV_WIDTH), (0, 0)))
    const = lambda shape: pl.BlockSpec(shape, lambda b, j: (0,) * len(shape))
    once = lambda shape: pl.BlockSpec(shape, lambda b, j: (0,) * len(shape), pipeline_mode=pl.Buffered(1))
    row = lambda n: pl.BlockSpec((1, n), lambda b, j: (0, 0))
    slabs = [w.reshape(steps, w.shape[0] // steps, w.shape[1]) for w in later_weights]
    slab_spec = lambda w: pl.BlockSpec((1,) + w.shape[1:], lambda b, j: (b * n + j, 0, 0))
    outs = pl.pallas_call(
        _mixer_kernel,
        grid=(B, n),
        in_specs=[
            pl.BlockSpec((1, L, D), lambda b, j: (b, j, 0)),
            pl.BlockSpec((1, 1, L), lambda b, j: (b * n + j, 0, 0)),
            row(D),
            once((D, D_IN)),
            const((CONV_HALO, CONV_CH)),
            row(CONV_CH), row(CONV_CH), row(CONV_CH), row(RET_V),
            once((D, D)),
            const((RET_HEADS // 2, R, 2 * R)),
            const((R, RET_V)), const((R, RET_V)), row(RET_V), row(LANES),
            const((L, LANES)), const((L, LANES)),
            const(fwd.shape), const(inv.shape), const(tap.shape),
        ] + [slab_spec(w) for w in slabs],
        out_specs=[pl.BlockSpec((1, L, D), lambda b, j: (b, j, 0))] + [slab_spec(w) for w in slabs],
        out_shape=[jax.ShapeDtypeStruct((B, S, D), _f32)] + [jax.ShapeDtypeStruct(w.shape, _bf16) for w in slabs],
        scratch_shapes=[
            pltpu.VMEM(fwd.shape, _bf16),
            pltpu.VMEM(inv.shape, _bf16),
            pltpu.VMEM((2 * FREQ_PAD, CONV_CH), _f32),
            pltpu.VMEM((L, D), _bf16),
            pltpu.VMEM((CONV_HALO, CONV_CH), _bf16),
            pltpu.VMEM((CONV_HALO + L + CONV_TAIL, CONV_CH), _bf16),
            pltpu.VMEM((L, CONV_CH), _f32),
            pltpu.VMEM((L, RET_QK), _bf16),
            pltpu.VMEM((L // R, RET_QK, R), _bf16),
            pltpu.VMEM((L, RET_V), _bf16),
            pltpu.VMEM((L, RET_V), _bf16),
            pltpu.VMEM((L, RET_V), _f32),
            pltpu.VMEM((RET_QK, RET_V), _f32),
            pltpu.VMEM((RET_QK, RET_V), _bf16),
            pltpu.VMEM((L, D), _bf16),
        ],
        compiler_params=pltpu.CompilerParams(
            dimension_semantics=("arbitrary", "arbitrary"), vmem_limit_bytes=VMEM_LIMIT),
        name="mixer",
    )(x, positions.reshape(steps, 1, L), norm_w.reshape(1, D), w_in, taps, conv_b.reshape(1, -1),
      ln_w.reshape(1, -1), ln_b.reshape(1, -1), gn_w.reshape(1, -1), w_out, dmat, xi, zeta, gl, invf, cost, sint,
      fwd, inv, tap, *slabs)
    return outs[0], [o.reshape(w.shape) for o, w in zip(outs[1:], later_weights)]


def _xattn_kernel(h_ref, nw_ref, wq_ref, kt_ref, v_ref, wo_ref, o_ref, qbuf, obuf):
    for m in range(0, h_ref.shape[1], ROW_CHUNK):
        hn = _rms(h_ref[0, m:m + ROW_CHUNK, :], nw_ref[...]).astype(_bf16)
        qbuf[m:m + ROW_CHUNK, :] = _dot(hn, wq_ref[...]).astype(_bf16)
    for i in range(XATTN_HEADS):
        cols = slice(i * XATTN_HEAD_DIM, (i + 1) * XATTN_HEAD_DIM)
        s = _dot(qbuf[:, cols], kt_ref[0, cols, :])
        e = jnp.exp(s - jnp.max(s, axis=-1, keepdims=True))
        o = _dot(e.astype(_bf16), v_ref[0, :, cols])
        obuf[:, cols] = (o * (1.0 / jnp.sum(e, axis=-1, keepdims=True))).astype(_bf16)
    for m in range(0, h_ref.shape[1], XATTN_TAIL_CHUNK):
        rows = slice(m, m + XATTN_TAIL_CHUNK)
        o_ref[0, rows, :] = h_ref[0, rows, :] + _dot(obuf[rows, :], wo_ref[...])


def _xattn(h, norm_w, xq_w, kt, v, xo_w):
    B, S, D = h.shape
    T = XATTN_BLOCK
    return pl.pallas_call(
        _xattn_kernel,
        grid=(B, S // T),
        in_specs=[
            pl.BlockSpec((1, T, D), lambda b, j: (b, j, 0)),
            pl.BlockSpec((1, D), lambda b, j: (0, 0)),
            pl.BlockSpec((D, D), lambda b, j: (0, 0)),
            pl.BlockSpec((1, D, N_MEM), lambda b, j: (b, 0, 0)),
            pl.BlockSpec((1, N_MEM, D), lambda b, j: (b, 0, 0)),
            pl.BlockSpec((D, D), lambda b, j: (0, 0)),
        ],
        out_specs=pl.BlockSpec((1, T, D), lambda b, j: (b, j, 0)),
        out_shape=jax.ShapeDtypeStruct((B, S, D), _f32),
        scratch_shapes=[pltpu.VMEM((T, D), _bf16), pltpu.VMEM((T, D), _bf16)],
        compiler_params=pltpu.CompilerParams(
            dimension_semantics=("arbitrary", "arbitrary"), vmem_limit_bytes=VMEM_LIMIT),
        name="xattn",
    )(h, norm_w.reshape(1, D), xq_w, kt, v, xo_w)


def _mlp_kernel(h_ref, nw_ref, wu_ref, wd_ref, fw_ref, o_ref, abuf):
    for m in range(0, h_ref.shape[0], MLP_ROW_CHUNK):
        hn = _rms(h_ref[m:m + MLP_ROW_CHUNK, :], nw_ref[...]).astype(_bf16)
        for c in range(0, D_FF, FF_CHUNK):
            u = jnp.maximum(_dot(hn, wu_ref[:, c:c + FF_CHUNK]), 0.0)
            abuf[m:m + MLP_ROW_CHUNK, c:c + FF_CHUNK] = (u * u).astype(_bf16)
    for m in range(0, h_ref.shape[0], MLP_TAIL_CHUNK):
        rows = slice(m, m + MLP_TAIL_CHUNK)
        o_ref[rows, :] = _rms(h_ref[rows, :] + _dot(abuf[rows, :], wd_ref[...]), fw_ref[...])


def _mlp(h, norm_w, up_w, down_w, norm_f_w):
    B, S, D = h.shape
    T = TOK_BLOCK
    h2 = h.reshape(B * S, D)
    out = pl.pallas_call(
        _mlp_kernel,
        grid=(B * S // T,),
        in_specs=[
            pl.BlockSpec((T, D), lambda i: (i, 0)),
            pl.BlockSpec((1, D), lambda i: (0, 0)),
            pl.BlockSpec((D, D_FF), lambda i: (0, 0), pipeline_mode=pl.Buffered(1)),
            pl.BlockSpec((D_FF, D), lambda i: (0, 0), pipeline_mode=pl.Buffered(1)),
            pl.BlockSpec((1, D), lambda i: (0, 0)),
        ],
        out_specs=pl.BlockSpec((T, D), lambda i: (i, 0)),
        out_shape=jax.ShapeDtypeStruct((B * S, D), _f32),
        scratch_shapes=[pltpu.VMEM((T, D_FF), _bf16)],
        compiler_params=pltpu.CompilerParams(
            dimension_semantics=("arbitrary",), vmem_limit_bytes=VMEM_LIMIT),
        name="mlp",
    )(h2, norm_w.reshape(1, D), up_w, down_w, norm_f_w.reshape(1, D))
    return out.reshape(B, S, D)


def kernel(x, mem, positions, norm_mix_w, w_in, conv_w, conv_b, conv_ln_w, conv_ln_b, ret_gn_w, w_out,
           norm_xattn_w, norm_mem_w, xq_w, xkv_w, xo_w, norm_mlp_w, mlp_up_w, mlp_down_w, norm_f_w):
    kt, v, w_in_b, w_out_b = _mem_kv(mem, norm_mem_w, xkv_w, w_in, w_out)
    h, (xq_b, xo_b, up_b, down_b) = _mixer(x, positions, norm_mix_w, w_in_b, conv_w, conv_b, conv_ln_w, conv_ln_b,
                                           ret_gn_w, w_out_b, (xq_w, xo_w, mlp_up_w, mlp_down_w))
    h = _xattn(h, norm_xattn_w, xq_b, kt, v, xo_b)
    return _mlp(h, norm_mlp_w, up_b, down_b, norm_f_w)
```

```python
import numpy as np
import jax
import jax.numpy as jnp
from jax import lax
from jax.experimental import pallas as pl
from jax.experimental.pallas import tpu as pltpu

D_MODEL = 1024
CHUNK = 64
CONV_CH = 512
CONV_WIDTH = 31
RET_HEADS = 8
RET_DV = 64
RET_DK = 32
RET_QK = RET_HEADS * RET_DK
RET_V = RET_HEADS * RET_DV
N_MEM = 256
XATTN_HEADS = 4
XATTN_HEAD_DIM = 256
D_FF = 4096
ROPE_BASE = 10000.0
EPS = 1e-6

OFF_A, OFF_B, OFF_Q, OFF_K, OFF_V, OFF_G = 0, 512, 1024, 1280, 1536, 2048
D_IN = 2560

LANES = 128
SUBLANES = 8
MXU_TILE = 256
MIX_BLOCK = 512
STATE_COLS = MXU_TILE
STATE_ROWS = STATE_COLS // RET_DV * RET_DK
RET_BLOCK = 128
CONV_HALO = 32
CONV_SEG = 128
DFT_N = CONV_HALO + CONV_SEG
N_FREQ = DFT_N // 2 + 1
FREQ_PAD = -(-N_FREQ // SUBLANES) * SUBLANES
CONV_TAIL = MXU_TILE - DFT_N
CONV_ROWS = 32
TOK_BLOCK = 1024
XATTN_BLOCK = 2048
KV_STEPS = 1
ROW_CHUNK = 256
MLP_ROW_CHUNK = 256
XATTN_TAIL_CHUNK = 512
MLP_TAIL_CHUNK = 256
FF_CHUNK = 1024
VMEM_LIMIT = 56 * 1024 * 1024

_f32 = jnp.float32
_bf16 = jnp.bfloat16


def _dot(a, b):
    return jnp.dot(a, b, preferred_element_type=_f32)


def _rms(x, w):
    return x * lax.rsqrt(jnp.mean(x * x, axis=-1, keepdims=True) + EPS) * w


def _sigmoid(x):
    return 1.0 / (1.0 + jnp.exp(-x))


def _swish(x):
    h = 0.5 * x
    return h + h * jnp.tanh(h)


def _cast_columns(src_ref, dst_ref):
    for c in range(0, src_ref.shape[-1], MXU_TILE):
        dst_ref[:, c:c + MXU_TILE] = src_ref[:, c:c + MXU_TILE].astype(_bf16)


def _mem_kv_kernel(mem_ref, nw_ref, wkv_ref, win32_ref, wout32_ref, kt_ref, v_ref, win_ref, wout_ref):
    @pl.when(pl.program_id(0) % KV_STEPS == 0)
    def _():
        m = _rms(mem_ref[0], nw_ref[...]).astype(_bf16)
        k = _dot(m, wkv_ref[:, :D_MODEL].astype(_bf16))
        kt_ref[0] = k.T.astype(_bf16)
        v_ref[0] = _dot(m, wkv_ref[:, D_MODEL:].astype(_bf16)).astype(_bf16)

    _cast_columns(win32_ref.at[0], win_ref.at[0])
    _cast_columns(wout32_ref.at[0], wout_ref.at[0])


def _mem_kv(mem, norm_mem_w, xkv_w, w_in, w_out):
    B = mem.shape[0]
    steps = B * KV_STEPS
    slab = lambda w: w.reshape(steps, w.shape[0] // steps, w.shape[1])
    slab_spec = lambda w: pl.BlockSpec((1, w.shape[0] // steps, w.shape[1]), lambda i: (i, 0, 0))
    kt, v, win_b, wout_b = pl.pallas_call(
        _mem_kv_kernel,
        grid=(steps,),
        in_specs=[
            pl.BlockSpec((1, N_MEM, D_MODEL), lambda i: (i // KV_STEPS, 0, 0)),
            pl.BlockSpec((1, D_MODEL), lambda i: (0, 0)),
            pl.BlockSpec((D_MODEL, 2 * D_MODEL), lambda i: (0, 0)),
            slab_spec(w_in), slab_spec(w_out),
        ],
        out_specs=[
            pl.BlockSpec((1, D_MODEL, N_MEM), lambda i: (i // KV_STEPS, 0, 0)),
            pl.BlockSpec((1, N_MEM, D_MODEL), lambda i: (i // KV_STEPS, 0, 0)),
            slab_spec(w_in), slab_spec(w_out),
        ],
        out_shape=[
            jax.ShapeDtypeStruct((B, D_MODEL, N_MEM), _bf16),
            jax.ShapeDtypeStruct((B, N_MEM, D_MODEL), _bf16),
            jax.ShapeDtypeStruct(slab(w_in).shape, _bf16),
            jax.ShapeDtypeStruct(slab(w_out).shape, _bf16),
        ],
        compiler_params=pltpu.CompilerParams(
            dimension_semantics=("arbitrary",), vmem_limit_bytes=VMEM_LIMIT),
        name="mem_kv",
    )(mem, norm_mem_w.reshape(1, D_MODEL), xkv_w, slab(w_in), slab(w_out))
    return kt, v, win_b.reshape(w_in.shape), wout_b.reshape(w_out.shape)


def _group_norm_halves(y, lo):
    inv = 1.0 / RET_DV
    s_lo = jnp.sum(jnp.where(lo, y, 0.0), axis=-1, keepdims=True)
    s_hi = jnp.sum(jnp.where(lo, 0.0, y), axis=-1, keepdims=True)
    d = y - jnp.where(lo, s_lo, s_hi) * inv
    d2 = d * d
    v_lo = jnp.sum(jnp.where(lo, d2, 0.0), axis=-1, keepdims=True)
    v_hi = jnp.sum(jnp.where(lo, 0.0, d2), axis=-1, keepdims=True)
    return d * lax.rsqrt(jnp.where(lo, v_lo, v_hi) * inv + EPS)


N_CAST = 4
CAST_SCALES = (XATTN_HEAD_DIM ** -0.5, 1.0, 1.0, 1.0)


def _mixer_kernel(*refs):
    n_in = 20
    (x_ref, pos_ref, nw_ref, win_ref, cw_ref, cb_ref, lnw_ref, lnb_ref, gnw_ref, wout_ref,
     dmat_ref, xi_ref, zeta_ref, gl_ref, invf_ref, cost_ref, sint_ref, fwd32_ref, inv32_ref, tap_ref) = refs[:n_in]
    cast_src = refs[n_in:n_in + N_CAST]
    o_ref = refs[n_in + N_CAST]
    cast_dst = refs[n_in + 1 + N_CAST:n_in + 1 + 2 * N_CAST]
    (fwd_ref, inv_ref, hspec, hbuf, hist, ubuf, ybuf, qbuf, ktbuf, vbuf, vzbuf, gbuf, state, state_b,
     mbuf) = refs[n_in + 1 + 2 * N_CAST:]
    L = MIX_BLOCK

    @pl.when((pl.program_id(0) == 0) & (pl.program_id(1) == 0))
    def _():
        fwd_ref[...] = fwd32_ref[...].astype(_bf16)
        inv_ref[...] = inv32_ref[...].astype(_bf16)
        ubuf[...] = jnp.zeros(ubuf.shape, _bf16)
        hspec[...] = jnp.dot(tap_ref[...], cw_ref[...], preferred_element_type=_f32, precision=lax.Precision.HIGHEST)

    @pl.when(pl.program_id(1) == 0)
    def _():
        hist[...] = jnp.zeros((CONV_HALO, CONV_CH), _bf16)
        state[...] = jnp.zeros_like(state)
        state_b[...] = jnp.zeros_like(state_b)

    for src, dst, scale in zip(cast_src, cast_dst, CAST_SCALES):
        dst[...] = (src[...] if scale == 1.0 else src[...] * scale).astype(_bf16)

    x = x_ref[0]
    ubuf[0:CONV_HALO, :] = hist[...]

    def glu_rows(row_lo, row_hi):
        for m in range(row_lo, row_hi, ROW_CHUNK):
            hn_m = _rms(x_ref[0, m:m + ROW_CHUNK, :], nw_ref[...]).astype(_bf16)
            hbuf[m:m + ROW_CHUNK, :] = hn_m
            a = _dot(hn_m, win_ref[:, OFF_A:OFF_A + CONV_CH])
            b = _dot(hn_m, win_ref[:, OFF_B:OFF_B + CONV_CH])
            ubuf[CONV_HALO + m:CONV_HALO + m + ROW_CHUNK, :] = (a * _sigmoid(b)).astype(_bf16)

    ang0 = pos_ref[0, :, 0:1].astype(_f32) * invf_ref[...]
    cos0, sin0 = jnp.cos(ang0), jnp.sin(ang0)
    lane = lax.broadcasted_iota(jnp.int32, (1, RET_QK), 1)
    first_half = lane % RET_DK < RET_DK // 2
    sub_blocks = [slice(r0, r0 + RET_BLOCK) for r0 in range(0, L, RET_BLOCK)]

    def rotary(t, rows):
        cos = cos0 * cost_ref[rows, :] - sin0 * sint_ref[rows, :]
        sin = sin0 * cost_ref[rows, :] + cos0 * sint_ref[rows, :]
        cos2 = jnp.concatenate([cos, cos], axis=-1)
        sin2 = jnp.where(first_half, -1.0, 1.0) * jnp.concatenate([sin, sin], axis=-1)
        partner = jnp.where(first_half, pltpu.roll(t, RET_QK - RET_DK // 2, 1), pltpu.roll(t, RET_DK // 2, 1))
        return t * cos2 + partner * sin2

    def project_rows(row_lo, row_hi):
        hn = hbuf[row_lo:row_hi, :]
        mine = [(i, rows, slice(rows.start - row_lo, rows.stop - row_lo)) for i, rows in enumerate(sub_blocks)
                if row_lo <= rows.start < row_hi]
        q_all = _dot(hn, win_ref[:, OFF_Q:OFF_K])
        for _, rows, local in mine:
            qbuf[rows, :] = rotary(q_all[local, :], rows).astype(_bf16)
        k_all = _dot(hn, win_ref[:, OFF_K:OFF_V])
        for i, rows, local in mine:
            ktbuf[i] = rotary(k_all[local, :], rows).T.astype(_bf16)
        v_all = _dot(hn, win_ref[:, OFF_V:OFF_G])
        vbuf[row_lo:row_hi, :] = v_all.astype(_bf16)
        for _, rows, local in mine:
            vzbuf[rows, :] = (v_all[local, :] * zeta_ref[...]).astype(_bf16)
        gbuf[row_lo:row_hi, :] = _swish(_dot(hn, win_ref[:, OFF_G:D_IN]))

    h_re, h_im = hspec[0:FREQ_PAD, :], hspec[FREQ_PAD:2 * FREQ_PAD, :]

    def conv_rows(row_lo, row_hi):
        seg_rows = range(row_lo, row_hi, CONV_SEG)
        specs = [_dot(fwd_ref[...], ubuf[row0:row0 + MXU_TILE, :]) for row0 in seg_rows]
        prods = []
        for spec in specs:
            s_re, s_im = spec[0:FREQ_PAD, :], spec[FREQ_PAD:2 * FREQ_PAD, :]
            prods.append(jnp.concatenate([s_re * h_re - s_im * h_im, s_re * h_im + s_im * h_re,
                                          jnp.zeros((MXU_TILE - 2 * FREQ_PAD, CONV_CH), _f32)],
                                         axis=0).astype(_bf16))
        for row0, prod in zip(seg_rows, prods):
            ybuf[row0:row0 + CONV_SEG, :] = _dot(inv_ref[...], prod)
        half_lnw, half_lnb = 0.5 * lnw_ref[...], 0.5 * lnb_ref[...]
        for r in range(row_lo, row_hi, CONV_ROWS):
            acc = ybuf[r:r + CONV_ROWS, :] + cb_ref[...]
            mu = jnp.mean(acc, axis=-1, keepdims=True)
            d = acc - mu
            var = jnp.mean(d * d, axis=-1, keepdims=True)
            half = d * lax.rsqrt(var + EPS) * half_lnw + half_lnb
            mbuf[r:r + CONV_ROWS, 0:CONV_CH] = (half + half * jnp.tanh(half)).astype(_bf16)

    lo = lax.broadcasted_iota(jnp.int32, (1, LANES), 1) < RET_DV
    same_head = (lax.broadcasted_iota(jnp.int32, (STATE_ROWS, STATE_COLS), 0) // RET_DK
                 == lax.broadcasted_iota(jnp.int32, (STATE_ROWS, STATE_COLS), 1) // RET_DV)

    def retention(i):
        rows = sub_blocks[i]
        qb = qbuf[rows, :]
        kt = ktbuf[i]
        vb = vbuf[rows, :]

        yx = _dot(qb, state_b[...]) * xi_ref[...]

        for p in range(RET_HEADS // 2):
            cols = slice(p * LANES, (p + 1) * LANES)
            blank = jnp.zeros((RET_DK, RET_BLOCK), _bf16)
            k_pair = jnp.concatenate(
                [jnp.concatenate([kt[h * RET_DK:(h + 1) * RET_DK] if h == 2 * p else blank,
                                  kt[h * RET_DK:(h + 1) * RET_DK] if h == 2 * p + 1 else blank], axis=1)
                 for h in range(RET_HEADS)], axis=0)
            s = _dot(qb, k_pair)
            vp = vb[:, cols]
            off = jnp.zeros_like(vp)
            v_pair = jnp.concatenate([jnp.where(lo, vp, off), jnp.where(lo, off, vp)], axis=0)
            y = _dot((s * dmat_ref[p]).astype(_bf16), v_pair) + yx[:, cols]
            yn = _group_norm_halves(y, lo) * gnw_ref[:, cols]
            mbuf[rows, CONV_CH + p * LANES:CONV_CH + (p + 1) * LANES] = (gbuf[rows, cols] * yn).astype(_bf16)

        for q in range(RET_QK // STATE_ROWS):
            srows = slice(q * STATE_ROWS, (q + 1) * STATE_ROWS)
            scols = slice(q * STATE_COLS, (q + 1) * STATE_COLS)
            kv = _dot(kt[srows, :], vzbuf[rows, scols])
            new = gl_ref[:, scols] * state[srows, scols] + jnp.where(same_head, kv, 0.0)
            state[srows, scols] = new
            state_b[srows, scols] = new.astype(_bf16)

    half_rows = L // 2
    per_half = len(sub_blocks) // 2
    glu_rows(0, half_rows)
    project_rows(0, half_rows)
    conv_rows(0, half_rows)
    glu_rows(half_rows, L)
    hist[...] = ubuf[L:L + CONV_HALO, :]
    retention(0)
    project_rows(half_rows, L)
    for i in range(1, per_half):
        retention(i)
    conv_rows(half_rows, L)
    for i in range(per_half, len(sub_blocks)):
        retention(i)

    o_ref[0] = x + _dot(mbuf[...], wout_ref[...])


def _retention_tables(L, step_rows):
    h = np.arange(RET_HEADS, dtype=np.float64)
    log_g = np.log1p(-np.exp2(-5.0 - h))
    idx = np.arange(L, dtype=np.float64)
    dist = np.abs(idx[:, None] - idx[None, :])
    visible = (idx[None, :] // CHUNK) <= (idx[:, None] // CHUNK)
    dmat = np.where(visible[None], np.exp(log_g[:, None, None] * dist[None]), 0.0)
    dmat = np.concatenate([dmat[0::2], dmat[1::2]], axis=-1)
    xi = np.exp(log_g[None, :] * (idx[:, None] + 1.0))
    zeta = np.exp(log_g[None, :] * (L - 1.0 - idx[:, None]))
    gl = np.exp(log_g * L)
    rep = lambda t: np.repeat(t, RET_DV, axis=-1)
    inv_freq = ROPE_BASE ** (-np.arange(RET_DK // 2, dtype=np.float32) / np.float32(RET_DK // 2))
    invf = np.tile(inv_freq.astype(np.float32), LANES // (RET_DK // 2))[None, :]
    rel = np.arange(step_rows, dtype=np.float64)[:, None] * invf.astype(np.float64)
    f = lambda t: jnp.asarray(t, dtype=_f32)
    scale = RET_DK ** -0.5
    return (f(dmat * scale), f(rep(xi) * scale), f(rep(zeta)), f(rep(gl[None, :])), f(invf),
            f(np.cos(rel)), f(np.sin(rel)))


def _conv_dft_tables():
    n = np.arange(DFT_N, dtype=np.float64)
    f = np.arange(N_FREQ, dtype=np.float64)[:, None]
    w = 2.0 * np.pi / DFT_N
    fwd = np.zeros((2 * FREQ_PAD, MXU_TILE))
    fwd[:N_FREQ, :DFT_N] = np.cos(w * f * n[None, :])
    fwd[FREQ_PAD:FREQ_PAD + N_FREQ, :DFT_N] = -np.sin(w * f * n[None, :])
    delay = (CONV_WIDTH - 1) - np.arange(CONV_WIDTH, dtype=np.float64)[None, :]
    tap = np.zeros((2 * FREQ_PAD, CONV_HALO))
    tap[:N_FREQ, :CONV_WIDTH] = np.cos(w * f * delay)
    tap[FREQ_PAD:FREQ_PAD + N_FREQ, :CONV_WIDTH] = -np.sin(w * f * delay)
    weight = np.full((1, N_FREQ), 2.0)
    weight[0, 0] = weight[0, -1] = 1.0
    out = n[CONV_HALO:, None]
    inv = np.zeros((CONV_SEG, MXU_TILE))
    inv[:, :N_FREQ] = weight * np.cos(w * out * f.T) / DFT_N
    inv[:, FREQ_PAD:FREQ_PAD + N_FREQ] = -weight * np.sin(w * out * f.T) / DFT_N
    return [jnp.asarray(t, dtype=_f32) for t in (fwd, inv, tap)]


def _mixer(x, positions, norm_w, w_in, conv_w, conv_b, ln_w, ln_b, gn_w, w_out, later_weights):
    B, S, D = x.shape
    L = MIX_BLOCK
    n = S // L
    steps = B * n
    R = RET_BLOCK
    dmat, xi, zeta, gl, invf, cost, sint = _retention_tables(R, L)
    fwd, inv, tap = _conv_dft_tables()
    taps = jnp.pad(conv_w, ((0, CONV_HALO - CONV_WIDTH), (0, 0)))
    const = lambda shape: pl.BlockSpec(shape, lambda b, j: (0,) * len(shape))
    once = lambda shape: pl.BlockSpec(shape, lambda b, j: (0,) * len(shape), pipeline_mode=pl.Buffered(1))
    row = lambda n: pl.BlockSpec((1, n), lambda b, j: (0, 0))
    slabs = [w.reshape(steps, w.shape[0] // steps, w.shape[1]) for w in later_weights]
    slab_spec = lambda w: pl.BlockSpec((1,) + w.shape[1:], lambda b, j: (b * n + j, 0, 0))
    outs = pl.pallas_call(
        _mixer_kernel,
        grid=(B, n),
        in_specs=[
            pl.BlockSpec((1, L, D), lambda b, j: (b, j, 0)),
            pl.BlockSpec((1, 1, L), lambda b, j: (b * n + j, 0, 0)),
            row(D),
            once((D, D_IN)),
            const((CONV_HALO, CONV_CH)),
            row(CONV_CH), row(CONV_CH), row(CONV_CH), row(RET_V),
            once((D, D)),
            const((RET_HEADS // 2, R, 2 * R)),
            const((R, RET_V)), const((R, RET_V)), row(RET_V), row(LANES),
            const((L, LANES)), const((L, LANES)),
            const(fwd.shape), const(inv.shape), const(tap.shape),
        ] + [slab_spec(w) for w in slabs],
        out_specs=[pl.BlockSpec((1, L, D), lambda b, j: (b, j, 0))] + [slab_spec(w) for w in slabs],
        out_shape=[jax.ShapeDtypeStruct((B, S, D), _f32)] + [jax.ShapeDtypeStruct(w.shape, _bf16) for w in slabs],
        scratch_shapes=[
            pltpu.VMEM(fwd.shape, _bf16),
            pltpu.VMEM(inv.shape, _bf16),
            pltpu.VMEM((2 * FREQ_PAD, CONV_CH), _f32),
            pltpu.VMEM((L, D), _bf16),
            pltpu.VMEM((CONV_HALO, CONV_CH), _bf16),
            pltpu.VMEM((CONV_HALO + L + CONV_TAIL, CONV_CH), _bf16),
            pltpu.VMEM((L, CONV_CH), _f32),
            pltpu.VMEM((L, RET_QK), _bf16),
            pltpu.VMEM((L // R, RET_QK, R), _bf16),
            pltpu.VMEM((L, RET_V), _bf16),
            pltpu.VMEM((L, RET_V), _bf16),
            pltpu.VMEM((L, RET_V), _f32),
            pltpu.VMEM((RET_QK, RET_V), _f32),
            pltpu.VMEM((RET_QK, RET_V), _bf16),
            pltpu.VMEM((L, D), _bf16),
        ],
        compiler_params=pltpu.CompilerParams(
            dimension_semantics=("arbitrary", "arbitrary"), vmem_limit_bytes=VMEM_LIMIT),
        name="mixer",
    )(x, positions.reshape(steps, 1, L), norm_w.reshape(1, D), w_in, taps, conv_b.reshape(1, -1),
      ln_w.reshape(1, -1), ln_b.reshape(1, -1), gn_w.reshape(1, -1), w_out, dmat, xi, zeta, gl, invf, cost, sint,
      fwd, inv, tap, *slabs)
    return outs[0], [o.reshape(w.shape) for o, w in zip(outs[1:], later_weights)]


def _xattn_kernel(h_ref, nw_ref, wq_ref, kt_ref, v_ref, wo_ref, o_ref, qbuf, obuf):
    for m in range(0, h_ref.shape[1], XATTN_TAIL_CHUNK):
        rows = slice(m, m + XATTN_TAIL_CHUNK)
        for r in range(m, m + XATTN_TAIL_CHUNK, ROW_CHUNK):
            hn = _rms(h_ref[0, r:r + ROW_CHUNK, :], nw_ref[...]).astype(_bf16)
            qbuf[r:r + ROW_CHUNK, :] = _dot(hn, wq_ref[...]).astype(_bf16)
        for i in range(XATTN_HEADS):
            cols = slice(i * XATTN_HEAD_DIM, (i + 1) * XATTN_HEAD_DIM)
            s = _dot(qbuf[rows, cols], kt_ref[0, cols, :])
            e = jnp.exp(s - jnp.max(s, axis=-1, keepdims=True))
            o = _dot(e.astype(_bf16), v_ref[0, :, cols])
            obuf[rows, cols] = (o * (1.0 / jnp.sum(e, axis=-1, keepdims=True))).astype(_bf16)
        o_ref[0, rows, :] = h_ref[0, rows, :] + _dot(obuf[rows, :], wo_ref[...])


def _xattn(h, norm_w, xq_w, kt, v, xo_w):
    B, S, D = h.shape
    T = XATTN_BLOCK
    return pl.pallas_call(
        _xattn_kernel,
        grid=(B, S // T),
        in_specs=[
            pl.BlockSpec((1, T, D), lambda b, j: (b, j, 0)),
            pl.BlockSpec((1, D), lambda b, j: (0, 0)),
            pl.BlockSpec((D, D), lambda b, j: (0, 0)),
            pl.BlockSpec((1, D, N_MEM), lambda b, j: (b, 0, 0)),
            pl.BlockSpec((1, N_MEM, D), lambda b, j: (b, 0, 0)),
            pl.BlockSpec((D, D), lambda b, j: (0, 0)),
        ],
        out_specs=pl.BlockSpec((1, T, D), lambda b, j: (b, j, 0)),
        out_shape=jax.ShapeDtypeStruct((B, S, D), _f32),
        scratch_shapes=[pltpu.VMEM((T, D), _bf16), pltpu.VMEM((T, D), _bf16)],
        compiler_params=pltpu.CompilerParams(
            dimension_semantics=("arbitrary", "arbitrary"), vmem_limit_bytes=VMEM_LIMIT),
        name="xattn",
    )(h, norm_w.reshape(1, D), xq_w, kt, v, xo_w)


def _mlp_kernel(h_ref, nw_ref, wu_ref, wd_ref, fw_ref, o_ref, abuf):
    for m in range(0, h_ref.shape[0], MLP_ROW_CHUNK):
        hn = _rms(h_ref[m:m + MLP_ROW_CHUNK, :], nw_ref[...]).astype(_bf16)
        for c in range(0, D_FF, FF_CHUNK):
            u = jnp.maximum(_dot(hn, wu_ref[:, c:c + FF_CHUNK]), 0.0)
            abuf[m:m + MLP_ROW_CHUNK, c:c + FF_CHUNK] = (u * u).astype(_bf16)
    for m in range(0, h_ref.shape[0], MLP_TAIL_CHUNK):
        rows = slice(m, m + MLP_TAIL_CHUNK)
        o_ref[rows, :] = _rms(h_ref[rows, :] + _dot(abuf[rows, :], wd_ref[...]), fw_ref[...])


def _mlp(h, norm_w, up_w, down_w, norm_f_w):
    B, S, D = h.shape
    T = TOK_BLOCK
    h2 = h.reshape(B * S, D)
    out = pl.pallas_call(
        _mlp_kernel,
        grid=(B * S // T,),
        in_specs=[
            pl.BlockSpec((T, D), lambda i: (i, 0)),
            pl.BlockSpec((1, D), lambda i: (0, 0)),
            pl.BlockSpec((D, D_FF), lambda i: (0, 0), pipeline_mode=pl.Buffered(1)),
            pl.BlockSpec((D_FF, D), lambda i: (0, 0), pipeline_mode=pl.Buffered(1)),
            pl.BlockSpec((1, D), lambda i: (0, 0)),
        ],
        out_specs=pl.BlockSpec((T, D), lambda i: (i, 0)),
        out_shape=jax.ShapeDtypeStruct((B * S, D), _f32),
        scratch_shapes=[pltpu.VMEM((T, D_FF), _bf16)],
        compiler_params=pltpu.CompilerParams(
            dimension_semantics=("arbitrary",), vmem_limit_bytes=VMEM_LIMIT),
        name="mlp",
    )(h2, norm_w.reshape(1, D), up_w, down_w, norm_f_w.reshape(1, D))
    return out.reshape(B, S, D)


def kernel(x, mem, positions, norm_mix_w, w_in, conv_w, conv_b, conv_ln_w, conv_ln_b, ret_gn_w, w_out,
           norm_xattn_w, norm_mem_w, xq_w, xkv_w, xo_w, norm_mlp_w, mlp_up_w, mlp_down_w, norm_f_w):
    kt, v, w_in_b, w_out_b = _mem_kv(mem, norm_mem_w, xkv_w, w_in, w_out)
    h, (xq_b, xo_b, up_b, down_b) = _mixer(x, positions, norm_mix_w, w_in_b, conv_w, conv_b, conv_ln_w, conv_ln_b,
                                           ret_gn_w, w_out_b, (xq_w, xo_w, mlp_up_w, mlp_down_w))
    h = _xattn(h, norm_xattn_w, xq_b, kt, v, xo_b)
    return _mlp(h, norm_mlp_w, up_b, down_b, norm_f_w)
```

```python
import numpy as np
import jax
import jax.numpy as jnp
from jax import lax
from jax.experimental import pallas as pl
from jax.experimental.pallas import tpu as pltpu

D_MODEL = 1024
CHUNK = 64
CONV_CH = 512
CONV_WIDTH = 31
RET_HEADS = 8
RET_DV = 64
RET_DK = 32
RET_QK = RET_HEADS * RET_DK
RET_V = RET_HEADS * RET_DV
N_MEM = 256
XATTN_HEADS = 4
XATTN_HEAD_DIM = 256
D_FF = 4096
ROPE_BASE = 10000.0
EPS = 1e-6

OFF_A, OFF_B, OFF_Q, OFF_K, OFF_V, OFF_G = 0, 512, 1024, 1280, 1536, 2048
D_IN = 2560

LANES = 128
SUBLANES = 8
MXU_TILE = 256
MIX_BLOCK = 512
STATE_COLS = MXU_TILE
STATE_ROWS = STATE_COLS // RET_DV * RET_DK
RET_BLOCK = 128
CONV_HALO = 32
CONV_SEG = 128
DFT_N = CONV_HALO + CONV_SEG
N_FREQ = DFT_N // 2 + 1
FREQ_PAD = -(-N_FREQ // SUBLANES) * SUBLANES
CONV_TAIL = MXU_TILE - DFT_N
CONV_ROWS = 32
TOK_BLOCK = 1024
XATTN_BLOCK = 2048
ROW_CHUNK = 256
MLP_ROW_CHUNK = 256
XATTN_TAIL_CHUNK = 512
MLP_TAIL_CHUNK = 256
FF_CHUNK = 1024
VMEM_LIMIT = 56 * 1024 * 1024

_f32 = jnp.float32
_bf16 = jnp.bfloat16


def _dot(a, b):
    return jnp.dot(a, b, preferred_element_type=_f32)


def _rms(x, w):
    return x * lax.rsqrt(jnp.mean(x * x, axis=-1, keepdims=True) + EPS) * w


def _sigmoid(x):
    return 1.0 / (1.0 + jnp.exp(-x))


def _swish(x):
    h = 0.5 * x
    return h + h * jnp.tanh(h)


def _cast_columns(src_ref, dst_ref):
    for c in range(0, src_ref.shape[-1], MXU_TILE):
        dst_ref[:, c:c + MXU_TILE] = src_ref[:, c:c + MXU_TILE].astype(_bf16)


def _mem_kv_kernel(mem_ref, nw_ref, wkv_ref, win32_ref, wout32_ref, kt_ref, v_ref, win_ref, wout_ref):
    m = _rms(mem_ref[0], nw_ref[...]).astype(_bf16)
    k = _dot(m, wkv_ref[:, :D_MODEL].astype(_bf16))
    kt_ref[0] = k.T.astype(_bf16)
    v_ref[0] = _dot(m, wkv_ref[:, D_MODEL:].astype(_bf16)).astype(_bf16)
    _cast_columns(win32_ref.at[0], win_ref.at[0])
    _cast_columns(wout32_ref.at[0], wout_ref.at[0])


def _mem_kv(mem, norm_mem_w, xkv_w, w_in, w_out):
    B = mem.shape[0]
    slab = lambda w: w.reshape(B, w.shape[0] // B, w.shape[1])
    slab_spec = lambda w: pl.BlockSpec((1, w.shape[0] // B, w.shape[1]), lambda b: (b, 0, 0))
    kt, v, win_b, wout_b = pl.pallas_call(
        _mem_kv_kernel,
        grid=(B,),
        in_specs=[
            pl.BlockSpec((1, N_MEM, D_MODEL), lambda b: (b, 0, 0)),
            pl.BlockSpec((1, D_MODEL), lambda b: (0, 0)),
            pl.BlockSpec((D_MODEL, 2 * D_MODEL), lambda b: (0, 0)),
            slab_spec(w_in), slab_spec(w_out),
        ],
        out_specs=[
            pl.BlockSpec((1, D_MODEL, N_MEM), lambda b: (b, 0, 0)),
            pl.BlockSpec((1, N_MEM, D_MODEL), lambda b: (b, 0, 0)),
            slab_spec(w_in), slab_spec(w_out),
        ],
        out_shape=[
            jax.ShapeDtypeStruct((B, D_MODEL, N_MEM), _bf16),
            jax.ShapeDtypeStruct((B, N_MEM, D_MODEL), _bf16),
            jax.ShapeDtypeStruct(slab(w_in).shape, _bf16),
            jax.ShapeDtypeStruct(slab(w_out).shape, _bf16),
        ],
        compiler_params=pltpu.CompilerParams(
            dimension_semantics=("arbitrary",), vmem_limit_bytes=VMEM_LIMIT),
        name="mem_kv",
    )(mem, norm_mem_w.reshape(1, D_MODEL), xkv_w, slab(w_in), slab(w_out))
    return kt, v, win_b.reshape(w_in.shape), wout_b.reshape(w_out.shape)


def _group_norm_halves(y, lo):
    inv = 1.0 / RET_DV
    s_lo = jnp.sum(jnp.where(lo, y, 0.0), axis=-1, keepdims=True)
    s_hi = jnp.sum(jnp.where(lo, 0.0, y), axis=-1, keepdims=True)
    d = y - jnp.where(lo, s_lo, s_hi) * inv
    d2 = d * d
    v_lo = jnp.sum(jnp.where(lo, d2, 0.0), axis=-1, keepdims=True)
    v_hi = jnp.sum(jnp.where(lo, 0.0, d2), axis=-1, keepdims=True)
    return d * lax.rsqrt(jnp.where(lo, v_lo, v_hi) * inv + EPS)


N_CAST = 4
CAST_SCALES = (XATTN_HEAD_DIM ** -0.5, 1.0, 1.0, 1.0)


def _mixer_kernel(*refs):
    n_in = 20
    (x_ref, pos_ref, nw_ref, win_ref, cw_ref, cb_ref, lnw_ref, lnb_ref, gnw_ref, wout_ref,
     dmat_ref, xi_ref, zeta_ref, gl_ref, invf_ref, cost_ref, sint_ref, fwd32_ref, inv32_ref, tap_ref) = refs[:n_in]
    cast_src = refs[n_in:n_in + N_CAST]
    o_ref = refs[n_in + N_CAST]
    cast_dst = refs[n_in + 1 + N_CAST:n_in + 1 + 2 * N_CAST]
    (inv_ref, hspec, hbuf, hist, ubuf, ybuf, qbuf, ktbuf, vbuf, vzbuf, gbuf, state, state_b, mbuf,
     fwd_ref) = refs[n_in + 1 + 2 * N_CAST:]
    L = MIX_BLOCK

    @pl.when((pl.program_id(0) == 0) & (pl.program_id(1) == 0))
    def _():
        fwd_ref[...] = fwd32_ref[...].astype(_bf16)
        inv_ref[...] = inv32_ref[...].astype(_bf16)
        ubuf[...] = jnp.zeros(ubuf.shape, _bf16)
        hspec[...] = jnp.dot(tap_ref[...], cw_ref[...], preferred_element_type=_f32, precision=lax.Precision.HIGHEST)

    @pl.when(pl.program_id(1) == 0)
    def _():
        hist[...] = jnp.zeros((CONV_HALO, CONV_CH), _bf16)
        state[...] = jnp.zeros_like(state)
        state_b[...] = jnp.zeros_like(state_b)

    for src, dst, scale in zip(cast_src, cast_dst, CAST_SCALES):
        dst[...] = (src[...] if scale == 1.0 else src[...] * scale).astype(_bf16)

    x = x_ref[0]
    ubuf[0:CONV_HALO, :] = hist[...]

    def glu_rows(row_lo, row_hi):
        for m in range(row_lo, row_hi, ROW_CHUNK):
            hn_m = _rms(x_ref[0, m:m + ROW_CHUNK, :], nw_ref[...]).astype(_bf16)
            hbuf[m:m + ROW_CHUNK, :] = hn_m
            a = _dot(hn_m, win_ref[:, OFF_A:OFF_A + CONV_CH])
            b = _dot(hn_m, win_ref[:, OFF_B:OFF_B + CONV_CH])
            ubuf[CONV_HALO + m:CONV_HALO + m + ROW_CHUNK, :] = (a * _sigmoid(b)).astype(_bf16)

    ang0 = pos_ref[0, :, 0:1].astype(_f32) * invf_ref[...]
    cos0, sin0 = jnp.cos(ang0), jnp.sin(ang0)
    lane = lax.broadcasted_iota(jnp.int32, (1, RET_QK), 1)
    first_half = lane % RET_DK < RET_DK // 2
    sub_blocks = [slice(r0, r0 + RET_BLOCK) for r0 in range(0, L, RET_BLOCK)]

    def rotary(t, rows):
        cos = cos0 * cost_ref[rows, :] - sin0 * sint_ref[rows, :]
        sin = sin0 * cost_ref[rows, :] + cos0 * sint_ref[rows, :]
        cos2 = jnp.concatenate([cos, cos], axis=-1)
        sin2 = jnp.where(first_half, -1.0, 1.0) * jnp.concatenate([sin, sin], axis=-1)
        partner = jnp.where(first_half, pltpu.roll(t, RET_QK - RET_DK // 2, 1), pltpu.roll(t, RET_DK // 2, 1))
        return t * cos2 + partner * sin2

    def project_rows(row_lo, row_hi):
        hn = hbuf[row_lo:row_hi, :]
        mine = [(i, rows, slice(rows.start - row_lo, rows.stop - row_lo)) for i, rows in enumerate(sub_blocks)
                if row_lo <= rows.start < row_hi]
        q_all = _dot(hn, win_ref[:, OFF_Q:OFF_K])
        for _, rows, local in mine:
            qbuf[rows, :] = rotary(q_all[local, :], rows).astype(_bf16)
        k_all = _dot(hn, win_ref[:, OFF_K:OFF_V])
        for i, rows, local in mine:
            ktbuf[i] = rotary(k_all[local, :], rows).T.astype(_bf16)
        v_all = _dot(hn, win_ref[:, OFF_V:OFF_G])
        vbuf[row_lo:row_hi, :] = v_all.astype(_bf16)
        for _, rows, local in mine:
            vzbuf[rows, :] = (v_all[local, :] * zeta_ref[...]).astype(_bf16)
        gbuf[row_lo:row_hi, :] = _swish(_dot(hn, win_ref[:, OFF_G:D_IN]))

    h_re, h_im = hspec[0:FREQ_PAD, :], hspec[FREQ_PAD:2 * FREQ_PAD, :]

    def conv_rows(row_lo, row_hi):
        seg_rows = range(row_lo, row_hi, CONV_SEG)
        specs = [_dot(fwd_ref[...], ubuf[row0:row0 + MXU_TILE, :]) for row0 in seg_rows]
        prods = []
        for spec in specs:
            s_re, s_im = spec[0:FREQ_PAD, :], spec[FREQ_PAD:2 * FREQ_PAD, :]
            prods.append(jnp.concatenate([s_re * h_re - s_im * h_im, s_re * h_im + s_im * h_re,
                                          jnp.zeros((MXU_TILE - 2 * FREQ_PAD, CONV_CH), _f32)],
                                         axis=0).astype(_bf16))
        for row0, prod in zip(seg_rows, prods):
            ybuf[row0:row0 + CONV_SEG, :] = _dot(inv_ref[...], prod)
        half_lnw, half_lnb = 0.5 * lnw_ref[...], 0.5 * lnb_ref[...]
        for r in range(row_lo, row_hi, CONV_ROWS):
            acc = ybuf[r:r + CONV_ROWS, :] + cb_ref[...]
            mu = jnp.mean(acc, axis=-1, keepdims=True)
            d = acc - mu
            var = jnp.mean(d * d, axis=-1, keepdims=True)
            half = d * lax.rsqrt(var + EPS) * half_lnw + half_lnb
            mbuf[r:r + CONV_ROWS, 0:CONV_CH] = (half + half * jnp.tanh(half)).astype(_bf16)

    lo = lax.broadcasted_iota(jnp.int32, (1, LANES), 1) < RET_DV
    same_head = (lax.broadcasted_iota(jnp.int32, (STATE_ROWS, STATE_COLS), 0) // RET_DK
                 == lax.broadcasted_iota(jnp.int32, (STATE_ROWS, STATE_COLS), 1) // RET_DV)

    def retention(i):
        rows = sub_blocks[i]
        qb = qbuf[rows, :]
        kt = ktbuf[i]
        vb = vbuf[rows, :]

        yx = _dot(qb, state_b[...]) * xi_ref[...]

        for p in range(RET_HEADS // 2):
            cols = slice(p * LANES, (p + 1) * LANES)
            blank = jnp.zeros((RET_DK, RET_BLOCK), _bf16)
            k_pair = jnp.concatenate(
                [jnp.concatenate([kt[h * RET_DK:(h + 1) * RET_DK] if h == 2 * p else blank,
                                  kt[h * RET_DK:(h + 1) * RET_DK] if h == 2 * p + 1 else blank], axis=1)
                 for h in range(RET_HEADS)], axis=0)
            s = _dot(qb, k_pair)
            vp = vb[:, cols]
            off = jnp.zeros_like(vp)
            v_pair = jnp.concatenate([jnp.where(lo, vp, off), jnp.where(lo, off, vp)], axis=0)
            y = _dot((s * dmat_ref[p]).astype(_bf16), v_pair) + yx[:, cols]
            yn = _group_norm_halves(y, lo) * gnw_ref[:, cols]
            mbuf[rows, CONV_CH + p * LANES:CONV_CH + (p + 1) * LANES] = (gbuf[rows, cols] * yn).astype(_bf16)

        for q in range(RET_QK // STATE_ROWS):
            srows = slice(q * STATE_ROWS, (q + 1) * STATE_ROWS)
            scols = slice(q * STATE_COLS, (q + 1) * STATE_COLS)
            kv = _dot(kt[srows, :], vzbuf[rows, scols])
            new = gl_ref[:, scols] * state[srows, scols] + jnp.where(same_head, kv, 0.0)
            state[srows, scols] = new
            state_b[srows, scols] = new.astype(_bf16)

    half_rows = L // 2
    per_half = len(sub_blocks) // 2
    glu_rows(0, half_rows)
    project_rows(0, half_rows)
    conv_rows(0, half_rows)
    glu_rows(half_rows, L)
    hist[...] = ubuf[L:L + CONV_HALO, :]
    retention(0)
    project_rows(half_rows, L)
    for i in range(1, per_half):
        retention(i)
    conv_rows(half_rows, L)
    for i in range(per_half, len(sub_blocks)):
        retention(i)

    o_ref[0] = x + _dot(mbuf[...], wout_ref[...])


def _retention_tables(L, step_rows):
    h = np.arange(RET_HEADS, dtype=np.float64)
    log_g = np.log1p(-np.exp2(-5.0 - h))
    idx = np.arange(L, dtype=np.float64)
    dist = np.abs(idx[:, None] - idx[None, :])
    visible = (idx[None, :] // CHUNK) <= (idx[:, None] // CHUNK)
    dmat = np.where(visible[None], np.exp(log_g[:, None, None] * dist[None]), 0.0)
    dmat = np.concatenate([dmat[0::2], dmat[1::2]], axis=-1)
    xi = np.exp(log_g[None, :] * (idx[:, None] + 1.0))
    zeta = np.exp(log_g[None, :] * (L - 1.0 - idx[:, None]))
    gl = np.exp(log_g * L)
    rep = lambda t: np.repeat(t, RET_DV, axis=-1)
    inv_freq = ROPE_BASE ** (-np.arange(RET_DK // 2, dtype=np.float32) / np.float32(RET_DK // 2))
    invf = np.tile(inv_freq.astype(np.float32), LANES // (RET_DK // 2))[None, :]
    rel = np.arange(step_rows, dtype=np.float64)[:, None] * invf.astype(np.float64)
    f = lambda t: jnp.asarray(t, dtype=_f32)
    scale = RET_DK ** -0.5
    return (f(dmat * scale), f(rep(xi) * scale), f(rep(zeta)), f(rep(gl[None, :])), f(invf),
            f(np.cos(rel)), f(np.sin(rel)))


def _conv_dft_tables():
    n = np.arange(DFT_N, dtype=np.float64)
    f = np.arange(N_FREQ, dtype=np.float64)[:, None]
    w = 2.0 * np.pi / DFT_N
    fwd = np.zeros((2 * FREQ_PAD, MXU_TILE))
    fwd[:N_FREQ, :DFT_N] = np.cos(w * f * n[None, :])
    fwd[FREQ_PAD:FREQ_PAD + N_FREQ, :DFT_N] = -np.sin(w * f * n[None, :])
    delay = (CONV_WIDTH - 1) - np.arange(CONV_WIDTH, dtype=np.float64)[None, :]
    tap = np.zeros((2 * FREQ_PAD, CONV_HALO))
    tap[:N_FREQ, :CONV_WIDTH] = np.cos(w * f * delay)
    tap[FREQ_PAD:FREQ_PAD + N_FREQ, :CONV_WIDTH] = -np.sin(w * f * delay)
    weight = np.full((1, N_FREQ), 2.0)
    weight[0, 0] = weight[0, -1] = 1.0
    out = n[CONV_HALO:, None]
    inv = np.zeros((CONV_SEG, MXU_TILE))
    inv[:, :N_FREQ] = weight * np.cos(w * out * f.T) / DFT_N
    inv[:, FREQ_PAD:FREQ_PAD + N_FREQ] = -weight * np.sin(w * out * f.T) / DFT_N
    return [jnp.asarray(t, dtype=_f32) for t in (fwd, inv, tap)]


def _mixer(x, positions, norm_w, w_in, conv_w, conv_b, ln_w, ln_b, gn_w, w_out, later_weights):
    B, S, D = x.shape
    L = MIX_BLOCK
    n = S // L
    steps = B * n
    R = RET_BLOCK
    dmat, xi, zeta, gl, invf, cost, sint = _retention_tables(R, L)
    fwd, inv, tap = _conv_dft_tables()
    taps = jnp.pad(conv_w, ((0, CONV_HALO - CONV_WIDTH), (0, 0)))
    const = lambda shape: pl.BlockSpec(shape, lambda b, j: (0,) * len(shape))
    once = lambda shape: pl.BlockSpec(shape, lambda b, j: (0,) * len(shape), pipeline_mode=pl.Buffered(1))
    row = lambda n: pl.BlockSpec((1, n), lambda b, j: (0, 0))
    slabs = [w.reshape(steps, w.shape[0] // steps, w.shape[1]) for w in later_weights]
    slab_spec = lambda w: pl.BlockSpec((1,) + w.shape[1:], lambda b, j: (b * n + j, 0, 0))
    outs = pl.pallas_call(
        _mixer_kernel,
        grid=(B, n),
        in_specs=[
            pl.BlockSpec((1, L, D), lambda b, j: (b, j, 0)),
            pl.BlockSpec((1, 1, L), lambda b, j: (b * n + j, 0, 0)),
            row(D),
            once((D, D_IN)),
            const((CONV_HALO, CONV_CH)),
            row(CONV_CH), row(CONV_CH), row(CONV_CH), row(RET_V),
            once((D, D)),
            const((RET_HEADS // 2, R, 2 * R)),
            const((R, RET_V)), const((R, RET_V)), row(RET_V), row(LANES),
            const((L, LANES)), const((L, LANES)),
            const(fwd.shape), const(inv.shape), const(tap.shape),
        ] + [slab_spec(w) for w in slabs],
        out_specs=[pl.BlockSpec((1, L, D), lambda b, j: (b, j, 0))] + [slab_spec(w) for w in slabs],
        out_shape=[jax.ShapeDtypeStruct((B, S, D), _f32)] + [jax.ShapeDtypeStruct(w.shape, _bf16) for w in slabs],
        scratch_shapes=[
            pltpu.VMEM(inv.shape, _bf16),
            pltpu.VMEM((2 * FREQ_PAD, CONV_CH), _f32),
            pltpu.VMEM((L, D), _bf16),
            pltpu.VMEM((CONV_HALO, CONV_CH), _bf16),
            pltpu.VMEM((CONV_HALO + L + CONV_TAIL, CONV_CH), _bf16),
            pltpu.VMEM((L, CONV_CH), _f32),
            pltpu.VMEM((L, RET_QK), _bf16),
            pltpu.VMEM((L // R, RET_QK, R), _bf16),
            pltpu.VMEM((L, RET_V), _bf16),
            pltpu.VMEM((L, RET_V), _bf16),
            pltpu.VMEM((L, RET_V), _f32),
            pltpu.VMEM((RET_QK, RET_V), _f32),
            pltpu.VMEM((RET_QK, RET_V), _bf16),
            pltpu.VMEM((L, D), _bf16),
            pltpu.VMEM(fwd.shape, _bf16),
        ],
        compiler_params=pltpu.CompilerParams(
            dimension_semantics=("arbitrary", "arbitrary"), vmem_limit_bytes=VMEM_LIMIT),
        name="mixer",
    )(x, positions.reshape(steps, 1, L), norm_w.reshape(1, D), w_in, taps, conv_b.reshape(1, -1),
      ln_w.reshape(1, -1), ln_b.reshape(1, -1), gn_w.reshape(1, -1), w_out, dmat, xi, zeta, gl, invf, cost, sint,
      fwd, inv, tap, *slabs)
    return outs[0], [o.reshape(w.shape) for o, w in zip(outs[1:], later_weights)]


def _xattn_kernel(h_ref, nw_ref, wq_ref, kt_ref, v_ref, wo_ref, o_ref, qbuf, obuf):
    for m in range(0, h_ref.shape[1], ROW_CHUNK):
        hn = _rms(h_ref[0, m:m + ROW_CHUNK, :], nw_ref[...]).astype(_bf16)
        qbuf[m:m + ROW_CHUNK, :] = _dot(hn, wq_ref[...]).astype(_bf16)
    for i in range(XATTN_HEADS):
        cols = slice(i * XATTN_HEAD_DIM, (i + 1) * XATTN_HEAD_DIM)
        s = _dot(qbuf[:, cols], kt_ref[0, cols, :])
        e = jnp.exp(s - jnp.max(s, axis=-1, keepdims=True))
        o = _dot(e.astype(_bf16), v_ref[0, :, cols])
        obuf[:, cols] = (o * (1.0 / jnp.sum(e, axis=-1, keepdims=True))).astype(_bf16)
    for m in range(0, h_ref.shape[1], XATTN_TAIL_CHUNK):
        rows = slice(m, m + XATTN_TAIL_CHUNK)
        o_ref[0, rows, :] = h_ref[0, rows, :] + _dot(obuf[rows, :], wo_ref[...])


def _xattn(h, norm_w, xq_w, kt, v, xo_w):
    B, S, D = h.shape
    T = XATTN_BLOCK
    return pl.pallas_call(
        _xattn_kernel,
        grid=(B, S // T),
        in_specs=[
            pl.BlockSpec((1, T, D), lambda b, j: (b, j, 0)),
            pl.BlockSpec((1, D), lambda b, j: (0, 0)),
            pl.BlockSpec((D, D), lambda b, j: (0, 0)),
            pl.BlockSpec((1, D, N_MEM), lambda b, j: (b, 0, 0)),
            pl.BlockSpec((1, N_MEM, D), lambda b, j: (b, 0, 0)),
            pl.BlockSpec((D, D), lambda b, j: (0, 0)),
        ],
        out_specs=pl.BlockSpec((1, T, D), lambda b, j: (b, j, 0)),
        out_shape=jax.ShapeDtypeStruct((B, S, D), _f32),
        scratch_shapes=[pltpu.VMEM((T, D), _bf16), pltpu.VMEM((T, D), _bf16)],
        compiler_params=pltpu.CompilerParams(
            dimension_semantics=("arbitrary", "arbitrary"), vmem_limit_bytes=VMEM_LIMIT),
        name="xattn",
    )(h, norm_w.reshape(1, D), xq_w, kt, v, xo_w)


def _mlp_kernel(h_ref, nw_ref, wu_ref, wd_ref, fw_ref, o_ref, abuf):
    for m in range(0, h_ref.shape[0], MLP_ROW_CHUNK):
        hn = _rms(h_ref[m:m + MLP_ROW_CHUNK, :], nw_ref[...]).astype(_bf16)
        for c in range(0, D_FF, FF_CHUNK):
            u = jnp.maximum(_dot(hn, wu_ref[:, c:c + FF_CHUNK]), 0.0)
            abuf[m:m + MLP_ROW_CHUNK, c:c + FF_CHUNK] = (u * u).astype(_bf16)
    for m in range(0, h_ref.shape[0], MLP_TAIL_CHUNK):
        rows = slice(m, m + MLP_TAIL_CHUNK)
        o_ref[rows, :] = _rms(h_ref[rows, :] + _dot(abuf[rows, :], wd_ref[...]), fw_ref[...])


def _mlp(h, norm_w, up_w, down_w, norm_f_w):
    B, S, D = h.shape
    T = TOK_BLOCK
    h2 = h.reshape(B * S, D)
    out = pl.pallas_call(
        _mlp_kernel,
        grid=(B * S // T,),
        in_specs=[
            pl.BlockSpec((T, D), lambda i: (i, 0)),
            pl.BlockSpec((1, D), lambda i: (0, 0)),
            pl.BlockSpec((D, D_FF), lambda i: (0, 0), pipeline_mode=pl.Buffered(1)),
            pl.BlockSpec((D_FF, D), lambda i: (0, 0), pipeline_mode=pl.Buffered(1)),
            pl.BlockSpec((1, D), lambda i: (0, 0)),
        ],
        out_specs=pl.BlockSpec((T, D), lambda i: (i, 0)),
        out_shape=jax.ShapeDtypeStruct((B * S, D), _f32),
        scratch_shapes=[pltpu.VMEM((T, D_FF), _bf16)],
        compiler_params=pltpu.CompilerParams(
            dimension_semantics=("arbitrary",), vmem_limit_bytes=VMEM_LIMIT),
        name="mlp",
    )(h2, norm_w.reshape(1, D), up_w, down_w, norm_f_w.reshape(1, D))
    return out.reshape(B, S, D)


def kernel(x, mem, positions, norm_mix_w, w_in, conv_w, conv_b, conv_ln_w, conv_ln_b, ret_gn_w, w_out,
           norm_xattn_w, norm_mem_w, xq_w, xkv_w, xo_w, norm_mlp_w, mlp_up_w, mlp_down_w, norm_f_w):
    kt, v, w_in_b, w_out_b = _mem_kv(mem, norm_mem_w, xkv_w, w_in, w_out)
    h, (xq_b, xo_b, up_b, down_b) = _mixer(x, positions, norm_mix_w, w_in_b, conv_w, conv_b, conv_ln_w, conv_ln_b,
                                           ret_gn_w, w_out_b, (xq_w, xo_w, mlp_up_w, mlp_down_w))
    h = _xattn(h, norm_xattn_w, xq_b, kt, v, xo_b)
    return _mlp(h, norm_mlp_w, up_b, down_b, norm_f_w)
```

```python
import numpy as np
import jax
import jax.numpy as jnp
from jax import lax
from jax.experimental import pallas as pl
from jax.experimental.pallas import tpu as pltpu

D_MODEL = 1024
CHUNK = 64
CONV_CH = 512
CONV_WIDTH = 31
RET_HEADS = 8
RET_DV = 64
RET_DK = 32
RET_QK = RET_HEADS * RET_DK
RET_V = RET_HEADS * RET_DV
N_MEM = 256
XATTN_HEADS = 4
XATTN_HEAD_DIM = 256
D_FF = 4096
ROPE_BASE = 10000.0
EPS = 1e-6

OFF_A, OFF_B, OFF_Q, OFF_K, OFF_V, OFF_G = 0, 512, 1024, 1280, 1536, 2048
D_IN = 2560

LANES = 128
SUBLANES = 8
MXU_TILE = 256
MIX_BLOCK = 512
STATE_COLS = MXU_TILE
STATE_ROWS = STATE_COLS // RET_DV * RET_DK
RET_BLOCK = 128
CONV_HALO = 32
CONV_SEG = 128
DFT_N = CONV_HALO + CONV_SEG
N_FREQ = DFT_N // 2 + 1
FREQ_PAD = -(-N_FREQ // SUBLANES) * SUBLANES
CONV_TAIL = MXU_TILE - DFT_N
CONV_ROWS = 32
TOK_BLOCK = 1024
XATTN_BLOCK = 2048
ROW_CHUNK = 256
MLP_ROW_CHUNK = 256
XATTN_TAIL_CHUNK = 512
MLP_TAIL_CHUNK = 256
FF_CHUNK = 1024
VMEM_LIMIT = 56 * 1024 * 1024

_f32 = jnp.float32
_bf16 = jnp.bfloat16


def _dot(a, b):
    return jnp.dot(a, b, preferred_element_type=_f32)


def _rms(x, w):
    return x * lax.rsqrt(jnp.mean(x * x, axis=-1, keepdims=True) + EPS) * w


def _sigmoid(x):
    return 1.0 / (1.0 + jnp.exp(-x))


def _swish(x):
    h = 0.5 * x
    return h + h * jnp.tanh(h)


def _cast_columns(src_ref, dst_ref):
    for c in range(0, src_ref.shape[-1], MXU_TILE):
        dst_ref[:, c:c + MXU_TILE] = src_ref[:, c:c + MXU_TILE].astype(_bf16)


def _mem_kv_kernel(mem_ref, nw_ref, wkv_ref, win32_ref, wout32_ref, kt_ref, v_ref, win_ref, wout_ref):
    m = _rms(mem_ref[0], nw_ref[...]).astype(_bf16)
    k = _dot(m, wkv_ref[:, :D_MODEL].astype(_bf16))
    kt_ref[0] = k.T.astype(_bf16)
    v_ref[0] = _dot(m, wkv_ref[:, D_MODEL:].astype(_bf16)).astype(_bf16)
    _cast_columns(win32_ref.at[0], win_ref.at[0])
    _cast_columns(wout32_ref.at[0], wout_ref.at[0])


def _mem_kv(mem, norm_mem_w, xkv_w, w_in, w_out):
    B = mem.shape[0]
    slab = lambda w: w.reshape(B, w.shape[0] // B, w.shape[1])
    slab_spec = lambda w: pl.BlockSpec((1, w.shape[0] // B, w.shape[1]), lambda b: (b, 0, 0))
    kt, v, win_b, wout_b = pl.pallas_call(
        _mem_kv_kernel,
        grid=(B,),
        in_specs=[
            pl.BlockSpec((1, N_MEM, D_MODEL), lambda b: (b, 0, 0)),
            pl.BlockSpec((1, D_MODEL), lambda b: (0, 0)),
            pl.BlockSpec((D_MODEL, 2 * D_MODEL), lambda b: (0, 0)),
            slab_spec(w_in), slab_spec(w_out),
        ],
        out_specs=[
            pl.BlockSpec((1, D_MODEL, N_MEM), lambda b: (b, 0, 0)),
            pl.BlockSpec((1, N_MEM, D_MODEL), lambda b: (b, 0, 0)),
            slab_spec(w_in), slab_spec(w_out),
        ],
        out_shape=[
            jax.ShapeDtypeStruct((B, D_MODEL, N_MEM), _bf16),
            jax.ShapeDtypeStruct((B, N_MEM, D_MODEL), _bf16),
            jax.ShapeDtypeStruct(slab(w_in).shape, _bf16),
            jax.ShapeDtypeStruct(slab(w_out).shape, _bf16),
        ],
        compiler_params=pltpu.CompilerParams(
            dimension_semantics=("arbitrary",), vmem_limit_bytes=VMEM_LIMIT),
        name="mem_kv",
    )(mem, norm_mem_w.reshape(1, D_MODEL), xkv_w, slab(w_in), slab(w_out))
    return kt, v, win_b.reshape(w_in.shape), wout_b.reshape(w_out.shape)


def _group_norm_halves(y, lo):
    inv = 1.0 / RET_DV
    s_lo = jnp.sum(jnp.where(lo, y, 0.0), axis=-1, keepdims=True)
    s_hi = jnp.sum(jnp.where(lo, 0.0, y), axis=-1, keepdims=True)
    d = y - jnp.where(lo, s_lo, s_hi) * inv
    d2 = d * d
    v_lo = jnp.sum(jnp.where(lo, d2, 0.0), axis=-1, keepdims=True)
    v_hi = jnp.sum(jnp.where(lo, 0.0, d2), axis=-1, keepdims=True)
    return d * lax.rsqrt(jnp.where(lo, v_lo, v_hi) * inv + EPS)


N_CAST = 4
CAST_SCALES = (XATTN_HEAD_DIM ** -0.5, 1.0, 1.0, 1.0)


def _mixer_kernel(*refs):
    n_in = 20
    (x_ref, pos_ref, nw_ref, win_ref, cw_ref, cb_ref, lnw_ref, lnb_ref, gnw_ref, wout_ref,
     dmat_ref, xi_ref, zeta_ref, gl_ref, invf_ref, cost_ref, sint_ref, fwd32_ref, inv32_ref, tap_ref) = refs[:n_in]
    cast_src = refs[n_in:n_in + N_CAST]
    o_ref = refs[n_in + N_CAST]
    cast_dst = refs[n_in + 1 + N_CAST:n_in + 1 + 2 * N_CAST]
    (inv_ref, hspec, cwbuf, hbuf, hist, ubuf, ybuf, qbuf, ktbuf, vbuf, vzbuf, gbuf, state, state_b, mbuf,
     fwd_ref) = refs[n_in + 1 + 2 * N_CAST:]
    L = MIX_BLOCK

    @pl.when((pl.program_id(0) == 0) & (pl.program_id(1) == 0))
    def _():
        fwd_ref[...] = fwd32_ref[...].astype(_bf16)
        inv_ref[...] = inv32_ref[...].astype(_bf16)
        ubuf[...] = jnp.zeros(ubuf.shape, _bf16)
        cwbuf[...] = jnp.zeros(cwbuf.shape, _f32)
        cwbuf[0:CONV_WIDTH, :] = cw_ref[...]
        hspec[...] = jnp.dot(tap_ref[...], cwbuf[...], preferred_element_type=_f32, precision=lax.Precision.HIGHEST)

    @pl.when(pl.program_id(1) == 0)
    def _():
        hist[...] = jnp.zeros((CONV_HALO, CONV_CH), _bf16)
        state[...] = jnp.zeros_like(state)
        state_b[...] = jnp.zeros_like(state_b)

    for src, dst, scale in zip(cast_src, cast_dst, CAST_SCALES):
        dst[...] = (src[...] if scale == 1.0 else src[...] * scale).astype(_bf16)

    x = x_ref[0]
    ubuf[0:CONV_HALO, :] = hist[...]

    def glu_rows(row_lo, row_hi):
        for m in range(row_lo, row_hi, ROW_CHUNK):
            hn_m = _rms(x_ref[0, m:m + ROW_CHUNK, :], nw_ref[...]).astype(_bf16)
            hbuf[m:m + ROW_CHUNK, :] = hn_m
            a = _dot(hn_m, win_ref[:, OFF_A:OFF_A + CONV_CH])
            b = _dot(hn_m, win_ref[:, OFF_B:OFF_B + CONV_CH])
            ubuf[CONV_HALO + m:CONV_HALO + m + ROW_CHUNK, :] = (a * _sigmoid(b)).astype(_bf16)

    first_pos = pos_ref[:, 0:1].astype(_f32)
    batch_row = lax.broadcasted_iota(jnp.int32, first_pos.shape, 0)
    ang0 = jnp.sum(jnp.where(batch_row == pl.program_id(0), first_pos, 0.0), axis=0, keepdims=True) * invf_ref[...]
    cos0, sin0 = jnp.cos(ang0), jnp.sin(ang0)
    lane = lax.broadcasted_iota(jnp.int32, (1, RET_QK), 1)
    first_half = lane % RET_DK < RET_DK // 2
    sub_blocks = [slice(r0, r0 + RET_BLOCK) for r0 in range(0, L, RET_BLOCK)]

    def rotary(t, rows):
        cos = cos0 * cost_ref[rows, :] - sin0 * sint_ref[rows, :]
        sin = sin0 * cost_ref[rows, :] + cos0 * sint_ref[rows, :]
        cos2 = jnp.concatenate([cos, cos], axis=-1)
        sin2 = jnp.where(first_half, -1.0, 1.0) * jnp.concatenate([sin, sin], axis=-1)
        partner = jnp.where(first_half, pltpu.roll(t, RET_QK - RET_DK // 2, 1), pltpu.roll(t, RET_DK // 2, 1))
        return t * cos2 + partner * sin2

    def project_rows(row_lo, row_hi):
        hn = hbuf[row_lo:row_hi, :]
        mine = [(i, rows, slice(rows.start - row_lo, rows.stop - row_lo)) for i, rows in enumerate(sub_blocks)
                if row_lo <= rows.start < row_hi]
        q_all = _dot(hn, win_ref[:, OFF_Q:OFF_K])
        for _, rows, local in mine:
            qbuf[rows, :] = rotary(q_all[local, :], rows).astype(_bf16)
        k_all = _dot(hn, win_ref[:, OFF_K:OFF_V])
        for i, rows, local in mine:
            ktbuf[i] = rotary(k_all[local, :], rows).T.astype(_bf16)
        v_all = _dot(hn, win_ref[:, OFF_V:OFF_G])
        vbuf[row_lo:row_hi, :] = v_all.astype(_bf16)
        for _, rows, local in mine:
            vzbuf[rows, :] = (v_all[local, :] * zeta_ref[...]).astype(_bf16)
        gbuf[row_lo:row_hi, :] = _swish(_dot(hn, win_ref[:, OFF_G:D_IN]))

    h_re, h_im = hspec[0:FREQ_PAD, :], hspec[FREQ_PAD:2 * FREQ_PAD, :]

    def conv_rows(row_lo, row_hi):
        seg_rows = range(row_lo, row_hi, CONV_SEG)
        specs = [_dot(fwd_ref[...], ubuf[row0:row0 + MXU_TILE, :]) for row0 in seg_rows]
        prods = []
        for spec in specs:
            s_re, s_im = spec[0:FREQ_PAD, :], spec[FREQ_PAD:2 * FREQ_PAD, :]
            prods.append(jnp.concatenate([s_re * h_re - s_im * h_im, s_re * h_im + s_im * h_re,
                                          jnp.zeros((MXU_TILE - 2 * FREQ_PAD, CONV_CH), _f32)],
                                         axis=0).astype(_bf16))
        for row0, prod in zip(seg_rows, prods):
            ybuf[row0:row0 + CONV_SEG, :] = _dot(inv_ref[...], prod)
        half_lnw, half_lnb = 0.5 * lnw_ref[...], 0.5 * lnb_ref[...]
        for r in range(row_lo, row_hi, CONV_ROWS):
            acc = ybuf[r:r + CONV_ROWS, :] + cb_ref[...]
            mu = jnp.mean(acc, axis=-1, keepdims=True)
            d = acc - mu
            var = jnp.mean(d * d, axis=-1, keepdims=True)
            half = d * lax.rsqrt(var + EPS) * half_lnw + half_lnb
            mbuf[r:r + CONV_ROWS, 0:CONV_CH] = (half + half * jnp.tanh(half)).astype(_bf16)

    lo = lax.broadcasted_iota(jnp.int32, (1, LANES), 1) < RET_DV
    same_head = (lax.broadcasted_iota(jnp.int32, (STATE_ROWS, STATE_COLS), 0) // RET_DK
                 == lax.broadcasted_iota(jnp.int32, (STATE_ROWS, STATE_COLS), 1) // RET_DV)

    def retention(i):
        rows = sub_blocks[i]
        qb = qbuf[rows, :]
        kt = ktbuf[i]
        vb = vbuf[rows, :]

        yx = _dot(qb, state_b[...]) * xi_ref[...]

        for p in range(RET_HEADS // 2):
            cols = slice(p * LANES, (p + 1) * LANES)
            blank = jnp.zeros((RET_DK, RET_BLOCK), _bf16)
            k_pair = jnp.concatenate(
                [jnp.concatenate([kt[h * RET_DK:(h + 1) * RET_DK] if h == 2 * p else blank,
                                  kt[h * RET_DK:(h + 1) * RET_DK] if h == 2 * p + 1 else blank], axis=1)
                 for h in range(RET_HEADS)], axis=0)
            s = _dot(qb, k_pair)
            vp = vb[:, cols]
            off = jnp.zeros_like(vp)
            v_pair = jnp.concatenate([jnp.where(lo, vp, off), jnp.where(lo, off, vp)], axis=0)
            y = _dot((s * dmat_ref[p]).astype(_bf16), v_pair) + yx[:, cols]
            yn = _group_norm_halves(y, lo) * gnw_ref[:, cols]
            mbuf[rows, CONV_CH + p * LANES:CONV_CH + (p + 1) * LANES] = (gbuf[rows, cols] * yn).astype(_bf16)

        for q in range(RET_QK // STATE_ROWS):
            srows = slice(q * STATE_ROWS, (q + 1) * STATE_ROWS)
            scols = slice(q * STATE_COLS, (q + 1) * STATE_COLS)
            kv = _dot(kt[srows, :], vzbuf[rows, scols])
            new = gl_ref[:, scols] * state[srows, scols] + jnp.where(same_head, kv, 0.0)
            state[srows, scols] = new
            state_b[srows, scols] = new.astype(_bf16)

    half_rows = L // 2
    per_half = len(sub_blocks) // 2
    glu_rows(0, half_rows)
    project_rows(0, half_rows)
    conv_rows(0, half_rows)
    glu_rows(half_rows, L)
    hist[...] = ubuf[L:L + CONV_HALO, :]
    retention(0)
    project_rows(half_rows, L)
    for i in range(1, per_half):
        retention(i)
    conv_rows(half_rows, L)
    for i in range(per_half, len(sub_blocks)):
        retention(i)

    o_ref[0] = x + _dot(mbuf[...], wout_ref[...])


def _retention_tables(L, step_rows):
    h = np.arange(RET_HEADS, dtype=np.float64)
    log_g = np.log1p(-np.exp2(-5.0 - h))
    idx = np.arange(L, dtype=np.float64)
    dist = np.abs(idx[:, None] - idx[None, :])
    visible = (idx[None, :] // CHUNK) <= (idx[:, None] // CHUNK)
    dmat = np.where(visible[None], np.exp(log_g[:, None, None] * dist[None]), 0.0)
    dmat = np.concatenate([dmat[0::2], dmat[1::2]], axis=-1)
    xi = np.exp(log_g[None, :] * (idx[:, None] + 1.0))
    zeta = np.exp(log_g[None, :] * (L - 1.0 - idx[:, None]))
    gl = np.exp(log_g * L)
    rep = lambda t: np.repeat(t, RET_DV, axis=-1)
    inv_freq = ROPE_BASE ** (-np.arange(RET_DK // 2, dtype=np.float32) / np.float32(RET_DK // 2))
    invf = np.tile(inv_freq.astype(np.float32), LANES // (RET_DK // 2))[None, :]
    rel = np.arange(step_rows, dtype=np.float64)[:, None] * invf.astype(np.float64)
    f = lambda t: jnp.asarray(t, dtype=_f32)
    scale = RET_DK ** -0.5
    return (f(dmat * scale), f(rep(xi) * scale), f(rep(zeta)), f(rep(gl[None, :])), f(invf),
            f(np.cos(rel)), f(np.sin(rel)))


def _conv_dft_tables():
    n = np.arange(DFT_N, dtype=np.float64)
    f = np.arange(N_FREQ, dtype=np.float64)[:, None]
    w = 2.0 * np.pi / DFT_N
    fwd = np.zeros((2 * FREQ_PAD, MXU_TILE))
    fwd[:N_FREQ, :DFT_N] = np.cos(w * f * n[None, :])
    fwd[FREQ_PAD:FREQ_PAD + N_FREQ, :DFT_N] = -np.sin(w * f * n[None, :])
    delay = (CONV_WIDTH - 1) - np.arange(CONV_WIDTH, dtype=np.float64)[None, :]
    tap = np.zeros((2 * FREQ_PAD, CONV_HALO))
    tap[:N_FREQ, :CONV_WIDTH] = np.cos(w * f * delay)
    tap[FREQ_PAD:FREQ_PAD + N_FREQ, :CONV_WIDTH] = -np.sin(w * f * delay)
    weight = np.full((1, N_FREQ), 2.0)
    weight[0, 0] = weight[0, -1] = 1.0
    out = n[CONV_HALO:, None]
    inv = np.zeros((CONV_SEG, MXU_TILE))
    inv[:, :N_FREQ] = weight * np.cos(w * out * f.T) / DFT_N
    inv[:, FREQ_PAD:FREQ_PAD + N_FREQ] = -weight * np.sin(w * out * f.T) / DFT_N
    return [jnp.asarray(t, dtype=_f32) for t in (fwd, inv, tap)]


def _mixer(x, positions, norm_w, w_in, conv_w, conv_b, ln_w, ln_b, gn_w, w_out, later_weights):
    B, S, D = x.shape
    L = MIX_BLOCK
    n = S // L
    steps = B * n
    R = RET_BLOCK
    dmat, xi, zeta, gl, invf, cost, sint = _retention_tables(R, L)
    fwd, inv, tap = _conv_dft_tables()
    const = lambda shape: pl.BlockSpec(shape, lambda b, j: (0,) * len(shape))
    once = lambda shape: pl.BlockSpec(shape, lambda b, j: (0,) * len(shape), pipeline_mode=pl.Buffered(1))
    row = lambda n: pl.BlockSpec((1, n), lambda b, j: (0, 0))
    slabs = [w.reshape(steps, w.shape[0] // steps, w.shape[1]) for w in later_weights]
    slab_spec = lambda w: pl.BlockSpec((1,) + w.shape[1:], lambda b, j: (b * n + j, 0, 0))
    outs = pl.pallas_call(
        _mixer_kernel,
        grid=(B, n),
        in_specs=[
            pl.BlockSpec((1, L, D), lambda b, j: (b, j, 0)),
            pl.BlockSpec((B, L), lambda b, j: (0, j)),
            row(D),
            once((D, D_IN)),
            const((CONV_WIDTH, CONV_CH)),
            row(CONV_CH), row(CONV_CH), row(CONV_CH), row(RET_V),
            once((D, D)),
            const((RET_HEADS // 2, R, 2 * R)),
            const((R, RET_V)), const((R, RET_V)), row(RET_V), row(LANES),
            const((L, LANES)), const((L, LANES)),
            const(fwd.shape), const(inv.shape), const(tap.shape),
        ] + [slab_spec(w) for w in slabs],
        out_specs=[pl.BlockSpec((1, L, D), lambda b, j: (b, j, 0))] + [slab_spec(w) for w in slabs],
        out_shape=[jax.ShapeDtypeStruct((B, S, D), _f32)] + [jax.ShapeDtypeStruct(w.shape, _bf16) for w in slabs],
        scratch_shapes=[
            pltpu.VMEM(inv.shape, _bf16),
            pltpu.VMEM((2 * FREQ_PAD, CONV_CH), _f32),
            pltpu.VMEM((CONV_HALO, CONV_CH), _f32),
            pltpu.VMEM((L, D), _bf16),
            pltpu.VMEM((CONV_HALO, CONV_CH), _bf16),
            pltpu.VMEM((CONV_HALO + L + CONV_TAIL, CONV_CH), _bf16),
            pltpu.VMEM((L, CONV_CH), _f32),
            pltpu.VMEM((L, RET_QK), _bf16),
            pltpu.VMEM((L // R, RET_QK, R), _bf16),
            pltpu.VMEM((L, RET_V), _bf16),
            pltpu.VMEM((L, RET_V), _bf16),
            pltpu.VMEM((L, RET_V), _f32),
            pltpu.VMEM((RET_QK, RET_V), _f32),
            pltpu.VMEM((RET_QK, RET_V), _bf16),
            pltpu.VMEM((L, D), _bf16),
            pltpu.VMEM(fwd.shape, _bf16),
        ],
        compiler_params=pltpu.CompilerParams(
            dimension_semantics=("arbitrary", "arbitrary"), vmem_limit_bytes=VMEM_LIMIT),
        name="mixer",
    )(x, positions, norm_w.reshape(1, D), w_in, conv_w, conv_b.reshape(1, -1),
      ln_w.reshape(1, -1), ln_b.reshape(1, -1), gn_w.reshape(1, -1), w_out, dmat, xi, zeta, gl, invf, cost, sint,
      fwd, inv, tap, *slabs)
    return outs[0], [o.reshape(w.shape) for o, w in zip(outs[1:], later_weights)]


def _xattn_kernel(h_ref, nw_ref, wq_ref, kt_ref, v_ref, wo_ref, o_ref, qbuf, obuf):
    for m in range(0, h_ref.shape[1], ROW_CHUNK):
        hn = _rms(h_ref[0, m:m + ROW_CHUNK, :], nw_ref[...]).astype(_bf16)
        qbuf[m:m + ROW_CHUNK, :] = _dot(hn, wq_ref[...]).astype(_bf16)
    for i in range(XATTN_HEADS):
        cols = slice(i * XATTN_HEAD_DIM, (i + 1) * XATTN_HEAD_DIM)
        s = _dot(qbuf[:, cols], kt_ref[0, cols, :])
        e = jnp.exp(s - jnp.max(s, axis=-1, keepdims=True))
        o = _dot(e.astype(_bf16), v_ref[0, :, cols])
        obuf[:, cols] = (o * (1.0 / jnp.sum(e, axis=-1, keepdims=True))).astype(_bf16)
    for m in range(0, h_ref.shape[1], XATTN_TAIL_CHUNK):
        rows = slice(m, m + XATTN_TAIL_CHUNK)
        o_ref[0, rows, :] = h_ref[0, rows, :] + _dot(obuf[rows, :], wo_ref[...])


def _xattn(h, norm_w, xq_w, kt, v, xo_w):
    B, S, D = h.shape
    T = XATTN_BLOCK
    return pl.pallas_call(
        _xattn_kernel,
        grid=(B, S // T),
        in_specs=[
            pl.BlockSpec((1, T, D), lambda b, j: (b, j, 0)),
            pl.BlockSpec((1, D), lambda b, j: (0, 0)),
            pl.BlockSpec((D, D), lambda b, j: (0, 0)),
            pl.BlockSpec((1, D, N_MEM), lambda b, j: (b, 0, 0)),
            pl.BlockSpec((1, N_MEM, D), lambda b, j: (b, 0, 0)),
            pl.BlockSpec((D, D), lambda b, j: (0, 0)),
        ],
        out_specs=pl.BlockSpec((1, T, D), lambda b, j: (b, j, 0)),
        out_shape=jax.ShapeDtypeStruct((B, S, D), _f32),
        scratch_shapes=[pltpu.VMEM((T, D), _bf16), pltpu.VMEM((T, D), _bf16)],
        compiler_params=pltpu.CompilerParams(
            dimension_semantics=("arbitrary", "arbitrary"), vmem_limit_bytes=VMEM_LIMIT),
        name="xattn",
    )(h, norm_w.reshape(1, D), xq_w, kt, v, xo_w)


def _mlp_kernel(h_ref, nw_ref, wu_ref, wd_ref, fw_ref, o_ref, abuf):
    for m in range(0, h_ref.shape[0], MLP_ROW_CHUNK):
        hn = _rms(h_ref[m:m + MLP_ROW_CHUNK, :], nw_ref[...]).astype(_bf16)
        for c in range(0, D_FF, FF_CHUNK):
            u = jnp.maximum(_dot(hn, wu_ref[:, c:c + FF_CHUNK]), 0.0)
            abuf[m:m + MLP_ROW_CHUNK, c:c + FF_CHUNK] = (u * u).astype(_bf16)
    for m in range(0, h_ref.shape[0], MLP_TAIL_CHUNK):
        rows = slice(m, m + MLP_TAIL_CHUNK)
        o_ref[rows, :] = _rms(h_ref[rows, :] + _dot(abuf[rows, :], wd_ref[...]), fw_ref[...])


def _mlp(h, norm_w, up_w, down_w, norm_f_w):
    B, S, D = h.shape
    T = TOK_BLOCK
    h2 = h.reshape(B * S, D)
    out = pl.pallas_call(
        _mlp_kernel,
        grid=(B * S // T,),
        in_specs=[
            pl.BlockSpec((T, D), lambda i: (i, 0)),
            pl.BlockSpec((1, D), lambda i: (0, 0)),
            pl.BlockSpec((D, D_FF), lambda i: (0, 0), pipeline_mode=pl.Buffered(1)),
            pl.BlockSpec((D_FF, D), lambda i: (0, 0), pipeline_mode=pl.Buffered(1)),
            pl.BlockSpec((1, D), lambda i: (0, 0)),
        ],
        out_specs=pl.BlockSpec((T, D), lambda i: (i, 0)),
        out_shape=jax.ShapeDtypeStruct((B * S, D), _f32),
        scratch_shapes=[pltpu.VMEM((T, D_FF), _bf16)],
        compiler_params=pltpu.CompilerParams(
            dimension_semantics=("arbitrary",), vmem_limit_bytes=VMEM_LIMIT),
        name="mlp",
    )(h2, norm_w.reshape(1, D), up_w, down_w, norm_f_w.reshape(1, D))
    return out.reshape(B, S, D)


def kernel(x, mem, positions, norm_mix_w, w_in, conv_w, conv_b, conv_ln_w, conv_ln_b, ret_gn_w, w_out,
           norm_xattn_w, norm_mem_w, xq_w, xkv_w, xo_w, norm_mlp_w, mlp_up_w, mlp_down_w, norm_f_w):
    kt, v, w_in_b, w_out_b = _mem_kv(mem, norm_mem_w, xkv_w, w_in, w_out)
    h, (xq_b, xo_b, up_b, down_b) = _mixer(x, positions, norm_mix_w, w_in_b, conv_w, conv_b, conv_ln_w, conv_ln_b,
                                           ret_gn_w, w_out_b, (xq_w, xo_w, mlp_up_w, mlp_down_w))
    h = _xattn(h, norm_xattn_w, xq_b, kt, v, xo_b)
    return _mlp(h, norm_mlp_w, up_b, down_b, norm_f_w)
```

```python
import numpy as np
import jax
import jax.numpy as jnp
from jax import lax
from jax.experimental import pallas as pl
from jax.experimental.pallas import tpu as pltpu

D_MODEL = 1024
CHUNK = 64
CONV_CH = 512
CONV_WIDTH = 31
RET_HEADS = 8
RET_DV = 64
RET_DK = 32
RET_QK = RET_HEADS * RET_DK
RET_V = RET_HEADS * RET_DV
N_MEM = 256
XATTN_HEADS = 4
XATTN_HEAD_DIM = 256
D_FF = 4096
ROPE_BASE = 10000.0
EPS = 1e-6

OFF_A, OFF_B, OFF_Q, OFF_K, OFF_V, OFF_G = 0, 512, 1024, 1280, 1536, 2048
D_IN = 2560

LANES = 128
SUBLANES = 8
MXU_TILE = 256
MIX_BLOCK = 512
STATE_COLS = MXU_TILE
STATE_ROWS = STATE_COLS // RET_DV * RET_DK
RET_BLOCK = 128
CONV_HALO = 32
CONV_SEG = 128
DFT_N = CONV_HALO + CONV_SEG
N_FREQ = DFT_N // 2 + 1
FREQ_PAD = -(-N_FREQ // SUBLANES) * SUBLANES
CONV_TAIL = MXU_TILE - DFT_N
CONV_ROWS = 32
TOK_BLOCK = 1024
XATTN_BLOCK = 2048
ROW_CHUNK = 256
MLP_ROW_CHUNK = 256
XATTN_TAIL_CHUNK = 512
MLP_TAIL_CHUNK = 256
FF_CHUNK = 1024
VMEM_LIMIT = 56 * 1024 * 1024

_f32 = jnp.float32
_bf16 = jnp.bfloat16


def _dot(a, b):
    return jnp.dot(a, b, preferred_element_type=_f32)


def _rms(x, w):
    return x * lax.rsqrt(jnp.mean(x * x, axis=-1, keepdims=True) + EPS) * w


def _sigmoid(x):
    return 1.0 / (1.0 + jnp.exp(-x))


def _swish(x):
    h = 0.5 * x
    return h + h * jnp.tanh(h)


def _cast_columns(src_ref, dst_ref):
    for c in range(0, src_ref.shape[-1], MXU_TILE):
        dst_ref[:, c:c + MXU_TILE] = src_ref[:, c:c + MXU_TILE].astype(_bf16)


def _mem_kv_kernel(mem_ref, nw_ref, wkv_ref, win32_ref, wout32_ref, kt_ref, v_ref, win_ref, wout_ref):
    m = _rms(mem_ref[0], nw_ref[...]).astype(_bf16)
    k = _dot(m, wkv_ref[:, :D_MODEL].astype(_bf16))
    kt_ref[0] = k.T.astype(_bf16)
    v_ref[0] = _dot(m, wkv_ref[:, D_MODEL:].astype(_bf16)).astype(_bf16)
    _cast_columns(win32_ref.at[0], win_ref.at[0])
    _cast_columns(wout32_ref.at[0], wout_ref.at[0])


def _mem_kv(mem, norm_mem_w, xkv_w, w_in, w_out):
    B = mem.shape[0]
    slab = lambda w: w.reshape(B, w.shape[0] // B, w.shape[1])
    slab_spec = lambda w: pl.BlockSpec((1, w.shape[0] // B, w.shape[1]), lambda b: (b, 0, 0))
    kt, v, win_b, wout_b = pl.pallas_call(
        _mem_kv_kernel,
        grid=(B,),
        in_specs=[
            pl.BlockSpec((1, N_MEM, D_MODEL), lambda b: (b, 0, 0)),
            pl.BlockSpec((1, D_MODEL), lambda b: (0, 0)),
            pl.BlockSpec((D_MODEL, 2 * D_MODEL), lambda b: (0, 0)),
            slab_spec(w_in), slab_spec(w_out),
        ],
        out_specs=[
            pl.BlockSpec((1, D_MODEL, N_MEM), lambda b: (b, 0, 0)),
            pl.BlockSpec((1, N_MEM, D_MODEL), lambda b: (b, 0, 0)),
            slab_spec(w_in), slab_spec(w_out),
        ],
        out_shape=[
            jax.ShapeDtypeStruct((B, D_MODEL, N_MEM), _bf16),
            jax.ShapeDtypeStruct((B, N_MEM, D_MODEL), _bf16),
            jax.ShapeDtypeStruct(slab(w_in).shape, _bf16),
            jax.ShapeDtypeStruct(slab(w_out).shape, _bf16),
        ],
        compiler_params=pltpu.CompilerParams(
            dimension_semantics=("arbitrary",), vmem_limit_bytes=VMEM_LIMIT),
        name="mem_kv",
    )(mem, norm_mem_w.reshape(1, D_MODEL), xkv_w, slab(w_in), slab(w_out))
    return kt, v, win_b.reshape(w_in.shape), wout_b.reshape(w_out.shape)


def _group_norm_halves(y, lo):
    inv = 1.0 / RET_DV
    s_lo = jnp.sum(jnp.where(lo, y, 0.0), axis=-1, keepdims=True)
    s_hi = jnp.sum(jnp.where(lo, 0.0, y), axis=-1, keepdims=True)
    d = y - jnp.where(lo, s_lo, s_hi) * inv
    d2 = d * d
    v_lo = jnp.sum(jnp.where(lo, d2, 0.0), axis=-1, keepdims=True)
    v_hi = jnp.sum(jnp.where(lo, 0.0, d2), axis=-1, keepdims=True)
    return d * lax.rsqrt(jnp.where(lo, v_lo, v_hi) * inv + EPS)


N_CAST = 4
CAST_SCALES = (XATTN_HEAD_DIM ** -0.5, 1.0, 1.0, 1.0)


def _mixer_kernel(*refs):
    n_in = 20
    (x_ref, pos_ref, nw_ref, win_ref, cw_ref, cb_ref, lnw_ref, lnb_ref, gnw_ref, wout_ref,
     dmat_ref, xi_ref, zeta_ref, gl_ref, invf_ref, cost_ref, sint_ref, fwd32_ref, inv32_ref, tap_ref) = refs[:n_in]
    cast_src = refs[n_in:n_in + N_CAST]
    o_ref = refs[n_in + N_CAST]
    cast_dst = refs[n_in + 1 + N_CAST:n_in + 1 + 2 * N_CAST]
    (inv_ref, hspec, cwbuf, hbuf, hist, ubuf, ybuf, qbuf, ktbuf, vbuf, vzbuf, gbuf, state, state_b, mbuf,
     fwd_ref) = refs[n_in + 1 + 2 * N_CAST:]
    L = MIX_BLOCK

    @pl.when((pl.program_id(0) == 0) & (pl.program_id(1) == 0))
    def _():
        fwd_ref[...] = fwd32_ref[...].astype(_bf16)
        inv_ref[...] = inv32_ref[...].astype(_bf16)
        ubuf[...] = jnp.zeros(ubuf.shape, _bf16)
        cwbuf[...] = jnp.zeros(cwbuf.shape, _f32)
        cwbuf[0:CONV_WIDTH, :] = cw_ref[...]
        hspec[...] = jnp.dot(tap_ref[...], cwbuf[...], preferred_element_type=_f32, precision=lax.Precision.HIGHEST)

    @pl.when(pl.program_id(1) == 0)
    def _():
        hist[...] = jnp.zeros((CONV_HALO, CONV_CH), _bf16)
        state[...] = jnp.zeros_like(state)
        state_b[...] = jnp.zeros_like(state_b)

    for src, dst, scale in zip(cast_src, cast_dst, CAST_SCALES):
        dst[...] = (src[...] if scale == 1.0 else src[...] * scale).astype(_bf16)

    x = x_ref[0]
    ubuf[0:CONV_HALO, :] = hist[...]

    def glu_rows(row_lo, row_hi):
        for m in range(row_lo, row_hi, ROW_CHUNK):
            hn_m = _rms(x_ref[0, m:m + ROW_CHUNK, :], nw_ref[...]).astype(_bf16)
            hbuf[m:m + ROW_CHUNK, :] = hn_m
            a = _dot(hn_m, win_ref[:, OFF_A:OFF_A + CONV_CH])
            b = _dot(hn_m, win_ref[:, OFF_B:OFF_B + CONV_CH])
            ubuf[CONV_HALO + m:CONV_HALO + m + ROW_CHUNK, :] = (a * _sigmoid(b)).astype(_bf16)

    first_pos = pos_ref[:, 0:1].astype(_f32)
    batch_row = lax.broadcasted_iota(jnp.int32, first_pos.shape, 0)
    ang0 = jnp.sum(jnp.where(batch_row == pl.program_id(0), first_pos, 0.0), axis=0, keepdims=True) * invf_ref[...]
    cos0, sin0 = jnp.cos(ang0), jnp.sin(ang0)
    lane = lax.broadcasted_iota(jnp.int32, (1, RET_QK), 1)
    first_half = lane % RET_DK < RET_DK // 2
    sub_blocks = [slice(r0, r0 + RET_BLOCK) for r0 in range(0, L, RET_BLOCK)]

    def rotary(t, rows):
        cos = cos0 * cost_ref[rows, :] - sin0 * sint_ref[rows, :]
        sin = sin0 * cost_ref[rows, :] + cos0 * sint_ref[rows, :]
        cos2 = jnp.concatenate([cos, cos], axis=-1)
        sin2 = jnp.where(first_half, -1.0, 1.0) * jnp.concatenate([sin, sin], axis=-1)
        partner = jnp.where(first_half, pltpu.roll(t, RET_QK - RET_DK // 2, 1), pltpu.roll(t, RET_DK // 2, 1))
        return t * cos2 + partner * sin2

    def project_rows(row_lo, row_hi):
        hn = hbuf[row_lo:row_hi, :]
        mine = [(i, rows, slice(rows.start - row_lo, rows.stop - row_lo)) for i, rows in enumerate(sub_blocks)
                if row_lo <= rows.start < row_hi]
        q_all = _dot(hn, win_ref[:, OFF_Q:OFF_K])
        for _, rows, local in mine:
            qbuf[rows, :] = rotary(q_all[local, :], rows).astype(_bf16)
        k_all = _dot(hn, win_ref[:, OFF_K:OFF_V])
        for i, rows, local in mine:
            ktbuf[i] = rotary(k_all[local, :], rows).T.astype(_bf16)
        v_all = _dot(hn, win_ref[:, OFF_V:OFF_G])
        vbuf[row_lo:row_hi, :] = v_all.astype(_bf16)
        for _, rows, local in mine:
            vzbuf[rows, :] = (v_all[local, :] * zeta_ref[...]).astype(_bf16)
        gbuf[row_lo:row_hi, :] = _swish(_dot(hn, win_ref[:, OFF_G:D_IN]))

    h_re, h_im = hspec[0:FREQ_PAD, :], hspec[FREQ_PAD:2 * FREQ_PAD, :]

    def conv_rows(row_lo, row_hi):
        seg_rows = range(row_lo, row_hi, CONV_SEG)
        specs = [_dot(fwd_ref[...], ubuf[row0:row0 + MXU_TILE, :]) for row0 in seg_rows]
        dc_row = lax.broadcasted_iota(jnp.int32, (SUBLANES, CONV_CH), 0) == 0
        bias_dc = jnp.where(dc_row, cb_ref[...] * float(DFT_N), 0.0)
        prods = []
        for spec in specs:
            s_re, s_im = spec[0:FREQ_PAD, :], spec[FREQ_PAD:2 * FREQ_PAD, :]
            p_re = s_re * h_re - s_im * h_im
            p_re = jnp.concatenate([p_re[0:SUBLANES, :] + bias_dc, p_re[SUBLANES:, :]], axis=0)
            prods.append(jnp.concatenate([p_re, s_re * h_im + s_im * h_re,
                                          jnp.zeros((MXU_TILE - 2 * FREQ_PAD, CONV_CH), _f32)],
                                         axis=0).astype(_bf16))
        for row0, prod in zip(seg_rows, prods):
            ybuf[row0:row0 + CONV_SEG, :] = _dot(inv_ref[...], prod)
        half_lnw, half_lnb = 0.5 * lnw_ref[...], 0.5 * lnb_ref[...]
        for r in range(row_lo, row_hi, CONV_ROWS):
            acc = ybuf[r:r + CONV_ROWS, :]
            mu = jnp.mean(acc, axis=-1, keepdims=True)
            d = acc - mu
            var = jnp.mean(d * d, axis=-1, keepdims=True)
            half = d * lax.rsqrt(var + EPS) * half_lnw + half_lnb
            mbuf[r:r + CONV_ROWS, 0:CONV_CH] = (half + half * jnp.tanh(half)).astype(_bf16)

    lo = lax.broadcasted_iota(jnp.int32, (1, LANES), 1) < RET_DV
    same_head = (lax.broadcasted_iota(jnp.int32, (STATE_ROWS, STATE_COLS), 0) // RET_DK
                 == lax.broadcasted_iota(jnp.int32, (STATE_ROWS, STATE_COLS), 1) // RET_DV)

    def retention(i):
        rows = sub_blocks[i]
        qb = qbuf[rows, :]
        kt = ktbuf[i]
        vb = vbuf[rows, :]

        yx = _dot(qb, state_b[...]) * xi_ref[...]

        for p in range(RET_HEADS // 2):
            cols = slice(p * LANES, (p + 1) * LANES)
            blank = jnp.zeros((RET_DK, RET_BLOCK), _bf16)
            k_pair = jnp.concatenate(
                [jnp.concatenate([kt[h * RET_DK:(h + 1) * RET_DK] if h == 2 * p else blank,
                                  kt[h * RET_DK:(h + 1) * RET_DK] if h == 2 * p + 1 else blank], axis=1)
                 for h in range(RET_HEADS)], axis=0)
            s = _dot(qb, k_pair)
            vp = vb[:, cols]
            off = jnp.zeros_like(vp)
            v_pair = jnp.concatenate([jnp.where(lo, vp, off), jnp.where(lo, off, vp)], axis=0)
            y = _dot((s * dmat_ref[p]).astype(_bf16), v_pair) + yx[:, cols]
            yn = _group_norm_halves(y, lo) * gnw_ref[:, cols]
            mbuf[rows, CONV_CH + p * LANES:CONV_CH + (p + 1) * LANES] = (gbuf[rows, cols] * yn).astype(_bf16)

        for q in range(RET_QK // STATE_ROWS):
            srows = slice(q * STATE_ROWS, (q + 1) * STATE_ROWS)
            scols = slice(q * STATE_COLS, (q + 1) * STATE_COLS)
            kv = _dot(kt[srows, :], vzbuf[rows, scols])
            new = gl_ref[:, scols] * state[srows, scols] + jnp.where(same_head, kv, 0.0)
            state[srows, scols] = new
            state_b[srows, scols] = new.astype(_bf16)

    half_rows = L // 2
    per_half = len(sub_blocks) // 2
    glu_rows(0, half_rows)
    project_rows(0, half_rows)
    conv_rows(0, half_rows)
    glu_rows(half_rows, L)
    hist[...] = ubuf[L:L + CONV_HALO, :]
    retention(0)
    project_rows(half_rows, L)
    for i in range(1, per_half):
        retention(i)
    conv_rows(half_rows, L)
    for i in range(per_half, len(sub_blocks)):
        retention(i)

    o_ref[0] = x + _dot(mbuf[...], wout_ref[...])


def _retention_tables(L, step_rows):
    h = np.arange(RET_HEADS, dtype=np.float64)
    log_g = np.log1p(-np.exp2(-5.0 - h))
    idx = np.arange(L, dtype=np.float64)
    dist = np.abs(idx[:, None] - idx[None, :])
    visible = (idx[None, :] // CHUNK) <= (idx[:, None] // CHUNK)
    dmat = np.where(visible[None], np.exp(log_g[:, None, None] * dist[None]), 0.0)
    dmat = np.concatenate([dmat[0::2], dmat[1::2]], axis=-1)
    xi = np.exp(log_g[None, :] * (idx[:, None] + 1.0))
    zeta = np.exp(log_g[None, :] * (L - 1.0 - idx[:, None]))
    gl = np.exp(log_g * L)
    rep = lambda t: np.repeat(t, RET_DV, axis=-1)
    inv_freq = ROPE_BASE ** (-np.arange(RET_DK // 2, dtype=np.float32) / np.float32(RET_DK // 2))
    invf = np.tile(inv_freq.astype(np.float32), LANES // (RET_DK // 2))[None, :]
    rel = np.arange(step_rows, dtype=np.float64)[:, None] * invf.astype(np.float64)
    f = lambda t: jnp.asarray(t, dtype=_f32)
    scale = RET_DK ** -0.5
    return (f(dmat * scale), f(rep(xi) * scale), f(rep(zeta)), f(rep(gl[None, :])), f(invf),
            f(np.cos(rel)), f(np.sin(rel)))


def _conv_dft_tables():
    n = np.arange(DFT_N, dtype=np.float64)
    f = np.arange(N_FREQ, dtype=np.float64)[:, None]
    w = 2.0 * np.pi / DFT_N
    fwd = np.zeros((2 * FREQ_PAD, MXU_TILE))
    fwd[:N_FREQ, :DFT_N] = np.cos(w * f * n[None, :])
    fwd[FREQ_PAD:FREQ_PAD + N_FREQ, :DFT_N] = -np.sin(w * f * n[None, :])
    delay = (CONV_WIDTH - 1) - np.arange(CONV_WIDTH, dtype=np.float64)[None, :]
    tap = np.zeros((2 * FREQ_PAD, CONV_HALO))
    tap[:N_FREQ, :CONV_WIDTH] = np.cos(w * f * delay)
    tap[FREQ_PAD:FREQ_PAD + N_FREQ, :CONV_WIDTH] = -np.sin(w * f * delay)
    weight = np.full((1, N_FREQ), 2.0)
    weight[0, 0] = weight[0, -1] = 1.0
    out = n[CONV_HALO:, None]
    inv = np.zeros((CONV_SEG, MXU_TILE))
    inv[:, :N_FREQ] = weight * np.cos(w * out * f.T) / DFT_N
    inv[:, FREQ_PAD:FREQ_PAD + N_FREQ] = -weight * np.sin(w * out * f.T) / DFT_N
    return [jnp.asarray(t, dtype=_f32) for t in (fwd, inv, tap)]


def _mixer(x, positions, norm_w, w_in, conv_w, conv_b, ln_w, ln_b, gn_w, w_out, later_weights):
    B, S, D = x.shape
    L = MIX_BLOCK
    n = S // L
    steps = B * n
    R = RET_BLOCK
    dmat, xi, zeta, gl, invf, cost, sint = _retention_tables(R, L)
    fwd, inv, tap = _conv_dft_tables()
    const = lambda shape: pl.BlockSpec(shape, lambda b, j: (0,) * len(shape))
    once = lambda shape: pl.BlockSpec(shape, lambda b, j: (0,) * len(shape), pipeline_mode=pl.Buffered(1))
    row = lambda n: pl.BlockSpec((1, n), lambda b, j: (0, 0))
    slabs = [w.reshape(steps, w.shape[0] // steps, w.shape[1]) for w in later_weights]
    slab_spec = lambda w: pl.BlockSpec((1,) + w.shape[1:], lambda b, j: (b * n + j, 0, 0))
    outs = pl.pallas_call(
        _mixer_kernel,
        grid=(B, n),
        in_specs=[
            pl.BlockSpec((1, L, D), lambda b, j: (b, j, 0)),
            pl.BlockSpec((B, L), lambda b, j: (0, j)),
            row(D),
            once((D, D_IN)),
            const((CONV_WIDTH, CONV_CH)),
            row(CONV_CH), row(CONV_CH), row(CONV_CH), row(RET_V),
            once((D, D)),
            const((RET_HEADS // 2, R, 2 * R)),
            const((R, RET_V)), const((R, RET_V)), row(RET_V), row(LANES),
            const((L, LANES)), const((L, LANES)),
            const(fwd.shape), const(inv.shape), const(tap.shape),
        ] + [slab_spec(w) for w in slabs],
        out_specs=[pl.BlockSpec((1, L, D), lambda b, j: (b, j, 0))] + [slab_spec(w) for w in slabs],
        out_shape=[jax.ShapeDtypeStruct((B, S, D), _f32)] + [jax.ShapeDtypeStruct(w.shape, _bf16) for w in slabs],
        scratch_shapes=[
            pltpu.VMEM(inv.shape, _bf16),
            pltpu.VMEM((2 * FREQ_PAD, CONV_CH), _f32),
            pltpu.VMEM((CONV_HALO, CONV_CH), _f32),
            pltpu.VMEM((L, D), _bf16),
            pltpu.VMEM((CONV_HALO, CONV_CH), _bf16),
            pltpu.VMEM((CONV_HALO + L + CONV_TAIL, CONV_CH), _bf16),
            pltpu.VMEM((L, CONV_CH), _f32),
            pltpu.VMEM((L, RET_QK), _bf16),
            pltpu.VMEM((L // R, RET_QK, R), _bf16),
            pltpu.VMEM((L, RET_V), _bf16),
            pltpu.VMEM((L, RET_V), _bf16),
            pltpu.VMEM((L, RET_V), _f32),
            pltpu.VMEM((RET_QK, RET_V), _f32),
            pltpu.VMEM((RET_QK, RET_V), _bf16),
            pltpu.VMEM((L, D), _bf16),
            pltpu.VMEM(fwd.shape, _bf16),
        ],
        compiler_params=pltpu.CompilerParams(
            dimension_semantics=("arbitrary", "arbitrary"), vmem_limit_bytes=VMEM_LIMIT),
        name="mixer",
    )(x, positions, norm_w.reshape(1, D), w_in, conv_w, conv_b.reshape(1, -1),
      ln_w.reshape(1, -1), ln_b.reshape(1, -1), gn_w.reshape(1, -1), w_out, dmat, xi, zeta, gl, invf, cost, sint,
      fwd, inv, tap, *slabs)
    return outs[0], [o.reshape(w.shape) for o, w in zip(outs[1:], later_weights)]


def _xattn_kernel(h_ref, nw_ref, wq_ref, kt_ref, v_ref, wo_ref, o_ref, qbuf, obuf):
    for m in range(0, h_ref.shape[1], ROW_CHUNK):
        hn = _rms(h_ref[0, m:m + ROW_CHUNK, :], nw_ref[...]).astype(_bf16)
        qbuf[m:m + ROW_CHUNK, :] = _dot(hn, wq_ref[...]).astype(_bf16)
    for i in range(XATTN_HEADS):
        cols = slice(i * XATTN_HEAD_DIM, (i + 1) * XATTN_HEAD_DIM)
        s = _dot(qbuf[:, cols], kt_ref[0, cols, :])
        e = jnp.exp(s - jnp.max(s, axis=-1, keepdims=True))
        o = _dot(e.astype(_bf16), v_ref[0, :, cols])
        obuf[:, cols] = (o * (1.0 / jnp.sum(e, axis=-1, keepdims=True))).astype(_bf16)
    for m in range(0, h_ref.shape[1], XATTN_TAIL_CHUNK):
        rows = slice(m, m + XATTN_TAIL_CHUNK)
        o_ref[0, rows, :] = h_ref[0, rows, :] + _dot(obuf[rows, :], wo_ref[...])


def _xattn(h, norm_w, xq_w, kt, v, xo_w):
    B, S, D = h.shape
    T = XATTN_BLOCK
    return pl.pallas_call(
        _xattn_kernel,
        grid=(B, S // T),
        in_specs=[
            pl.BlockSpec((1, T, D), lambda b, j: (b, j, 0)),
            pl.BlockSpec((1, D), lambda b, j: (0, 0)),
            pl.BlockSpec((D, D), lambda b, j: (0, 0)),
            pl.BlockSpec((1, D, N_MEM), lambda b, j: (b, 0, 0)),
            pl.BlockSpec((1, N_MEM, D), lambda b, j: (b, 0, 0)),
            pl.BlockSpec((D, D), lambda b, j: (0, 0)),
        ],
        out_specs=pl.BlockSpec((1, T, D), lambda b, j: (b, j, 0)),
        out_shape=jax.ShapeDtypeStruct((B, S, D), _f32),
        scratch_shapes=[pltpu.VMEM((T, D), _bf16), pltpu.VMEM((T, D), _bf16)],
        compiler_params=pltpu.CompilerParams(
            dimension_semantics=("arbitrary", "arbitrary"), vmem_limit_bytes=VMEM_LIMIT),
        name="xattn",
    )(h, norm_w.reshape(1, D), xq_w, kt, v, xo_w)


def _mlp_kernel(h_ref, nw_ref, wu_ref, wd_ref, fw_ref, o_ref, abuf):
    for m in range(0, h_ref.shape[0], MLP_ROW_CHUNK):
        hn = _rms(h_ref[m:m + MLP_ROW_CHUNK, :], nw_ref[...]).astype(_bf16)
        for c in range(0, D_FF, FF_CHUNK):
            u = jnp.maximum(_dot(hn, wu_ref[:, c:c + FF_CHUNK]), 0.0)
            abuf[m:m + MLP_ROW_CHUNK, c:c + FF_CHUNK] = (u * u).astype(_bf16)
    for m in range(0, h_ref.shape[0], MLP_TAIL_CHUNK):
        rows = slice(m, m + MLP_TAIL_CHUNK)
        o_ref[rows, :] = _rms(h_ref[rows, :] + _dot(abuf[rows, :], wd_ref[...]), fw_ref[...])


def _mlp(h, norm_w, up_w, down_w, norm_f_w):
    B, S, D = h.shape
    T = TOK_BLOCK
    h2 = h.reshape(B * S, D)
    out = pl.pallas_call(
        _mlp_kernel,
        grid=(B * S // T,),
        in_specs=[
            pl.BlockSpec((T, D), lambda i: (i, 0)),
            pl.BlockSpec((1, D), lambda i: (0, 0)),
            pl.BlockSpec((D, D_FF), lambda i: (0, 0), pipeline_mode=pl.Buffered(1)),
            pl.BlockSpec((D_FF, D), lambda i: (0, 0), pipeline_mode=pl.Buffered(1)),
            pl.BlockSpec((1, D), lambda i: (0, 0)),
        ],
        out_specs=pl.BlockSpec((T, D), lambda i: (i, 0)),
        out_shape=jax.ShapeDtypeStruct((B * S, D), _f32),
        scratch_shapes=[pltpu.VMEM((T, D_FF), _bf16)],
        compiler_params=pltpu.CompilerParams(
            dimension_semantics=("arbitrary",), vmem_limit_bytes=VMEM_LIMIT),
        name="mlp",
    )(h2, norm_w.reshape(1, D), up_w, down_w, norm_f_w.reshape(1, D))
    return out.reshape(B, S, D)


def kernel(x, mem, positions, norm_mix_w, w_in, conv_w, conv_b, conv_ln_w, conv_ln_b, ret_gn_w, w_out,
           norm_xattn_w, norm_mem_w, xq_w, xkv_w, xo_w, norm_mlp_w, mlp_up_w, mlp_down_w, norm_f_w):
    kt, v, w_in_b, w_out_b = _mem_kv(mem, norm_mem_w, xkv_w, w_in, w_out)
    h, (xq_b, xo_b, up_b, down_b) = _mixer(x, positions, norm_mix_w, w_in_b, conv_w, conv_b, conv_ln_w, conv_ln_b,
                                           ret_gn_w, w_out_b, (xq_w, xo_w, mlp_up_w, mlp_down_w))
    h = _xattn(h, norm_xattn_w, xq_b, kt, v, xo_b)
    return _mlp(h, norm_mlp_w, up_b, down_b, norm_f_w)
```

```python
import numpy as np
import jax
import jax.numpy as jnp
from jax import lax
from jax.experimental import pallas as pl
from jax.experimental.pallas import tpu as pltpu

D_MODEL = 1024
CHUNK = 64
CONV_CH = 512
CONV_WIDTH = 31
RET_HEADS = 8
RET_DV = 64
RET_DK = 32
RET_QK = RET_HEADS * RET_DK
RET_V = RET_HEADS * RET_DV
N_MEM = 256
XATTN_HEADS = 4
XATTN_HEAD_DIM = 256
D_FF = 4096
ROPE_BASE = 10000.0
EPS = 1e-6

OFF_A, OFF_B, OFF_Q, OFF_K, OFF_V, OFF_G = 0, 512, 1024, 1280, 1536, 2048
D_IN = 2560

LANES = 128
SUBLANES = 8
MXU_TILE = 256
MIX_BLOCK = 512
STATE_COLS = MXU_TILE
STATE_ROWS = STATE_COLS // RET_DV * RET_DK
RET_BLOCK = 128
CONV_HALO = 32
CONV_SEG = 128
DFT_N = CONV_HALO + CONV_SEG
N_FREQ = DFT_N // 2 + 1
FREQ_PAD = -(-N_FREQ // SUBLANES) * SUBLANES
CONV_TAIL = MXU_TILE - DFT_N
CONV_ROWS = 32
TOK_BLOCK = 1024
XATTN_BLOCK = 2048
ROW_CHUNK = 256
MLP_ROW_CHUNK = 256
XATTN_TAIL_CHUNK = 512
MLP_TAIL_CHUNK = 256
FF_CHUNK = 1024
VMEM_LIMIT = 56 * 1024 * 1024

_f32 = jnp.float32
_bf16 = jnp.bfloat16


def _dot(a, b):
    return jnp.dot(a, b, preferred_element_type=_f32)


def _rms(x, w):
    return x * lax.rsqrt(jnp.mean(x * x, axis=-1, keepdims=True) + EPS) * w


def _sigmoid(x):
    return 1.0 / (1.0 + jnp.exp(-x))


def _swish(x):
    h = 0.5 * x
    return h + h * jnp.tanh(h)


def _cast_columns(src_ref, dst_ref):
    for c in range(0, src_ref.shape[-1], MXU_TILE):
        dst_ref[:, c:c + MXU_TILE] = src_ref[:, c:c + MXU_TILE].astype(_bf16)


def _mem_kv_kernel(mem_ref, nw_ref, wkv_ref, win32_ref, wout32_ref, kt_ref, v_ref, win_ref, wout_ref):
    m = _rms(mem_ref[0], nw_ref[...]).astype(_bf16)
    k = _dot(m, wkv_ref[:, :D_MODEL].astype(_bf16))
    kt_ref[0] = k.T.astype(_bf16)
    v_ref[0] = _dot(m, wkv_ref[:, D_MODEL:].astype(_bf16)).astype(_bf16)
    _cast_columns(win32_ref.at[0], win_ref.at[0])
    _cast_columns(wout32_ref.at[0], wout_ref.at[0])


def _mem_kv(mem, norm_mem_w, xkv_w, w_in, w_out):
    B = mem.shape[0]
    slab = lambda w: w.reshape(B, w.shape[0] // B, w.shape[1])
    slab_spec = lambda w: pl.BlockSpec((1, w.shape[0] // B, w.shape[1]), lambda b: (b, 0, 0))
    kt, v, win_b, wout_b = pl.pallas_call(
        _mem_kv_kernel,
        grid=(B,),
        in_specs=[
            pl.BlockSpec((1, N_MEM, D_MODEL), lambda b: (b, 0, 0)),
            pl.BlockSpec((1, D_MODEL), lambda b: (0, 0)),
            pl.BlockSpec((D_MODEL, 2 * D_MODEL), lambda b: (0, 0)),
            slab_spec(w_in), slab_spec(w_out),
        ],
        out_specs=[
            pl.BlockSpec((1, D_MODEL, N_MEM), lambda b: (b, 0, 0)),
            pl.BlockSpec((1, N_MEM, D_MODEL), lambda b: (b, 0, 0)),
            slab_spec(w_in), slab_spec(w_out),
        ],
        out_shape=[
            jax.ShapeDtypeStruct((B, D_MODEL, N_MEM), _bf16),
            jax.ShapeDtypeStruct((B, N_MEM, D_MODEL), _bf16),
            jax.ShapeDtypeStruct(slab(w_in).shape, _bf16),
            jax.ShapeDtypeStruct(slab(w_out).shape, _bf16),
        ],
        compiler_params=pltpu.CompilerParams(
            dimension_semantics=("arbitrary",), vmem_limit_bytes=VMEM_LIMIT),
        name="mem_kv",
    )(mem, norm_mem_w.reshape(1, D_MODEL), xkv_w, slab(w_in), slab(w_out))
    return kt, v, win_b.reshape(w_in.shape), wout_b.reshape(w_out.shape)


def _center_halves(y, lo):
    s_lo = jnp.sum(jnp.where(lo, y, 0.0), axis=-1, keepdims=True)
    s_hi = jnp.sum(jnp.where(lo, 0.0, y), axis=-1, keepdims=True)
    return y - jnp.where(lo, s_lo, s_hi) * (1.0 / RET_DV)


def _scale_halves(d, lo):
    d2 = d * d
    v_lo = jnp.sum(jnp.where(lo, d2, 0.0), axis=-1, keepdims=True)
    v_hi = jnp.sum(jnp.where(lo, 0.0, d2), axis=-1, keepdims=True)
    return d * lax.rsqrt(jnp.where(lo, v_lo, v_hi) * (1.0 / RET_DV) + EPS)


N_CAST = 4
CAST_SCALES = (XATTN_HEAD_DIM ** -0.5, 1.0, 1.0, 1.0)


def _mixer_kernel(*refs):
    n_in = 20
    (x_ref, pos_ref, nw_ref, win_ref, cw_ref, cb_ref, lnw_ref, lnb_ref, gnw_ref, wout_ref,
     dmat_ref, xi_ref, zeta_ref, gl_ref, invf_ref, cost_ref, sint_ref, fwd32_ref, inv32_ref, tap_ref) = refs[:n_in]
    cast_src = refs[n_in:n_in + N_CAST]
    o_ref = refs[n_in + N_CAST]
    cast_dst = refs[n_in + 1 + N_CAST:n_in + 1 + 2 * N_CAST]
    (inv_ref, hspec, cwbuf, hbuf, hist, ubuf, ybuf, qbuf, ktbuf, vbuf, vzbuf, gbuf, state, state_b, mbuf,
     fwd_ref) = refs[n_in + 1 + 2 * N_CAST:]
    L = MIX_BLOCK

    @pl.when((pl.program_id(0) == 0) & (pl.program_id(1) == 0))
    def _():
        fwd_ref[...] = fwd32_ref[...].astype(_bf16)
        inv_ref[...] = inv32_ref[...].astype(_bf16)
        ubuf[...] = jnp.zeros(ubuf.shape, _bf16)
        cwbuf[...] = jnp.zeros(cwbuf.shape, _f32)
        cwbuf[0:CONV_WIDTH, :] = cw_ref[...]
        hspec[...] = jnp.dot(tap_ref[...], cwbuf[...], preferred_element_type=_f32, precision=lax.Precision.HIGHEST)

    @pl.when(pl.program_id(1) == 0)
    def _():
        hist[...] = jnp.zeros((CONV_HALO, CONV_CH), _bf16)
        state[...] = jnp.zeros_like(state)
        state_b[...] = jnp.zeros_like(state_b)

    for src, dst, scale in zip(cast_src, cast_dst, CAST_SCALES):
        dst[...] = (src[...] if scale == 1.0 else src[...] * scale).astype(_bf16)

    x = x_ref[0]
    ubuf[0:CONV_HALO, :] = hist[...]

    def glu_rows(row_lo, row_hi):
        for m in range(row_lo, row_hi, ROW_CHUNK):
            hn_m = _rms(x_ref[0, m:m + ROW_CHUNK, :], nw_ref[...]).astype(_bf16)
            hbuf[m:m + ROW_CHUNK, :] = hn_m
            a = _dot(hn_m, win_ref[:, OFF_A:OFF_A + CONV_CH])
            b = _dot(hn_m, win_ref[:, OFF_B:OFF_B + CONV_CH])
            ubuf[CONV_HALO + m:CONV_HALO + m + ROW_CHUNK, :] = (a * _sigmoid(b)).astype(_bf16)

    first_pos = pos_ref[:, 0:1].astype(_f32)
    batch_row = lax.broadcasted_iota(jnp.int32, first_pos.shape, 0)
    ang0 = jnp.sum(jnp.where(batch_row == pl.program_id(0), first_pos, 0.0), axis=0, keepdims=True) * invf_ref[...]
    cos0, sin0 = jnp.cos(ang0), jnp.sin(ang0)
    lane = lax.broadcasted_iota(jnp.int32, (1, RET_QK), 1)
    first_half = lane % RET_DK < RET_DK // 2
    sub_blocks = [slice(r0, r0 + RET_BLOCK) for r0 in range(0, L, RET_BLOCK)]

    def rotary(t, rows):
        cos = cos0 * cost_ref[rows, :] - sin0 * sint_ref[rows, :]
        sin = sin0 * cost_ref[rows, :] + cos0 * sint_ref[rows, :]
        cos2 = jnp.concatenate([cos, cos], axis=-1)
        sin2 = jnp.where(first_half, -1.0, 1.0) * jnp.concatenate([sin, sin], axis=-1)
        partner = jnp.where(first_half, pltpu.roll(t, RET_QK - RET_DK // 2, 1), pltpu.roll(t, RET_DK // 2, 1))
        return t * cos2 + partner * sin2

    def project_rows(row_lo, row_hi):
        hn = hbuf[row_lo:row_hi, :]
        mine = [(i, rows, slice(rows.start - row_lo, rows.stop - row_lo)) for i, rows in enumerate(sub_blocks)
                if row_lo <= rows.start < row_hi]
        q_all = _dot(hn, win_ref[:, OFF_Q:OFF_K])
        for _, rows, local in mine:
            qbuf[rows, :] = rotary(q_all[local, :], rows).astype(_bf16)
        k_all = _dot(hn, win_ref[:, OFF_K:OFF_V])
        for i, rows, local in mine:
            ktbuf[i] = rotary(k_all[local, :], rows).T.astype(_bf16)
        v_raw = _dot(hn, win_ref[:, OFF_V:OFF_G])
        v_all = jnp.concatenate([_center_halves(v_raw[:, c:c + LANES], lo) for c in range(0, RET_V, LANES)], axis=-1)
        vbuf[row_lo:row_hi, :] = v_all.astype(_bf16)
        for _, rows, local in mine:
            vzbuf[rows, :] = (v_all[local, :] * zeta_ref[...]).astype(_bf16)
        gbuf[row_lo:row_hi, :] = _swish(_dot(hn, win_ref[:, OFF_G:D_IN]))

    h_re, h_im = hspec[0:FREQ_PAD, :], hspec[FREQ_PAD:2 * FREQ_PAD, :]

    def conv_rows(row_lo, row_hi):
        seg_rows = range(row_lo, row_hi, CONV_SEG)
        specs = [_dot(fwd_ref[...], ubuf[row0:row0 + MXU_TILE, :]) for row0 in seg_rows]
        dc_row = lax.broadcasted_iota(jnp.int32, (SUBLANES, CONV_CH), 0) == 0
        bias_dc = jnp.where(dc_row, cb_ref[...] * float(DFT_N), 0.0)
        prods = []
        for spec in specs:
            s_re, s_im = spec[0:FREQ_PAD, :], spec[FREQ_PAD:2 * FREQ_PAD, :]
            p_re = s_re * h_re - s_im * h_im
            p_re = jnp.concatenate([p_re[0:SUBLANES, :] + bias_dc, p_re[SUBLANES:, :]], axis=0)
            prods.append(jnp.concatenate([p_re, s_re * h_im + s_im * h_re,
                                          jnp.zeros((MXU_TILE - 2 * FREQ_PAD, CONV_CH), _f32)],
                                         axis=0).astype(_bf16))
        for row0, prod in zip(seg_rows, prods):
            ybuf[row0:row0 + CONV_SEG, :] = _dot(inv_ref[...], prod)
        half_lnw, half_lnb = 0.5 * lnw_ref[...], 0.5 * lnb_ref[...]
        for r in range(row_lo, row_hi, CONV_ROWS):
            acc = ybuf[r:r + CONV_ROWS, :]
            mu = jnp.mean(acc, axis=-1, keepdims=True)
            d = acc - mu
            var = jnp.mean(d * d, axis=-1, keepdims=True)
            half = d * lax.rsqrt(var + EPS) * half_lnw + half_lnb
            mbuf[r:r + CONV_ROWS, 0:CONV_CH] = (half + half * jnp.tanh(half)).astype(_bf16)

    lo = lax.broadcasted_iota(jnp.int32, (1, LANES), 1) < RET_DV
    same_head = (lax.broadcasted_iota(jnp.int32, (STATE_ROWS, STATE_COLS), 0) // RET_DK
                 == lax.broadcasted_iota(jnp.int32, (STATE_ROWS, STATE_COLS), 1) // RET_DV)

    def retention(i):
        rows = sub_blocks[i]
        qb = qbuf[rows, :]
        kt = ktbuf[i]
        vb = vbuf[rows, :]

        yx = _dot(qb, state_b[...]) * xi_ref[...]

        for p in range(RET_HEADS // 2):
            cols = slice(p * LANES, (p + 1) * LANES)
            blank = jnp.zeros((RET_DK, RET_BLOCK), _bf16)
            k_pair = jnp.concatenate(
                [jnp.concatenate([kt[h * RET_DK:(h + 1) * RET_DK] if h == 2 * p else blank,
                                  kt[h * RET_DK:(h + 1) * RET_DK] if h == 2 * p + 1 else blank], axis=1)
                 for h in range(RET_HEADS)], axis=0)
            s = _dot(qb, k_pair)
            vp = vb[:, cols]
            off = jnp.zeros_like(vp)
            v_pair = jnp.concatenate([jnp.where(lo, vp, off), jnp.where(lo, off, vp)], axis=0)
            y = _dot((s * dmat_ref[p]).astype(_bf16), v_pair) + yx[:, cols]
            yn = _scale_halves(y, lo) * gnw_ref[:, cols]
            mbuf[rows, CONV_CH + p * LANES:CONV_CH + (p + 1) * LANES] = (gbuf[rows, cols] * yn).astype(_bf16)

        for q in range(RET_QK // STATE_ROWS):
            srows = slice(q * STATE_ROWS, (q + 1) * STATE_ROWS)
            scols = slice(q * STATE_COLS, (q + 1) * STATE_COLS)
            kv = _dot(kt[srows, :], vzbuf[rows, scols])
            new = gl_ref[:, scols] * state[srows, scols] + jnp.where(same_head, kv, 0.0)
            state[srows, scols] = new
            state_b[srows, scols] = new.astype(_bf16)

    half_rows = L // 2
    per_half = len(sub_blocks) // 2
    glu_rows(0, half_rows)
    project_rows(0, half_rows)
    conv_rows(0, half_rows)
    glu_rows(half_rows, L)
    hist[...] = ubuf[L:L + CONV_HALO, :]
    retention(0)
    project_rows(half_rows, L)
    for i in range(1, per_half):
        retention(i)
    conv_rows(half_rows, L)
    for i in range(per_half, len(sub_blocks)):
        retention(i)

    o_ref[0] = x + _dot(mbuf[...], wout_ref[...])


def _retention_tables(L, step_rows):
    h = np.arange(RET_HEADS, dtype=np.float64)
    log_g = np.log1p(-np.exp2(-5.0 - h))
    idx = np.arange(L, dtype=np.float64)
    dist = np.abs(idx[:, None] - idx[None, :])
    visible = (idx[None, :] // CHUNK) <= (idx[:, None] // CHUNK)
    dmat = np.where(visible[None], np.exp(log_g[:, None, None] * dist[None]), 0.0)
    dmat = np.concatenate([dmat[0::2], dmat[1::2]], axis=-1)
    xi = np.exp(log_g[None, :] * (idx[:, None] + 1.0))
    zeta = np.exp(log_g[None, :] * (L - 1.0 - idx[:, None]))
    gl = np.exp(log_g * L)
    rep = lambda t: np.repeat(t, RET_DV, axis=-1)
    inv_freq = ROPE_BASE ** (-np.arange(RET_DK // 2, dtype=np.float32) / np.float32(RET_DK // 2))
    invf = np.tile(inv_freq.astype(np.float32), LANES // (RET_DK // 2))[None, :]
    rel = np.arange(step_rows, dtype=np.float64)[:, None] * invf.astype(np.float64)
    f = lambda t: jnp.asarray(t, dtype=_f32)
    scale = RET_DK ** -0.5
    return (f(dmat * scale), f(rep(xi) * scale), f(rep(zeta)), f(rep(gl[None, :])), f(invf),
            f(np.cos(rel)), f(np.sin(rel)))


def _conv_dft_tables():
    n = np.arange(DFT_N, dtype=np.float64)
    f = np.arange(N_FREQ, dtype=np.float64)[:, None]
    w = 2.0 * np.pi / DFT_N
    fwd = np.zeros((2 * FREQ_PAD, MXU_TILE))
    fwd[:N_FREQ, :DFT_N] = np.cos(w * f * n[None, :])
    fwd[FREQ_PAD:FREQ_PAD + N_FREQ, :DFT_N] = -np.sin(w * f * n[None, :])
    delay = (CONV_WIDTH - 1) - np.arange(CONV_WIDTH, dtype=np.float64)[None, :]
    tap = np.zeros((2 * FREQ_PAD, CONV_HALO))
    tap[:N_FREQ, :CONV_WIDTH] = np.cos(w * f * delay)
    tap[FREQ_PAD:FREQ_PAD + N_FREQ, :CONV_WIDTH] = -np.sin(w * f * delay)
    weight = np.full((1, N_FREQ), 2.0)
    weight[0, 0] = weight[0, -1] = 1.0
    out = n[CONV_HALO:, None]
    inv = np.zeros((CONV_SEG, MXU_TILE))
    inv[:, :N_FREQ] = weight * np.cos(w * out * f.T) / DFT_N
    inv[:, FREQ_PAD:FREQ_PAD + N_FREQ] = -weight * np.sin(w * out * f.T) / DFT_N
    return [jnp.asarray(t, dtype=_f32) for t in (fwd, inv, tap)]


def _mixer(x, positions, norm_w, w_in, conv_w, conv_b, ln_w, ln_b, gn_w, w_out, later_weights):
    B, S, D = x.shape
    L = MIX_BLOCK
    n = S // L
    steps = B * n
    R = RET_BLOCK
    dmat, xi, zeta, gl, invf, cost, sint = _retention_tables(R, L)
    fwd, inv, tap = _conv_dft_tables()
    const = lambda shape: pl.BlockSpec(shape, lambda b, j: (0,) * len(shape))
    once = lambda shape: pl.BlockSpec(shape, lambda b, j: (0,) * len(shape), pipeline_mode=pl.Buffered(1))
    row = lambda n: pl.BlockSpec((1, n), lambda b, j: (0, 0))
    slabs = [w.reshape(steps, w.shape[0] // steps, w.shape[1]) for w in later_weights]
    slab_spec = lambda w: pl.BlockSpec((1,) + w.shape[1:], lambda b, j: (b * n + j, 0, 0))
    outs = pl.pallas_call(
        _mixer_kernel,
        grid=(B, n),
        in_specs=[
            pl.BlockSpec((1, L, D), lambda b, j: (b, j, 0)),
            pl.BlockSpec((B, L), lambda b, j: (0, j)),
            row(D),
            once((D, D_IN)),
            const((CONV_WIDTH, CONV_CH)),
            row(CONV_CH), row(CONV_CH), row(CONV_CH), row(RET_V),
            once((D, D)),
            const((RET_HEADS // 2, R, 2 * R)),
            const((R, RET_V)), const((R, RET_V)), row(RET_V), row(LANES),
            const((L, LANES)), const((L, LANES)),
            const(fwd.shape), const(inv.shape), const(tap.shape),
        ] + [slab_spec(w) for w in slabs],
        out_specs=[pl.BlockSpec((1, L, D), lambda b, j: (b, j, 0))] + [slab_spec(w) for w in slabs],
        out_shape=[jax.ShapeDtypeStruct((B, S, D), _f32)] + [jax.ShapeDtypeStruct(w.shape, _bf16) for w in slabs],
        scratch_shapes=[
            pltpu.VMEM(inv.shape, _bf16),
            pltpu.VMEM((2 * FREQ_PAD, CONV_CH), _f32),
            pltpu.VMEM((CONV_HALO, CONV_CH), _f32),
            pltpu.VMEM((L, D), _bf16),
            pltpu.VMEM((CONV_HALO, CONV_CH), _bf16),
            pltpu.VMEM((CONV_HALO + L + CONV_TAIL, CONV_CH), _bf16),
            pltpu.VMEM((L, CONV_CH), _f32),
            pltpu.VMEM((L, RET_QK), _bf16),
            pltpu.VMEM((L // R, RET_QK, R), _bf16),
            pltpu.VMEM((L, RET_V), _bf16),
            pltpu.VMEM((L, RET_V), _bf16),
            pltpu.VMEM((L, RET_V), _f32),
            pltpu.VMEM((RET_QK, RET_V), _f32),
            pltpu.VMEM((RET_QK, RET_V), _bf16),
            pltpu.VMEM((L, D), _bf16),
            pltpu.VMEM(fwd.shape, _bf16),
        ],
        compiler_params=pltpu.CompilerParams(
            dimension_semantics=("arbitrary", "arbitrary"), vmem_limit_bytes=VMEM_LIMIT),
        name="mixer",
    )(x, positions, norm_w.reshape(1, D), w_in, conv_w, conv_b.reshape(1, -1),
      ln_w.reshape(1, -1), ln_b.reshape(1, -1), gn_w.reshape(1, -1), w_out, dmat, xi, zeta, gl, invf, cost, sint,
      fwd, inv, tap, *slabs)
    return outs[0], [o.reshape(w.shape) for o, w in zip(outs[1:], later_weights)]


def _xattn_kernel(h_ref, nw_ref, wq_ref, kt_ref, v_ref, wo_ref, o_ref, qbuf, obuf):
    for m in range(0, h_ref.shape[1], ROW_CHUNK):
        hn = _rms(h_ref[0, m:m + ROW_CHUNK, :], nw_ref[...]).astype(_bf16)
        qbuf[m:m + ROW_CHUNK, :] = _dot(hn, wq_ref[...]).astype(_bf16)
    for i in range(XATTN_HEADS):
        cols = slice(i * XATTN_HEAD_DIM, (i + 1) * XATTN_HEAD_DIM)
        s = _dot(qbuf[:, cols], kt_ref[0, cols, :])
        e = jnp.exp(s - jnp.max(s, axis=-1, keepdims=True))
        o = _dot(e.astype(_bf16), v_ref[0, :, cols])
        obuf[:, cols] = (o * (1.0 / jnp.sum(e, axis=-1, keepdims=True))).astype(_bf16)
    for m in range(0, h_ref.shape[1], XATTN_TAIL_CHUNK):
        rows = slice(m, m + XATTN_TAIL_CHUNK)
        o_ref[0, rows, :] = h_ref[0, rows, :] + _dot(obuf[rows, :], wo_ref[...])


def _xattn(h, norm_w, xq_w, kt, v, xo_w):
    B, S, D = h.shape
    T = XATTN_BLOCK
    return pl.pallas_call(
        _xattn_kernel,
        grid=(B, S // T),
        in_specs=[
            pl.BlockSpec((1, T, D), lambda b, j: (b, j, 0)),
            pl.BlockSpec((1, D), lambda b, j: (0, 0)),
            pl.BlockSpec((D, D), lambda b, j: (0, 0)),
            pl.BlockSpec((1, D, N_MEM), lambda b, j: (b, 0, 0)),
            pl.BlockSpec((1, N_MEM, D), lambda b, j: (b, 0, 0)),
            pl.BlockSpec((D, D), lambda b, j: (0, 0)),
        ],
        out_specs=pl.BlockSpec((1, T, D), lambda b, j: (b, j, 0)),
        out_shape=jax.ShapeDtypeStruct((B, S, D), _f32),
        scratch_shapes=[pltpu.VMEM((T, D), _bf16), pltpu.VMEM((T, D), _bf16)],
        compiler_params=pltpu.CompilerParams(
            dimension_semantics=("arbitrary", "arbitrary"), vmem_limit_bytes=VMEM_LIMIT),
        name="xattn",
    )(h, norm_w.reshape(1, D), xq_w, kt, v, xo_w)


def _mlp_kernel(h_ref, nw_ref, wu_ref, wd_ref, fw_ref, o_ref, abuf):
    for m in range(0, h_ref.shape[0], MLP_ROW_CHUNK):
        hn = _rms(h_ref[m:m + MLP_ROW_CHUNK, :], nw_ref[...]).astype(_bf16)
        for c in range(0, D_FF, FF_CHUNK):
            u = jnp.maximum(_dot(hn, wu_ref[:, c:c + FF_CHUNK]), 0.0)
            abuf[m:m + MLP_ROW_CHUNK, c:c + FF_CHUNK] = (u * u).astype(_bf16)
    for m in range(0, h_ref.shape[0], MLP_TAIL_CHUNK):
        rows = slice(m, m + MLP_TAIL_CHUNK)
        o_ref[rows, :] = _rms(h_ref[rows, :] + _dot(abuf[rows, :], wd_ref[...]), fw_ref[...])


def _mlp(h, norm_w, up_w, down_w, norm_f_w):
    B, S, D = h.shape
    T = TOK_BLOCK
    h2 = h.reshape(B * S, D)
    out = pl.pallas_call(
        _mlp_kernel,
        grid=(B * S // T,),
        in_specs=[
            pl.BlockSpec((T, D), lambda i: (i, 0)),
            pl.BlockSpec((1, D), lambda i: (0, 0)),
            pl.BlockSpec((D, D_FF), lambda i: (0, 0), pipeline_mode=pl.Buffered(1)),
            pl.BlockSpec((D_FF, D), lambda i: (0, 0), pipeline_mode=pl.Buffered(1)),
            pl.BlockSpec((1, D), lambda i: (0, 0)),
        ],
        out_specs=pl.BlockSpec((T, D), lambda i: (i, 0)),
        out_shape=jax.ShapeDtypeStruct((B * S, D), _f32),
        scratch_shapes=[pltpu.VMEM((T, D_FF), _bf16)],
        compiler_params=pltpu.CompilerParams(
            dimension_semantics=("arbitrary",), vmem_limit_bytes=VMEM_LIMIT),
        name="mlp",
    )(h2, norm_w.reshape(1, D), up_w, down_w, norm_f_w.reshape(1, D))
    return out.reshape(B, S, D)


def kernel(x, mem, positions, norm_mix_w, w_in, conv_w, conv_b, conv_ln_w, conv_ln_b, ret_gn_w, w_out,
           norm_xattn_w, norm_mem_w, xq_w, xkv_w, xo_w, norm_mlp_w, mlp_up_w, mlp_down_w, norm_f_w):
    kt, v, w_in_b, w_out_b = _mem_kv(mem, norm_mem_w, xkv_w, w_in, w_out)
    h, (xq_b, xo_b, up_b, down_b) = _mixer(x, positions, norm_mix_w, w_in_b, conv_w, conv_b, conv_ln_w, conv_ln_b,
                                           ret_gn_w, w_out_b, (xq_w, xo_w, mlp_up_w, mlp_down_w))
    h = _xattn(h, norm_xattn_w, xq_b, kt, v, xo_b)
    return _mlp(h, norm_mlp_w, up_b, down_b, norm_f_w)
```

```python
import numpy as np
import jax
import jax.numpy as jnp
from jax import lax
from jax.experimental import pallas as pl
from jax.experimental.pallas import tpu as pltpu

D_MODEL = 1024
CHUNK = 64
CONV_CH = 512
CONV_WIDTH = 31
RET_HEADS = 8
RET_DV = 64
RET_DK = 32
RET_QK = RET_HEADS * RET_DK
RET_V = RET_HEADS * RET_DV
N_MEM = 256
XATTN_HEADS = 4
XATTN_HEAD_DIM = 256
D_FF = 4096
ROPE_BASE = 10000.0
EPS = 1e-6

OFF_A, OFF_B, OFF_Q, OFF_K, OFF_V, OFF_G = 0, 512, 1024, 1280, 1536, 2048
D_IN = 2560

LANES = 128
SUBLANES = 8
MXU_TILE = 256
MIX_BLOCK = 512
STATE_COLS = MXU_TILE
STATE_ROWS = STATE_COLS // RET_DV * RET_DK
RET_BLOCK = 128
CONV_HALO = 32
CONV_SEG = 128
DFT_N = CONV_HALO + CONV_SEG
N_FREQ = DFT_N // 2 + 1
FREQ_PAD = -(-N_FREQ // SUBLANES) * SUBLANES
CONV_TAIL = MXU_TILE - DFT_N
CONV_ROWS = 32
TOK_BLOCK = 1024
XATTN_BLOCK = 2048
ROW_CHUNK = 256
MLP_ROW_CHUNK = 256
XATTN_TAIL_CHUNK = 512
MLP_TAIL_CHUNK = 256
FF_CHUNK = 1024
VMEM_LIMIT = 56 * 1024 * 1024

_f32 = jnp.float32
_bf16 = jnp.bfloat16


def _dot(a, b):
    return jnp.dot(a, b, preferred_element_type=_f32)


def _rms(x, w):
    return x * lax.rsqrt(jnp.mean(x * x, axis=-1, keepdims=True) + EPS) * w


def _sigmoid(x):
    return 1.0 / (1.0 + jnp.exp(-x))


def _swish(x):
    h = 0.5 * x
    return h + h * jnp.tanh(h)


def _cast_columns(src_ref, dst_ref, row_scale=None):
    for c in range(0, src_ref.shape[-1], MXU_TILE):
        piece = src_ref[:, c:c + MXU_TILE]
        dst_ref[:, c:c + MXU_TILE] = (piece if row_scale is None else piece * row_scale).astype(_bf16)


def _mem_kv_kernel(mem_ref, nw_ref, wkv_ref, nmix_ref, win32_ref, wout32_ref, kt_ref, v_ref, win_ref, wout_ref):
    m = _rms(mem_ref[0], nw_ref[...]).astype(_bf16)
    k = _dot(m, wkv_ref[:, :D_MODEL].astype(_bf16))
    kt_ref[0] = k.T.astype(_bf16)
    v_ref[0] = _dot(m, wkv_ref[:, D_MODEL:].astype(_bf16)).astype(_bf16)
    gain = jnp.broadcast_to(nmix_ref[...], (LANES, nmix_ref.shape[1])).T
    _cast_columns(win32_ref.at[0], win_ref.at[0], jnp.concatenate([gain] * (MXU_TILE // LANES), axis=-1))
    _cast_columns(wout32_ref.at[0], wout_ref.at[0])


def _mem_kv(mem, norm_mem_w, xkv_w, norm_mix_w, w_in, w_out):
    B = mem.shape[0]
    slab = lambda w: w.reshape(B, w.shape[0] // B, w.shape[1])
    slab_spec = lambda w: pl.BlockSpec((1, w.shape[0] // B, w.shape[1]), lambda b: (b, 0, 0))
    kt, v, win_b, wout_b = pl.pallas_call(
        _mem_kv_kernel,
        grid=(B,),
        in_specs=[
            pl.BlockSpec((1, N_MEM, D_MODEL), lambda b: (b, 0, 0)),
            pl.BlockSpec((1, D_MODEL), lambda b: (0, 0)),
            pl.BlockSpec((D_MODEL, 2 * D_MODEL), lambda b: (0, 0)),
            pl.BlockSpec((1, D_MODEL // B), lambda b: (0, b)),
            slab_spec(w_in), slab_spec(w_out),
        ],
        out_specs=[
            pl.BlockSpec((1, D_MODEL, N_MEM), lambda b: (b, 0, 0)),
            pl.BlockSpec((1, N_MEM, D_MODEL), lambda b: (b, 0, 0)),
            slab_spec(w_in), slab_spec(w_out),
        ],
        out_shape=[
            jax.ShapeDtypeStruct((B, D_MODEL, N_MEM), _bf16),
            jax.ShapeDtypeStruct((B, N_MEM, D_MODEL), _bf16),
            jax.ShapeDtypeStruct(slab(w_in).shape, _bf16),
            jax.ShapeDtypeStruct(slab(w_out).shape, _bf16),
        ],
        compiler_params=pltpu.CompilerParams(
            dimension_semantics=("arbitrary",), vmem_limit_bytes=VMEM_LIMIT),
        name="mem_kv",
    )(mem, norm_mem_w.reshape(1, D_MODEL), xkv_w, norm_mix_w.reshape(1, D_MODEL), slab(w_in), slab(w_out))
    return kt, v, win_b.reshape(w_in.shape), wout_b.reshape(w_out.shape)


def _group_norm_halves(y, lo):
    inv = 1.0 / RET_DV
    s_lo = jnp.sum(jnp.where(lo, y, 0.0), axis=-1, keepdims=True)
    s_hi = jnp.sum(jnp.where(lo, 0.0, y), axis=-1, keepdims=True)
    d = y - jnp.where(lo, s_lo, s_hi) * inv
    d2 = d * d
    v_lo = jnp.sum(jnp.where(lo, d2, 0.0), axis=-1, keepdims=True)
    v_hi = jnp.sum(jnp.where(lo, 0.0, d2), axis=-1, keepdims=True)
    return d * lax.rsqrt(jnp.where(lo, v_lo, v_hi) * inv + EPS)


N_CAST = 4
CAST_SCALES = (XATTN_HEAD_DIM ** -0.5, 1.0, 1.0, 1.0)


def _mixer_kernel(*refs):
    n_in = 19
    (x_ref, pos_ref, win_ref, cw_ref, cb_ref, lnw_ref, lnb_ref, gnw_ref, wout_ref,
     dmat_ref, xi_ref, zeta_ref, gl_ref, invf_ref, cost_ref, sint_ref, fwd32_ref, inv32_ref, tap_ref) = refs[:n_in]
    cast_src = refs[n_in:n_in + N_CAST]
    o_ref = refs[n_in + N_CAST]
    cast_dst = refs[n_in + 1 + N_CAST:n_in + 1 + 2 * N_CAST]
    (inv_ref, hspec, cwbuf, hbuf, hist, ubuf, ybuf, qbuf, ktbuf, vbuf, vzbuf, gbuf, state, state_b, mbuf,
     fwd_ref) = refs[n_in + 1 + 2 * N_CAST:]
    L = MIX_BLOCK

    @pl.when((pl.program_id(0) == 0) & (pl.program_id(1) == 0))
    def _():
        fwd_ref[...] = fwd32_ref[...].astype(_bf16)
        inv_ref[...] = inv32_ref[...].astype(_bf16)
        ubuf[...] = jnp.zeros(ubuf.shape, _bf16)
        cwbuf[...] = jnp.zeros(cwbuf.shape, _f32)
        cwbuf[0:CONV_WIDTH, :] = cw_ref[...]
        hspec[...] = jnp.dot(tap_ref[...], cwbuf[...], preferred_element_type=_f32, precision=lax.Precision.HIGHEST)

    @pl.when(pl.program_id(1) == 0)
    def _():
        hist[...] = jnp.zeros((CONV_HALO, CONV_CH), _bf16)
        state[...] = jnp.zeros_like(state)
        state_b[...] = jnp.zeros_like(state_b)

    for src, dst, scale in zip(cast_src, cast_dst, CAST_SCALES):
        dst[...] = (src[...] if scale == 1.0 else src[...] * scale).astype(_bf16)

    x = x_ref[0]
    ubuf[0:CONV_HALO, :] = hist[...]

    def glu_rows(row_lo, row_hi):
        for m in range(row_lo, row_hi, ROW_CHUNK):
            xm = x_ref[0, m:m + ROW_CHUNK, :]
            hn_m = (xm * lax.rsqrt(jnp.mean(xm * xm, axis=-1, keepdims=True) + EPS)).astype(_bf16)
            hbuf[m:m + ROW_CHUNK, :] = hn_m
            a = _dot(hn_m, win_ref[:, OFF_A:OFF_A + CONV_CH])
            b = _dot(hn_m, win_ref[:, OFF_B:OFF_B + CONV_CH])
            ubuf[CONV_HALO + m:CONV_HALO + m + ROW_CHUNK, :] = (a * _sigmoid(b)).astype(_bf16)

    first_pos = pos_ref[:, 0:1].astype(_f32)
    batch_row = lax.broadcasted_iota(jnp.int32, first_pos.shape, 0)
    ang0 = jnp.sum(jnp.where(batch_row == pl.program_id(0), first_pos, 0.0), axis=0, keepdims=True) * invf_ref[...]
    cos0, sin0 = jnp.cos(ang0), jnp.sin(ang0)
    lane = lax.broadcasted_iota(jnp.int32, (1, RET_QK), 1)
    first_half = lane % RET_DK < RET_DK // 2
    sub_blocks = [slice(r0, r0 + RET_BLOCK) for r0 in range(0, L, RET_BLOCK)]

    def rotary(t, rows):
        cos = cos0 * cost_ref[rows, :] - sin0 * sint_ref[rows, :]
        sin = sin0 * cost_ref[rows, :] + cos0 * sint_ref[rows, :]
        cos2 = jnp.concatenate([cos, cos], axis=-1)
        sin2 = jnp.where(first_half, -1.0, 1.0) * jnp.concatenate([sin, sin], axis=-1)
        partner = jnp.where(first_half, pltpu.roll(t, RET_QK - RET_DK // 2, 1), pltpu.roll(t, RET_DK // 2, 1))
        return t * cos2 + partner * sin2

    def project_rows(row_lo, row_hi):
        hn = hbuf[row_lo:row_hi, :]
        mine = [(i, rows, slice(rows.start - row_lo, rows.stop - row_lo)) for i, rows in enumerate(sub_blocks)
                if row_lo <= rows.start < row_hi]
        q_all = _dot(hn, win_ref[:, OFF_Q:OFF_K])
        for _, rows, local in mine:
            qbuf[rows, :] = rotary(q_all[local, :], rows).astype(_bf16)
        k_all = _dot(hn, win_ref[:, OFF_K:OFF_V])
        for i, rows, local in mine:
            ktbuf[i] = rotary(k_all[local, :], rows).T.astype(_bf16)
        v_all = _dot(hn, win_ref[:, OFF_V:OFF_G])
        vbuf[row_lo:row_hi, :] = v_all.astype(_bf16)
        for _, rows, local in mine:
            vzbuf[rows, :] = (v_all[local, :] * zeta_ref[...]).astype(_bf16)
        gbuf[row_lo:row_hi, :] = _swish(_dot(hn, win_ref[:, OFF_G:D_IN]))

    h_re, h_im = hspec[0:FREQ_PAD, :], hspec[FREQ_PAD:2 * FREQ_PAD, :]

    def conv_rows(row_lo, row_hi):
        seg_rows = range(row_lo, row_hi, CONV_SEG)
        specs = [_dot(fwd_ref[...], ubuf[row0:row0 + MXU_TILE, :]) for row0 in seg_rows]
        dc_row = lax.broadcasted_iota(jnp.int32, (SUBLANES, CONV_CH), 0) == 0
        bias_dc = jnp.where(dc_row, cb_ref[...] * float(DFT_N), 0.0)
        prods = []
        for spec in specs:
            s_re, s_im = spec[0:FREQ_PAD, :], spec[FREQ_PAD:2 * FREQ_PAD, :]
            p_re = s_re * h_re - s_im * h_im
            p_re = jnp.concatenate([p_re[0:SUBLANES, :] + bias_dc, p_re[SUBLANES:, :]], axis=0)
            prods.append(jnp.concatenate([p_re, s_re * h_im + s_im * h_re,
                                          jnp.zeros((MXU_TILE - 2 * FREQ_PAD, CONV_CH), _f32)],
                                         axis=0).astype(_bf16))
        for row0, prod in zip(seg_rows, prods):
            ybuf[row0:row0 + CONV_SEG, :] = _dot(inv_ref[...], prod)
        half_lnw, half_lnb = 0.5 * lnw_ref[...], 0.5 * lnb_ref[...]
        for r in range(row_lo, row_hi, CONV_ROWS):
            acc = ybuf[r:r + CONV_ROWS, :]
            mu = jnp.mean(acc, axis=-1, keepdims=True)
            d = acc - mu
            var = jnp.mean(d * d, axis=-1, keepdims=True)
            half = d * lax.rsqrt(var + EPS) * half_lnw + half_lnb
            mbuf[r:r + CONV_ROWS, 0:CONV_CH] = (half + half * jnp.tanh(half)).astype(_bf16)

    lo = lax.broadcasted_iota(jnp.int32, (1, LANES), 1) < RET_DV
    same_head = (lax.broadcasted_iota(jnp.int32, (STATE_ROWS, STATE_COLS), 0) // RET_DK
                 == lax.broadcasted_iota(jnp.int32, (STATE_ROWS, STATE_COLS), 1) // RET_DV)

    def retention(i):
        rows = sub_blocks[i]
        qb = qbuf[rows, :]
        kt = ktbuf[i]
        vb = vbuf[rows, :]

        yx = _dot(qb, state_b[...]) * xi_ref[...]

        for p in range(RET_HEADS // 2):
            cols = slice(p * LANES, (p + 1) * LANES)
            blank = jnp.zeros((RET_DK, RET_BLOCK), _bf16)
            k_pair = jnp.concatenate(
                [jnp.concatenate([kt[h * RET_DK:(h + 1) * RET_DK] if h == 2 * p else blank,
                                  kt[h * RET_DK:(h + 1) * RET_DK] if h == 2 * p + 1 else blank], axis=1)
                 for h in range(RET_HEADS)], axis=0)
            s = _dot(qb, k_pair)
            vp = vb[:, cols]
            off = jnp.zeros_like(vp)
            v_pair = jnp.concatenate([jnp.where(lo, vp, off), jnp.where(lo, off, vp)], axis=0)
            y = _dot((s * dmat_ref[p]).astype(_bf16), v_pair) + yx[:, cols]
            yn = _group_norm_halves(y, lo) * gnw_ref[:, cols]
            mbuf[rows, CONV_CH + p * LANES:CONV_CH + (p + 1) * LANES] = (gbuf[rows, cols] * yn).astype(_bf16)

        for q in range(RET_QK // STATE_ROWS):
            srows = slice(q * STATE_ROWS, (q + 1) * STATE_ROWS)
            scols = slice(q * STATE_COLS, (q + 1) * STATE_COLS)
            kv = _dot(kt[srows, :], vzbuf[rows, scols])
            new = gl_ref[:, scols] * state[srows, scols] + jnp.where(same_head, kv, 0.0)
            state[srows, scols] = new
            state_b[srows, scols] = new.astype(_bf16)

    half_rows = L // 2
    per_half = len(sub_blocks) // 2
    glu_rows(0, half_rows)
    project_rows(0, half_rows)
    conv_rows(0, half_rows)
    glu_rows(half_rows, L)
    hist[...] = ubuf[L:L + CONV_HALO, :]
    retention(0)
    project_rows(half_rows, L)
    for i in range(1, per_half):
        retention(i)
    conv_rows(half_rows, L)
    for i in range(per_half, len(sub_blocks)):
        retention(i)

    o_ref[0] = x + _dot(mbuf[...], wout_ref[...])


def _retention_tables(L, step_rows):
    h = np.arange(RET_HEADS, dtype=np.float64)
    log_g = np.log1p(-np.exp2(-5.0 - h))
    idx = np.arange(L, dtype=np.float64)
    dist = np.abs(idx[:, None] - idx[None, :])
    visible = (idx[None, :] // CHUNK) <= (idx[:, None] // CHUNK)
    dmat = np.where(visible[None], np.exp(log_g[:, None, None] * dist[None]), 0.0)
    dmat = np.concatenate([dmat[0::2], dmat[1::2]], axis=-1)
    xi = np.exp(log_g[None, :] * (idx[:, None] + 1.0))
    zeta = np.exp(log_g[None, :] * (L - 1.0 - idx[:, None]))
    gl = np.exp(log_g * L)
    rep = lambda t: np.repeat(t, RET_DV, axis=-1)
    inv_freq = ROPE_BASE ** (-np.arange(RET_DK // 2, dtype=np.float32) / np.float32(RET_DK // 2))
    invf = np.tile(inv_freq.astype(np.float32), LANES // (RET_DK // 2))[None, :]
    rel = np.arange(step_rows, dtype=np.float64)[:, None] * invf.astype(np.float64)
    f = lambda t: jnp.asarray(t, dtype=_f32)
    scale = RET_DK ** -0.5
    return (f(dmat * scale), f(rep(xi) * scale), f(rep(zeta)), f(rep(gl[None, :])), f(invf),
            f(np.cos(rel)), f(np.sin(rel)))


def _conv_dft_tables():
    n = np.arange(DFT_N, dtype=np.float64)
    f = np.arange(N_FREQ, dtype=np.float64)[:, None]
    w = 2.0 * np.pi / DFT_N
    fwd = np.zeros((2 * FREQ_PAD, MXU_TILE))
    fwd[:N_FREQ, :DFT_N] = np.cos(w * f * n[None, :])
    fwd[FREQ_PAD:FREQ_PAD + N_FREQ, :DFT_N] = -np.sin(w * f * n[None, :])
    delay = (CONV_WIDTH - 1) - np.arange(CONV_WIDTH, dtype=np.float64)[None, :]
    tap = np.zeros((2 * FREQ_PAD, CONV_HALO))
    tap[:N_FREQ, :CONV_WIDTH] = np.cos(w * f * delay)
    tap[FREQ_PAD:FREQ_PAD + N_FREQ, :CONV_WIDTH] = -np.sin(w * f * delay)
    weight = np.full((1, N_FREQ), 2.0)
    weight[0, 0] = weight[0, -1] = 1.0
    out = n[CONV_HALO:, None]
    inv = np.zeros((CONV_SEG, MXU_TILE))
    inv[:, :N_FREQ] = weight * np.cos(w * out * f.T) / DFT_N
    inv[:, FREQ_PAD:FREQ_PAD + N_FREQ] = -weight * np.sin(w * out * f.T) / DFT_N
    return [jnp.asarray(t, dtype=_f32) for t in (fwd, inv, tap)]


def _mixer(x, positions, w_in, conv_w, conv_b, ln_w, ln_b, gn_w, w_out, later_weights):
    B, S, D = x.shape
    L = MIX_BLOCK
    n = S // L
    steps = B * n
    R = RET_BLOCK
    dmat, xi, zeta, gl, invf, cost, sint = _retention_tables(R, L)
    fwd, inv, tap = _conv_dft_tables()
    const = lambda shape: pl.BlockSpec(shape, lambda b, j: (0,) * len(shape))
    once = lambda shape: pl.BlockSpec(shape, lambda b, j: (0,) * len(shape), pipeline_mode=pl.Buffered(1))
    row = lambda n: pl.BlockSpec((1, n), lambda b, j: (0, 0))
    slabs = [w.reshape(steps, w.shape[0] // steps, w.shape[1]) for w in later_weights]
    slab_spec = lambda w: pl.BlockSpec((1,) + w.shape[1:], lambda b, j: (b * n + j, 0, 0))
    outs = pl.pallas_call(
        _mixer_kernel,
        grid=(B, n),
        in_specs=[
            pl.BlockSpec((1, L, D), lambda b, j: (b, j, 0)),
            pl.BlockSpec((B, L), lambda b, j: (0, j)),
            once((D, D_IN)),
            const((CONV_WIDTH, CONV_CH)),
            row(CONV_CH), row(CONV_CH), row(CONV_CH), row(RET_V),
            once((D, D)),
            const((RET_HEADS // 2, R, 2 * R)),
            const((R, RET_V)), const((R, RET_V)), row(RET_V), row(LANES),
            const((L, LANES)), const((L, LANES)),
            const(fwd.shape), const(inv.shape), const(tap.shape),
        ] + [slab_spec(w) for w in slabs],
        out_specs=[pl.BlockSpec((1, L, D), lambda b, j: (b, j, 0))] + [slab_spec(w) for w in slabs],
        out_shape=[jax.ShapeDtypeStruct((B, S, D), _f32)] + [jax.ShapeDtypeStruct(w.shape, _bf16) for w in slabs],
        scratch_shapes=[
            pltpu.VMEM(inv.shape, _bf16),
            pltpu.VMEM((2 * FREQ_PAD, CONV_CH), _f32),
            pltpu.VMEM((CONV_HALO, CONV_CH), _f32),
            pltpu.VMEM((L, D), _bf16),
            pltpu.VMEM((CONV_HALO, CONV_CH), _bf16),
            pltpu.VMEM((CONV_HALO + L + CONV_TAIL, CONV_CH), _bf16),
            pltpu.VMEM((L, CONV_CH), _f32),
            pltpu.VMEM((L, RET_QK), _bf16),
            pltpu.VMEM((L // R, RET_QK, R), _bf16),
            pltpu.VMEM((L, RET_V), _bf16),
            pltpu.VMEM((L, RET_V), _bf16),
            pltpu.VMEM((L, RET_V), _f32),
            pltpu.VMEM((RET_QK, RET_V), _f32),
            pltpu.VMEM((RET_QK, RET_V), _bf16),
            pltpu.VMEM((L, D), _bf16),
            pltpu.VMEM(fwd.shape, _bf16),
        ],
        compiler_params=pltpu.CompilerParams(
            dimension_semantics=("arbitrary", "arbitrary"), vmem_limit_bytes=VMEM_LIMIT),
        name="mixer",
    )(x, positions, w_in, conv_w, conv_b.reshape(1, -1),
      ln_w.reshape(1, -1), ln_b.reshape(1, -1), gn_w.reshape(1, -1), w_out, dmat, xi, zeta, gl, invf, cost, sint,
      fwd, inv, tap, *slabs)
    return outs[0], [o.reshape(w.shape) for o, w in zip(outs[1:], later_weights)]


def _xattn_kernel(h_ref, nw_ref, wq_ref, kt_ref, v_ref, wo_ref, o_ref, qbuf, obuf):
    for m in range(0, h_ref.shape[1], ROW_CHUNK):
        hn = _rms(h_ref[0, m:m + ROW_CHUNK, :], nw_ref[...]).astype(_bf16)
        qbuf[m:m + ROW_CHUNK, :] = _dot(hn, wq_ref[...]).astype(_bf16)
    for i in range(XATTN_HEADS):
        cols = slice(i * XATTN_HEAD_DIM, (i + 1) * XATTN_HEAD_DIM)
        s = _dot(qbuf[:, cols], kt_ref[0, cols, :])
        e = jnp.exp(s - jnp.max(s, axis=-1, keepdims=True))
        o = _dot(e.astype(_bf16), v_ref[0, :, cols])
        obuf[:, cols] = (o * (1.0 / jnp.sum(e, axis=-1, keepdims=True))).astype(_bf16)
    for m in range(0, h_ref.shape[1], XATTN_TAIL_CHUNK):
        rows = slice(m, m + XATTN_TAIL_CHUNK)
        o_ref[0, rows, :] = h_ref[0, rows, :] + _dot(obuf[rows, :], wo_ref[...])


def _xattn(h, norm_w, xq_w, kt, v, xo_w):
    B, S, D = h.shape
    T = XATTN_BLOCK
    return pl.pallas_call(
        _xattn_kernel,
        grid=(B, S // T),
        in_specs=[
            pl.BlockSpec((1, T, D), lambda b, j: (b, j, 0)),
            pl.BlockSpec((1, D), lambda b, j: (0, 0)),
            pl.BlockSpec((D, D), lambda b, j: (0, 0)),
            pl.BlockSpec((1, D, N_MEM), lambda b, j: (b, 0, 0)),
            pl.BlockSpec((1, N_MEM, D), lambda b, j: (b, 0, 0)),
            pl.BlockSpec((D, D), lambda b, j: (0, 0)),
        ],
        out_specs=pl.BlockSpec((1, T, D), lambda b, j: (b, j, 0)),
        out_shape=jax.ShapeDtypeStruct((B, S, D), _f32),
        scratch_shapes=[pltpu.VMEM((T, D), _bf16), pltpu.VMEM((T, D), _bf16)],
        compiler_params=pltpu.CompilerParams(
            dimension_semantics=("arbitrary", "arbitrary"), vmem_limit_bytes=VMEM_LIMIT),
        name="xattn",
    )(h, norm_w.reshape(1, D), xq_w, kt, v, xo_w)


def _mlp_kernel(h_ref, nw_ref, wu_ref, wd_ref, fw_ref, o_ref, abuf):
    for m in range(0, h_ref.shape[0], MLP_ROW_CHUNK):
        hn = _rms(h_ref[m:m + MLP_ROW_CHUNK, :], nw_ref[...]).astype(_bf16)
        for c in range(0, D_FF, FF_CHUNK):
            u = jnp.maximum(_dot(hn, wu_ref[:, c:c + FF_CHUNK]), 0.0)
            abuf[m:m + MLP_ROW_CHUNK, c:c + FF_CHUNK] = (u * u).astype(_bf16)
    for m in range(0, h_ref.shape[0], MLP_TAIL_CHUNK):
        rows = slice(m, m + MLP_TAIL_CHUNK)
        o_ref[rows, :] = _rms(h_ref[rows, :] + _dot(abuf[rows, :], wd_ref[...]), fw_ref[...])


def _mlp(h, norm_w, up_w, down_w, norm_f_w):
    B, S, D = h.shape
    T = TOK_BLOCK
    h2 = h.reshape(B * S, D)
    out = pl.pallas_call(
        _mlp_kernel,
        grid=(B * S // T,),
        in_specs=[
            pl.BlockSpec((T, D), lambda i: (i, 0)),
            pl.BlockSpec((1, D), lambda i: (0, 0)),
            pl.BlockSpec((D, D_FF), lambda i: (0, 0), pipeline_mode=pl.Buffered(1)),
            pl.BlockSpec((D_FF, D), lambda i: (0, 0), pipeline_mode=pl.Buffered(1)),
            pl.BlockSpec((1, D), lambda i: (0, 0)),
        ],
        out_specs=pl.BlockSpec((T, D), lambda i: (i, 0)),
        out_shape=jax.ShapeDtypeStruct((B * S, D), _f32),
        scratch_shapes=[pltpu.VMEM((T, D_FF), _bf16)],
        compiler_params=pltpu.CompilerParams(
            dimension_semantics=("arbitrary",), vmem_limit_bytes=VMEM_LIMIT),
        name="mlp",
    )(h2, norm_w.reshape(1, D), up_w, down_w, norm_f_w.reshape(1, D))
    return out.reshape(B, S, D)


def kernel(x, mem, positions, norm_mix_w, w_in, conv_w, conv_b, conv_ln_w, conv_ln_b, ret_gn_w, w_out,
           norm_xattn_w, norm_mem_w, xq_w, xkv_w, xo_w, norm_mlp_w, mlp_up_w, mlp_down_w, norm_f_w):
    kt, v, w_in_b, w_out_b = _mem_kv(mem, norm_mem_w, xkv_w, norm_mix_w, w_in, w_out)
    h, (xq_b, xo_b, up_b, down_b) = _mixer(x, positions, w_in_b, conv_w, conv_b, conv_ln_w, conv_ln_b,
                                           ret_gn_w, w_out_b, (xq_w, xo_w, mlp_up_w, mlp_down_w))
    h = _xattn(h, norm_xattn_w, xq_b, kt, v, xo_b)
    return _mlp(h, norm_mlp_w, up_b, down_b, norm_f_w)
```

```python
import numpy as np
import jax
import jax.numpy as jnp
from jax import lax
from jax.experimental import pallas as pl
from jax.experimental.pallas import tpu as pltpu

D_MODEL = 1024
CHUNK = 64
CONV_CH = 512
CONV_WIDTH = 31
RET_HEADS = 8
RET_DV = 64
RET_DK = 32
RET_QK = RET_HEADS * RET_DK
RET_V = RET_HEADS * RET_DV
N_MEM = 256
XATTN_HEADS = 4
XATTN_HEAD_DIM = 256
D_FF = 4096
ROPE_BASE = 10000.0
EPS = 1e-6

OFF_A, OFF_B, OFF_Q, OFF_K, OFF_V, OFF_G = 0, 512, 1024, 1280, 1536, 2048
D_IN = 2560

LANES = 128
SUBLANES = 8
MXU_TILE = 256
MIX_BLOCK = 512
STATE_COLS = MXU_TILE
STATE_ROWS = STATE_COLS // RET_DV * RET_DK
RET_BLOCK = 128
CONV_HALO = 32
CONV_SEG = 128
DFT_N = CONV_HALO + CONV_SEG
N_FREQ = DFT_N // 2 + 1
FREQ_PAD = -(-N_FREQ // SUBLANES) * SUBLANES
CONV_TAIL = MXU_TILE - DFT_N
CONV_ROWS = 32
TOK_BLOCK = 1024
XATTN_BLOCK = 2048
ROW_CHUNK = 256
MLP_ROW_CHUNK = 256
XATTN_TAIL_CHUNK = 512
MLP_TAIL_CHUNK = 256
FF_CHUNK = 1024
VMEM_LIMIT = 56 * 1024 * 1024

_f32 = jnp.float32
_bf16 = jnp.bfloat16


def _dot(a, b):
    return jnp.dot(a, b, preferred_element_type=_f32)


def _rms(x, w):
    return x * lax.rsqrt(jnp.mean(x * x, axis=-1, keepdims=True) + EPS) * w


def _sigmoid(x):
    return 1.0 / (1.0 + jnp.exp(-x))


def _swish(x):
    h = 0.5 * x
    return h + h * jnp.tanh(h)


def _cast_columns(src_ref, dst_ref):
    for c in range(0, src_ref.shape[-1], MXU_TILE):
        dst_ref[:, c:c + MXU_TILE] = src_ref[:, c:c + MXU_TILE].astype(_bf16)


def _mem_kv_kernel(mem_ref, nw_ref, wkv_ref, win32_ref, wout32_ref, kt_ref, v_ref, win_ref, wout_ref):
    m = _rms(mem_ref[0], nw_ref[...]).astype(_bf16)
    k = _dot(m, wkv_ref[:, :D_MODEL].astype(_bf16))
    kt_ref[0] = k.T.astype(_bf16)
    v_ref[0] = _dot(m, wkv_ref[:, D_MODEL:].astype(_bf16)).astype(_bf16)
    _cast_columns(win32_ref.at[0], win_ref.at[0])
    _cast_columns(wout32_ref.at[0], wout_ref.at[0])


def _mem_kv(mem, norm_mem_w, xkv_w, w_in, w_out):
    B = mem.shape[0]
    slab = lambda w: w.reshape(B, w.shape[0] // B, w.shape[1])
    slab_spec = lambda w: pl.BlockSpec((1, w.shape[0] // B, w.shape[1]), lambda b: (b, 0, 0))
    kt, v, win_b, wout_b = pl.pallas_call(
        _mem_kv_kernel,
        grid=(B,),
        in_specs=[
            pl.BlockSpec((1, N_MEM, D_MODEL), lambda b: (b, 0, 0)),
            pl.BlockSpec((1, D_MODEL), lambda b: (0, 0)),
            pl.BlockSpec((D_MODEL, 2 * D_MODEL), lambda b: (0, 0)),
            slab_spec(w_in), slab_spec(w_out),
        ],
        out_specs=[
            pl.BlockSpec((1, D_MODEL, N_MEM), lambda b: (b, 0, 0)),
            pl.BlockSpec((1, N_MEM, D_MODEL), lambda b: (b, 0, 0)),
            slab_spec(w_in), slab_spec(w_out),
        ],
        out_shape=[
            jax.ShapeDtypeStruct((B, D_MODEL, N_MEM), _bf16),
            jax.ShapeDtypeStruct((B, N_MEM, D_MODEL), _bf16),
            jax.ShapeDtypeStruct(slab(w_in).shape, _bf16),
            jax.ShapeDtypeStruct(slab(w_out).shape, _bf16),
        ],
        compiler_params=pltpu.CompilerParams(
            dimension_semantics=("arbitrary",), vmem_limit_bytes=VMEM_LIMIT),
        name="mem_kv",
    )(mem, norm_mem_w.reshape(1, D_MODEL), xkv_w, slab(w_in), slab(w_out))
    return kt, v, win_b.reshape(w_in.shape), wout_b.reshape(w_out.shape)


def _group_norm_halves(y, lo):
    inv = 1.0 / RET_DV
    s_lo = jnp.sum(jnp.where(lo, y, 0.0), axis=-1, keepdims=True)
    s_hi = jnp.sum(jnp.where(lo, 0.0, y), axis=-1, keepdims=True)
    d = y - jnp.where(lo, s_lo, s_hi) * inv
    d2 = d * d
    v_lo = jnp.sum(jnp.where(lo, d2, 0.0), axis=-1, keepdims=True)
    v_hi = jnp.sum(jnp.where(lo, 0.0, d2), axis=-1, keepdims=True)
    return d * lax.rsqrt(jnp.where(lo, v_lo, v_hi) * inv + EPS)


N_CAST = 4
CAST_SCALES = (XATTN_HEAD_DIM ** -0.5, 1.0, 1.0, 1.0)


def _mixer_kernel(*refs):
    n_in = 20
    (x_ref, pos_ref, nw_ref, win_ref, cw_ref, cb_ref, lnw_ref, lnb_ref, gnw_ref, wout_ref,
     dmat_ref, xi_ref, zeta_ref, gl_ref, invf_ref, cost_ref, sint_ref, fwd32_ref, inv32_ref, tap_ref) = refs[:n_in]
    cast_src = refs[n_in:n_in + N_CAST]
    o_ref = refs[n_in + N_CAST]
    cast_dst = refs[n_in + 1 + N_CAST:n_in + 1 + 2 * N_CAST]
    (inv_ref, hspec, cwbuf, hbuf, hist, ubuf, ybuf, qbuf, ktbuf, vbuf, vzbuf, gbuf, state, state_b, mbuf,
     fwd_ref) = refs[n_in + 1 + 2 * N_CAST:]
    L = MIX_BLOCK

    @pl.when((pl.program_id(0) == 0) & (pl.program_id(1) == 0))
    def _():
        fwd_ref[...] = fwd32_ref[...].astype(_bf16)
        inv_ref[...] = inv32_ref[...].astype(_bf16)
        ubuf[...] = jnp.zeros(ubuf.shape, _bf16)
        cwbuf[...] = jnp.zeros(cwbuf.shape, _f32)
        cwbuf[0:CONV_WIDTH, :] = cw_ref[...]
        hspec[...] = jnp.dot(tap_ref[...], cwbuf[...], preferred_element_type=_f32, precision=lax.Precision.HIGHEST)

    @pl.when(pl.program_id(1) == 0)
    def _():
        hist[...] = jnp.zeros((CONV_HALO, CONV_CH), _bf16)
        state[...] = jnp.zeros_like(state)
        state_b[...] = jnp.zeros_like(state_b)

    for src, dst, scale in zip(cast_src, cast_dst, CAST_SCALES):
        dst[...] = (src[...] if scale == 1.0 else src[...] * scale).astype(_bf16)

    x = x_ref[0]
    ubuf[0:CONV_HALO, :] = hist[...]

    def glu_rows(row_lo, row_hi):
        for m in range(row_lo, row_hi, ROW_CHUNK):
            hn_m = _rms(x_ref[0, m:m + ROW_CHUNK, :], nw_ref[...]).astype(_bf16)
            hbuf[m:m + ROW_CHUNK, :] = hn_m
            a = _dot(hn_m, win_ref[:, OFF_A:OFF_A + CONV_CH])
            b = _dot(hn_m, win_ref[:, OFF_B:OFF_B + CONV_CH])
            ubuf[CONV_HALO + m:CONV_HALO + m + ROW_CHUNK, :] = (a * _sigmoid(b)).astype(_bf16)

    first_pos = pos_ref[:, 0:1].astype(_f32)
    batch_row = lax.broadcasted_iota(jnp.int32, first_pos.shape, 0)
    ang0 = jnp.sum(jnp.where(batch_row == pl.program_id(0), first_pos, 0.0), axis=0, keepdims=True) * invf_ref[...]
    cos0, sin0 = jnp.cos(ang0), jnp.sin(ang0)
    lane = lax.broadcasted_iota(jnp.int32, (1, RET_QK), 1)
    first_half = lane % RET_DK < RET_DK // 2
    sub_blocks = [slice(r0, r0 + RET_BLOCK) for r0 in range(0, L, RET_BLOCK)]

    def rotary(t, rows):
        cos = cos0 * cost_ref[rows, :] - sin0 * sint_ref[rows, :]
        sin = sin0 * cost_ref[rows, :] + cos0 * sint_ref[rows, :]
        cos2 = jnp.concatenate([cos, cos], axis=-1)
        sin2 = jnp.where(first_half, -1.0, 1.0) * jnp.concatenate([sin, sin], axis=-1)
        partner = jnp.where(first_half, pltpu.roll(t, RET_QK - RET_DK // 2, 1), pltpu.roll(t, RET_DK // 2, 1))
        return t * cos2 + partner * sin2

    def project_rows(row_lo, row_hi):
        hn = hbuf[row_lo:row_hi, :]
        mine = [(i, rows, slice(rows.start - row_lo, rows.stop - row_lo)) for i, rows in enumerate(sub_blocks)
                if row_lo <= rows.start < row_hi]
        q_all = _dot(hn, win_ref[:, OFF_Q:OFF_K])
        for _, rows, local in mine:
            qbuf[rows, :] = rotary(q_all[local, :], rows).astype(_bf16)
        k_all = _dot(hn, win_ref[:, OFF_K:OFF_V])
        for i, rows, local in mine:
            ktbuf[i] = rotary(k_all[local, :], rows).T.astype(_bf16)
        v_all = _dot(hn, win_ref[:, OFF_V:OFF_G])
        vbuf[row_lo:row_hi, :] = v_all.astype(_bf16)
        for _, rows, local in mine:
            vzbuf[rows, :] = (v_all[local, :] * zeta_ref[...]).astype(_bf16)
        gbuf[row_lo:row_hi, :] = _swish(_dot(hn, win_ref[:, OFF_G:D_IN]))

    h_re, h_im = hspec[0:FREQ_PAD, :], hspec[FREQ_PAD:2 * FREQ_PAD, :]

    def conv_rows(row_lo, row_hi):
        seg_rows = range(row_lo, row_hi, CONV_SEG)
        specs = [_dot(fwd_ref[...], ubuf[row0:row0 + MXU_TILE, :]) for row0 in seg_rows]
        dc_row = lax.broadcasted_iota(jnp.int32, (SUBLANES, CONV_CH), 0) == 0
        bias_dc = jnp.where(dc_row, cb_ref[...] * float(DFT_N), 0.0)
        prods = []
        for spec in specs:
            s_re, s_im = spec[0:FREQ_PAD, :], spec[FREQ_PAD:2 * FREQ_PAD, :]
            p_re = s_re * h_re - s_im * h_im
            p_re = jnp.concatenate([p_re[0:SUBLANES, :] + bias_dc, p_re[SUBLANES:, :]], axis=0)
            prods.append(jnp.concatenate([p_re, s_re * h_im + s_im * h_re,
                                          jnp.zeros((MXU_TILE - 2 * FREQ_PAD, CONV_CH), _f32)],
                                         axis=0).astype(_bf16))
        for row0, prod in zip(seg_rows, prods):
            ybuf[row0:row0 + CONV_SEG, :] = _dot(inv_ref[...], prod)
        half_lnw, half_lnb = 0.5 * lnw_ref[...], 0.5 * lnb_ref[...]
        for r in range(row_lo, row_hi, CONV_ROWS):
            acc = ybuf[r:r + CONV_ROWS, :]
            mu = jnp.mean(acc, axis=-1, keepdims=True)
            d = acc - mu
            var = jnp.mean(d * d, axis=-1, keepdims=True)
            half = d * lax.rsqrt(var + EPS) * half_lnw + half_lnb
            mbuf[r:r + CONV_ROWS, 0:CONV_CH] = (half + half * jnp.tanh(half)).astype(_bf16)

    lo = lax.broadcasted_iota(jnp.int32, (1, LANES), 1) < RET_DV
    same_head = (lax.broadcasted_iota(jnp.int32, (STATE_ROWS, STATE_COLS), 0) // RET_DK
                 == lax.broadcasted_iota(jnp.int32, (STATE_ROWS, STATE_COLS), 1) // RET_DV)

    def retention(i):
        rows = sub_blocks[i]
        qb = qbuf[rows, :]
        kt = ktbuf[i]
        vb = vbuf[rows, :]

        yx = _dot(qb, state_b[...]) * xi_ref[...]

        for p in range(RET_HEADS // 2):
            cols = slice(p * LANES, (p + 1) * LANES)
            blank = jnp.zeros((RET_DK, RET_BLOCK), _bf16)
            k_pair = jnp.concatenate(
                [jnp.concatenate([kt[h * RET_DK:(h + 1) * RET_DK] if h == 2 * p else blank,
                                  kt[h * RET_DK:(h + 1) * RET_DK] if h == 2 * p + 1 else blank], axis=1)
                 for h in range(RET_HEADS)], axis=0)
            s = _dot(qb, k_pair)
            vp = vb[:, cols]
            off = jnp.zeros_like(vp)
            v_pair = jnp.concatenate([jnp.where(lo, vp, off), jnp.where(lo, off, vp)], axis=0)
            y = _dot((s * dmat_ref[p]).astype(_bf16), v_pair) + yx[:, cols]
            yn = _group_norm_halves(y, lo) * gnw_ref[:, cols]
            mbuf[rows, CONV_CH + p * LANES:CONV_CH + (p + 1) * LANES] = (gbuf[rows, cols] * yn).astype(_bf16)

        for q in range(RET_QK // STATE_ROWS):
            srows = slice(q * STATE_ROWS, (q + 1) * STATE_ROWS)
            scols = slice(q * STATE_COLS, (q + 1) * STATE_COLS)
            kv = _dot(kt[srows, :], vzbuf[rows, scols])
            new = gl_ref[:, scols] * state[srows, scols] + jnp.where(same_head, kv, 0.0)
            state[srows, scols] = new
            state_b[srows, scols] = new.astype(_bf16)

    half_rows = L // 2
    per_half = len(sub_blocks) // 2
    glu_rows(0, half_rows)
    project_rows(0, half_rows)
    conv_rows(0, half_rows)
    glu_rows(half_rows, L)
    hist[...] = ubuf[L:L + CONV_HALO, :]
    retention(0)
    project_rows(half_rows, L)
    for i in range(1, per_half):
        retention(i)
    conv_rows(half_rows, L)
    for i in range(per_half, len(sub_blocks)):
        retention(i)

    o_ref[0] = x + _dot(mbuf[...], wout_ref[...])


def _retention_tables(L, step_rows):
    h = np.arange(RET_HEADS, dtype=np.float64)
    log_g = np.log1p(-np.exp2(-5.0 - h))
    idx = np.arange(L, dtype=np.float64)
    dist = np.abs(idx[:, None] - idx[None, :])
    visible = (idx[None, :] // CHUNK) <= (idx[:, None] // CHUNK)
    dmat = np.where(visible[None], np.exp(log_g[:, None, None] * dist[None]), 0.0)
    dmat = np.concatenate([dmat[0::2], dmat[1::2]], axis=-1)
    xi = np.exp(log_g[None, :] * (idx[:, None] + 1.0))
    zeta = np.exp(log_g[None, :] * (L - 1.0 - idx[:, None]))
    gl = np.exp(log_g * L)
    rep = lambda t: np.repeat(t, RET_DV, axis=-1)
    inv_freq = ROPE_BASE ** (-np.arange(RET_DK // 2, dtype=np.float32) / np.float32(RET_DK // 2))
    invf = np.tile(inv_freq.astype(np.float32), LANES // (RET_DK // 2))[None, :]
    rel = np.arange(step_rows, dtype=np.float64)[:, None] * invf.astype(np.float64)
    f = lambda t: jnp.asarray(t, dtype=_f32)
    scale = RET_DK ** -0.5
    return (f(dmat * scale), f(rep(xi) * scale), f(rep(zeta)), f(rep(gl[None, :])), f(invf),
            f(np.cos(rel)), f(np.sin(rel)))


def _conv_dft_tables():
    n = np.arange(DFT_N, dtype=np.float64)
    f = np.arange(N_FREQ, dtype=np.float64)[:, None]
    w = 2.0 * np.pi / DFT_N
    fwd = np.zeros((2 * FREQ_PAD, MXU_TILE))
    fwd[:N_FREQ, :DFT_N] = np.cos(w * f * n[None, :])
    fwd[FREQ_PAD:FREQ_PAD + N_FREQ, :DFT_N] = -np.sin(w * f * n[None, :])
    delay = (CONV_WIDTH - 1) - np.arange(CONV_WIDTH, dtype=np.float64)[None, :]
    tap = np.zeros((2 * FREQ_PAD, CONV_HALO))
    tap[:N_FREQ, :CONV_WIDTH] = np.cos(w * f * delay)
    tap[FREQ_PAD:FREQ_PAD + N_FREQ, :CONV_WIDTH] = -np.sin(w * f * delay)
    weight = np.full((1, N_FREQ), 2.0)
    weight[0, 0] = weight[0, -1] = 1.0
    out = n[CONV_HALO:, None]
    inv = np.zeros((CONV_SEG, MXU_TILE))
    inv[:, :N_FREQ] = weight * np.cos(w * out * f.T) / DFT_N
    inv[:, FREQ_PAD:FREQ_PAD + N_FREQ] = -weight * np.sin(w * out * f.T) / DFT_N
    return [jnp.asarray(t, dtype=_f32) for t in (fwd, inv, tap)]


def _mixer(x, positions, norm_w, w_in, conv_w, conv_b, ln_w, ln_b, gn_w, w_out, later_weights):
    B, S, D = x.shape
    L = MIX_BLOCK
    n = S // L
    steps = B * n
    R = RET_BLOCK
    dmat, xi, zeta, gl, invf, cost, sint = _retention_tables(R, L)
    fwd, inv, tap = _conv_dft_tables()
    const = lambda shape: pl.BlockSpec(shape, lambda b, j: (0,) * len(shape))
    once = lambda shape: pl.BlockSpec(shape, lambda b, j: (0,) * len(shape), pipeline_mode=pl.Buffered(1))
    row = lambda n: pl.BlockSpec((1, n), lambda b, j: (0, 0))
    slabs = [w.reshape(steps, w.shape[0] // steps, w.shape[1]) for w in later_weights]
    slab_spec = lambda w: pl.BlockSpec((1,) + w.shape[1:], lambda b, j: (b * n + j, 0, 0))
    outs = pl.pallas_call(
        _mixer_kernel,
        grid=(B, n),
        in_specs=[
            pl.BlockSpec((1, L, D), lambda b, j: (b, j, 0)),
            pl.BlockSpec((B, L), lambda b, j: (0, j)),
            row(D),
            once((D, D_IN)),
            const((CONV_WIDTH, CONV_CH)),
            row(CONV_CH), row(CONV_CH), row(CONV_CH), row(RET_V),
            once((D, D)),
            const((RET_HEADS // 2, R, 2 * R)),
            const((R, RET_V)), const((R, RET_V)), row(RET_V), row(LANES),
            const((L, LANES)), const((L, LANES)),
            const(fwd.shape), const(inv.shape), const(tap.shape),
        ] + [slab_spec(w) for w in slabs],
        out_specs=[pl.BlockSpec((1, L, D), lambda b, j: (b, j, 0))] + [slab_spec(w) for w in slabs],
        out_shape=[jax.ShapeDtypeStruct((B, S, D), _f32)] + [jax.ShapeDtypeStruct(w.shape, _bf16) for w in slabs],
        scratch_shapes=[
            pltpu.VMEM(inv.shape, _bf16),
            pltpu.VMEM((2 * FREQ_PAD, CONV_CH), _f32),
            pltpu.VMEM((CONV_HALO, CONV_CH), _f32),
            pltpu.VMEM((L, D), _bf16),
            pltpu.VMEM((CONV_HALO, CONV_CH), _bf16),
            pltpu.VMEM((CONV_HALO + L + CONV_TAIL, CONV_CH), _bf16),
            pltpu.VMEM((L, CONV_CH), _f32),
            pltpu.VMEM((L, RET_QK), _bf16),
            pltpu.VMEM((L // R, RET_QK, R), _bf16),
            pltpu.VMEM((L, RET_V), _bf16),
            pltpu.VMEM((L, RET_V), _bf16),
            pltpu.VMEM((L, RET_V), _f32),
            pltpu.VMEM((RET_QK, RET_V), _f32),
            pltpu.VMEM((RET_QK, RET_V), _bf16),
            pltpu.VMEM((L, D), _bf16),
            pltpu.VMEM(fwd.shape, _bf16),
        ],
        compiler_params=pltpu.CompilerParams(
            dimension_semantics=("arbitrary", "arbitrary"), vmem_limit_bytes=VMEM_LIMIT),
        name="mixer",
    )(x, positions, norm_w.reshape(1, D), w_in, conv_w, conv_b.reshape(1, -1),
      ln_w.reshape(1, -1), ln_b.reshape(1, -1), gn_w.reshape(1, -1), w_out, dmat, xi, zeta, gl, invf, cost, sint,
      fwd, inv, tap, *slabs)
    return outs[0], [o.reshape(w.shape) for o, w in zip(outs[1:], later_weights)]


def _xattn_kernel(h_ref, nw_ref, wq_ref, kt_ref, v_ref, wo_ref, o_ref, qbuf, obuf):
    for m in range(0, h_ref.shape[1], ROW_CHUNK):
        hn = _rms(h_ref[0, m:m + ROW_CHUNK, :], nw_ref[...]).astype(_bf16)
        qbuf[m:m + ROW_CHUNK, :] = _dot(hn, wq_ref[...]).astype(_bf16)
    for i in range(XATTN_HEADS):
        cols = slice(i * XATTN_HEAD_DIM, (i + 1) * XATTN_HEAD_DIM)
        s = _dot(qbuf[:, cols], kt_ref[0, cols, :])
        e = jnp.exp(s - jnp.max(s, axis=-1, keepdims=True))
        o = _dot(e.astype(_bf16), v_ref[0, :, cols])
        obuf[:, cols] = (o * (1.0 / jnp.sum(e, axis=-1, keepdims=True))).astype(_bf16)
    for m in range(0, h_ref.shape[1], XATTN_TAIL_CHUNK):
        rows = slice(m, m + XATTN_TAIL_CHUNK)
        o_ref[0, rows, :] = h_ref[0, rows, :] + _dot(obuf[rows, :], wo_ref[...])


def _xattn(h, norm_w, xq_w, kt, v, xo_w):
    B, S, D = h.shape
    T = XATTN_BLOCK
    return pl.pallas_call(
        _xattn_kernel,
        grid=(B, S // T),
        in_specs=[
            pl.BlockSpec((1, T, D), lambda b, j: (b, j, 0)),
            pl.BlockSpec((1, D), lambda b, j: (0, 0)),
            pl.BlockSpec((D, D), lambda b, j: (0, 0)),
            pl.BlockSpec((1, D, N_MEM), lambda b, j: (b, 0, 0)),
            pl.BlockSpec((1, N_MEM, D), lambda b, j: (b, 0, 0)),
            pl.BlockSpec((D, D), lambda b, j: (0, 0)),
        ],
        out_specs=pl.BlockSpec((1, T, D), lambda b, j: (b, j, 0)),
        out_shape=jax.ShapeDtypeStruct((B, S, D), _f32),
        scratch_shapes=[pltpu.VMEM((T, D), _bf16), pltpu.VMEM((T, D), _bf16)],
        compiler_params=pltpu.CompilerParams(
            dimension_semantics=("arbitrary", "arbitrary"), vmem_limit_bytes=VMEM_LIMIT),
        name="xattn",
    )(h, norm_w.reshape(1, D), xq_w, kt, v, xo_w)


def _mlp_kernel(h_ref, nw_ref, wu_ref, wd_ref, fw_ref, o_ref, abuf):
    for m in range(0, h_ref.shape[0], MLP_ROW_CHUNK):
        hn = _rms(h_ref[m:m + MLP_ROW_CHUNK, :], nw_ref[...]).astype(_bf16)
        for c in range(0, D_FF, FF_CHUNK):
            u = jnp.maximum(_dot(hn, wu_ref[:, c:c + FF_CHUNK]), 0.0)
            abuf[m:m + MLP_ROW_CHUNK, c:c + FF_CHUNK] = (u * u).astype(_bf16)
    for m in range(0, h_ref.shape[0], MLP_TAIL_CHUNK):
        rows = slice(m, m + MLP_TAIL_CHUNK)
        o_ref[rows, :] = _rms(h_ref[rows, :] + _dot(abuf[rows, :], wd_ref[...]), fw_ref[...])


def _mlp(h, norm_w, up_w, down_w, norm_f_w):
    B, S, D = h.shape
    T = TOK_BLOCK
    h2 = h.reshape(B * S, D)
    out = pl.pallas_call(
        _mlp_kernel,
        grid=(B * S // T,),
        in_specs=[
            pl.BlockSpec((T, D), lambda i: (i, 0)),
            pl.BlockSpec((1, D), lambda i: (0, 0)),
            pl.BlockSpec((D, D_FF), lambda i: (0, 0), pipeline_mode=pl.Buffered(1)),
            pl.BlockSpec((D_FF, D), lambda i: (0, 0), pipeline_mode=pl.Buffered(1)),
            pl.BlockSpec((1, D), lambda i: (0, 0)),
        ],
        out_specs=pl.BlockSpec((T, D), lambda i: (i, 0)),
        out_shape=jax.ShapeDtypeStruct((B * S, D), _f32),
        scratch_shapes=[pltpu.VMEM((T, D_FF), _bf16)],
        compiler_params=pltpu.CompilerParams(
            dimension_semantics=("arbitrary",), vmem_limit_bytes=VMEM_LIMIT),
        name="mlp",
    )(h2, norm_w.reshape(1, D), up_w, down_w, norm_f_w.reshape(1, D))
    return out.reshape(B, S, D)


def _xattn_mlp_kernel(h_ref, nxw_ref, wq_ref, kt_ref, v_ref, wo_ref, nmw_ref, wu_ref, wd_ref, fw_ref, o_ref,
                      qbuf, obuf, abuf):
    T = h_ref.shape[1]
    for m in range(0, T, ROW_CHUNK):
        hn = _rms(h_ref[0, m:m + ROW_CHUNK, :], nxw_ref[...]).astype(_bf16)
        qbuf[m:m + ROW_CHUNK, :] = _dot(hn, wq_ref[...]).astype(_bf16)
    for i in range(XATTN_HEADS):
        cols = slice(i * XATTN_HEAD_DIM, (i + 1) * XATTN_HEAD_DIM)
        s = _dot(qbuf[:, cols], kt_ref[0, cols, :])
        e = jnp.exp(s - jnp.max(s, axis=-1, keepdims=True))
        o = _dot(e.astype(_bf16), v_ref[0, :, cols])
        obuf[:, cols] = (o * (1.0 / jnp.sum(e, axis=-1, keepdims=True))).astype(_bf16)
    for m in range(0, T, XATTN_TAIL_CHUNK):
        rows = slice(m, m + XATTN_TAIL_CHUNK)
        o_ref[0, rows, :] = h_ref[0, rows, :] + _dot(obuf[rows, :], wo_ref[...])
    for m in range(0, T, MLP_ROW_CHUNK):
        hn = _rms(o_ref[0, m:m + MLP_ROW_CHUNK, :], nmw_ref[...]).astype(_bf16)
        for c in range(0, D_FF, FF_CHUNK):
            u = jnp.maximum(_dot(hn, wu_ref[:, c:c + FF_CHUNK]), 0.0)
            abuf[m:m + MLP_ROW_CHUNK, c:c + FF_CHUNK] = (u * u).astype(_bf16)
    for m in range(0, T, MLP_TAIL_CHUNK):
        rows = slice(m, m + MLP_TAIL_CHUNK)
        o_ref[0, rows, :] = _rms(o_ref[0, rows, :] + _dot(abuf[rows, :], wd_ref[...]), fw_ref[...])


def _xattn_mlp(h, norm_x_w, xq_w, kt, v, xo_w, norm_m_w, up_w, down_w, norm_f_w):
    B, S, D = h.shape
    T = TOK_BLOCK
    row = lambda: pl.BlockSpec((1, D), lambda b, j: (0, 0))
    once = lambda shape: pl.BlockSpec(shape, lambda b, j: (0, 0), pipeline_mode=pl.Buffered(1))
    return pl.pallas_call(
        _xattn_mlp_kernel,
        grid=(B, S // T),
        in_specs=[
            pl.BlockSpec((1, T, D), lambda b, j: (b, j, 0)),
            row(), once((D, D)),
            pl.BlockSpec((1, D, N_MEM), lambda b, j: (b, 0, 0)),
            pl.BlockSpec((1, N_MEM, D), lambda b, j: (b, 0, 0)),
            once((D, D)),
            row(), once((D, D_FF)), once((D_FF, D)), row(),
        ],
        out_specs=pl.BlockSpec((1, T, D), lambda b, j: (b, j, 0)),
        out_shape=jax.ShapeDtypeStruct((B, S, D), _f32),
        scratch_shapes=[pltpu.VMEM((T, D), _bf16), pltpu.VMEM((T, D), _bf16),
                        pltpu.VMEM((T, D_FF), _bf16)],
        compiler_params=pltpu.CompilerParams(
            dimension_semantics=("arbitrary", "arbitrary"), vmem_limit_bytes=VMEM_LIMIT),
        name="xattn_mlp",
    )(h, norm_x_w.reshape(1, D), xq_w, kt, v, xo_w, norm_m_w.reshape(1, D), up_w, down_w, norm_f_w.reshape(1, D))


def kernel(x, mem, positions, norm_mix_w, w_in, conv_w, conv_b, conv_ln_w, conv_ln_b, ret_gn_w, w_out,
           norm_xattn_w, norm_mem_w, xq_w, xkv_w, xo_w, norm_mlp_w, mlp_up_w, mlp_down_w, norm_f_w):
    kt, v, w_in_b, w_out_b = _mem_kv(mem, norm_mem_w, xkv_w, w_in, w_out)
    h, (xq_b, xo_b, up_b, down_b) = _mixer(x, positions, norm_mix_w, w_in_b, conv_w, conv_b, conv_ln_w, conv_ln_b,
                                           ret_gn_w, w_out_b, (xq_w, xo_w, mlp_up_w, mlp_down_w))
    return _xattn_mlp(h, norm_xattn_w, xq_b, kt, v, xo_b, norm_mlp_w, up_b, down_b, norm_f_w)
```

```python
import numpy as np
import jax
import jax.numpy as jnp
from jax import lax
from jax.experimental import pallas as pl
from jax.experimental.pallas import tpu as pltpu

D_MODEL = 1024
CHUNK = 64
CONV_CH = 512
CONV_WIDTH = 31
RET_HEADS = 8
RET_DV = 64
RET_DK = 32
RET_QK = RET_HEADS * RET_DK
RET_V = RET_HEADS * RET_DV
N_MEM = 256
XATTN_HEADS = 4
XATTN_HEAD_DIM = 256
D_FF = 4096
ROPE_BASE = 10000.0
EPS = 1e-6

OFF_A, OFF_B, OFF_Q, OFF_K, OFF_V, OFF_G = 0, 512, 1024, 1280, 1536, 2048
D_IN = 2560

LANES = 128
SUBLANES = 8
MXU_TILE = 256
MIX_BLOCK = 512
STATE_COLS = MXU_TILE
STATE_ROWS = STATE_COLS // RET_DV * RET_DK
RET_BLOCK = 128
CONV_HALO = 32
CONV_SEG = 128
DFT_N = CONV_HALO + CONV_SEG
N_FREQ = DFT_N // 2 + 1
FREQ_PAD = -(-N_FREQ // SUBLANES) * SUBLANES
CONV_TAIL = MXU_TILE - DFT_N
CONV_ROWS = 32
TOK_BLOCK = 1024
ROW_CHUNK = 256
MLP_ROW_CHUNK = 256
XATTN_TAIL_CHUNK = 512
MLP_TAIL_CHUNK = 256
FF_CHUNK = 1024
VMEM_LIMIT = 56 * 1024 * 1024

_f32 = jnp.float32
_bf16 = jnp.bfloat16


def _dot(a, b):
    return jnp.dot(a, b, preferred_element_type=_f32)


def _rms(x, w):
    return x * lax.rsqrt(jnp.mean(x * x, axis=-1, keepdims=True) + EPS) * w


def _sigmoid(x):
    return 1.0 / (1.0 + jnp.exp(-x))


def _swish(x):
    h = 0.5 * x
    return h + h * jnp.tanh(h)


def _cast_columns(src_ref, dst_ref):
    for c in range(0, src_ref.shape[-1], MXU_TILE):
        dst_ref[:, c:c + MXU_TILE] = src_ref[:, c:c + MXU_TILE].astype(_bf16)


def _mem_kv_kernel(mem_ref, nw_ref, wkv_ref, win32_ref, wout32_ref, kt_ref, v_ref, win_ref, wout_ref):
    m = _rms(mem_ref[0], nw_ref[...]).astype(_bf16)
    k = _dot(m, wkv_ref[:, :D_MODEL].astype(_bf16))
    kt_ref[0] = k.T.astype(_bf16)
    v_ref[0] = _dot(m, wkv_ref[:, D_MODEL:].astype(_bf16)).astype(_bf16)
    _cast_columns(win32_ref.at[0], win_ref.at[0])
    _cast_columns(wout32_ref.at[0], wout_ref.at[0])


def _mem_kv(mem, norm_mem_w, xkv_w, w_in, w_out):
    B = mem.shape[0]
    slab = lambda w: w.reshape(B, w.shape[0] // B, w.shape[1])
    slab_spec = lambda w: pl.BlockSpec((1, w.shape[0] // B, w.shape[1]), lambda b: (b, 0, 0))
    kt, v, win_b, wout_b = pl.pallas_call(
        _mem_kv_kernel,
        grid=(B,),
        in_specs=[
            pl.BlockSpec((1, N_MEM, D_MODEL), lambda b: (b, 0, 0)),
            pl.BlockSpec((1, D_MODEL), lambda b: (0, 0)),
            pl.BlockSpec((D_MODEL, 2 * D_MODEL), lambda b: (0, 0)),
            slab_spec(w_in), slab_spec(w_out),
        ],
        out_specs=[
            pl.BlockSpec((1, D_MODEL, N_MEM), lambda b: (b, 0, 0)),
            pl.BlockSpec((1, N_MEM, D_MODEL), lambda b: (b, 0, 0)),
            slab_spec(w_in), slab_spec(w_out),
        ],
        out_shape=[
            jax.ShapeDtypeStruct((B, D_MODEL, N_MEM), _bf16),
            jax.ShapeDtypeStruct((B, N_MEM, D_MODEL), _bf16),
            jax.ShapeDtypeStruct(slab(w_in).shape, _bf16),
            jax.ShapeDtypeStruct(slab(w_out).shape, _bf16),
        ],
        compiler_params=pltpu.CompilerParams(
            dimension_semantics=("arbitrary",), vmem_limit_bytes=VMEM_LIMIT),
        name="mem_kv",
    )(mem, norm_mem_w.reshape(1, D_MODEL), xkv_w, slab(w_in), slab(w_out))
    return kt, v, win_b.reshape(w_in.shape), wout_b.reshape(w_out.shape)


def _group_norm_halves(y, lo):
    inv = 1.0 / RET_DV
    s_lo = jnp.sum(jnp.where(lo, y, 0.0), axis=-1, keepdims=True)
    s_hi = jnp.sum(jnp.where(lo, 0.0, y), axis=-1, keepdims=True)
    d = y - jnp.where(lo, s_lo, s_hi) * inv
    d2 = d * d
    v_lo = jnp.sum(jnp.where(lo, d2, 0.0), axis=-1, keepdims=True)
    v_hi = jnp.sum(jnp.where(lo, 0.0, d2), axis=-1, keepdims=True)
    return d * lax.rsqrt(jnp.where(lo, v_lo, v_hi) * inv + EPS)


N_CAST = 4
CAST_SCALES = (XATTN_HEAD_DIM ** -0.5, 1.0, 1.0, 1.0)


def _mixer_kernel(*refs):
    n_in = 20
    (x_ref, pos_ref, nw_ref, win_ref, cw_ref, cb_ref, lnw_ref, lnb_ref, gnw_ref, wout_ref,
     dmat_ref, xi_ref, zeta_ref, gl_ref, invf_ref, cost_ref, sint_ref, fwd32_ref, inv32_ref, tap_ref) = refs[:n_in]
    cast_src = refs[n_in:n_in + N_CAST]
    o_ref = refs[n_in + N_CAST]
    cast_dst = refs[n_in + 1 + N_CAST:n_in + 1 + 2 * N_CAST]
    (inv_ref, hspec, cwbuf, hbuf, hist, ubuf, ybuf, qbuf, ktbuf, vbuf, vzbuf, gbuf, state, state_b, mbuf,
     fwd_ref) = refs[n_in + 1 + 2 * N_CAST:]
    L = MIX_BLOCK

    @pl.when((pl.program_id(0) == 0) & (pl.program_id(1) == 0))
    def _():
        fwd_ref[...] = fwd32_ref[...].astype(_bf16)
        inv_ref[...] = inv32_ref[...].astype(_bf16)
        ubuf[...] = jnp.zeros(ubuf.shape, _bf16)
        cwbuf[...] = jnp.zeros(cwbuf.shape, _f32)
        cwbuf[0:CONV_WIDTH, :] = cw_ref[...]
        hspec[...] = jnp.dot(tap_ref[...], cwbuf[...], preferred_element_type=_f32, precision=lax.Precision.HIGHEST)

    @pl.when(pl.program_id(1) == 0)
    def _():
        hist[...] = jnp.zeros((CONV_HALO, CONV_CH), _bf16)
        state[...] = jnp.zeros_like(state)
        state_b[...] = jnp.zeros_like(state_b)

    for src, dst, scale in zip(cast_src, cast_dst, CAST_SCALES):
        dst[...] = (src[...] if scale == 1.0 else src[...] * scale).astype(_bf16)

    x = x_ref[0]
    ubuf[0:CONV_HALO, :] = hist[...]

    def glu_rows(row_lo, row_hi):
        for m in range(row_lo, row_hi, ROW_CHUNK):
            hn_m = _rms(x_ref[0, m:m + ROW_CHUNK, :], nw_ref[...]).astype(_bf16)
            hbuf[m:m + ROW_CHUNK, :] = hn_m
            a = _dot(hn_m, win_ref[:, OFF_A:OFF_A + CONV_CH])
            b = _dot(hn_m, win_ref[:, OFF_B:OFF_B + CONV_CH])
            ubuf[CONV_HALO + m:CONV_HALO + m + ROW_CHUNK, :] = (a * _sigmoid(b)).astype(_bf16)

    first_pos = pos_ref[:, 0:1].astype(_f32)
    batch_row = lax.broadcasted_iota(jnp.int32, first_pos.shape, 0)
    ang0 = jnp.sum(jnp.where(batch_row == pl.program_id(0), first_pos, 0.0), axis=0, keepdims=True) * invf_ref[...]
    cos0, sin0 = jnp.cos(ang0), jnp.sin(ang0)
    lane = lax.broadcasted_iota(jnp.int32, (1, RET_QK), 1)
    first_half = lane % RET_DK < RET_DK // 2
    sub_blocks = [slice(r0, r0 + RET_BLOCK) for r0 in range(0, L, RET_BLOCK)]

    def rotary(t, rows):
        cos = cos0 * cost_ref[rows, :] - sin0 * sint_ref[rows, :]
        sin = sin0 * cost_ref[rows, :] + cos0 * sint_ref[rows, :]
        cos2 = jnp.concatenate([cos, cos], axis=-1)
        sin2 = jnp.where(first_half, -1.0, 1.0) * jnp.concatenate([sin, sin], axis=-1)
        partner = jnp.where(first_half, pltpu.roll(t, RET_QK - RET_DK // 2, 1), pltpu.roll(t, RET_DK // 2, 1))
        return t * cos2 + partner * sin2

    def project_rows(row_lo, row_hi):
        hn = hbuf[row_lo:row_hi, :]
        mine = [(i, rows, slice(rows.start - row_lo, rows.stop - row_lo)) for i, rows in enumerate(sub_blocks)
                if row_lo <= rows.start < row_hi]
        q_all = _dot(hn, win_ref[:, OFF_Q:OFF_K])
        for _, rows, local in mine:
            qbuf[rows, :] = rotary(q_all[local, :], rows).astype(_bf16)
        k_all = _dot(hn, win_ref[:, OFF_K:OFF_V])
        for i, rows, local in mine:
            ktbuf[i] = rotary(k_all[local, :], rows).T.astype(_bf16)
        v_all = _dot(hn, win_ref[:, OFF_V:OFF_G])
        vbuf[row_lo:row_hi, :] = v_all.astype(_bf16)
        for _, rows, local in mine:
            vzbuf[rows, :] = (v_all[local, :] * zeta_ref[...]).astype(_bf16)
        gbuf[row_lo:row_hi, :] = _swish(_dot(hn, win_ref[:, OFF_G:D_IN]))

    h_re, h_im = hspec[0:FREQ_PAD, :], hspec[FREQ_PAD:2 * FREQ_PAD, :]

    def conv_rows(row_lo, row_hi):
        seg_rows = range(row_lo, row_hi, CONV_SEG)
        specs = [_dot(fwd_ref[...], ubuf[row0:row0 + MXU_TILE, :]) for row0 in seg_rows]
        dc_row = lax.broadcasted_iota(jnp.int32, (SUBLANES, CONV_CH), 0) == 0
        bias_dc = jnp.where(dc_row, cb_ref[...] * float(DFT_N), 0.0)
        prods = []
        for spec in specs:
            s_re, s_im = spec[0:FREQ_PAD, :], spec[FREQ_PAD:2 * FREQ_PAD, :]
            p_re = s_re * h_re - s_im * h_im
            p_re = jnp.concatenate([p_re[0:SUBLANES, :] + bias_dc, p_re[SUBLANES:, :]], axis=0)
            prods.append(jnp.concatenate([p_re, s_re * h_im + s_im * h_re,
                                          jnp.zeros((MXU_TILE - 2 * FREQ_PAD, CONV_CH), _f32)],
                                         axis=0).astype(_bf16))
        for row0, prod in zip(seg_rows, prods):
            ybuf[row0:row0 + CONV_SEG, :] = _dot(inv_ref[...], prod)
        half_lnw, half_lnb = 0.5 * lnw_ref[...], 0.5 * lnb_ref[...]
        for r in range(row_lo, row_hi, CONV_ROWS):
            acc = ybuf[r:r + CONV_ROWS, :]
            mu = jnp.mean(acc, axis=-1, keepdims=True)
            d = acc - mu
            var = jnp.mean(d * d, axis=-1, keepdims=True)
            half = d * lax.rsqrt(var + EPS) * half_lnw + half_lnb
            mbuf[r:r + CONV_ROWS, 0:CONV_CH] = (half + half * jnp.tanh(half)).astype(_bf16)

    lo = lax.broadcasted_iota(jnp.int32, (1, LANES), 1) < RET_DV
    same_head = (lax.broadcasted_iota(jnp.int32, (STATE_ROWS, STATE_COLS), 0) // RET_DK
                 == lax.broadcasted_iota(jnp.int32, (STATE_ROWS, STATE_COLS), 1) // RET_DV)

    def retention(i):
        rows = sub_blocks[i]
        qb = qbuf[rows, :]
        kt = ktbuf[i]
        vb = vbuf[rows, :]

        yx = _dot(qb, state_b[...]) * xi_ref[...]

        for p in range(RET_HEADS // 2):
            cols = slice(p * LANES, (p + 1) * LANES)
            blank = jnp.zeros((RET_DK, RET_BLOCK), _bf16)
            k_pair = jnp.concatenate(
                [jnp.concatenate([kt[h * RET_DK:(h + 1) * RET_DK] if h == 2 * p else blank,
                                  kt[h * RET_DK:(h + 1) * RET_DK] if h == 2 * p + 1 else blank], axis=1)
                 for h in range(RET_HEADS)], axis=0)
            s = _dot(qb, k_pair)
            vp = vb[:, cols]
            off = jnp.zeros_like(vp)
            v_pair = jnp.concatenate([jnp.where(lo, vp, off), jnp.where(lo, off, vp)], axis=0)
            y = _dot((s * dmat_ref[p]).astype(_bf16), v_pair) + yx[:, cols]
            yn = _group_norm_halves(y, lo) * gnw_ref[:, cols]
            mbuf[rows, CONV_CH + p * LANES:CONV_CH + (p + 1) * LANES] = (gbuf[rows, cols] * yn).astype(_bf16)

        for q in range(RET_QK // STATE_ROWS):
            srows = slice(q * STATE_ROWS, (q + 1) * STATE_ROWS)
            scols = slice(q * STATE_COLS, (q + 1) * STATE_COLS)
            kv = _dot(kt[srows, :], vzbuf[rows, scols])
            new = gl_ref[:, scols] * state[srows, scols] + jnp.where(same_head, kv, 0.0)
            state[srows, scols] = new
            state_b[srows, scols] = new.astype(_bf16)

    half_rows = L // 2
    per_half = len(sub_blocks) // 2
    glu_rows(0, half_rows)
    project_rows(0, half_rows)
    conv_rows(0, half_rows)
    glu_rows(half_rows, L)
    hist[...] = ubuf[L:L + CONV_HALO, :]
    retention(0)
    project_rows(half_rows, L)
    for i in range(1, per_half):
        retention(i)
    conv_rows(half_rows, L)
    for i in range(per_half, len(sub_blocks)):
        retention(i)

    o_ref[0] = x + _dot(mbuf[...], wout_ref[...])


def _retention_tables(L, step_rows):
    h = np.arange(RET_HEADS, dtype=np.float64)
    log_g = np.log1p(-np.exp2(-5.0 - h))
    idx = np.arange(L, dtype=np.float64)
    dist = np.abs(idx[:, None] - idx[None, :])
    visible = (idx[None, :] // CHUNK) <= (idx[:, None] // CHUNK)
    dmat = np.where(visible[None], np.exp(log_g[:, None, None] * dist[None]), 0.0)
    dmat = np.concatenate([dmat[0::2], dmat[1::2]], axis=-1)
    xi = np.exp(log_g[None, :] * (idx[:, None] + 1.0))
    zeta = np.exp(log_g[None, :] * (L - 1.0 - idx[:, None]))
    gl = np.exp(log_g * L)
    rep = lambda t: np.repeat(t, RET_DV, axis=-1)
    inv_freq = ROPE_BASE ** (-np.arange(RET_DK // 2, dtype=np.float32) / np.float32(RET_DK // 2))
    invf = np.tile(inv_freq.astype(np.float32), LANES // (RET_DK // 2))[None, :]
    rel = np.arange(step_rows, dtype=np.float64)[:, None] * invf.astype(np.float64)
    f = lambda t: jnp.asarray(t, dtype=_f32)
    scale = RET_DK ** -0.5
    return (f(dmat * scale), f(rep(xi) * scale), f(rep(zeta)), f(rep(gl[None, :])), f(invf),
            f(np.cos(rel)), f(np.sin(rel)))


def _conv_dft_tables():
    n = np.arange(DFT_N, dtype=np.float64)
    f = np.arange(N_FREQ, dtype=np.float64)[:, None]
    w = 2.0 * np.pi / DFT_N
    fwd = np.zeros((2 * FREQ_PAD, MXU_TILE))
    fwd[:N_FREQ, :DFT_N] = np.cos(w * f * n[None, :])
    fwd[FREQ_PAD:FREQ_PAD + N_FREQ, :DFT_N] = -np.sin(w * f * n[None, :])
    delay = (CONV_WIDTH - 1) - np.arange(CONV_WIDTH, dtype=np.float64)[None, :]
    tap = np.zeros((2 * FREQ_PAD, CONV_HALO))
    tap[:N_FREQ, :CONV_WIDTH] = np.cos(w * f * delay)
    tap[FREQ_PAD:FREQ_PAD + N_FREQ, :CONV_WIDTH] = -np.sin(w * f * delay)
    weight = np.full((1, N_FREQ), 2.0)
    weight[0, 0] = weight[0, -1] = 1.0
    out = n[CONV_HALO:, None]
    inv = np.zeros((CONV_SEG, MXU_TILE))
    inv[:, :N_FREQ] = weight * np.cos(w * out * f.T) / DFT_N
    inv[:, FREQ_PAD:FREQ_PAD + N_FREQ] = -weight * np.sin(w * out * f.T) / DFT_N
    return [jnp.asarray(t, dtype=_f32) for t in (fwd, inv, tap)]


def _mixer(x, positions, norm_w, w_in, conv_w, conv_b, ln_w, ln_b, gn_w, w_out, later_weights):
    B, S, D = x.shape
    L = MIX_BLOCK
    n = S // L
    steps = B * n
    R = RET_BLOCK
    dmat, xi, zeta, gl, invf, cost, sint = _retention_tables(R, L)
    fwd, inv, tap = _conv_dft_tables()
    const = lambda shape: pl.BlockSpec(shape, lambda b, j: (0,) * len(shape))
    once = lambda shape: pl.BlockSpec(shape, lambda b, j: (0,) * len(shape), pipeline_mode=pl.Buffered(1))
    row = lambda n: pl.BlockSpec((1, n), lambda b, j: (0, 0))
    slabs = [w.reshape(steps, w.shape[0] // steps, w.shape[1]) for w in later_weights]
    slab_spec = lambda w: pl.BlockSpec((1,) + w.shape[1:], lambda b, j: (b * n + j, 0, 0))
    outs = pl.pallas_call(
        _mixer_kernel,
        grid=(B, n),
        in_specs=[
            pl.BlockSpec((1, L, D), lambda b, j: (b, j, 0)),
            pl.BlockSpec((B, L), lambda b, j: (0, j)),
            row(D),
            once((D, D_IN)),
            const((CONV_WIDTH, CONV_CH)),
            row(CONV_CH), row(CONV_CH), row(CONV_CH), row(RET_V),
            once((D, D)),
            const((RET_HEADS // 2, R, 2 * R)),
            const((R, RET_V)), const((R, RET_V)), row(RET_V), row(LANES),
            const((L, LANES)), const((L, LANES)),
            const(fwd.shape), const(inv.shape), const(tap.shape),
        ] + [slab_spec(w) for w in slabs],
        out_specs=[pl.BlockSpec((1, L, D), lambda b, j: (b, j, 0))] + [slab_spec(w) for w in slabs],
        out_shape=[jax.ShapeDtypeStruct((B, S, D), _f32)] + [jax.ShapeDtypeStruct(w.shape, _bf16) for w in slabs],
        scratch_shapes=[
            pltpu.VMEM(inv.shape, _bf16),
            pltpu.VMEM((2 * FREQ_PAD, CONV_CH), _f32),
            pltpu.VMEM((CONV_HALO, CONV_CH), _f32),
            pltpu.VMEM((L, D), _bf16),
            pltpu.VMEM((CONV_HALO, CONV_CH), _bf16),
            pltpu.VMEM((CONV_HALO + L + CONV_TAIL, CONV_CH), _bf16),
            pltpu.VMEM((L, CONV_CH), _f32),
            pltpu.VMEM((L, RET_QK), _bf16),
            pltpu.VMEM((L // R, RET_QK, R), _bf16),
            pltpu.VMEM((L, RET_V), _bf16),
            pltpu.VMEM((L, RET_V), _bf16),
            pltpu.VMEM((L, RET_V), _f32),
            pltpu.VMEM((RET_QK, RET_V), _f32),
            pltpu.VMEM((RET_QK, RET_V), _bf16),
            pltpu.VMEM((L, D), _bf16),
            pltpu.VMEM(fwd.shape, _bf16),
        ],
        compiler_params=pltpu.CompilerParams(
            dimension_semantics=("arbitrary", "arbitrary"), vmem_limit_bytes=VMEM_LIMIT),
        name="mixer",
    )(x, positions, norm_w.reshape(1, D), w_in, conv_w, conv_b.reshape(1, -1),
      ln_w.reshape(1, -1), ln_b.reshape(1, -1), gn_w.reshape(1, -1), w_out, dmat, xi, zeta, gl, invf, cost, sint,
      fwd, inv, tap, *slabs)
    return outs[0], [o.reshape(w.shape) for o, w in zip(outs[1:], later_weights)]


def _xattn_mlp_kernel(h_ref, nxw_ref, wq_ref, kt_ref, v_ref, wo_ref, nmw_ref, wu_hbm, wd_hbm, fw_ref, o_ref,
                      qbuf, obuf, abuf, wu_ref, wd_ref, sem):
    T = h_ref.shape[1]
    first = (pl.program_id(0) == 0) & (pl.program_id(1) == 0)
    up_copy = pltpu.make_async_copy(wu_hbm, wu_ref, sem.at[0])
    down_copy = pltpu.make_async_copy(wd_hbm, wd_ref, sem.at[1])

    @pl.when(first)
    def _():
        up_copy.start()
        down_copy.start()

    for m in range(0, T, ROW_CHUNK):
        hn = _rms(h_ref[0, m:m + ROW_CHUNK, :], nxw_ref[...]).astype(_bf16)
        qbuf[m:m + ROW_CHUNK, :] = _dot(hn, wq_ref[...]).astype(_bf16)
    for i in range(XATTN_HEADS):
        cols = slice(i * XATTN_HEAD_DIM, (i + 1) * XATTN_HEAD_DIM)
        s = _dot(qbuf[:, cols], kt_ref[0, cols, :])
        e = jnp.exp(s - jnp.max(s, axis=-1, keepdims=True))
        o = _dot(e.astype(_bf16), v_ref[0, :, cols])
        obuf[:, cols] = (o * (1.0 / jnp.sum(e, axis=-1, keepdims=True))).astype(_bf16)
    for m in range(0, T, XATTN_TAIL_CHUNK):
        rows = slice(m, m + XATTN_TAIL_CHUNK)
        o_ref[0, rows, :] = h_ref[0, rows, :] + _dot(obuf[rows, :], wo_ref[...])
    @pl.when(first)
    def _():
        up_copy.wait()
        down_copy.wait()

    for m in range(0, T, MLP_ROW_CHUNK):
        hn = _rms(o_ref[0, m:m + MLP_ROW_CHUNK, :], nmw_ref[...]).astype(_bf16)
        for c in range(0, D_FF, FF_CHUNK):
            u = jnp.maximum(_dot(hn, wu_ref[:, c:c + FF_CHUNK]), 0.0)
            abuf[m:m + MLP_ROW_CHUNK, c:c + FF_CHUNK] = (u * u).astype(_bf16)
    for m in range(0, T, MLP_TAIL_CHUNK):
        rows = slice(m, m + MLP_TAIL_CHUNK)
        o_ref[0, rows, :] = _rms(o_ref[0, rows, :] + _dot(abuf[rows, :], wd_ref[...]), fw_ref[...])


def _xattn_mlp(h, norm_x_w, xq_w, kt, v, xo_w, norm_m_w, up_w, down_w, norm_f_w):
    B, S, D = h.shape
    T = TOK_BLOCK
    row = lambda: pl.BlockSpec((1, D), lambda b, j: (0, 0))
    once = lambda shape: pl.BlockSpec(shape, lambda b, j: (0, 0), pipeline_mode=pl.Buffered(1))
    return pl.pallas_call(
        _xattn_mlp_kernel,
        grid=(B, S // T),
        in_specs=[
            pl.BlockSpec((1, T, D), lambda b, j: (b, j, 0)),
            row(), once((D, D)),
            pl.BlockSpec((1, D, N_MEM), lambda b, j: (b, 0, 0)),
            pl.BlockSpec((1, N_MEM, D), lambda b, j: (b, 0, 0)),
            once((D, D)),
            row(), pl.BlockSpec(memory_space=pl.ANY), pl.BlockSpec(memory_space=pl.ANY), row(),
        ],
        out_specs=pl.BlockSpec((1, T, D), lambda b, j: (b, j, 0)),
        out_shape=jax.ShapeDtypeStruct((B, S, D), _f32),
        scratch_shapes=[pltpu.VMEM((T, D), _bf16), pltpu.VMEM((T, D), _bf16),
                        pltpu.VMEM((T, D_FF), _bf16),
                        pltpu.VMEM((D, D_FF), _bf16), pltpu.VMEM((D_FF, D), _bf16),
                        pltpu.SemaphoreType.DMA((2,))],
        compiler_params=pltpu.CompilerParams(
            dimension_semantics=("arbitrary", "arbitrary"), vmem_limit_bytes=VMEM_LIMIT),
        name="xattn_mlp",
    )(h, norm_x_w.reshape(1, D), xq_w, kt, v, xo_w, norm_m_w.reshape(1, D), up_w, down_w, norm_f_w.reshape(1, D))


def kernel(x, mem, positions, norm_mix_w, w_in, conv_w, conv_b, conv_ln_w, conv_ln_b, ret_gn_w, w_out,
           norm_xattn_w, norm_mem_w, xq_w, xkv_w, xo_w, norm_mlp_w, mlp_up_w, mlp_down_w, norm_f_w):
    kt, v, w_in_b, w_out_b = _mem_kv(mem, norm_mem_w, xkv_w, w_in, w_out)
    h, (xq_b, xo_b, up_b, down_b) = _mixer(x, positions, norm_mix_w, w_in_b, conv_w, conv_b, conv_ln_w, conv_ln_b,
                                           ret_gn_w, w_out_b, (xq_w, xo_w, mlp_up_w, mlp_down_w))
    return _xattn_mlp(h, norm_xattn_w, xq_b, kt, v, xo_b, norm_mlp_w, up_b, down_b, norm_f_w)
```

```python
import numpy as np
import jax
import jax.numpy as jnp
from jax import lax
from jax.experimental import pallas as pl
from jax.experimental.pallas import tpu as pltpu

D_MODEL = 1024
CHUNK = 64
CONV_CH = 512
CONV_WIDTH = 31
RET_HEADS = 8
RET_DV = 64
RET_DK = 32
RET_QK = RET_HEADS * RET_DK
RET_V = RET_HEADS * RET_DV
N_MEM = 256
XATTN_HEADS = 4
XATTN_HEAD_DIM = 256
D_FF = 4096
ROPE_BASE = 10000.0
EPS = 1e-6

OFF_A, OFF_B, OFF_Q, OFF_K, OFF_V, OFF_G = 0, 512, 1024, 1280, 1536, 2048
D_IN = 2560

LANES = 128
SUBLANES = 8
MXU_TILE = 256
MIX_BLOCK = 512
STATE_COLS = MXU_TILE
STATE_ROWS = STATE_COLS // RET_DV * RET_DK
RET_BLOCK = 128
CONV_HALO = 32
CONV_SEG = 128
DFT_N = CONV_HALO + CONV_SEG
N_FREQ = DFT_N // 2 + 1
FREQ_PAD = -(-N_FREQ // SUBLANES) * SUBLANES
CONV_TAIL = MXU_TILE - DFT_N
CONV_ROWS = 32
TOK_BLOCK = 1024
ROW_CHUNK = 256
MLP_ROW_CHUNK = 256
XATTN_TAIL_CHUNK = 512
MLP_TAIL_CHUNK = 256
FF_CHUNK = 1024
VMEM_LIMIT = 56 * 1024 * 1024

_f32 = jnp.float32
_bf16 = jnp.bfloat16


def _dot(a, b):
    return jnp.dot(a, b, preferred_element_type=_f32)


def _rms(x, w):
    return x * lax.rsqrt(jnp.mean(x * x, axis=-1, keepdims=True) + EPS) * w


def _sigmoid(x):
    return 1.0 / (1.0 + jnp.exp(-x))


def _swish(x):
    h = 0.5 * x
    return h + h * jnp.tanh(h)


def _cast_columns(src_ref, dst_ref):
    for c in range(0, src_ref.shape[-1], MXU_TILE):
        dst_ref[:, c:c + MXU_TILE] = src_ref[:, c:c + MXU_TILE].astype(_bf16)


def _group_norm_halves(y, lo):
    inv = 1.0 / RET_DV
    s_lo = jnp.sum(jnp.where(lo, y, 0.0), axis=-1, keepdims=True)
    s_hi = jnp.sum(jnp.where(lo, 0.0, y), axis=-1, keepdims=True)
    d = y - jnp.where(lo, s_lo, s_hi) * inv
    d2 = d * d
    v_lo = jnp.sum(jnp.where(lo, d2, 0.0), axis=-1, keepdims=True)
    v_hi = jnp.sum(jnp.where(lo, 0.0, d2), axis=-1, keepdims=True)
    return d * lax.rsqrt(jnp.where(lo, v_lo, v_hi) * inv + EPS)


N_CAST = 5
CAST_SCALES = (XATTN_HEAD_DIM ** -0.5, 1.0, 1.0, 1.0, 1.0)


def _mixer_kernel(*refs):
    n_in = 20
    (x_ref, pos_ref, nw_ref, win32_ref, cw_ref, cb_ref, lnw_ref, lnb_ref, gnw_ref, wout32_ref,
     dmat_ref, xi_ref, zeta_ref, gl_ref, invf_ref, cost_ref, sint_ref, fwd32_ref, inv32_ref, tap_ref) = refs[:n_in]
    cast_src = refs[n_in:n_in + N_CAST]
    o_ref = refs[n_in + N_CAST]
    cast_dst = refs[n_in + 1 + N_CAST:n_in + 1 + 2 * N_CAST]
    (inv_ref, hspec, cwbuf, hbuf, hist, ubuf, ybuf, qbuf, ktbuf, vbuf, vzbuf, gbuf, state, state_b, mbuf,
     win_ref, wout_ref, fwd_ref) = refs[n_in + 1 + 2 * N_CAST:]
    L = MIX_BLOCK

    @pl.when((pl.program_id(0) == 0) & (pl.program_id(1) == 0))
    def _():
        _cast_columns(win32_ref, win_ref)
        _cast_columns(wout32_ref, wout_ref)
        fwd_ref[...] = fwd32_ref[...].astype(_bf16)
        inv_ref[...] = inv32_ref[...].astype(_bf16)
        ubuf[...] = jnp.zeros(ubuf.shape, _bf16)
        cwbuf[...] = jnp.zeros(cwbuf.shape, _f32)
        cwbuf[0:CONV_WIDTH, :] = cw_ref[...]
        hspec[...] = jnp.dot(tap_ref[...], cwbuf[...], preferred_element_type=_f32, precision=lax.Precision.HIGHEST)

    @pl.when(pl.program_id(1) == 0)
    def _():
        hist[...] = jnp.zeros((CONV_HALO, CONV_CH), _bf16)
        state[...] = jnp.zeros_like(state)
        state_b[...] = jnp.zeros_like(state_b)

    for src, dst, scale in zip(cast_src, cast_dst, CAST_SCALES):
        dst[...] = (src[...] if scale == 1.0 else src[...] * scale).astype(_bf16)

    x = x_ref[0]
    ubuf[0:CONV_HALO, :] = hist[...]

    def glu_rows(row_lo, row_hi):
        for m in range(row_lo, row_hi, ROW_CHUNK):
            hn_m = _rms(x_ref[0, m:m + ROW_CHUNK, :], nw_ref[...]).astype(_bf16)
            hbuf[m:m + ROW_CHUNK, :] = hn_m
            a = _dot(hn_m, win_ref[:, OFF_A:OFF_A + CONV_CH])
            b = _dot(hn_m, win_ref[:, OFF_B:OFF_B + CONV_CH])
            ubuf[CONV_HALO + m:CONV_HALO + m + ROW_CHUNK, :] = (a * _sigmoid(b)).astype(_bf16)

    first_pos = pos_ref[:, 0:1].astype(_f32)
    batch_row = lax.broadcasted_iota(jnp.int32, first_pos.shape, 0)
    ang0 = jnp.sum(jnp.where(batch_row == pl.program_id(0), first_pos, 0.0), axis=0, keepdims=True) * invf_ref[...]
    cos0, sin0 = jnp.cos(ang0), jnp.sin(ang0)
    lane = lax.broadcasted_iota(jnp.int32, (1, RET_QK), 1)
    first_half = lane % RET_DK < RET_DK // 2
    sub_blocks = [slice(r0, r0 + RET_BLOCK) for r0 in range(0, L, RET_BLOCK)]

    def rotary(t, rows):
        cos = cos0 * cost_ref[rows, :] - sin0 * sint_ref[rows, :]
        sin = sin0 * cost_ref[rows, :] + cos0 * sint_ref[rows, :]
        cos2 = jnp.concatenate([cos, cos], axis=-1)
        sin2 = jnp.where(first_half, -1.0, 1.0) * jnp.concatenate([sin, sin], axis=-1)
        partner = jnp.where(first_half, pltpu.roll(t, RET_QK - RET_DK // 2, 1), pltpu.roll(t, RET_DK // 2, 1))
        return t * cos2 + partner * sin2

    def project_rows(row_lo, row_hi):
        hn = hbuf[row_lo:row_hi, :]
        mine = [(i, rows, slice(rows.start - row_lo, rows.stop - row_lo)) for i, rows in enumerate(sub_blocks)
                if row_lo <= rows.start < row_hi]
        q_all = _dot(hn, win_ref[:, OFF_Q:OFF_K])
        for _, rows, local in mine:
            qbuf[rows, :] = rotary(q_all[local, :], rows).astype(_bf16)
        k_all = _dot(hn, win_ref[:, OFF_K:OFF_V])
        for i, rows, local in mine:
            ktbuf[i] = rotary(k_all[local, :], rows).T.astype(_bf16)
        v_all = _dot(hn, win_ref[:, OFF_V:OFF_G])
        vbuf[row_lo:row_hi, :] = v_all.astype(_bf16)
        for _, rows, local in mine:
            vzbuf[rows, :] = (v_all[local, :] * zeta_ref[...]).astype(_bf16)
        gbuf[row_lo:row_hi, :] = _swish(_dot(hn, win_ref[:, OFF_G:D_IN]))

    h_re, h_im = hspec[0:FREQ_PAD, :], hspec[FREQ_PAD:2 * FREQ_PAD, :]

    def conv_rows(row_lo, row_hi):
        seg_rows = range(row_lo, row_hi, CONV_SEG)
        specs = [_dot(fwd_ref[...], ubuf[row0:row0 + MXU_TILE, :]) for row0 in seg_rows]
        dc_row = lax.broadcasted_iota(jnp.int32, (SUBLANES, CONV_CH), 0) == 0
        bias_dc = jnp.where(dc_row, cb_ref[...] * float(DFT_N), 0.0)
        prods = []
        for spec in specs:
            s_re, s_im = spec[0:FREQ_PAD, :], spec[FREQ_PAD:2 * FREQ_PAD, :]
            p_re = s_re * h_re - s_im * h_im
            p_re = jnp.concatenate([p_re[0:SUBLANES, :] + bias_dc, p_re[SUBLANES:, :]], axis=0)
            prods.append(jnp.concatenate([p_re, s_re * h_im + s_im * h_re,
                                          jnp.zeros((MXU_TILE - 2 * FREQ_PAD, CONV_CH), _f32)],
                                         axis=0).astype(_bf16))
        for row0, prod in zip(seg_rows, prods):
            ybuf[row0:row0 + CONV_SEG, :] = _dot(inv_ref[...], prod)
        half_lnw, half_lnb = 0.5 * lnw_ref[...], 0.5 * lnb_ref[...]
        for r in range(row_lo, row_hi, CONV_ROWS):
            acc = ybuf[r:r + CONV_ROWS, :]
            mu = jnp.mean(acc, axis=-1, keepdims=True)
            d = acc - mu
            var = jnp.mean(d * d, axis=-1, keepdims=True)
            half = d * lax.rsqrt(var + EPS) * half_lnw + half_lnb
            mbuf[r:r + CONV_ROWS, 0:CONV_CH] = (half + half * jnp.tanh(half)).astype(_bf16)

    lo = lax.broadcasted_iota(jnp.int32, (1, LANES), 1) < RET_DV
    same_head = (lax.broadcasted_iota(jnp.int32, (STATE_ROWS, STATE_COLS), 0) // RET_DK
                 == lax.broadcasted_iota(jnp.int32, (STATE_ROWS, STATE_COLS), 1) // RET_DV)

    def retention(i):
        rows = sub_blocks[i]
        qb = qbuf[rows, :]
        kt = ktbuf[i]
        vb = vbuf[rows, :]

        yx = _dot(qb, state_b[...]) * xi_ref[...]

        for p in range(RET_HEADS // 2):
            cols = slice(p * LANES, (p + 1) * LANES)
            blank = jnp.zeros((RET_DK, RET_BLOCK), _bf16)
            k_pair = jnp.concatenate(
                [jnp.concatenate([kt[h * RET_DK:(h + 1) * RET_DK] if h == 2 * p else blank,
                                  kt[h * RET_DK:(h + 1) * RET_DK] if h == 2 * p + 1 else blank], axis=1)
                 for h in range(RET_HEADS)], axis=0)
            s = _dot(qb, k_pair)
            vp = vb[:, cols]
            off = jnp.zeros_like(vp)
            v_pair = jnp.concatenate([jnp.where(lo, vp, off), jnp.where(lo, off, vp)], axis=0)
            y = _dot((s * dmat_ref[p]).astype(_bf16), v_pair) + yx[:, cols]
            yn = _group_norm_halves(y, lo) * gnw_ref[:, cols]
            mbuf[rows, CONV_CH + p * LANES:CONV_CH + (p + 1) * LANES] = (gbuf[rows, cols] * yn).astype(_bf16)

        for q in range(RET_QK // STATE_ROWS):
            srows = slice(q * STATE_ROWS, (q + 1) * STATE_ROWS)
            scols = slice(q * STATE_COLS, (q + 1) * STATE_COLS)
            kv = _dot(kt[srows, :], vzbuf[rows, scols])
            new = gl_ref[:, scols] * state[srows, scols] + jnp.where(same_head, kv, 0.0)
            state[srows, scols] = new
            state_b[srows, scols] = new.astype(_bf16)

    half_rows = L // 2
    per_half = len(sub_blocks) // 2
    glu_rows(0, half_rows)
    project_rows(0, half_rows)
    conv_rows(0, half_rows)
    glu_rows(half_rows, L)
    hist[...] = ubuf[L:L + CONV_HALO, :]
    retention(0)
    project_rows(half_rows, L)
    for i in range(1, per_half):
        retention(i)
    conv_rows(half_rows, L)
    for i in range(per_half, len(sub_blocks)):
        retention(i)

    o_ref[0] = x + _dot(mbuf[...], wout_ref[...])


def _retention_tables(L, step_rows):
    h = np.arange(RET_HEADS, dtype=np.float64)
    log_g = np.log1p(-np.exp2(-5.0 - h))
    idx = np.arange(L, dtype=np.float64)
    dist = np.abs(idx[:, None] - idx[None, :])
    visible = (idx[None, :] // CHUNK) <= (idx[:, None] // CHUNK)
    dmat = np.where(visible[None], np.exp(log_g[:, None, None] * dist[None]), 0.0)
    dmat = np.concatenate([dmat[0::2], dmat[1::2]], axis=-1)
    xi = np.exp(log_g[None, :] * (idx[:, None] + 1.0))
    zeta = np.exp(log_g[None, :] * (L - 1.0 - idx[:, None]))
    gl = np.exp(log_g * L)
    rep = lambda t: np.repeat(t, RET_DV, axis=-1)
    inv_freq = ROPE_BASE ** (-np.arange(RET_DK // 2, dtype=np.float32) / np.float32(RET_DK // 2))
    invf = np.tile(inv_freq.astype(np.float32), LANES // (RET_DK // 2))[None, :]
    rel = np.arange(step_rows, dtype=np.float64)[:, None] * invf.astype(np.float64)
    f = lambda t: jnp.asarray(t, dtype=_f32)
    scale = RET_DK ** -0.5
    return (f(dmat * scale), f(rep(xi) * scale), f(rep(zeta)), f(rep(gl[None, :])), f(invf),
            f(np.cos(rel)), f(np.sin(rel)))


def _conv_dft_tables():
    n = np.arange(DFT_N, dtype=np.float64)
    f = np.arange(N_FREQ, dtype=np.float64)[:, None]
    w = 2.0 * np.pi / DFT_N
    fwd = np.zeros((2 * FREQ_PAD, MXU_TILE))
    fwd[:N_FREQ, :DFT_N] = np.cos(w * f * n[None, :])
    fwd[FREQ_PAD:FREQ_PAD + N_FREQ, :DFT_N] = -np.sin(w * f * n[None, :])
    delay = (CONV_WIDTH - 1) - np.arange(CONV_WIDTH, dtype=np.float64)[None, :]
    tap = np.zeros((2 * FREQ_PAD, CONV_HALO))
    tap[:N_FREQ, :CONV_WIDTH] = np.cos(w * f * delay)
    tap[FREQ_PAD:FREQ_PAD + N_FREQ, :CONV_WIDTH] = -np.sin(w * f * delay)
    weight = np.full((1, N_FREQ), 2.0)
    weight[0, 0] = weight[0, -1] = 1.0
    out = n[CONV_HALO:, None]
    inv = np.zeros((CONV_SEG, MXU_TILE))
    inv[:, :N_FREQ] = weight * np.cos(w * out * f.T) / DFT_N
    inv[:, FREQ_PAD:FREQ_PAD + N_FREQ] = -weight * np.sin(w * out * f.T) / DFT_N
    return [jnp.asarray(t, dtype=_f32) for t in (fwd, inv, tap)]


def _mixer(x, positions, norm_w, w_in, conv_w, conv_b, ln_w, ln_b, gn_w, w_out, later_weights):
    B, S, D = x.shape
    L = MIX_BLOCK
    n = S // L
    steps = B * n
    R = RET_BLOCK
    dmat, xi, zeta, gl, invf, cost, sint = _retention_tables(R, L)
    fwd, inv, tap = _conv_dft_tables()
    const = lambda shape: pl.BlockSpec(shape, lambda b, j: (0,) * len(shape))
    once = lambda shape: pl.BlockSpec(shape, lambda b, j: (0,) * len(shape), pipeline_mode=pl.Buffered(1))
    row = lambda n: pl.BlockSpec((1, n), lambda b, j: (0, 0))
    slabs = [w.reshape(steps, w.shape[0] // steps, w.shape[1]) for w in later_weights]
    slab_spec = lambda w: pl.BlockSpec((1,) + w.shape[1:], lambda b, j: (b * n + j, 0, 0))
    outs = pl.pallas_call(
        _mixer_kernel,
        grid=(B, n),
        in_specs=[
            pl.BlockSpec((1, L, D), lambda b, j: (b, j, 0)),
            pl.BlockSpec((B, L), lambda b, j: (0, j)),
            row(D),
            once((D, D_IN)),
            const((CONV_WIDTH, CONV_CH)),
            row(CONV_CH), row(CONV_CH), row(CONV_CH), row(RET_V),
            once((D, D)),
            const((RET_HEADS // 2, R, 2 * R)),
            const((R, RET_V)), const((R, RET_V)), row(RET_V), row(LANES),
            const((L, LANES)), const((L, LANES)),
            const(fwd.shape), const(inv.shape), const(tap.shape),
        ] + [slab_spec(w) for w in slabs],
        out_specs=[pl.BlockSpec((1, L, D), lambda b, j: (b, j, 0))] + [slab_spec(w) for w in slabs],
        out_shape=[jax.ShapeDtypeStruct((B, S, D), _f32)] + [jax.ShapeDtypeStruct(w.shape, _bf16) for w in slabs],
        scratch_shapes=[
            pltpu.VMEM(inv.shape, _bf16),
            pltpu.VMEM((2 * FREQ_PAD, CONV_CH), _f32),
            pltpu.VMEM((CONV_HALO, CONV_CH), _f32),
            pltpu.VMEM((L, D), _bf16),
            pltpu.VMEM((CONV_HALO, CONV_CH), _bf16),
            pltpu.VMEM((CONV_HALO + L + CONV_TAIL, CONV_CH), _bf16),
            pltpu.VMEM((L, CONV_CH), _f32),
            pltpu.VMEM((L, RET_QK), _bf16),
            pltpu.VMEM((L // R, RET_QK, R), _bf16),
            pltpu.VMEM((L, RET_V), _bf16),
            pltpu.VMEM((L, RET_V), _bf16),
            pltpu.VMEM((L, RET_V), _f32),
            pltpu.VMEM((RET_QK, RET_V), _f32),
            pltpu.VMEM((RET_QK, RET_V), _bf16),
            pltpu.VMEM((L, D), _bf16),
            pltpu.VMEM((D, D_IN), _bf16), pltpu.VMEM((D, D), _bf16),
            pltpu.VMEM(fwd.shape, _bf16),
        ],
        compiler_params=pltpu.CompilerParams(
            dimension_semantics=("arbitrary", "arbitrary"), vmem_limit_bytes=VMEM_LIMIT),
        name="mixer",
    )(x, positions, norm_w.reshape(1, D), w_in, conv_w, conv_b.reshape(1, -1),
      ln_w.reshape(1, -1), ln_b.reshape(1, -1), gn_w.reshape(1, -1), w_out, dmat, xi, zeta, gl, invf, cost, sint,
      fwd, inv, tap, *slabs)
    return outs[0], [o.reshape(w.shape) for o, w in zip(outs[1:], later_weights)]


def _xattn_mlp_kernel(h_ref, nxw_ref, wq_ref, mem_ref, nkw_ref, wkv_ref, wo_ref, nmw_ref, wu_hbm, wd_hbm, fw_ref,
                      o_ref, qbuf, obuf, abuf, ktbuf, vbuf, wu_ref, wd_ref, sem):
    T = h_ref.shape[1]

    @pl.when(pl.program_id(1) == 0)
    def _():
        mn = _rms(mem_ref[0], nkw_ref[...]).astype(_bf16)
        ktbuf[...] = _dot(mn, wkv_ref[:, :D_MODEL]).T.astype(_bf16)
        vbuf[...] = _dot(mn, wkv_ref[:, D_MODEL:]).astype(_bf16)

    first = (pl.program_id(0) == 0) & (pl.program_id(1) == 0)
    up_copy = pltpu.make_async_copy(wu_hbm, wu_ref, sem.at[0])
    down_copy = pltpu.make_async_copy(wd_hbm, wd_ref, sem.at[1])

    @pl.when(first)
    def _():
        up_copy.start()
        down_copy.start()

    for m in range(0, T, ROW_CHUNK):
        hn = _rms(h_ref[0, m:m + ROW_CHUNK, :], nxw_ref[...]).astype(_bf16)
        qbuf[m:m + ROW_CHUNK, :] = _dot(hn, wq_ref[...]).astype(_bf16)
    for i in range(XATTN_HEADS):
        cols = slice(i * XATTN_HEAD_DIM, (i + 1) * XATTN_HEAD_DIM)
        s = _dot(qbuf[:, cols], ktbuf[cols, :])
        e = jnp.exp(s - jnp.max(s, axis=-1, keepdims=True))
        o = _dot(e.astype(_bf16), vbuf[:, cols])
        obuf[:, cols] = (o * (1.0 / jnp.sum(e, axis=-1, keepdims=True))).astype(_bf16)
    for m in range(0, T, XATTN_TAIL_CHUNK):
        rows = slice(m, m + XATTN_TAIL_CHUNK)
        o_ref[0, rows, :] = h_ref[0, rows, :] + _dot(obuf[rows, :], wo_ref[...])

    @pl.when(first)
    def _():
        up_copy.wait()
        down_copy.wait()

    for m in range(0, T, MLP_ROW_CHUNK):
        hn = _rms(o_ref[0, m:m + MLP_ROW_CHUNK, :], nmw_ref[...]).astype(_bf16)
        for c in range(0, D_FF, FF_CHUNK):
            u = jnp.maximum(_dot(hn, wu_ref[:, c:c + FF_CHUNK]), 0.0)
            abuf[m:m + MLP_ROW_CHUNK, c:c + FF_CHUNK] = (u * u).astype(_bf16)
    for m in range(0, T, MLP_TAIL_CHUNK):
        rows = slice(m, m + MLP_TAIL_CHUNK)
        o_ref[0, rows, :] = _rms(o_ref[0, rows, :] + _dot(abuf[rows, :], wd_ref[...]), fw_ref[...])


def _xattn_mlp(h, norm_x_w, xq_w, mem, norm_k_w, xkv_w, xo_w, norm_m_w, up_w, down_w, norm_f_w):
    B, S, D = h.shape
    T = TOK_BLOCK
    row = lambda: pl.BlockSpec((1, D), lambda b, j: (0, 0))
    once = lambda shape: pl.BlockSpec(shape, lambda b, j: (0, 0), pipeline_mode=pl.Buffered(1))
    return pl.pallas_call(
        _xattn_mlp_kernel,
        grid=(B, S // T),
        in_specs=[
            pl.BlockSpec((1, T, D), lambda b, j: (b, j, 0)),
            row(), once((D, D)),
            pl.BlockSpec((1, N_MEM, D), lambda b, j: (b, 0, 0)),
            row(), once((D, 2 * D)),
            once((D, D)),
            row(), pl.BlockSpec(memory_space=pl.ANY), pl.BlockSpec(memory_space=pl.ANY), row(),
        ],
        out_specs=pl.BlockSpec((1, T, D), lambda b, j: (b, j, 0)),
        out_shape=jax.ShapeDtypeStruct((B, S, D), _f32),
        scratch_shapes=[pltpu.VMEM((T, D), _bf16), pltpu.VMEM((T, D), _bf16),
                        pltpu.VMEM((T, D_FF), _bf16),
                        pltpu.VMEM((D, N_MEM), _bf16), pltpu.VMEM((N_MEM, D), _bf16),
                        pltpu.VMEM((D, D_FF), _bf16), pltpu.VMEM((D_FF, D), _bf16),
                        pltpu.SemaphoreType.DMA((2,))],
        compiler_params=pltpu.CompilerParams(
            dimension_semantics=("arbitrary", "arbitrary"), vmem_limit_bytes=VMEM_LIMIT),
        name="xattn_mlp",
    )(h, norm_x_w.reshape(1, D), xq_w, mem, norm_k_w.reshape(1, D), xkv_w, xo_w, norm_m_w.reshape(1, D), up_w, down_w,
      norm_f_w.reshape(1, D))


def kernel(x, mem, positions, norm_mix_w, w_in, conv_w, conv_b, conv_ln_w, conv_ln_b, ret_gn_w, w_out,
           norm_xattn_w, norm_mem_w, xq_w, xkv_w, xo_w, norm_mlp_w, mlp_up_w, mlp_down_w, norm_f_w):
    h, (xq_b, xo_b, up_b, down_b, xkv_b) = _mixer(x, positions, norm_mix_w, w_in, conv_w, conv_b, conv_ln_w,
                                                  conv_ln_b, ret_gn_w, w_out,
                                                  (xq_w, xo_w, mlp_up_w, mlp_down_w, xkv_w))
    return _xattn_mlp(h, norm_xattn_w, xq_b, mem, norm_mem_w, xkv_b, xo_b, norm_mlp_w, up_b, down_b, norm_f_w)
```
